```python
import math
import jax
import jax.numpy as jnp
from jax import lax
import numpy as np

D_MODEL = 1024
BATCH = 2
SEQ = 8192
DEPTH = 2
DEC_BATCH = 32
DEC_SEQ = 8
PAST_LEN = 8192
PAGE_SIZE = 128

SSD_HEADS = 8
SSD_HEAD_DIM = 64
SSD_INNER = SSD_HEADS * SSD_HEAD_DIM
SSD_GROUPS = 2
SSD_STATE = 64
GDN_HEADS = 4
GDN_HEAD_DIM = 128
GDN_INNER = GDN_HEADS * GDN_HEAD_DIM
CONV_K = 4
CHUNK = 64
SSD_CONV_CH = SSD_INNER + 2 * SSD_GROUPS * SSD_STATE
GDN_CONV_CH = 3 * GDN_INNER
CONV_CH = SSD_CONV_CH + GDN_CONV_CH
IN0_WIDTH = CONV_CH + SSD_INNER + SSD_HEADS + GDN_INNER + 2 * GDN_HEADS
FOX_HEADS = 8
FOX_HEAD_DIM = 64
FOX_INNER = FOX_HEADS * FOX_HEAD_DIM
Q_BLOCK = 128
FORGET_BIAS = 3.0
S5_CH = 16
S5_GROUPS = 32
S5_INNER = S5_CH * S5_GROUPS
S5_STATE = 64
IN1_WIDTH = 3 * FOX_INNER + FOX_HEADS + S5_INNER
D_FF = 2816
N_EXPERTS = 8
TOP_K = 2
D_FF_EXPERT = 3584
EPS = 1e-6

kernel_name = 'hybrid_ssd_gdn_fox_s5_adaln_decode_step'


def rmsnorm(x, w):
    xf = x.astype(jnp.float32)
    y = xf * lax.rsqrt(jnp.mean(xf * xf, axis=-1, keepdims=True) + EPS)
    return (y * w.astype(jnp.float32)).astype(x.dtype)


def l2norm(x):
    xf = x.astype(jnp.float32)
    return xf * lax.rsqrt(jnp.sum(xf * xf, axis=-1, keepdims=True) + EPS)


def ada_modulation(c, w_ada, b_ada):
    mod = (jax.nn.silu(c) @ w_ada + b_ada)[:, None, :]
    return jnp.split(mod, 6, axis=-1)


def modulate(x, g, shift, scale):
    return rmsnorm(x, g) * (1 + scale) + shift


def causal_conv(u, buf, w, bias):
    l = u.shape[1]
    upad = jnp.concatenate([buf.astype(u.dtype), u], axis=1)
    out = bias + sum(w[j] * upad[:, j:j + l] for j in range(CONV_K))
    return jax.nn.silu(out), upad[:, upad.shape[1] - (CONV_K - 1):]


def ssd_chunked(x, dt, A, Bm, Cm, h0):
    f32 = jnp.float32
    b, l, h, p = x.shape
    n = Bm.shape[-1]
    q = math.gcd(l, CHUNK)
    c = l // q
    xdt = (x.astype(f32) * dt[..., None]).reshape(b, c, q, h, p)
    acum = jnp.cumsum((dt * A).reshape(b, c, q, h), axis=2)
    Bc = Bm.astype(f32).reshape(b, c, q, h, n)
    Cc = Cm.astype(f32).reshape(b, c, q, h, n)
    causal = jnp.tril(jnp.ones((q, q), bool))[None, None, :, :, None]
    seg = acum[:, :, :, None, :] - acum[:, :, None, :, :]
    decay = jnp.where(causal, jnp.exp(jnp.where(causal, seg, 0.0)), 0.0)
    scores = jnp.einsum('bcihn,bcjhn->bcijh', Cc, Bc) * decay
    y_diag = jnp.einsum('bcijh,bcjhp->bcihp', scores, xdt)
    to_end = jnp.exp(acum[:, :, -1:, :] - acum)
    chunk_states = jnp.einsum('bcjhn,bcjhp->bchpn', Bc * to_end[..., None], xdt)
    chunk_decay = jnp.exp(acum[:, :, -1, :])

    def step(state, inp):
        st, dec = inp
        return state * dec[..., None, None] + st, state

    h_final, h_enter = lax.scan(step, h0.astype(f32),
                                (jnp.moveaxis(chunk_states, 1, 0), jnp.moveaxis(chunk_decay, 1, 0)))
    h_enter = jnp.moveaxis(h_enter, 0, 1)
    y_off = jnp.einsum('bcihn,bchpn->bcihp', Cc * jnp.exp(acum)[..., None], h_enter)
    return (y_diag + y_off).reshape(b, l, h, p), h_final


def gdn_chunked(q, k, v, g, beta, s0):
    f32 = jnp.float32
    b, l, h, dk = q.shape
    dv = v.shape[-1]
    Q = math.gcd(l, CHUNK)
    c = l // Q

    def blk(t):
        return t.astype(f32).reshape(b, c, Q, h, t.shape[-1]).transpose(0, 1, 3, 2, 4)

    qc = blk(q) * dk ** -0.5
    kc = blk(k)
    vc = blk(v)
    gc = jnp.cumsum(g.astype(f32).reshape(b, c, Q, h).transpose(0, 1, 3, 2), axis=-1)
    bc = beta.astype(f32).reshape(b, c, Q, h).transpose(0, 1, 3, 2)
    incl = jnp.tril(jnp.ones((Q, Q), bool))
    strict = jnp.tril(jnp.ones((Q, Q), bool), k=-1)
    diff = gc[..., :, None] - gc[..., None, :]
    decay = jnp.where(incl, jnp.exp(jnp.where(incl, diff, 0.0)), 0.0)
    kb = kc * bc[..., None]
    vb = vc * bc[..., None]
    m = jnp.where(strict, jnp.einsum('bchid,bchjd->bchij', kb, kc) * decay, 0.0)
    t_sys = jnp.eye(Q, dtype=f32) + m
    rhs = jnp.concatenate([vb, kb * jnp.exp(gc)[..., None]], axis=-1)
    sol = lax.linalg.triangular_solve(t_sys, rhs, left_side=True, lower=True)
    u = sol[..., :dv]
    w = sol[..., dv:]
    attn = jnp.einsum('bchid,bchjd->bchij', qc, kc) * decay
    qd = qc * jnp.exp(gc)[..., None]
    ke = kc * jnp.exp(gc[..., -1:] - gc)[..., None]
    dl = jnp.exp(gc[..., -1])

    def step(S, inp):
        u_c, w_c, qd_c, ke_c, a_c, dl_c = inp
        v_new = u_c - jnp.einsum('bhid,bhde->bhie', w_c, S)
        o = jnp.einsum('bhid,bhde->bhie', qd_c, S) + jnp.einsum('bhij,bhje->bhie', a_c, v_new)
        S = S * dl_c[..., None, None] + jnp.einsum('bhid,bhie->bhde', ke_c, v_new)
        return S, o

    xs = tuple(jnp.moveaxis(t, 1, 0) for t in (u, w, qd, ke, attn, dl))
    s_final, o = lax.scan(step, s0.astype(f32), xs)
    return jnp.transpose(o, (1, 0, 3, 2, 4)).reshape(b, l, h, dv), s_final


def fox_attention(q, k, v, logf, past):
    f32 = jnp.float32
    b, l, h, d = q.shape
    if past is None:
        k_all, v_all, logf_all = k.astype(f32), v.astype(f32), logf
    else:
        k_past, v_past, logf_past = past
        k_all = jnp.concatenate([k_past.astype(f32), k.astype(f32)], axis=1)
        v_all = jnp.concatenate([v_past.astype(f32), v.astype(f32)], axis=1)
        logf_all = jnp.concatenate([logf_past.astype(f32), logf], axis=1)
    n_keys = k_all.shape[1]
    offset = n_keys - l
    F = jnp.cumsum(logf_all, axis=1)
    Fk = jnp.moveaxis(F, 1, 2)
    Fq = F[:, offset:]
    kpos = jnp.arange(n_keys)
    qb = math.gcd(l, Q_BLOCK)
    nb = l // qb
    scale = d ** -0.5

    def block(args):
        q_i, fq_i, qpos_i = args
        s = jnp.einsum('bqhd,bkhd->bhqk', q_i.astype(f32), k_all) * scale
        s = s + jnp.moveaxis(fq_i, 1, 2)[..., None] - Fk[:, :, None, :]
        s = jnp.where(qpos_i[:, None] >= kpos[None, :], s, -jnp.inf)
        pr = jax.nn.softmax(s, axis=-1)
        return jnp.einsum('bhqk,bkhd->bqhd', pr, v_all)

    q_blocks = jnp.moveaxis(q.reshape(b, nb, qb, h, d), 1, 0)
    fq_blocks = jnp.moveaxis(Fq.reshape(b, nb, qb, h), 1, 0)
    qpos_blocks = (offset + jnp.arange(l)).reshape(nb, qb)
    o = lax.map(block, (q_blocks, fq_blocks, qpos_blocks))
    return jnp.moveaxis(o, 0, 1).reshape(b, l, h, d)


def s5_scan(u, A_re, A_im, log_step, B_re, B_im, C_re, C_im, Dd, x0_re, x0_im):
    f32 = jnp.float32
    b, l, _ = u.shape
    uu = u.astype(f32).reshape(b, l, S5_GROUPS, S5_CH)
    ar = A_re.astype(f32)
    ai = A_im.astype(f32)
    step = jnp.exp(log_step.astype(f32))[:, None]
    mag = jnp.exp(ar * step)
    lb_re = mag * jnp.cos(ai * step)
    lb_im = mag * jnp.sin(ai * step)
    den = ar * ar + ai * ai
    nr = lb_re - 1.0
    cr = (nr * ar + lb_im * ai) / den
    ci = (lb_im * ar - nr * ai) / den
    bb_re = cr[..., None] * B_re - ci[..., None] * B_im
    bb_im = cr[..., None] * B_im + ci[..., None] * B_re
    bu_re = jnp.einsum('gph,blgh->blgp', bb_re, uu)
    bu_im = jnp.einsum('gph,blgh->blgp', bb_im, uu)
    shp = bu_re.shape
    a_re = jnp.broadcast_to(lb_re, shp)
    a_im = jnp.broadcast_to(lb_im, shp)

    def combine(e1, e2):
        a1r, a1i, b1r, b1i = e1
        a2r, a2i, b2r, b2i = e2
        return (a2r * a1r - a2i * a1i, a2r * a1i + a2i * a1r,
                a2r * b1r - a2i * b1i + b2r, a2r * b1i + a2i * b1r + b2i)

    pr, pi, xr, xi = lax.associative_scan(combine, (a_re, a_im, bu_re, bu_im), axis=1)
    x0r = x0_re.astype(f32)[:, None]
    x0i = x0_im.astype(f32)[:, None]
    xr = xr + pr * x0r - pi * x0i
    xi = xi + pr * x0i + pi * x0r
    y = (jnp.einsum('ghp,blgp->blgh', C_re, xr) - jnp.einsum('ghp,blgp->blgh', C_im, xi)
         + Dd * uu)
    return y.reshape(b, l, S5_INNER), xr[:, -1], xi[:, -1]


def swiglu(h, w1, w3, w2):
    return (jax.nn.silu(h @ w1) * (h @ w3)) @ w2


def moe_swiglu(h, router_w, w1, w3, w2):
    logits = h.astype(jnp.float32) @ router_w.astype(jnp.float32)
    top_v, top_i = lax.top_k(logits, TOP_K)
    gates = jax.nn.softmax(top_v, axis=-1)
    comb = jnp.sum(jax.nn.one_hot(top_i, N_EXPERTS, dtype=jnp.float32) * gates[..., None], axis=-2)
    out = jnp.zeros(h.shape, jnp.float32)
    for e in range(N_EXPERTS):
        out = out + comb[..., e:e + 1] * swiglu(h, w1[e], w3[e], w2[e]).astype(jnp.float32)
    return out.astype(h.dtype)


def even_layer(x, c, conv_buf, ssd_h0, gdn_s0, p):
    f32 = jnp.float32
    b, l, _ = x.shape
    sh1, sc1, g1, sh2, sc2, g2 = ada_modulation(c, p['w_ada'], p['b_ada'])
    h = modulate(x, p['norm_mix'], sh1, sc1)
    proj = h @ p['w_in']
    conv_out, conv_new = causal_conv(proj[..., :CONV_CH], conv_buf, p['conv_w'], p['conv_b'])
    xs, Bm, Cm, qg, kg, vg = jnp.split(
        conv_out, [SSD_INNER, SSD_INNER + SSD_GROUPS * SSD_STATE, SSD_CONV_CH,
                   SSD_CONV_CH + GDN_INNER, SSD_CONV_CH + 2 * GDN_INNER], axis=-1)
    z_ssd, dt_raw, z_gdn, a_gdn, b_gdn = jnp.split(
        proj[..., CONV_CH:], [SSD_INNER, SSD_INNER + SSD_HEADS, SSD_INNER + SSD_HEADS + GDN_INNER,
                              SSD_INNER + SSD_HEADS + GDN_INNER + GDN_HEADS], axis=-1)
    xs = xs.reshape(b, l, SSD_HEADS, SSD_HEAD_DIM)
    rep = SSD_HEADS // SSD_GROUPS
    Bm = jnp.repeat(Bm.reshape(b, l, SSD_GROUPS, SSD_STATE), rep, axis=2)
    Cm = jnp.repeat(Cm.reshape(b, l, SSD_GROUPS, SSD_STATE), rep, axis=2)
    dt = jax.nn.softplus((dt_raw + p['ssd_dt_bias']).astype(f32))
    A = -jnp.exp(p['ssd_A_log'].astype(f32))
    y, ssd_new = ssd_chunked(xs, dt, A, Bm, Cm, ssd_h0)
    y = (y + p['ssd_D'].astype(f32)[:, None] * xs.astype(f32)).reshape(b, l, SSD_INNER)
    y_ssd = rmsnorm(y * jax.nn.silu(z_ssd.astype(f32)), p['ssd_norm'])
    q = l2norm(qg.reshape(b, l, GDN_HEADS, GDN_HEAD_DIM))
    k = l2norm(kg.reshape(b, l, GDN_HEADS, GDN_HEAD_DIM))
    v = vg.reshape(b, l, GDN_HEADS, GDN_HEAD_DIM)
    g_log = -jnp.exp(p['gdn_A_log'].astype(f32)) * jax.nn.softplus((a_gdn + p['gdn_dt_bias']).astype(f32))
    beta = jax.nn.sigmoid(b_gdn.astype(f32))
    o, gdn_new = gdn_chunked(q, k, v, g_log, beta, gdn_s0)
    o = rmsnorm(o, p['gdn_norm']) * jax.nn.silu(z_gdn.astype(f32).reshape(b, l, GDN_HEADS, GDN_HEAD_DIM))
    mix = jnp.concatenate([y_ssd, o.reshape(b, l, GDN_INNER)], axis=-1).astype(x.dtype)
    x = x + g1 * (mix @ p['w_out'])
    h = modulate(x, p['norm_ffn'], sh2, sc2)
    x = x + g2 * swiglu(h, p['ffn_w1'], p['ffn_w3'], p['ffn_w2'])
    return x, conv_new, ssd_new, gdn_new


def odd_layer(x, c, past, s5_re0, s5_im0, p):
    f32 = jnp.float32
    b, l, _ = x.shape
    sh1, sc1, g1, sh2, sc2, g2 = ada_modulation(c, p['w_ada'], p['b_ada'])
    h = modulate(x, p['norm_mix'], sh1, sc1)
    proj = h @ p['w_in']
    q, k, v, f_raw, u = jnp.split(
        proj, [FOX_INNER, 2 * FOX_INNER, 3 * FOX_INNER, 3 * FOX_INNER + FOX_HEADS], axis=-1)
    q = q.reshape(b, l, FOX_HEADS, FOX_HEAD_DIM)
    k = k.reshape(b, l, FOX_HEADS, FOX_HEAD_DIM)
    v = v.reshape(b, l, FOX_HEADS, FOX_HEAD_DIM)
    logf = jax.nn.log_sigmoid((f_raw + p['fox_f_bias']).astype(f32))
    o_fox = fox_attention(q, k, v, logf, past).reshape(b, l, FOX_INNER)
    y_s5, s5_re, s5_im = s5_scan(u, p['s5_A_re'], p['s5_A_im'], p['s5_log_step'], p['s5_B_re'],
                                 p['s5_B_im'], p['s5_C_re'], p['s5_C_im'], p['s5_D'], s5_re0, s5_im0)
    hs = jax.nn.gelu(y_s5)
    o_s5 = hs * jax.nn.sigmoid(hs @ p['glu_w'].astype(f32) + p['glu_b'].astype(f32))
    mix = jnp.concatenate([o_fox, o_s5], axis=-1).astype(x.dtype)
    x = x + g1 * (mix @ p['w_out'])
    h = modulate(x, p['norm_ffn'], sh2, sc2)
    x = x + g2 * moe_swiglu(h, p['router_w'], p['moe_w1'], p['moe_w3'], p['moe_w2'])
    return x, k, v, logf, s5_re, s5_im


def run_trunk(x, c, conv_buf, ssd_h0, gdn_s0, past, s5_re0, s5_im0, pe, po, norm_final):
    for layer in range(DEPTH):
        if layer % 2 == 0:
            x, conv_new, ssd_new, gdn_new = even_layer(x, c, conv_buf, ssd_h0, gdn_s0, pe)
        else:
            x, k_new, v_new, logf_new, s5_re, s5_im = odd_layer(x, c, past, s5_re0, s5_im0, po)
    y = rmsnorm(x, norm_final)
    return y, conv_new, ssd_new, gdn_new, k_new, v_new, logf_new, s5_re, s5_im


def setup_inputs(seed: int = 0) -> dict:
    key = jax.random.key(seed)
    ks = iter(jax.random.split(key, 80))
    f32 = jnp.float32

    def nrm(shape, scale=1.0):
        return jax.random.normal(next(ks), shape, f32) * scale

    def gain(n):
        return 1.0 + nrm((n,), 0.02)

    def unif(shape, lo, hi):
        return jax.random.uniform(next(ks), shape, f32, lo, hi)

    n_pages = PAST_LEN // PAGE_SIZE
    n_pool = (DEC_BATCH * n_pages * 5) // 4
    page_table = jax.random.permutation(next(ks), n_pool)[:DEC_BATCH * n_pages]
    page_table = page_table.reshape(DEC_BATCH, n_pages).astype(jnp.int32)
    ssd_dt = jnp.exp(unif((SSD_HEADS,), math.log(1e-3), math.log(1e-1)))
    gdn_dt = jnp.exp(unif((GDN_HEADS,), math.log(1e-3), math.log(1e-1)))
    inv = D_MODEL ** -0.5
    return {
        'x_prompt': nrm((BATCH, SEQ, D_MODEL)),
        'x_sample': nrm((DEC_BATCH, DEC_SEQ, D_MODEL)),
        'state_conv0': nrm((DEC_BATCH, CONV_K - 1, CONV_CH)),
        'state_ssd': nrm((DEC_BATCH, SSD_HEADS, SSD_HEAD_DIM, SSD_STATE), 0.1),
        'state_gdn': nrm((DEC_BATCH, GDN_HEADS, GDN_HEAD_DIM, GDN_HEAD_DIM), 0.1),
        'cache_k': nrm((n_pool, PAGE_SIZE, FOX_HEADS, FOX_HEAD_DIM)),
        'cache_v': nrm((n_pool, PAGE_SIZE, FOX_HEADS, FOX_HEAD_DIM)),
        'cache_logf': jax.nn.log_sigmoid(FORGET_BIAS + nrm((n_pool, PAGE_SIZE, FOX_HEADS))),
        'state_s5_re': nrm((DEC_BATCH, S5_GROUPS, S5_STATE), 0.3),
        'state_s5_im': nrm((DEC_BATCH, S5_GROUPS, S5_STATE), 0.3),
        'page_table': page_table,
        'c_prompt': nrm((BATCH, D_MODEL)),
        'c_sample': nrm((DEC_BATCH, D_MODEL)),
        'ada0_w': nrm((D_MODEL, 6 * D_MODEL), 0.5 * inv),
        'ada0_b': nrm((6 * D_MODEL,), 0.02),
        'norm_mix0': gain(D_MODEL),
        'w_in0': nrm((D_MODEL, IN0_WIDTH), inv),
        'conv0_w': nrm((CONV_K, CONV_CH), 0.5),
        'conv0_b': nrm((CONV_CH,), 0.02),
        'ssd_dt_bias': ssd_dt + jnp.log(-jnp.expm1(-ssd_dt)),
        'ssd_A_log': jnp.log(unif((SSD_HEADS,), 1.0, 16.0)),
        'ssd_D': gain(SSD_HEADS),
        'ssd_norm': gain(SSD_INNER),
        'gdn_dt_bias': gdn_dt + jnp.log(-jnp.expm1(-gdn_dt)),
        'gdn_A_log': jnp.log(unif((GDN_HEADS,), 1.0, 16.0)),
        'gdn_norm': gain(GDN_HEAD_DIM),
        'w_out0': nrm((SSD_INNER + GDN_INNER, D_MODEL), (SSD_INNER + GDN_INNER) ** -0.5),
        'norm_ffn0': gain(D_MODEL),
        'ffn_w1': nrm((D_MODEL, D_FF), inv),
        'ffn_w3': nrm((D_MODEL, D_FF), inv),
        'ffn_w2': nrm((D_FF, D_MODEL), D_FF ** -0.5),
        'ada1_w': nrm((D_MODEL, 6 * D_MODEL), 0.5 * inv),
        'ada1_b': nrm((6 * D_MODEL,), 0.02),
        'norm_mix1': gain(D_MODEL),
        'w_in1': nrm((D_MODEL, IN1_WIDTH), inv),
        'fox_f_bias': FORGET_BIAS + nrm((FOX_HEADS,), 0.5),
        's5_A_re': -0.5 + nrm((S5_GROUPS, S5_STATE), 0.01),
        's5_A_im': math.pi * jnp.arange(S5_STATE, dtype=f32)[None, :] + nrm((S5_GROUPS, S5_STATE), 0.01),
        's5_log_step': unif((S5_GROUPS,), math.log(1e-3), math.log(1e-1)),
        's5_B_re': nrm((S5_GROUPS, S5_STATE, S5_CH), (2 * S5_CH) ** -0.5),
        's5_B_im': nrm((S5_GROUPS, S5_STATE, S5_CH), (2 * S5_CH) ** -0.5),
        's5_C_re': nrm((S5_GROUPS, S5_CH, S5_STATE), (2 * S5_STATE) ** -0.5),
        's5_C_im': nrm((S5_GROUPS, S5_CH, S5_STATE), (2 * S5_STATE) ** -0.5),
        's5_D': nrm((S5_GROUPS, S5_CH)),
        'glu_w': nrm((S5_INNER, S5_INNER), S5_INNER ** -0.5),
        'glu_b': nrm((S5_INNER,), 0.02),
        'w_out1': nrm((FOX_INNER + S5_INNER, D_MODEL), (FOX_INNER + S5_INNER) ** -0.5),
        'norm_ffn1': gain(D_MODEL),
        'router_w': nrm((D_MODEL, N_EXPERTS), inv),
        'moe_w1': nrm((N_EXPERTS, D_MODEL, D_FF_EXPERT), inv),
        'moe_w3': nrm((N_EXPERTS, D_MODEL, D_FF_EXPERT), inv),
        'moe_w2': nrm((N_EXPERTS, D_FF_EXPERT, D_MODEL), D_FF_EXPERT ** -0.5),
        'norm_final': gain(D_MODEL),
    }


def reference(x_prompt, x_sample, state_conv0, state_ssd, state_gdn, cache_k, cache_v, cache_logf,
              state_s5_re, state_s5_im, page_table, c_prompt, c_sample,
              ada0_w, ada0_b, norm_mix0, w_in0, conv0_w, conv0_b, ssd_dt_bias, ssd_A_log, ssd_D,
              ssd_norm, gdn_dt_bias, gdn_A_log, gdn_norm, w_out0, norm_ffn0, ffn_w1, ffn_w3, ffn_w2,
              ada1_w, ada1_b, norm_mix1, w_in1, fox_f_bias, s5_A_re, s5_A_im, s5_log_step,
              s5_B_re, s5_B_im, s5_C_re, s5_C_im, s5_D, glu_w, glu_b, w_out1, norm_ffn1,
              router_w, moe_w1, moe_w3, moe_w2, norm_final):
    f32 = jnp.float32
    pe = {'w_ada': ada0_w, 'b_ada': ada0_b, 'norm_mix': norm_mix0, 'w_in': w_in0,
          'conv_w': conv0_w, 'conv_b': conv0_b, 'ssd_dt_bias': ssd_dt_bias, 'ssd_A_log': ssd_A_log,
          'ssd_D': ssd_D, 'ssd_norm': ssd_norm, 'gdn_dt_bias': gdn_dt_bias, 'gdn_A_log': gdn_A_log,
          'gdn_norm': gdn_norm, 'w_out': w_out0, 'norm_ffn': norm_ffn0,
          'ffn_w1': ffn_w1, 'ffn_w3': ffn_w3, 'ffn_w2': ffn_w2}
    po = {'w_ada': ada1_w, 'b_ada': ada1_b, 'norm_mix': norm_mix1, 'w_in': w_in1,
          'fox_f_bias': fox_f_bias, 's5_A_re': s5_A_re, 's5_A_im': s5_A_im, 's5_log_step': s5_log_step,
          's5_B_re': s5_B_re, 's5_B_im': s5_B_im, 's5_C_re': s5_C_re, 's5_C_im': s5_C_im, 's5_D': s5_D,
          'glu_w': glu_w, 'glu_b': glu_b, 'w_out': w_out1, 'norm_ffn': norm_ffn1,
          'router_w': router_w, 'moe_w1': moe_w1, 'moe_w3': moe_w3, 'moe_w2': moe_w2}
    bp = x_prompt.shape[0]
    (y_prompt, p_conv, p_ssd, p_gdn, p_k, p_v, p_logf, p_s5_re, p_s5_im) = run_trunk(
        x_prompt, c_prompt,
        jnp.zeros((bp, CONV_K - 1, CONV_CH), x_prompt.dtype),
        jnp.zeros((bp, SSD_HEADS, SSD_HEAD_DIM, SSD_STATE), f32),
        jnp.zeros((bp, GDN_HEADS, GDN_HEAD_DIM, GDN_HEAD_DIM), f32),
        None,
        jnp.zeros((bp, S5_GROUPS, S5_STATE), f32),
        jnp.zeros((bp, S5_GROUPS, S5_STATE), f32),
        pe, po, norm_final)
    db = x_sample.shape[0]
    k_past = cache_k[page_table].reshape(db, -1, FOX_HEADS, FOX_HEAD_DIM)
    v_past = cache_v[page_table].reshape(db, -1, FOX_HEADS, FOX_HEAD_DIM)
    logf_past = cache_logf[page_table].reshape(db, -1, FOX_HEADS)
    (y_sample, s_conv, s_ssd, s_gdn, s_k, s_v, s_logf, s_s5_re, s_s5_im) = run_trunk(
        x_sample, c_sample, state_conv0, state_ssd, state_gdn, (k_past, v_past, logf_past),
        state_s5_re, state_s5_im, pe, po, norm_final)
    return (y_prompt, y_sample, p_conv, p_ssd, p_gdn, p_k, p_v, p_logf, p_s5_re, p_s5_im,
            s_conv, s_ssd, s_gdn, s_k, s_v, s_logf, s_s5_re, s_s5_im)
```

```python
import functools
import math

import jax
import jax.numpy as jnp
from jax import lax
from jax.experimental import pallas as pl
from jax.experimental.pallas import tpu as pltpu

f32 = jnp.float32
bf16 = jnp.bfloat16

D_MODEL = 1024
CONV_K = 4
CHUNK = 64
SSD_HEADS = 8
SSD_HEAD_DIM = 64
SSD_INNER = 512
SSD_GROUPS = 2
SSD_STATE = 64
GDN_HEADS = 4
GDN_HEAD_DIM = 128
GDN_INNER = 512
SSD_CONV_CH = 768
CONV_CH = 2304
FOX_HEADS = 8
FOX_HEAD_DIM = 64
FOX_INNER = 512
Q_BLOCK = 128
S5_CH = 16
S5_GROUPS = 32
S5_INNER = 512
S5_STATE = 64
D_FF = 2816
N_EXPERTS = 8
D_FF_EXPERT = 3584
EPS = 1e-6

LANES = 128
IN0_PAD = 3456
IN1_PAD = 2304
VMEM_LIMIT = 56 * 1024 * 1024


def _cparams(*sem):
    return pltpu.CompilerParams(dimension_semantics=sem, vmem_limit_bytes=VMEM_LIMIT)


def _modnorm(x, g, scale, shift):
    ms = jnp.mean(x * x, axis=-1, keepdims=True)
    y = x * lax.rsqrt(ms + EPS) * g
    return y * (1.0 + scale) + shift


def _split3(a):
    a0 = a.astype(bf16)
    r = a - a0.astype(f32)
    a1 = r.astype(bf16)
    a2 = (r - a1.astype(f32)).astype(bf16)
    return a0, a1, a2


def _dot(a, b):
    return jnp.dot(a, b, preferred_element_type=f32)


def _dot_f32(a, b):
    a0, a1, a2 = _split3(a)
    b0, b1, b2 = _split3(b)
    return (_dot(a0, b0) + _dot(a0, b1) + _dot(a1, b0)
            + _dot(a1, b1) + _dot(a0, b2) + _dot(a2, b0))


def _mod_spec(mod, n_tiles):
    n_mod, rows, d = mod.shape
    per = n_tiles // n_mod
    return pl.BlockSpec((1, rows, d), lambda i, *_: (i // per, 0, 0))


def _inproj_kernel(x_ref, g_ref, sc_ref, sh_ref, w_ref, o_ref, h_scr):
    @pl.when(pl.program_id(1) == 0)
    def _():
        h_scr[...] = _modnorm(x_ref[...], g_ref[...], sc_ref[0], sh_ref[0]).astype(bf16)

    o_ref[...] = _dot(h_scr[...], w_ref[...])


def _inproj(x, g, scale, shift, w, tm, tn):
    t, d = x.shape
    n = w.shape[1]
    n_tiles = t // tm
    return pl.pallas_call(
        _inproj_kernel,
        out_shape=jax.ShapeDtypeStruct((t, n), f32),
        grid=(n_tiles, n // tn),
        in_specs=[
            pl.BlockSpec((tm, d), lambda i, j: (i, 0)),
            pl.BlockSpec((1, d), lambda i, j: (0, 0)),
            _mod_spec(scale, n_tiles),
            _mod_spec(shift, n_tiles),
            pl.BlockSpec((d, tn), lambda i, j: (0, j)),
        ],
        out_specs=pl.BlockSpec((tm, tn), lambda i, j: (i, j)),
        scratch_shapes=[pltpu.VMEM((tm, d), bf16)],
        compiler_params=_cparams("parallel", "arbitrary"),
        name="inproj",
    )(x, g, scale, shift, w)


def _outproj_kernel(mix_ref, w_ref, x_ref, gate_ref, o_ref):
    o_ref[...] = x_ref[...] + gate_ref[0] * _dot(mix_ref[...].astype(bf16), w_ref[...])


def _outproj(mix, w, x, gate, tm):
    t, d = x.shape
    k = mix.shape[1]
    n_tiles = t // tm
    return pl.pallas_call(
        _outproj_kernel,
        out_shape=jax.ShapeDtypeStruct((t, d), f32),
        grid=(n_tiles,),
        in_specs=[
            pl.BlockSpec((tm, k), lambda i: (i, 0)),
            pl.BlockSpec((k, d), lambda i: (0, 0)),
            pl.BlockSpec((tm, d), lambda i: (i, 0)),
            _mod_spec(gate, n_tiles),
        ],
        out_specs=pl.BlockSpec((tm, d), lambda i: (i, 0)),
        compiler_params=_cparams("parallel"),
        name="outproj",
    )(mix, w, x, gate)


def _ffn_kernel(x_ref, g_ref, sc_ref, sh_ref, gate_ref, w1_ref, w3_ref, w2_ref, o_ref, h_scr):
    j = pl.program_id(1)

    @pl.when(j == 0)
    def _():
        h_scr[...] = _modnorm(x_ref[...], g_ref[...], sc_ref[0], sh_ref[0]).astype(bf16)
        o_ref[...] = jnp.zeros_like(o_ref)

    h = h_scr[...]
    a = _dot(h, w1_ref[...])
    b = _dot(h, w3_ref[...])
    act = (a * jax.nn.sigmoid(a)) * b
    o_ref[...] += _dot(act.astype(bf16), w2_ref[...])

    @pl.when(j == pl.num_programs(1) - 1)
    def _():
        o_ref[...] = x_ref[...] + gate_ref[0] * o_ref[...]


def _ffn(x, g, scale, shift, gate, w1, w3, w2, tm, tf):
    t, d = x.shape
    f = w1.shape[1]
    n_tiles = t // tm
    return pl.pallas_call(
        _ffn_kernel,
        out_shape=jax.ShapeDtypeStruct((t, d), f32),
        grid=(n_tiles, f // tf),
        in_specs=[
            pl.BlockSpec((tm, d), lambda i, j: (i, 0)),
            pl.BlockSpec((1, d), lambda i, j: (0, 0)),
            _mod_spec(scale, n_tiles),
            _mod_spec(shift, n_tiles),
            _mod_spec(gate, n_tiles),
            pl.BlockSpec((d, tf), lambda i, j: (0, j)),
            pl.BlockSpec((d, tf), lambda i, j: (0, j)),
            pl.BlockSpec((tf, d), lambda i, j: (j, 0)),
        ],
        out_specs=pl.BlockSpec((tm, d), lambda i, j: (i, 0)),
        scratch_shapes=[pltpu.VMEM((tm, d), bf16)],
        compiler_params=_cparams("parallel", "arbitrary"),
        name="ffn",
    )(x, g, scale, shift, gate, w1, w3, w2)


def _router_kernel(x_ref, g_ref, sc_ref, sh_ref, rw_ref, comb_ref, rank_ref, rankt_ref, cnt_ref):
    tm = x_ref.shape[0]
    h = _modnorm(x_ref[...], g_ref[...], sc_ref[0], sh_ref[0])
    logits = _dot_f32(h, rw_ref[...])
    lane = lax.broadcasted_iota(jnp.int32, (tm, LANES), 1)
    lg = jnp.where(lane < N_EXPERTS, logits, -jnp.inf)
    m1 = jnp.max(lg, axis=1, keepdims=True)
    i1 = jnp.min(jnp.where(lg == m1, lane, LANES), axis=1, keepdims=True)
    lg2 = jnp.where(lane == i1, -jnp.inf, lg)
    m2 = jnp.max(lg2, axis=1, keepdims=True)
    i2 = jnp.min(jnp.where(lg2 == m2, lane, LANES), axis=1, keepdims=True)
    e2 = jnp.exp(m2 - m1)
    den = 1.0 + e2
    comb_ref[...] = jnp.where(lane == i1, 1.0 / den, jnp.where(lane == i2, e2 / den, 0.0))
    sel = (lane == i1) | (lane == i2)
    ind = jnp.where(sel, 1.0, 0.0)
    row = lax.broadcasted_iota(jnp.int32, (tm, tm), 0)
    col = lax.broadcasted_iota(jnp.int32, (tm, tm), 1)
    below = jnp.where(col < row, 1.0, 0.0).astype(bf16)
    rank = jnp.where(sel, _dot(below, ind.astype(bf16)), -1.0)
    rank_ref[...] = rank
    rankt_ref[0] = rank.T[:N_EXPERTS, :]
    cnt_ref[0] = jnp.broadcast_to(jnp.sum(ind, axis=0, keepdims=True), (8, LANES))


def _router(x, g, scale, shift, rw, tm):
    t, d = x.shape
    n_tiles = t // tm
    return pl.pallas_call(
        _router_kernel,
        out_shape=(
            jax.ShapeDtypeStruct((t, LANES), f32),
            jax.ShapeDtypeStruct((t, LANES), f32),
            jax.ShapeDtypeStruct((n_tiles, N_EXPERTS, tm), f32),
            jax.ShapeDtypeStruct((n_tiles, 8, LANES), f32),
        ),
        grid=(n_tiles,),
        in_specs=[
            pl.BlockSpec((tm, d), lambda i: (i, 0)),
            pl.BlockSpec((1, d), lambda i: (0, 0)),
            _mod_spec(scale, n_tiles),
            _mod_spec(shift, n_tiles),
            pl.BlockSpec((d, LANES), lambda i: (0, 0)),
        ],
        out_specs=(
            pl.BlockSpec((tm, LANES), lambda i: (i, 0)),
            pl.BlockSpec((tm, LANES), lambda i: (i, 0)),
            pl.BlockSpec((1, N_EXPERTS, tm), lambda i: (i, 0, 0)),
            pl.BlockSpec((1, 8, LANES), lambda i: (i, 0, 0)),
        ),
        compiler_params=_cparams("parallel"),
        name="router",
    )(x, g, scale, shift, rw)


def _moe_kernel(cnt_ref, x_ref, g_ref, sc_ref, sh_ref, gate_ref, comb_ref, rank_ref, rankt_ref,
                w1_ref, w3_ref, w2_ref, o_ref, h_scr, hc_scr, y_scr, *, cap):
    i = pl.program_id(0)
    e = pl.program_id(1)
    fc = pl.program_id(2)
    last_fc = pl.num_programs(2) - 1
    tm = x_ref.shape[0]
    n_batches = (cnt_ref[i * N_EXPERTS + e] + cap - 1) // cap

    @pl.when((e == 0) & (fc == 0))
    def _():
        h_scr[...] = _modnorm(x_ref[...], g_ref[...], sc_ref[0], sh_ref[0]).astype(bf16)
        o_ref[...] = jnp.zeros_like(o_ref)

    @pl.when(fc == 0)
    def _():
        def compact(b, carry):
            r0 = pl.multiple_of(b * cap, 8)
            rowid = (lax.broadcasted_iota(jnp.int32, (cap, tm), 0) + r0).astype(f32)
            onehot = jnp.where(rankt_ref[0] == rowid, 1.0, 0.0).astype(bf16)
            hc_scr[pl.ds(r0, cap), :] = _dot(onehot, h_scr[...]).astype(bf16)
            y_scr[pl.ds(r0, cap), :] = jnp.zeros((cap, y_scr.shape[1]), f32)
            return carry
        lax.fori_loop(0, n_batches, compact, 0)

    def expert(b, carry):
        r0 = pl.multiple_of(b * cap, 8)
        hc = hc_scr[pl.ds(r0, cap), :]
        a = _dot(hc, w1_ref[0])
        g = _dot(hc, w3_ref[0])
        act = (a * jax.nn.sigmoid(a)) * g
        y_scr[pl.ds(r0, cap), :] += _dot(act.astype(bf16), w2_ref[0])
        return carry
    lax.fori_loop(0, n_batches, expert, 0)

    @pl.when(fc == last_fc)
    def _():
        lane = lax.broadcasted_iota(jnp.int32, (tm, LANES), 1)
        rank_e = jnp.sum(jnp.where(lane == e, rank_ref[...], 0.0), axis=1, keepdims=True)
        gate_e = jnp.sum(jnp.where(lane == e, comb_ref[...], 0.0), axis=1, keepdims=True)

        def expand(b, carry):
            r0 = pl.multiple_of(b * cap, 8)
            colid = (lax.broadcasted_iota(jnp.int32, (tm, cap), 1) + r0).astype(f32)
            onehot = jnp.where(rank_e == colid, 1.0, 0.0).astype(bf16)
            o_ref[...] += gate_e * _dot(onehot, y_scr[pl.ds(r0, cap), :].astype(bf16))
            return carry
        lax.fori_loop(0, n_batches, expand, 0)

    @pl.when((e == N_EXPERTS - 1) & (fc == last_fc))
    def _():
        o_ref[...] = x_ref[...] + gate_ref[0] * o_ref[...]


def _moe(x, g, scale, shift, gate, rw, w1, w3, w2, tm, tf, cap):
    t, d = x.shape
    f = w1.shape[2]
    n_tiles = t // tm
    cap = min(cap, tm)
    n_cap = -(-tm // cap)
    comb, rank, rankt, cnt = _router(x, g, scale, shift, rw, tm)
    counts = cnt[:, 0, :N_EXPERTS].astype(jnp.int32).reshape(-1)
    rankt = rankt.reshape(n_tiles * N_EXPERTS, 1, tm)
    grid_spec = pltpu.PrefetchScalarGridSpec(
        num_scalar_prefetch=1,
        grid=(n_tiles, N_EXPERTS, f // tf),
        in_specs=[
            pl.BlockSpec((tm, d), lambda i, e, c, cnt: (i, 0)),
            pl.BlockSpec((1, d), lambda i, e, c, cnt: (0, 0)),
            _mod_spec(scale, n_tiles),
            _mod_spec(shift, n_tiles),
            _mod_spec(gate, n_tiles),
            pl.BlockSpec((tm, LANES), lambda i, e, c, cnt: (i, 0)),
            pl.BlockSpec((tm, LANES), lambda i, e, c, cnt: (i, 0)),
            pl.BlockSpec((1, 1, tm), lambda i, e, c, cnt: (i * N_EXPERTS + e, 0, 0)),
            pl.BlockSpec((1, d, tf), lambda i, e, c, cnt: (e, 0, c)),
            pl.BlockSpec((1, d, tf), lambda i, e, c, cnt: (e, 0, c)),
            pl.BlockSpec((1, tf, d), lambda i, e, c, cnt: (e, c, 0)),
        ],
        out_specs=pl.BlockSpec((tm, d), lambda i, e, c, cnt: (i, 0)),
        scratch_shapes=[
            pltpu.VMEM((tm, d), bf16),
            pltpu.VMEM((n_cap * cap, d), bf16),
            pltpu.VMEM((n_cap * cap, d), f32),
        ],
    )
    return pl.pallas_call(
        functools.partial(_moe_kernel, cap=cap),
        out_shape=jax.ShapeDtypeStruct((t, d), f32),
        grid_spec=grid_spec,
        compiler_params=_cparams("parallel", "arbitrary", "arbitrary"),
        name="moe",
    )(counts, x, g, scale, shift, gate, comb, rank, rankt, w1, w3, w2)


def _rmsnorm(x, w):
    xf = x.astype(f32)
    y = xf * lax.rsqrt(jnp.mean(xf * xf, axis=-1, keepdims=True) + EPS)
    return (y * w.astype(f32)).astype(x.dtype)


def _l2norm(x):
    xf = x.astype(f32)
    return xf * lax.rsqrt(jnp.sum(xf * xf, axis=-1, keepdims=True) + EPS)


def _causal_conv(u, buf, w, bias):
    l = u.shape[1]
    upad = jnp.concatenate([buf.astype(u.dtype), u], axis=1)
    out = bias + sum(w[j] * upad[:, j:j + l] for j in range(CONV_K))
    return jax.nn.silu(out), upad[:, upad.shape[1] - (CONV_K - 1):]


def _ssd_chunked(x, dt, A, Bm, Cm, h0):
    b, l, h, p = x.shape
    n = Bm.shape[-1]
    q = math.gcd(l, CHUNK)
    c = l // q
    xdt = (x.astype(f32) * dt[..., None]).reshape(b, c, q, h, p)
    acum = jnp.cumsum((dt * A).reshape(b, c, q, h), axis=2)
    Bc = Bm.astype(f32).reshape(b, c, q, h, n)
    Cc = Cm.astype(f32).reshape(b, c, q, h, n)
    causal = jnp.tril(jnp.ones((q, q), bool))[None, None, :, :, None]
    seg = acum[:, :, :, None, :] - acum[:, :, None, :, :]
    decay = jnp.where(causal, jnp.exp(jnp.where(causal, seg, 0.0)), 0.0)
    scores = jnp.einsum('bcihn,bcjhn->bcijh', Cc, Bc) * decay
    y_diag = jnp.einsum('bcijh,bcjhp->bcihp', scores, xdt)
    to_end = jnp.exp(acum[:, :, -1:, :] - acum)
    chunk_states = jnp.einsum('bcjhn,bcjhp->bchpn', Bc * to_end[..., None], xdt)
    chunk_decay = jnp.exp(acum[:, :, -1, :])

    def step(state, inp):
        st, dec = inp
        return state * dec[..., None, None] + st, state

    h_final, h_enter = lax.scan(step, h0.astype(f32),
                                (jnp.moveaxis(chunk_states, 1, 0), jnp.moveaxis(chunk_decay, 1, 0)))
    h_enter = jnp.moveaxis(h_enter, 0, 1)
    y_off = jnp.einsum('bcihn,bchpn->bcihp', Cc * jnp.exp(acum)[..., None], h_enter)
    return (y_diag + y_off).reshape(b, l, h, p), h_final


def _gdn_chunked(q, k, v, g, beta, s0):
    b, l, h, dk = q.shape
    dv = v.shape[-1]
    Q = math.gcd(l, CHUNK)
    c = l // Q

    def blk(t):
        return t.astype(f32).reshape(b, c, Q, h, t.shape[-1]).transpose(0, 1, 3, 2, 4)

    qc = blk(q) * dk ** -0.5
    kc = blk(k)
    vc = blk(v)
    gc = jnp.cumsum(g.astype(f32).reshape(b, c, Q, h).transpose(0, 1, 3, 2), axis=-1)
    bc = beta.astype(f32).reshape(b, c, Q, h).transpose(0, 1, 3, 2)
    incl = jnp.tril(jnp.ones((Q, Q), bool))
    strict = jnp.tril(jnp.ones((Q, Q), bool), k=-1)
    diff = gc[..., :, None] - gc[..., None, :]
    decay = jnp.where(incl, jnp.exp(jnp.where(incl, diff, 0.0)), 0.0)
    kb = kc * bc[..., None]
    vb = vc * bc[..., None]
    m = jnp.where(strict, jnp.einsum('bchid,bchjd->bchij', kb, kc) * decay, 0.0)
    t_sys = jnp.eye(Q, dtype=f32) + m
    rhs = jnp.concatenate([vb, kb * jnp.exp(gc)[..., None]], axis=-1)
    sol = lax.linalg.triangular_solve(t_sys, rhs, left_side=True, lower=True)
    u = sol[..., :dv]
    w = sol[..., dv:]
    attn = jnp.einsum('bchid,bchjd->bchij', qc, kc) * decay
    qd = qc * jnp.exp(gc)[..., None]
    ke = kc * jnp.exp(gc[..., -1:] - gc)[..., None]
    dl = jnp.exp(gc[..., -1])

    def step(S, inp):
        u_c, w_c, qd_c, ke_c, a_c, dl_c = inp
        v_new = u_c - jnp.einsum('bhid,bhde->bhie', w_c, S)
        o = jnp.einsum('bhid,bhde->bhie', qd_c, S) + jnp.einsum('bhij,bhje->bhie', a_c, v_new)
        S = S * dl_c[..., None, None] + jnp.einsum('bhid,bhie->bhde', ke_c, v_new)
        return S, o

    xs = tuple(jnp.moveaxis(t, 1, 0) for t in (u, w, qd, ke, attn, dl))
    s_final, o = lax.scan(step, s0.astype(f32), xs)
    return jnp.transpose(o, (1, 0, 3, 2, 4)).reshape(b, l, h, dv), s_final


def _fox_attention(q, k, v, logf, past):
    b, l, h, d = q.shape
    if past is None:
        k_all, v_all, logf_all = k.astype(f32), v.astype(f32), logf
    else:
        k_past, v_past, logf_past = past
        k_all = jnp.concatenate([k_past.astype(f32), k.astype(f32)], axis=1)
        v_all = jnp.concatenate([v_past.astype(f32), v.astype(f32)], axis=1)
        logf_all = jnp.concatenate([logf_past.astype(f32), logf], axis=1)
    n_keys = k_all.shape[1]
    offset = n_keys - l
    F = jnp.cumsum(logf_all, axis=1)
    Fk = jnp.moveaxis(F, 1, 2)
    Fq = F[:, offset:]
    kpos = jnp.arange(n_keys)
    qb = math.gcd(l, Q_BLOCK)
    nb = l // qb
    scale = d ** -0.5

    def block(args):
        q_i, fq_i, qpos_i = args
        s = jnp.einsum('bqhd,bkhd->bhqk', q_i.astype(f32), k_all) * scale
        s = s + jnp.moveaxis(fq_i, 1, 2)[..., None] - Fk[:, :, None, :]
        s = jnp.where(qpos_i[:, None] >= kpos[None, :], s, -jnp.inf)
        pr = jax.nn.softmax(s, axis=-1)
        return jnp.einsum('bhqk,bkhd->bqhd', pr, v_all)

    q_blocks = jnp.moveaxis(q.reshape(b, nb, qb, h, d), 1, 0)
    fq_blocks = jnp.moveaxis(Fq.reshape(b, nb, qb, h), 1, 0)
    qpos_blocks = (offset + jnp.arange(l)).reshape(nb, qb)
    o = lax.map(block, (q_blocks, fq_blocks, qpos_blocks))
    return jnp.moveaxis(o, 0, 1).reshape(b, l, h, d)


def _s5_scan(u, A_re, A_im, log_step, B_re, B_im, C_re, C_im, Dd, x0_re, x0_im):
    b, l, _ = u.shape
    uu = u.astype(f32).reshape(b, l, S5_GROUPS, S5_CH)
    ar = A_re.astype(f32)
    ai = A_im.astype(f32)
    step = jnp.exp(log_step.astype(f32))[:, None]
    mag = jnp.exp(ar * step)
    lb_re = mag * jnp.cos(ai * step)
    lb_im = mag * jnp.sin(ai * step)
    den = ar * ar + ai * ai
    nr = lb_re - 1.0
    cr = (nr * ar + lb_im * ai) / den
    ci = (lb_im * ar - nr * ai) / den
    bb_re = cr[..., None] * B_re - ci[..., None] * B_im
    bb_im = cr[..., None] * B_im + ci[..., None] * B_re
    bu_re = jnp.einsum('gph,blgh->blgp', bb_re, uu)
    bu_im = jnp.einsum('gph,blgh->blgp', bb_im, uu)
    shp = bu_re.shape
    a_re = jnp.broadcast_to(lb_re, shp)
    a_im = jnp.broadcast_to(lb_im, shp)

    def combine(e1, e2):
        a1r, a1i, b1r, b1i = e1
        a2r, a2i, b2r, b2i = e2
        return (a2r * a1r - a2i * a1i, a2r * a1i + a2i * a1r,
                a2r * b1r - a2i * b1i + b2r, a2r * b1i + a2i * b1r + b2i)

    pr, pi, xr, xi = lax.associative_scan(combine, (a_re, a_im, bu_re, bu_im), axis=1)
    x0r = x0_re.astype(f32)[:, None]
    x0i = x0_im.astype(f32)[:, None]
    xr = xr + pr * x0r - pi * x0i
    xi = xi + pr * x0i + pi * x0r
    y = (jnp.einsum('ghp,blgp->blgh', C_re, xr) - jnp.einsum('ghp,blgp->blgh', C_im, xi)
         + Dd * uu)
    return y.reshape(b, l, S5_INNER), xr[:, -1], xi[:, -1]


def _mods(c, w_ada, b_ada, l, tm):
    mod = jax.nn.silu(c) @ w_ada + b_ada
    parts = jnp.split(mod, 6, axis=-1)
    if l % tm == 0:
        return [p[:, None, :] for p in parts]
    b = c.shape[0]
    return [jnp.repeat(p, l, axis=0).reshape((b * l) // tm, tm, D_MODEL) for p in parts]


def _run_trunk(x, c, conv_buf, ssd_h0, gdn_s0, past, s5_re0, s5_im0, p, tm):
    b, l, d = x.shape
    t = b * l
    x2 = x.reshape(t, d)

    sh1, sc1, g1, sh2, sc2, g2 = _mods(c, p['ada0_w'], p['ada0_b'], l, tm)
    proj = _inproj(x2, p['norm_mix0'], sc1, sh1, p['w_in0'], tm, 1152).reshape(b, l, IN0_PAD)
    conv_out, conv_new = _causal_conv(proj[..., :CONV_CH], conv_buf, p['conv0_w'], p['conv0_b'])
    xs, Bm, Cm, qg, kg, vg = jnp.split(
        conv_out, [SSD_INNER, SSD_INNER + SSD_GROUPS * SSD_STATE, SSD_CONV_CH,
                   SSD_CONV_CH + GDN_INNER, SSD_CONV_CH + 2 * GDN_INNER], axis=-1)
    z_ssd = proj[..., CONV_CH:CONV_CH + SSD_INNER]
    z_gdn = proj[..., CONV_CH + SSD_INNER:CONV_CH + SSD_INNER + GDN_INNER]
    small = proj[..., CONV_CH + SSD_INNER + GDN_INNER:]
    dt_raw = small[..., :SSD_HEADS]
    a_gdn = small[..., SSD_HEADS:SSD_HEADS + GDN_HEADS]
    b_gdn = small[..., SSD_HEADS + GDN_HEADS:SSD_HEADS + 2 * GDN_HEADS]
    xs = xs.reshape(b, l, SSD_HEADS, SSD_HEAD_DIM)
    rep = SSD_HEADS // SSD_GROUPS
    Bm = jnp.repeat(Bm.reshape(b, l, SSD_GROUPS, SSD_STATE), rep, axis=2)
    Cm = jnp.repeat(Cm.reshape(b, l, SSD_GROUPS, SSD_STATE), rep, axis=2)
    dt = jax.nn.softplus((dt_raw + p['ssd_dt_bias']).astype(f32))
    A = -jnp.exp(p['ssd_A_log'].astype(f32))
    y, ssd_new = _ssd_chunked(xs, dt, A, Bm, Cm, ssd_h0)
    y = (y + p['ssd_D'].astype(f32)[:, None] * xs.astype(f32)).reshape(b, l, SSD_INNER)
    y_ssd = _rmsnorm(y * jax.nn.silu(z_ssd.astype(f32)), p['ssd_norm'])
    q = _l2norm(qg.reshape(b, l, GDN_HEADS, GDN_HEAD_DIM))
    k = _l2norm(kg.reshape(b, l, GDN_HEADS, GDN_HEAD_DIM))
    v = vg.reshape(b, l, GDN_HEADS, GDN_HEAD_DIM)
    g_log = -jnp.exp(p['gdn_A_log'].astype(f32)) * jax.nn.softplus((a_gdn + p['gdn_dt_bias']).astype(f32))
    beta = jax.nn.sigmoid(b_gdn.astype(f32))
    o, gdn_new = _gdn_chunked(q, k, v, g_log, beta, gdn_s0)
    o = _rmsnorm(o, p['gdn_norm']) * jax.nn.silu(z_gdn.astype(f32).reshape(b, l, GDN_HEADS, GDN_HEAD_DIM))
    mix = jnp.concatenate([y_ssd, o.reshape(b, l, GDN_INNER)], axis=-1).reshape(t, d)
    x2 = _outproj(mix, p['w_out0'], x2, g1, tm)
    x2 = _ffn(x2, p['norm_ffn0'], sc2, sh2, g2, p['ffn_w1'], p['ffn_w3'], p['ffn_w2'], min(tm, 512), 1408)

    sh1, sc1, g1, sh2, sc2, g2 = _mods(c, p['ada1_w'], p['ada1_b'], l, tm)
    proj = _inproj(x2, p['norm_mix1'], sc1, sh1, p['w_in1'], tm, 1152).reshape(b, l, IN1_PAD)
    q = proj[..., :FOX_INNER].reshape(b, l, FOX_HEADS, FOX_HEAD_DIM)
    k_new = proj[..., FOX_INNER:2 * FOX_INNER].reshape(b, l, FOX_HEADS, FOX_HEAD_DIM)
    v_new = proj[..., 2 * FOX_INNER:3 * FOX_INNER].reshape(b, l, FOX_HEADS, FOX_HEAD_DIM)
    u = proj[..., 3 * FOX_INNER:3 * FOX_INNER + S5_INNER]
    f_raw = proj[..., 3 * FOX_INNER + S5_INNER:3 * FOX_INNER + S5_INNER + FOX_HEADS]
    logf = jax.nn.log_sigmoid((f_raw + p['fox_f_bias']).astype(f32))
    o_fox = _fox_attention(q, k_new, v_new, logf, past).reshape(b, l, FOX_INNER)
    y_s5, s5_re, s5_im = _s5_scan(u, p['s5_A_re'], p['s5_A_im'], p['s5_log_step'], p['s5_B_re'],
                                  p['s5_B_im'], p['s5_C_re'], p['s5_C_im'], p['s5_D'], s5_re0, s5_im0)
    hs = jax.nn.gelu(y_s5)
    o_s5 = hs * jax.nn.sigmoid(hs @ p['glu_w'].astype(f32) + p['glu_b'].astype(f32))
    mix = jnp.concatenate([o_fox, o_s5], axis=-1).reshape(t, d)
    x2 = _outproj(mix, p['w_out1'], x2, g1, tm)
    x2 = _moe(x2, p['norm_ffn1'], sc2, sh2, g2, p['router_w'], p['moe_w1'], p['moe_w3'], p['moe_w2'],
              tm, 896, 320)
    y_out = _rmsnorm(x2, p['norm_final']).reshape(b, l, d)
    return y_out, conv_new, ssd_new, gdn_new, k_new, v_new, logf, s5_re, s5_im


def kernel(x_prompt, x_sample, state_conv0, state_ssd, state_gdn, cache_k, cache_v, cache_logf,
           state_s5_re, state_s5_im, page_table, c_prompt, c_sample,
           ada0_w, ada0_b, norm_mix0, w_in0, conv0_w, conv0_b, ssd_dt_bias, ssd_A_log, ssd_D,
           ssd_norm, gdn_dt_bias, gdn_A_log, gdn_norm, w_out0, norm_ffn0, ffn_w1, ffn_w3, ffn_w2,
           ada1_w, ada1_b, norm_mix1, w_in1, fox_f_bias, s5_A_re, s5_A_im, s5_log_step,
           s5_B_re, s5_B_im, s5_C_re, s5_C_im, s5_D, glu_w, glu_b, w_out1, norm_ffn1,
           router_w, moe_w1, moe_w3, moe_w2, norm_final):
    d = D_MODEL
    c0 = CONV_CH
    w_in0p = jnp.concatenate([
        w_in0[:, :c0],
        w_in0[:, c0:c0 + SSD_INNER],
        w_in0[:, c0 + SSD_INNER + SSD_HEADS:c0 + SSD_INNER + SSD_HEADS + GDN_INNER],
        w_in0[:, c0 + SSD_INNER:c0 + SSD_INNER + SSD_HEADS],
        w_in0[:, c0 + SSD_INNER + SSD_HEADS + GDN_INNER:],
        jnp.zeros((d, IN0_PAD - w_in0.shape[1]), f32)], axis=1).astype(bf16)
    f0 = 3 * FOX_INNER
    w_in1p = jnp.concatenate([
        w_in1[:, :f0],
        w_in1[:, f0 + FOX_HEADS:],
        w_in1[:, f0:f0 + FOX_HEADS],
        jnp.zeros((d, IN1_PAD - w_in1.shape[1]), f32)], axis=1).astype(bf16)
    router_wp = jnp.concatenate([router_w.astype(f32), jnp.zeros((d, LANES - N_EXPERTS), f32)], axis=1)
    p = {
        'ada0_w': ada0_w, 'ada0_b': ada0_b, 'norm_mix0': norm_mix0.reshape(1, d), 'w_in0': w_in0p,
        'conv0_w': conv0_w, 'conv0_b': conv0_b, 'ssd_dt_bias': ssd_dt_bias, 'ssd_A_log': ssd_A_log,
        'ssd_D': ssd_D, 'ssd_norm': ssd_norm, 'gdn_dt_bias': gdn_dt_bias, 'gdn_A_log': gdn_A_log,
        'gdn_norm': gdn_norm, 'w_out0': w_out0.astype(bf16), 'norm_ffn0': norm_ffn0.reshape(1, d),
        'ffn_w1': ffn_w1.astype(bf16), 'ffn_w3': ffn_w3.astype(bf16), 'ffn_w2': ffn_w2.astype(bf16),
        'ada1_w': ada1_w, 'ada1_b': ada1_b, 'norm_mix1': norm_mix1.reshape(1, d), 'w_in1': w_in1p,
        'fox_f_bias': fox_f_bias, 's5_A_re': s5_A_re, 's5_A_im': s5_A_im, 's5_log_step': s5_log_step,
        's5_B_re': s5_B_re, 's5_B_im': s5_B_im, 's5_C_re': s5_C_re, 's5_C_im': s5_C_im, 's5_D': s5_D,
        'glu_w': glu_w, 'glu_b': glu_b, 'w_out1': w_out1.astype(bf16), 'norm_ffn1': norm_ffn1.reshape(1, d),
        'router_w': router_wp, 'moe_w1': moe_w1.astype(bf16), 'moe_w3': moe_w3.astype(bf16),
        'moe_w2': moe_w2.astype(bf16), 'norm_final': norm_final,
    }
    bp = x_prompt.shape[0]
    outs_p = _run_trunk(
        x_prompt, c_prompt,
        jnp.zeros((bp, CONV_K - 1, CONV_CH), x_prompt.dtype),
        jnp.zeros((bp, SSD_HEADS, SSD_HEAD_DIM, SSD_STATE), f32),
        jnp.zeros((bp, GDN_HEADS, GDN_HEAD_DIM, GDN_HEAD_DIM), f32),
        None,
        jnp.zeros((bp, S5_GROUPS, S5_STATE), f32),
        jnp.zeros((bp, S5_GROUPS, S5_STATE), f32),
        p, 1024)
    db = x_sample.shape[0]
    k_past = cache_k[page_table].reshape(db, -1, FOX_HEADS, FOX_HEAD_DIM)
    v_past = cache_v[page_table].reshape(db, -1, FOX_HEADS, FOX_HEAD_DIM)
    logf_past = cache_logf[page_table].reshape(db, -1, FOX_HEADS)
    outs_s = _run_trunk(
        x_sample, c_sample, state_conv0, state_ssd, state_gdn, (k_past, v_past, logf_past),
        state_s5_re, state_s5_im, p, 256)
    return (outs_p[0], outs_s[0]) + tuple(outs_p[1:]) + tuple(outs_s[1:])
```

```python
import functools
import math

import jax
import jax.numpy as jnp
from jax import lax
from jax.experimental import pallas as pl
from jax.experimental.pallas import tpu as pltpu

f32 = jnp.float32
bf16 = jnp.bfloat16

D_MODEL = 1024
CONV_K = 4
CHUNK = 64
SSD_HEADS = 8
SSD_HEAD_DIM = 64
SSD_INNER = 512
SSD_GROUPS = 2
SSD_STATE = 64
GDN_HEADS = 4
GDN_HEAD_DIM = 128
GDN_INNER = 512
SSD_CONV_CH = 768
CONV_CH = 2304
FOX_HEADS = 8
FOX_HEAD_DIM = 64
FOX_INNER = 512
Q_BLOCK = 128
S5_CH = 16
S5_GROUPS = 32
S5_INNER = 512
S5_STATE = 64
D_FF = 2816
N_EXPERTS = 8
D_FF_EXPERT = 3584
EPS = 1e-6

LANES = 128
IN0_PAD = 3456
IN1_PAD = 2304
VMEM_LIMIT = 56 * 1024 * 1024


def _cparams(*sem):
    return pltpu.CompilerParams(dimension_semantics=sem, vmem_limit_bytes=VMEM_LIMIT)


def _modnorm(x, g, scale, shift):
    ms = jnp.mean(x * x, axis=-1, keepdims=True)
    y = x * lax.rsqrt(ms + EPS) * g
    return y * (1.0 + scale) + shift


def _split3(a):
    a0 = a.astype(bf16)
    r = a - a0.astype(f32)
    a1 = r.astype(bf16)
    a2 = (r - a1.astype(f32)).astype(bf16)
    return a0, a1, a2


def _dot(a, b):
    return jnp.dot(a, b, preferred_element_type=f32)


def _dot_f32(a, b):
    a0, a1, a2 = _split3(a)
    b0, b1, b2 = _split3(b)
    return (_dot(a0, b0) + _dot(a0, b1) + _dot(a1, b0)
            + _dot(a1, b1) + _dot(a0, b2) + _dot(a2, b0))


def _mod_spec(mod, n_tiles):
    n_mod, rows, d = mod.shape
    per = n_tiles // n_mod
    return pl.BlockSpec((1, rows, d), lambda i, *_: (i // per, 0, 0))


def _inproj_kernel(x_ref, g_ref, sc_ref, sh_ref, w_ref, o_ref, h_scr):
    @pl.when(pl.program_id(1) == 0)
    def _():
        h_scr[...] = _modnorm(x_ref[...], g_ref[...], sc_ref[0], sh_ref[0]).astype(bf16)

    o_ref[...] = _dot(h_scr[...], w_ref[...])


def _inproj(x, g, scale, shift, w, tm, tn):
    t, d = x.shape
    n = w.shape[1]
    n_tiles = t // tm
    return pl.pallas_call(
        _inproj_kernel,
        out_shape=jax.ShapeDtypeStruct((t, n), f32),
        grid=(n_tiles, n // tn),
        in_specs=[
            pl.BlockSpec((tm, d), lambda i, j: (i, 0)),
            pl.BlockSpec((1, d), lambda i, j: (0, 0)),
            _mod_spec(scale, n_tiles),
            _mod_spec(shift, n_tiles),
            pl.BlockSpec((d, tn), lambda i, j: (0, j)),
        ],
        out_specs=pl.BlockSpec((tm, tn), lambda i, j: (i, j)),
        scratch_shapes=[pltpu.VMEM((tm, d), bf16)],
        compiler_params=_cparams("parallel", "arbitrary"),
        name="inproj",
    )(x, g, scale, shift, w)


def _outproj_kernel(*refs, n_parts):
    mix_refs = refs[:n_parts]
    w_refs = refs[n_parts:2 * n_parts]
    x_ref, gate_ref, o_ref = refs[2 * n_parts:]
    acc = None
    for m_ref, w_ref in zip(mix_refs, w_refs):
        if len(m_ref.shape) == 3:
            terms = [_dot(m_ref[q].astype(bf16), w_ref[q]) for q in range(m_ref.shape[0])]
        else:
            terms = [_dot(m_ref[...].astype(bf16), w_ref[...])]
        for term in terms:
            acc = term if acc is None else acc + term
    o_ref[...] = x_ref[...] + gate_ref[0] * acc


def _outproj(parts, weights, x, gate, tm):
    t, d = x.shape
    n_tiles = t // tm
    in_specs = []
    for a in parts:
        if a.ndim == 3:
            in_specs.append(pl.BlockSpec((a.shape[0], tm, a.shape[2]), lambda i: (0, i, 0)))
        else:
            in_specs.append(pl.BlockSpec((tm, a.shape[1]), lambda i: (i, 0)))
    for w in weights:
        in_specs.append(pl.BlockSpec(w.shape, (lambda i: (0, 0, 0)) if w.ndim == 3 else (lambda i: (0, 0))))
    in_specs += [pl.BlockSpec((tm, d), lambda i: (i, 0)), _mod_spec(gate, n_tiles)]
    return pl.pallas_call(
        functools.partial(_outproj_kernel, n_parts=len(parts)),
        out_shape=jax.ShapeDtypeStruct((t, d), f32),
        grid=(n_tiles,),
        in_specs=in_specs,
        out_specs=pl.BlockSpec((tm, d), lambda i: (i, 0)),
        compiler_params=_cparams("parallel"),
        name="outproj",
    )(*parts, *weights, x, gate)


def _ffn_kernel(x_ref, g_ref, sc_ref, sh_ref, gate_ref, w1_ref, w3_ref, w2_ref, o_ref, h_scr):
    j = pl.program_id(1)

    @pl.when(j == 0)
    def _():
        h_scr[...] = _modnorm(x_ref[...], g_ref[...], sc_ref[0], sh_ref[0]).astype(bf16)
        o_ref[...] = jnp.zeros_like(o_ref)

    h = h_scr[...]
    a = _dot(h, w1_ref[...])
    b = _dot(h, w3_ref[...])
    act = (a * jax.nn.sigmoid(a)) * b
    o_ref[...] += _dot(act.astype(bf16), w2_ref[...])

    @pl.when(j == pl.num_programs(1) - 1)
    def _():
        o_ref[...] = x_ref[...] + gate_ref[0] * o_ref[...]


def _ffn(x, g, scale, shift, gate, w1, w3, w2, tm, tf):
    t, d = x.shape
    f = w1.shape[1]
    n_tiles = t // tm
    return pl.pallas_call(
        _ffn_kernel,
        out_shape=jax.ShapeDtypeStruct((t, d), f32),
        grid=(n_tiles, f // tf),
        in_specs=[
            pl.BlockSpec((tm, d), lambda i, j: (i, 0)),
            pl.BlockSpec((1, d), lambda i, j: (0, 0)),
            _mod_spec(scale, n_tiles),
            _mod_spec(shift, n_tiles),
            _mod_spec(gate, n_tiles),
            pl.BlockSpec((d, tf), lambda i, j: (0, j)),
            pl.BlockSpec((d, tf), lambda i, j: (0, j)),
            pl.BlockSpec((tf, d), lambda i, j: (j, 0)),
        ],
        out_specs=pl.BlockSpec((tm, d), lambda i, j: (i, 0)),
        scratch_shapes=[pltpu.VMEM((tm, d), bf16)],
        compiler_params=_cparams("parallel", "arbitrary"),
        name="ffn",
    )(x, g, scale, shift, gate, w1, w3, w2)


def _router_kernel(x_ref, g_ref, sc_ref, sh_ref, rw_ref, comb_ref, rank_ref, rankt_ref, cnt_ref):
    tm = x_ref.shape[0]
    h = _modnorm(x_ref[...], g_ref[...], sc_ref[0], sh_ref[0])
    logits = _dot_f32(h, rw_ref[...])
    lane = lax.broadcasted_iota(jnp.int32, (tm, LANES), 1)
    lg = jnp.where(lane < N_EXPERTS, logits, -jnp.inf)
    m1 = jnp.max(lg, axis=1, keepdims=True)
    i1 = jnp.min(jnp.where(lg == m1, lane, LANES), axis=1, keepdims=True)
    lg2 = jnp.where(lane == i1, -jnp.inf, lg)
    m2 = jnp.max(lg2, axis=1, keepdims=True)
    i2 = jnp.min(jnp.where(lg2 == m2, lane, LANES), axis=1, keepdims=True)
    e2 = jnp.exp(m2 - m1)
    den = 1.0 + e2
    comb_ref[...] = jnp.where(lane == i1, 1.0 / den, jnp.where(lane == i2, e2 / den, 0.0))
    sel = (lane == i1) | (lane == i2)
    ind = jnp.where(sel, 1.0, 0.0)
    row = lax.broadcasted_iota(jnp.int32, (tm, tm), 0)
    col = lax.broadcasted_iota(jnp.int32, (tm, tm), 1)
    below = jnp.where(col < row, 1.0, 0.0).astype(bf16)
    rank = jnp.where(sel, _dot(below, ind.astype(bf16)), -1.0)
    rank_ref[...] = rank
    rankt_ref[0] = rank.T[:N_EXPERTS, :]
    cnt_ref[0] = jnp.broadcast_to(jnp.sum(ind, axis=0, keepdims=True), (8, LANES))


def _router(x, g, scale, shift, rw, tm):
    t, d = x.shape
    n_tiles = t // tm
    return pl.pallas_call(
        _router_kernel,
        out_shape=(
            jax.ShapeDtypeStruct((t, LANES), f32),
            jax.ShapeDtypeStruct((t, LANES), f32),
            jax.ShapeDtypeStruct((n_tiles, N_EXPERTS, tm), f32),
            jax.ShapeDtypeStruct((n_tiles, 8, LANES), f32),
        ),
        grid=(n_tiles,),
        in_specs=[
            pl.BlockSpec((tm, d), lambda i: (i, 0)),
            pl.BlockSpec((1, d), lambda i: (0, 0)),
            _mod_spec(scale, n_tiles),
            _mod_spec(shift, n_tiles),
            pl.BlockSpec((d, LANES), lambda i: (0, 0)),
        ],
        out_specs=(
            pl.BlockSpec((tm, LANES), lambda i: (i, 0)),
            pl.BlockSpec((tm, LANES), lambda i: (i, 0)),
            pl.BlockSpec((1, N_EXPERTS, tm), lambda i: (i, 0, 0)),
            pl.BlockSpec((1, 8, LANES), lambda i: (i, 0, 0)),
        ),
        compiler_params=_cparams("parallel"),
        name="router",
    )(x, g, scale, shift, rw)


def _moe_kernel(cnt_ref, x_ref, g_ref, sc_ref, sh_ref, gate_ref, comb_ref, rank_ref, rankt_ref,
                w1_ref, w3_ref, w2_ref, o_ref, h_scr, hc_scr, y_scr, *, cap):
    i = pl.program_id(0)
    e = pl.program_id(1)
    fc = pl.program_id(2)
    last_fc = pl.num_programs(2) - 1
    tm = x_ref.shape[0]
    n_batches = (cnt_ref[i * N_EXPERTS + e] + cap - 1) // cap

    @pl.when((e == 0) & (fc == 0))
    def _():
        h_scr[...] = _modnorm(x_ref[...], g_ref[...], sc_ref[0], sh_ref[0]).astype(bf16)
        o_ref[...] = jnp.zeros_like(o_ref)

    @pl.when(fc == 0)
    def _():
        def compact(b, carry):
            r0 = pl.multiple_of(b * cap, 8)
            rowid = (lax.broadcasted_iota(jnp.int32, (cap, tm), 0) + r0).astype(f32)
            onehot = jnp.where(rankt_ref[0] == rowid, 1.0, 0.0).astype(bf16)
            hc_scr[pl.ds(r0, cap), :] = _dot(onehot, h_scr[...]).astype(bf16)
            y_scr[pl.ds(r0, cap), :] = jnp.zeros((cap, y_scr.shape[1]), f32)
            return carry
        lax.fori_loop(0, n_batches, compact, 0)

    def expert(b, carry):
        r0 = pl.multiple_of(b * cap, 8)
        hc = hc_scr[pl.ds(r0, cap), :]
        a = _dot(hc, w1_ref[0])
        g = _dot(hc, w3_ref[0])
        act = (a * jax.nn.sigmoid(a)) * g
        y_scr[pl.ds(r0, cap), :] += _dot(act.astype(bf16), w2_ref[0])
        return carry
    lax.fori_loop(0, n_batches, expert, 0)

    @pl.when(fc == last_fc)
    def _():
        lane = lax.broadcasted_iota(jnp.int32, (tm, LANES), 1)
        rank_e = jnp.sum(jnp.where(lane == e, rank_ref[...], 0.0), axis=1, keepdims=True)
        gate_e = jnp.sum(jnp.where(lane == e, comb_ref[...], 0.0), axis=1, keepdims=True)

        def expand(b, carry):
            r0 = pl.multiple_of(b * cap, 8)
            colid = (lax.broadcasted_iota(jnp.int32, (tm, cap), 1) + r0).astype(f32)
            onehot = jnp.where(rank_e == colid, 1.0, 0.0).astype(bf16)
            o_ref[...] += gate_e * _dot(onehot, y_scr[pl.ds(r0, cap), :].astype(bf16))
            return carry
        lax.fori_loop(0, n_batches, expand, 0)

    @pl.when((e == N_EXPERTS - 1) & (fc == last_fc))
    def _():
        o_ref[...] = x_ref[...] + gate_ref[0] * o_ref[...]


def _moe(x, g, scale, shift, gate, rw, w1, w3, w2, tm, tf, cap):
    t, d = x.shape
    f = w1.shape[2]
    n_tiles = t // tm
    cap = min(cap, tm)
    n_cap = -(-tm // cap)
    comb, rank, rankt, cnt = _router(x, g, scale, shift, rw, tm)
    counts = cnt[:, 0, :N_EXPERTS].astype(jnp.int32).reshape(-1)
    rankt = rankt.reshape(n_tiles * N_EXPERTS, 1, tm)
    grid_spec = pltpu.PrefetchScalarGridSpec(
        num_scalar_prefetch=1,
        grid=(n_tiles, N_EXPERTS, f // tf),
        in_specs=[
            pl.BlockSpec((tm, d), lambda i, e, c, cnt: (i, 0)),
            pl.BlockSpec((1, d), lambda i, e, c, cnt: (0, 0)),
            _mod_spec(scale, n_tiles),
            _mod_spec(shift, n_tiles),
            _mod_spec(gate, n_tiles),
            pl.BlockSpec((tm, LANES), lambda i, e, c, cnt: (i, 0)),
            pl.BlockSpec((tm, LANES), lambda i, e, c, cnt: (i, 0)),
            pl.BlockSpec((1, 1, tm), lambda i, e, c, cnt: (i * N_EXPERTS + e, 0, 0)),
            pl.BlockSpec((1, d, tf), lambda i, e, c, cnt: (e, 0, c)),
            pl.BlockSpec((1, d, tf), lambda i, e, c, cnt: (e, 0, c)),
            pl.BlockSpec((1, tf, d), lambda i, e, c, cnt: (e, c, 0)),
        ],
        out_specs=pl.BlockSpec((tm, d), lambda i, e, c, cnt: (i, 0)),
        scratch_shapes=[
            pltpu.VMEM((tm, d), bf16),
            pltpu.VMEM((n_cap * cap, d), bf16),
            pltpu.VMEM((n_cap * cap, d), f32),
        ],
    )
    return pl.pallas_call(
        functools.partial(_moe_kernel, cap=cap),
        out_shape=jax.ShapeDtypeStruct((t, d), f32),
        grid_spec=grid_spec,
        compiler_params=_cparams("parallel", "arbitrary", "arbitrary"),
        name="moe",
    )(counts, x, g, scale, shift, gate, comb, rank, rankt, w1, w3, w2)


S5_LANES = S5_GROUPS * S5_STATE
S5_QUARTERS = 4
S5_QS = S5_LANES // S5_QUARTERS
S5_QC = S5_INNER // S5_QUARTERS


def _s5_load_bu(u_refs, bre_ref, bim_ref, up_scr, bu_r, bu_i, n_sub, m):
    for c in range(S5_QUARTERS):
        for k in range(m):
            up_scr[k * n_sub:(k + 1) * n_sub, c * S5_QC:(c + 1) * S5_QC] = u_refs[c][pl.ds(k, n_sub, stride=m), :]
        uc = up_scr[:, c * S5_QC:(c + 1) * S5_QC].astype(bf16)
        bu_r[:, c * S5_QS:(c + 1) * S5_QS] = _dot(uc, bre_ref[c])
        bu_i[:, c * S5_QS:(c + 1) * S5_QS] = _dot(uc, bim_ref[c])


def _s5_local_scan(lam_r_ref, lam_i_ref, init_r_ref, init_i_ref, bu_r, bu_i, n_sub, m):
    width = 8192 // n_sub
    for c in range(S5_LANES // width):
        cols = slice(c * width, (c + 1) * width)
        lr = jnp.broadcast_to(lam_r_ref[:, cols], (n_sub, width))
        li = jnp.broadcast_to(lam_i_ref[:, cols], (n_sub, width))
        if init_r_ref is None:
            x0 = (jnp.zeros((n_sub, width), f32), jnp.zeros((n_sub, width), f32))
        else:
            x0 = (init_r_ref[:, cols], init_i_ref[:, cols])

        def step(k, carry):
            xr, xi = carry
            rows = pl.ds(pl.multiple_of(k * n_sub, 8), n_sub)
            nr = lr * xr - li * xi + bu_r[rows, cols]
            ni = lr * xi + li * xr + bu_i[rows, cols]
            bu_r[rows, cols] = nr
            bu_i[rows, cols] = ni
            return nr, ni
        lax.fori_loop(0, m, step, x0)


def _gelu_tanh(x):
    return 0.5 * x * (1.0 + jnp.tanh(math.sqrt(2.0 / math.pi) * (x + 0.044715 * (x * x * x))))


def _s5_output(xb_r, xb_i, up_scr, cre_ref, cim_ref, dd_ref, gw_ref, gb_ref, o_ref, op_scr, n_sub, m):
    ys = []
    for c in range(S5_QUARTERS):
        cols = slice(c * S5_QS, (c + 1) * S5_QS)
        ys.append(_dot(xb_r[:, cols].astype(bf16), cre_ref[c]) - _dot(xb_i[:, cols].astype(bf16), cim_ref[c]))
    y = jnp.concatenate(ys, axis=1) + dd_ref[...] * up_scr[...]
    hs = _gelu_tanh(y)
    op_scr[...] = hs * jax.nn.sigmoid(_dot(hs.astype(bf16), gw_ref[...]) + gb_ref[...])
    for c in range(S5_QUARTERS):
        for k in range(m):
            o_ref[c, pl.ds(k, n_sub, stride=m), :] = op_scr[k * n_sub:(k + 1) * n_sub, c * S5_QC:(c + 1) * S5_QC]


def _s5_chain_kernel(u0_ref, u1_ref, u2_ref, u3_ref, x0r_ref, x0i_ref, lam_r_ref, lam_i_ref, bre_ref, bim_ref, cre_ref, cim_ref,
                     dd_ref, gw_ref, gb_ref, o_ref, sr_ref, si_ref,
                     pow_r, pow_i, bu_r, bu_i, up_scr, op_scr, en_r, en_i, *, m):
    n_sub = 8
    j = pl.program_id(1)

    @pl.when((pl.program_id(0) == 0) & (j == 0))
    def _():
        def pstep(k, carry):
            pr, pi = carry
            pow_r[pl.ds(k, 1), :] = pr
            pow_i[pl.ds(k, 1), :] = pi
            lr = lam_r_ref[...]
            li = lam_i_ref[...]
            return lr * pr - li * pi, lr * pi + li * pr
        lax.fori_loop(0, m, pstep, (lam_r_ref[...], lam_i_ref[...]))

    @pl.when(j == 0)
    def _():
        sr_ref[0] = x0r_ref[0]
        si_ref[0] = x0i_ref[0]

    _s5_load_bu((u0_ref, u1_ref, u2_ref, u3_ref), bre_ref, bim_ref, up_scr, bu_r, bu_i, n_sub, m)
    _s5_local_scan(lam_r_ref, lam_i_ref, None, None, bu_r, bu_i, n_sub, m)

    pm_r = pow_r[m - 1:m, :]
    pm_i = pow_i[m - 1:m, :]
    e_r = sr_ref[0]
    e_i = si_ref[0]
    for s in range(n_sub):
        en_r[s:s + 1, :] = e_r
        en_i[s:s + 1, :] = e_i
        row = (m - 1) * n_sub + s
        e_r, e_i = (bu_r[row:row + 1, :] + pm_r * e_r - pm_i * e_i,
                    bu_i[row:row + 1, :] + pm_r * e_i + pm_i * e_r)
    sr_ref[0] = e_r
    si_ref[0] = e_i

    def fix(k, carry):
        rows = pl.ds(pl.multiple_of(k * n_sub, 8), n_sub)
        pr = pow_r[pl.ds(k, 1), :]
        pi = pow_i[pl.ds(k, 1), :]
        er = en_r[...]
        ei = en_i[...]
        bu_r[rows, :] = bu_r[rows, :] + pr * er - pi * ei
        bu_i[rows, :] = bu_i[rows, :] + pr * ei + pi * er
        return carry
    lax.fori_loop(0, m, fix, 0)

    _s5_output(bu_r, bu_i, up_scr, cre_ref, cim_ref, dd_ref, gw_ref, gb_ref, o_ref, op_scr, n_sub, m)


def _s5_batch_kernel(u0_ref, u1_ref, u2_ref, u3_ref, x0r_ref, x0i_ref, lam_r_ref, lam_i_ref, bre_ref, bim_ref, cre_ref, cim_ref,
                     dd_ref, gw_ref, gb_ref, o_ref, sr_ref, si_ref,
                     bu_r, bu_i, up_scr, op_scr, *, n_sub, m):
    _s5_load_bu((u0_ref, u1_ref, u2_ref, u3_ref), bre_ref, bim_ref, up_scr, bu_r, bu_i, n_sub, m)
    _s5_local_scan(lam_r_ref, lam_i_ref, x0r_ref, x0i_ref, bu_r, bu_i, n_sub, m)
    last = slice((m - 1) * n_sub, m * n_sub)
    sr_ref[...] = bu_r[last, :]
    si_ref[...] = bu_i[last, :]
    _s5_output(bu_r, bu_i, up_scr, cre_ref, cim_ref, dd_ref, gw_ref, gb_ref, o_ref, op_scr, n_sub, m)


def _s5_params(p):
    ar = p['s5_A_re'].astype(f32)
    ai = p['s5_A_im'].astype(f32)
    step = jnp.exp(p['s5_log_step'].astype(f32))[:, None]
    mag = jnp.exp(ar * step)
    lb_re = mag * jnp.cos(ai * step)
    lb_im = mag * jnp.sin(ai * step)
    den = ar * ar + ai * ai
    nr = lb_re - 1.0
    cr = (nr * ar + lb_im * ai) / den
    ci = (lb_im * ar - nr * ai) / den
    bb_re = cr[..., None] * p['s5_B_re'] - ci[..., None] * p['s5_B_im']
    bb_im = cr[..., None] * p['s5_B_im'] + ci[..., None] * p['s5_B_re']
    gq = S5_GROUPS // S5_QUARTERS
    eye = jnp.eye(gq, dtype=f32)

    def bq(bb):
        t = bb.reshape(S5_QUARTERS, gq, S5_STATE, S5_CH)
        return jnp.einsum('cgph,gk->cghkp', t, eye).reshape(S5_QUARTERS, S5_QC, S5_QS).astype(bf16)

    def cq(cc):
        t = cc.astype(f32).reshape(S5_QUARTERS, gq, S5_CH, S5_STATE)
        return jnp.einsum('cghp,gk->cgpkh', t, eye).reshape(S5_QUARTERS, S5_QS, S5_QC).astype(bf16)

    return dict(lam_r=lb_re.reshape(1, S5_LANES), lam_i=lb_im.reshape(1, S5_LANES),
                bre=bq(bb_re), bim=bq(bb_im), cre=cq(p['s5_C_re']), cim=cq(p['s5_C_im']),
                dd=p['s5_D'].astype(f32).reshape(1, S5_INNER),
                gw=p['glu_w'].astype(bf16), gb=p['glu_b'].astype(f32).reshape(1, S5_INNER))


def _s5_const_specs(nd):
    z2 = (lambda *_: (0, 0))
    z3 = (lambda *_: (0, 0, 0))
    return [
        pl.BlockSpec((1, S5_LANES), z2), pl.BlockSpec((1, S5_LANES), z2),
        pl.BlockSpec((S5_QUARTERS, S5_QC, S5_QS), z3), pl.BlockSpec((S5_QUARTERS, S5_QC, S5_QS), z3),
        pl.BlockSpec((S5_QUARTERS, S5_QS, S5_QC), z3), pl.BlockSpec((S5_QUARTERS, S5_QS, S5_QC), z3),
        pl.BlockSpec((1, S5_INNER), z2), pl.BlockSpec((S5_INNER, S5_INNER), z2), pl.BlockSpec((1, S5_INNER), z2),
    ]


def _s5_chain(proj, ucol, x0r, x0i, sp, b, l, m):
    chunk = 8 * m
    nc = l // chunk
    consts = [sp[k] for k in ('lam_r', 'lam_i', 'bre', 'bim', 'cre', 'cim', 'dd', 'gw', 'gb')]
    return pl.pallas_call(
        functools.partial(_s5_chain_kernel, m=m),
        out_shape=(jax.ShapeDtypeStruct((S5_QUARTERS, b * l, S5_QC), f32),
                   jax.ShapeDtypeStruct((b, 1, S5_LANES), f32),
                   jax.ShapeDtypeStruct((b, 1, S5_LANES), f32)),
        grid=(b, nc),
        in_specs=[pl.BlockSpec((chunk, S5_QC), functools.partial(lambda i, j, c: (i * nc + j, ucol + c), c=c))
                  for c in range(S5_QUARTERS)] + [
                  pl.BlockSpec((1, 1, S5_LANES), lambda i, j: (i, 0, 0)),
                  pl.BlockSpec((1, 1, S5_LANES), lambda i, j: (i, 0, 0))] + _s5_const_specs(2),
        out_specs=(pl.BlockSpec((S5_QUARTERS, chunk, S5_QC), lambda i, j: (0, i * nc + j, 0)),
                   pl.BlockSpec((1, 1, S5_LANES), lambda i, j: (i, 0, 0)),
                   pl.BlockSpec((1, 1, S5_LANES), lambda i, j: (i, 0, 0))),
        scratch_shapes=[pltpu.VMEM((m, S5_LANES), f32), pltpu.VMEM((m, S5_LANES), f32),
                        pltpu.VMEM((chunk, S5_LANES), f32), pltpu.VMEM((chunk, S5_LANES), f32),
                        pltpu.VMEM((chunk, S5_INNER), f32), pltpu.VMEM((chunk, S5_INNER), f32),
                        pltpu.VMEM((8, S5_LANES), f32), pltpu.VMEM((8, S5_LANES), f32)],
        compiler_params=_cparams("arbitrary", "arbitrary"),
        name="s5_chain",
    )(proj, proj, proj, proj, x0r, x0i, *consts)


def _s5_batch(proj, ucol, x0r, x0i, sp, n_sub, m):
    t = n_sub * m
    consts = [sp[k] for k in ('lam_r', 'lam_i', 'bre', 'bim', 'cre', 'cim', 'dd', 'gw', 'gb')]
    return pl.pallas_call(
        functools.partial(_s5_batch_kernel, n_sub=n_sub, m=m),
        out_shape=(jax.ShapeDtypeStruct((S5_QUARTERS, t, S5_QC), f32),
                   jax.ShapeDtypeStruct((n_sub, S5_LANES), f32),
                   jax.ShapeDtypeStruct((n_sub, S5_LANES), f32)),
        grid=(1,),
        in_specs=[pl.BlockSpec((t, S5_QC), functools.partial(lambda i, c: (0, ucol + c), c=c))
                  for c in range(S5_QUARTERS)] + [
                  pl.BlockSpec((n_sub, S5_LANES), lambda i: (0, 0)),
                  pl.BlockSpec((n_sub, S5_LANES), lambda i: (0, 0))] + _s5_const_specs(1),
        out_specs=(pl.BlockSpec((S5_QUARTERS, t, S5_QC), lambda i: (0, 0, 0)),
                   pl.BlockSpec((n_sub, S5_LANES), lambda i: (0, 0)),
                   pl.BlockSpec((n_sub, S5_LANES), lambda i: (0, 0))),
        scratch_shapes=[pltpu.VMEM((t, S5_LANES), f32), pltpu.VMEM((t, S5_LANES), f32),
                        pltpu.VMEM((t, S5_INNER), f32), pltpu.VMEM((t, S5_INNER), f32)],
        compiler_params=_cparams("arbitrary"),
        name="s5_batch",
    )(proj, proj, proj, proj, x0r, x0i, *consts)


def _tri(n, strict=False, upper=False):
    r = lax.broadcasted_iota(jnp.int32, (n, n), 0)
    c = lax.broadcasted_iota(jnp.int32, (n, n), 1)
    if upper:
        r, c = c, r
    return jnp.where((c < r) if strict else (c <= r), 1.0, 0.0).astype(bf16)


def _dot_exact_lhs(a_exact_bf16, b):
    b0, b1, b2 = _split3(b)
    return _dot(a_exact_bf16, b0) + _dot(a_exact_bf16, b1) + _dot(a_exact_bf16, b2)


def _log_sigmoid(x):
    return jnp.minimum(x, 0.0) - jnp.log1p(jnp.exp(-jnp.abs(x)))


def _fcum_kernel(fr_ref, bias_ref, logf_ref, f_ref, ft_ref, carry):
    @pl.when(pl.program_id(1) == 0)
    def _():
        carry[...] = jnp.zeros_like(carry)

    n = fr_ref.shape[0]
    logf = _log_sigmoid(fr_ref[...] + bias_ref[...])
    logf_ref[...] = logf
    f = _dot_exact_lhs(_tri(n), logf) + carry[0:1, :]
    f_ref[...] = f
    ft_ref[0] = f.T[:8, :]
    carry[0:1, :] = f[n - 1:n, :]


def _fcum(proj, col, bias, b, l, chunk):
    nc = l // chunk
    return pl.pallas_call(
        _fcum_kernel,
        out_shape=(jax.ShapeDtypeStruct((b * l, LANES), f32),
                   jax.ShapeDtypeStruct((b * l, LANES), f32),
                   jax.ShapeDtypeStruct((b, 8, l), f32)),
        grid=(b, nc),
        in_specs=[pl.BlockSpec((chunk, LANES), lambda i, j: (i * nc + j, col)),
                  pl.BlockSpec((1, LANES), lambda i, j: (0, 0))],
        out_specs=(pl.BlockSpec((chunk, LANES), lambda i, j: (i * nc + j, 0)),
                   pl.BlockSpec((chunk, LANES), lambda i, j: (i * nc + j, 0)),
                   pl.BlockSpec((1, 8, chunk), lambda i, j: (i, 0, j))),
        scratch_shapes=[pltpu.VMEM((8, LANES), f32)],
        compiler_params=_cparams("arbitrary", "arbitrary"),
        name="fcum",
    )(proj, bias)


def _fox_kernel(q_ref, k_ref, v_ref, f_ref, ft_ref, o_ref, m_scr, l_scr, acc_scr, *, scale):
    hp = pl.program_id(1)
    qi = pl.program_id(2)
    ki = pl.program_id(3)
    tq = q_ref.shape[0]
    tk = k_ref.shape[0]
    hd = FOX_HEAD_DIM

    @pl.when(ki == 0)
    def _():
        m_scr[...] = jnp.full_like(m_scr, -jnp.inf)
        l_scr[...] = jnp.zeros_like(l_scr)
        acc_scr[...] = jnp.zeros_like(acc_scr)

    def update(masked):
        lane = lax.broadcasted_iota(jnp.int32, (tq, LANES), 1)
        for hh in range(2):
            cols = slice(hh * hd, (hh + 1) * hd)
            q = (q_ref[:, cols] * scale).astype(bf16)
            k = k_ref[:, cols].astype(bf16)
            s = lax.dot_general(q, k, (((1,), (1,)), ((), ())), preferred_element_type=f32)
            head = 2 * hp + hh
            fq = jnp.sum(jnp.where(lane == head, f_ref[...], 0.0), axis=1, keepdims=True)
            fk = ft_ref[0, pl.ds(head, 1), :]
            s = s + fq - fk
            if masked:
                r = lax.broadcasted_iota(jnp.int32, (tq, tk), 0)
                c = lax.broadcasted_iota(jnp.int32, (tq, tk), 1)
                s = jnp.where(r >= c, s, -jnp.inf)
            m_old = m_scr[hh]
            m_new = jnp.maximum(m_old, jnp.max(s, axis=1, keepdims=True))
            alpha = jnp.exp(m_old - m_new)
            p = jnp.exp(s - m_new)
            l_scr[hh] = alpha * l_scr[hh] + jnp.sum(p, axis=1, keepdims=True)
            acc_scr[hh] = alpha * acc_scr[hh] + _dot(p.astype(bf16), v_ref[:, cols].astype(bf16))
            m_scr[hh] = m_new

    @pl.when(ki < qi)
    def _():
        update(False)

    @pl.when(ki == qi)
    def _():
        update(True)
        o_ref[...] = jnp.concatenate([acc_scr[0] / l_scr[0], acc_scr[1] / l_scr[1]], axis=1)


def _fox_prompt(proj, f, ft, b, l, tq):
    nq = l // tq
    n_hp = FOX_HEADS // 2
    return pl.pallas_call(
        functools.partial(_fox_kernel, scale=FOX_HEAD_DIM ** -0.5),
        out_shape=jax.ShapeDtypeStruct((b * l, FOX_INNER), f32),
        grid=(b, n_hp, nq, nq),
        in_specs=[
            pl.BlockSpec((tq, LANES), lambda i, h, qi, ki: (i * nq + qi, h)),
            pl.BlockSpec((tq, LANES), lambda i, h, qi, ki: (i * nq + jnp.minimum(ki, qi), n_hp + h)),
            pl.BlockSpec((tq, LANES), lambda i, h, qi, ki: (i * nq + jnp.minimum(ki, qi), 2 * n_hp + h)),
            pl.BlockSpec((tq, LANES), lambda i, h, qi, ki: (i * nq + qi, 0)),
            pl.BlockSpec((1, 8, tq), lambda i, h, qi, ki: (i, 0, jnp.minimum(ki, qi))),
        ],
        out_specs=pl.BlockSpec((tq, LANES), lambda i, h, qi, ki: (i * nq + qi, h)),
        scratch_shapes=[pltpu.VMEM((2, tq, 1), f32), pltpu.VMEM((2, tq, 1), f32),
                        pltpu.VMEM((2, tq, FOX_HEAD_DIM), f32)],
        compiler_params=_cparams("parallel", "parallel", "arbitrary", "arbitrary"),
        name="fox_prompt",
    )(proj, proj, proj, f, ft)


def _rmsnorm(x, w):
    xf = x.astype(f32)
    y = xf * lax.rsqrt(jnp.mean(xf * xf, axis=-1, keepdims=True) + EPS)
    return (y * w.astype(f32)).astype(x.dtype)


def _l2norm(x):
    xf = x.astype(f32)
    return xf * lax.rsqrt(jnp.sum(xf * xf, axis=-1, keepdims=True) + EPS)


def _causal_conv(u, buf, w, bias):
    l = u.shape[1]
    upad = jnp.concatenate([buf.astype(u.dtype), u], axis=1)
    out = bias + sum(w[j] * upad[:, j:j + l] for j in range(CONV_K))
    return jax.nn.silu(out), upad[:, upad.shape[1] - (CONV_K - 1):]


def _ssd_chunked(x, dt, A, Bm, Cm, h0):
    b, l, h, p = x.shape
    n = Bm.shape[-1]
    q = math.gcd(l, CHUNK)
    c = l // q
    xdt = (x.astype(f32) * dt[..., None]).reshape(b, c, q, h, p)
    acum = jnp.cumsum((dt * A).reshape(b, c, q, h), axis=2)
    Bc = Bm.astype(f32).reshape(b, c, q, h, n)
    Cc = Cm.astype(f32).reshape(b, c, q, h, n)
    causal = jnp.tril(jnp.ones((q, q), bool))[None, None, :, :, None]
    seg = acum[:, :, :, None, :] - acum[:, :, None, :, :]
    decay = jnp.where(causal, jnp.exp(jnp.where(causal, seg, 0.0)), 0.0)
    scores = jnp.einsum('bcihn,bcjhn->bcijh', Cc, Bc) * decay
    y_diag = jnp.einsum('bcijh,bcjhp->bcihp', scores, xdt)
    to_end = jnp.exp(acum[:, :, -1:, :] - acum)
    chunk_states = jnp.einsum('bcjhn,bcjhp->bchpn', Bc * to_end[..., None], xdt)
    chunk_decay = jnp.exp(acum[:, :, -1, :])

    def step(state, inp):
        st, dec = inp
        return state * dec[..., None, None] + st, state

    h_final, h_enter = lax.scan(step, h0.astype(f32),
                                (jnp.moveaxis(chunk_states, 1, 0), jnp.moveaxis(chunk_decay, 1, 0)))
    h_enter = jnp.moveaxis(h_enter, 0, 1)
    y_off = jnp.einsum('bcihn,bchpn->bcihp', Cc * jnp.exp(acum)[..., None], h_enter)
    return (y_diag + y_off).reshape(b, l, h, p), h_final


def _gdn_chunked(q, k, v, g, beta, s0):
    b, l, h, dk = q.shape
    dv = v.shape[-1]
    Q = math.gcd(l, CHUNK)
    c = l // Q

    def blk(t):
        return t.astype(f32).reshape(b, c, Q, h, t.shape[-1]).transpose(0, 1, 3, 2, 4)

    qc = blk(q) * dk ** -0.5
    kc = blk(k)
    vc = blk(v)
    gc = jnp.cumsum(g.astype(f32).reshape(b, c, Q, h).transpose(0, 1, 3, 2), axis=-1)
    bc = beta.astype(f32).reshape(b, c, Q, h).transpose(0, 1, 3, 2)
    incl = jnp.tril(jnp.ones((Q, Q), bool))
    strict = jnp.tril(jnp.ones((Q, Q), bool), k=-1)
    diff = gc[..., :, None] - gc[..., None, :]
    decay = jnp.where(incl, jnp.exp(jnp.where(incl, diff, 0.0)), 0.0)
    kb = kc * bc[..., None]
    vb = vc * bc[..., None]
    m = jnp.where(strict, jnp.einsum('bchid,bchjd->bchij', kb, kc) * decay, 0.0)
    t_sys = jnp.eye(Q, dtype=f32) + m
    rhs = jnp.concatenate([vb, kb * jnp.exp(gc)[..., None]], axis=-1)
    sol = lax.linalg.triangular_solve(t_sys, rhs, left_side=True, lower=True)
    u = sol[..., :dv]
    w = sol[..., dv:]
    attn = jnp.einsum('bchid,bchjd->bchij', qc, kc) * decay
    qd = qc * jnp.exp(gc)[..., None]
    ke = kc * jnp.exp(gc[..., -1:] - gc)[..., None]
    dl = jnp.exp(gc[..., -1])

    def step(S, inp):
        u_c, w_c, qd_c, ke_c, a_c, dl_c = inp
        v_new = u_c - jnp.einsum('bhid,bhde->bhie', w_c, S)
        o = jnp.einsum('bhid,bhde->bhie', qd_c, S) + jnp.einsum('bhij,bhje->bhie', a_c, v_new)
        S = S * dl_c[..., None, None] + jnp.einsum('bhid,bhie->bhde', ke_c, v_new)
        return S, o

    xs = tuple(jnp.moveaxis(t, 1, 0) for t in (u, w, qd, ke, attn, dl))
    s_final, o = lax.scan(step, s0.astype(f32), xs)
    return jnp.transpose(o, (1, 0, 3, 2, 4)).reshape(b, l, h, dv), s_final


def _fox_attention(q, k, v, logf, past):
    b, l, h, d = q.shape
    if past is None:
        k_all, v_all, logf_all = k.astype(f32), v.astype(f32), logf
    else:
        k_past, v_past, logf_past = past
        k_all = jnp.concatenate([k_past.astype(f32), k.astype(f32)], axis=1)
        v_all = jnp.concatenate([v_past.astype(f32), v.astype(f32)], axis=1)
        logf_all = jnp.concatenate([logf_past.astype(f32), logf], axis=1)
    n_keys = k_all.shape[1]
    offset = n_keys - l
    F = jnp.cumsum(logf_all, axis=1)
    Fk = jnp.moveaxis(F, 1, 2)
    Fq = F[:, offset:]
    kpos = jnp.arange(n_keys)
    qb = math.gcd(l, Q_BLOCK)
    nb = l // qb
    scale = d ** -0.5

    def block(args):
        q_i, fq_i, qpos_i = args
        s = jnp.einsum('bqhd,bkhd->bhqk', q_i.astype(f32), k_all) * scale
        s = s + jnp.moveaxis(fq_i, 1, 2)[..., None] - Fk[:, :, None, :]
        s = jnp.where(qpos_i[:, None] >= kpos[None, :], s, -jnp.inf)
        pr = jax.nn.softmax(s, axis=-1)
        return jnp.einsum('bhqk,bkhd->bqhd', pr, v_all)

    q_blocks = jnp.moveaxis(q.reshape(b, nb, qb, h, d), 1, 0)
    fq_blocks = jnp.moveaxis(Fq.reshape(b, nb, qb, h), 1, 0)
    qpos_blocks = (offset + jnp.arange(l)).reshape(nb, qb)
    o = lax.map(block, (q_blocks, fq_blocks, qpos_blocks))
    return jnp.moveaxis(o, 0, 1).reshape(b, l, h, d)


def _s5_scan(u, A_re, A_im, log_step, B_re, B_im, C_re, C_im, Dd, x0_re, x0_im):
    b, l, _ = u.shape
    uu = u.astype(f32).reshape(b, l, S5_GROUPS, S5_CH)
    ar = A_re.astype(f32)
    ai = A_im.astype(f32)
    step = jnp.exp(log_step.astype(f32))[:, None]
    mag = jnp.exp(ar * step)
    lb_re = mag * jnp.cos(ai * step)
    lb_im = mag * jnp.sin(ai * step)
    den = ar * ar + ai * ai
    nr = lb_re - 1.0
    cr = (nr * ar + lb_im * ai) / den
    ci = (lb_im * ar - nr * ai) / den
    bb_re = cr[..., None] * B_re - ci[..., None] * B_im
    bb_im = cr[..., None] * B_im + ci[..., None] * B_re
    bu_re = jnp.einsum('gph,blgh->blgp', bb_re, uu)
    bu_im = jnp.einsum('gph,blgh->blgp', bb_im, uu)
    shp = bu_re.shape
    a_re = jnp.broadcast_to(lb_re, shp)
    a_im = jnp.broadcast_to(lb_im, shp)

    def combine(e1, e2):
        a1r, a1i, b1r, b1i = e1
        a2r, a2i, b2r, b2i = e2
        return (a2r * a1r - a2i * a1i, a2r * a1i + a2i * a1r,
                a2r * b1r - a2i * b1i + b2r, a2r * b1i + a2i * b1r + b2i)

    pr, pi, xr, xi = lax.associative_scan(combine, (a_re, a_im, bu_re, bu_im), axis=1)
    x0r = x0_re.astype(f32)[:, None]
    x0i = x0_im.astype(f32)[:, None]
    xr = xr + pr * x0r - pi * x0i
    xi = xi + pr * x0i + pi * x0r
    y = (jnp.einsum('ghp,blgp->blgh', C_re, xr) - jnp.einsum('ghp,blgp->blgh', C_im, xi)
         + Dd * uu)
    return y.reshape(b, l, S5_INNER), xr[:, -1], xi[:, -1]


def _mods(c, w_ada, b_ada, l, tm):
    mod = jax.nn.silu(c) @ w_ada + b_ada
    parts = jnp.split(mod, 6, axis=-1)
    if l % tm == 0:
        return [p[:, None, :] for p in parts]
    b = c.shape[0]
    return [jnp.repeat(p, l, axis=0).reshape((b * l) // tm, tm, D_MODEL) for p in parts]


def _run_trunk(x, c, conv_buf, ssd_h0, gdn_s0, past, s5_re0, s5_im0, p, tm):
    b, l, d = x.shape
    t = b * l
    x2 = x.reshape(t, d)

    sh1, sc1, g1, sh2, sc2, g2 = _mods(c, p['ada0_w'], p['ada0_b'], l, tm)
    proj = _inproj(x2, p['norm_mix0'], sc1, sh1, p['w_in0'], tm, 1152).reshape(b, l, IN0_PAD)
    conv_out, conv_new = _causal_conv(proj[..., :CONV_CH], conv_buf, p['conv0_w'], p['conv0_b'])
    xs, Bm, Cm, qg, kg, vg = jnp.split(
        conv_out, [SSD_INNER, SSD_INNER + SSD_GROUPS * SSD_STATE, SSD_CONV_CH,
                   SSD_CONV_CH + GDN_INNER, SSD_CONV_CH + 2 * GDN_INNER], axis=-1)
    z_ssd = proj[..., CONV_CH:CONV_CH + SSD_INNER]
    z_gdn = proj[..., CONV_CH + SSD_INNER:CONV_CH + SSD_INNER + GDN_INNER]
    small = proj[..., CONV_CH + SSD_INNER + GDN_INNER:]
    dt_raw = small[..., :SSD_HEADS]
    a_gdn = small[..., SSD_HEADS:SSD_HEADS + GDN_HEADS]
    b_gdn = small[..., SSD_HEADS + GDN_HEADS:SSD_HEADS + 2 * GDN_HEADS]
    xs = xs.reshape(b, l, SSD_HEADS, SSD_HEAD_DIM)
    rep = SSD_HEADS // SSD_GROUPS
    Bm = jnp.repeat(Bm.reshape(b, l, SSD_GROUPS, SSD_STATE), rep, axis=2)
    Cm = jnp.repeat(Cm.reshape(b, l, SSD_GROUPS, SSD_STATE), rep, axis=2)
    dt = jax.nn.softplus((dt_raw + p['ssd_dt_bias']).astype(f32))
    A = -jnp.exp(p['ssd_A_log'].astype(f32))
    y, ssd_new = _ssd_chunked(xs, dt, A, Bm, Cm, ssd_h0)
    y = (y + p['ssd_D'].astype(f32)[:, None] * xs.astype(f32)).reshape(b, l, SSD_INNER)
    y_ssd = _rmsnorm(y * jax.nn.silu(z_ssd.astype(f32)), p['ssd_norm'])
    q = _l2norm(qg.reshape(b, l, GDN_HEADS, GDN_HEAD_DIM))
    k = _l2norm(kg.reshape(b, l, GDN_HEADS, GDN_HEAD_DIM))
    v = vg.reshape(b, l, GDN_HEADS, GDN_HEAD_DIM)
    g_log = -jnp.exp(p['gdn_A_log'].astype(f32)) * jax.nn.softplus((a_gdn + p['gdn_dt_bias']).astype(f32))
    beta = jax.nn.sigmoid(b_gdn.astype(f32))
    o, gdn_new = _gdn_chunked(q, k, v, g_log, beta, gdn_s0)
    o = _rmsnorm(o, p['gdn_norm']) * jax.nn.silu(z_gdn.astype(f32).reshape(b, l, GDN_HEADS, GDN_HEAD_DIM))
    mix = jnp.concatenate([y_ssd, o.reshape(b, l, GDN_INNER)], axis=-1).reshape(t, d)
    x2 = _outproj([mix], [p['w_out0']], x2, g1, tm)
    x2 = _ffn(x2, p['norm_ffn0'], sc2, sh2, g2, p['ffn_w1'], p['ffn_w3'], p['ffn_w2'], min(tm, 512), 1408)

    sh1, sc1, g1, sh2, sc2, g2 = _mods(c, p['ada1_w'], p['ada1_b'], l, tm)
    proj2 = _inproj(x2, p['norm_mix1'], sc1, sh1, p['w_in1'], tm, 1152)
    proj = proj2.reshape(b, l, IN1_PAD)
    q = proj[..., :FOX_INNER].reshape(b, l, FOX_HEADS, FOX_HEAD_DIM)
    k_new = proj[..., FOX_INNER:2 * FOX_INNER].reshape(b, l, FOX_HEADS, FOX_HEAD_DIM)
    v_new = proj[..., 2 * FOX_INNER:3 * FOX_INNER].reshape(b, l, FOX_HEADS, FOX_HEAD_DIM)
    sp = p['s5']
    ucol = (3 * FOX_INNER) // S5_QC
    fcol = (3 * FOX_INNER + S5_INNER) // LANES
    if past is None:
        logf_pad, fcs, fcs_t = _fcum(proj2, fcol, p['fox_f_bias'], b, l, 512)
        logf = logf_pad[:, :FOX_HEADS].reshape(b, l, FOX_HEADS)
        o_fox = _fox_prompt(proj2, fcs, fcs_t, b, l, 512)
        o_s5, s5_re, s5_im = _s5_chain(proj2, ucol, s5_re0.reshape(b, 1, S5_LANES),
                                       s5_im0.reshape(b, 1, S5_LANES), sp, b, l, 64)
    else:
        f_raw = proj[..., 3 * FOX_INNER + S5_INNER:3 * FOX_INNER + S5_INNER + FOX_HEADS]
        logf = jax.nn.log_sigmoid((f_raw + p['fox_f_bias'][0, :FOX_HEADS]).astype(f32))
        o_fox = _fox_attention(q, k_new, v_new, logf, past).reshape(t, FOX_INNER)
        o_s5, s5_re, s5_im = _s5_batch(proj2, ucol, s5_re0.reshape(b, S5_LANES),
                                       s5_im0.reshape(b, S5_LANES), sp, b, l)
    s5_re = s5_re.reshape(b, S5_GROUPS, S5_STATE)
    s5_im = s5_im.reshape(b, S5_GROUPS, S5_STATE)
    w_out1 = p['w_out1']
    x2 = _outproj([o_fox, o_s5], [w_out1[:FOX_INNER], w_out1[FOX_INNER:].reshape(S5_QUARTERS, S5_QC, d)], x2, g1, tm)
    x2 = _moe(x2, p['norm_ffn1'], sc2, sh2, g2, p['router_w'], p['moe_w1'], p['moe_w3'], p['moe_w2'],
              tm, 896, 320)
    y_out = _rmsnorm(x2, p['norm_final']).reshape(b, l, d)
    return y_out, conv_new, ssd_new, gdn_new, k_new, v_new, logf, s5_re, s5_im


def kernel(x_prompt, x_sample, state_conv0, state_ssd, state_gdn, cache_k, cache_v, cache_logf,
           state_s5_re, state_s5_im, page_table, c_prompt, c_sample,
           ada0_w, ada0_b, norm_mix0, w_in0, conv0_w, conv0_b, ssd_dt_bias, ssd_A_log, ssd_D,
           ssd_norm, gdn_dt_bias, gdn_A_log, gdn_norm, w_out0, norm_ffn0, ffn_w1, ffn_w3, ffn_w2,
           ada1_w, ada1_b, norm_mix1, w_in1, fox_f_bias, s5_A_re, s5_A_im, s5_log_step,
           s5_B_re, s5_B_im, s5_C_re, s5_C_im, s5_D, glu_w, glu_b, w_out1, norm_ffn1,
           router_w, moe_w1, moe_w3, moe_w2, norm_final):
    d = D_MODEL
    c0 = CONV_CH
    w_in0p = jnp.concatenate([
        w_in0[:, :c0],
        w_in0[:, c0:c0 + SSD_INNER],
        w_in0[:, c0 + SSD_INNER + SSD_HEADS:c0 + SSD_INNER + SSD_HEADS + GDN_INNER],
        w_in0[:, c0 + SSD_INNER:c0 + SSD_INNER + SSD_HEADS],
        w_in0[:, c0 + SSD_INNER + SSD_HEADS + GDN_INNER:],
        jnp.zeros((d, IN0_PAD - w_in0.shape[1]), f32)], axis=1).astype(bf16)
    f0 = 3 * FOX_INNER
    w_in1p = jnp.concatenate([
        w_in1[:, :f0],
        w_in1[:, f0 + FOX_HEADS:],
        w_in1[:, f0:f0 + FOX_HEADS],
        jnp.zeros((d, IN1_PAD - w_in1.shape[1]), f32)], axis=1).astype(bf16)
    router_wp = jnp.concatenate([router_w.astype(f32), jnp.zeros((d, LANES - N_EXPERTS), f32)], axis=1)
    p = {
        'ada0_w': ada0_w, 'ada0_b': ada0_b, 'norm_mix0': norm_mix0.reshape(1, d), 'w_in0': w_in0p,
        'conv0_w': conv0_w, 'conv0_b': conv0_b, 'ssd_dt_bias': ssd_dt_bias, 'ssd_A_log': ssd_A_log,
        'ssd_D': ssd_D, 'ssd_norm': ssd_norm, 'gdn_dt_bias': gdn_dt_bias, 'gdn_A_log': gdn_A_log,
        'gdn_norm': gdn_norm, 'w_out0': w_out0.astype(bf16), 'norm_ffn0': norm_ffn0.reshape(1, d),
        'ffn_w1': ffn_w1.astype(bf16), 'ffn_w3': ffn_w3.astype(bf16), 'ffn_w2': ffn_w2.astype(bf16),
        'ada1_w': ada1_w, 'ada1_b': ada1_b, 'norm_mix1': norm_mix1.reshape(1, d), 'w_in1': w_in1p,
        'fox_f_bias': jnp.concatenate([fox_f_bias.astype(f32), jnp.zeros((LANES - FOX_HEADS,), f32)]).reshape(1, LANES),
        's5_A_re': s5_A_re, 's5_A_im': s5_A_im, 's5_log_step': s5_log_step,
        's5_B_re': s5_B_re, 's5_B_im': s5_B_im, 's5_C_re': s5_C_re, 's5_C_im': s5_C_im, 's5_D': s5_D,
        'glu_w': glu_w, 'glu_b': glu_b, 'w_out1': w_out1.astype(bf16), 'norm_ffn1': norm_ffn1.reshape(1, d),
        'router_w': router_wp, 'moe_w1': moe_w1.astype(bf16), 'moe_w3': moe_w3.astype(bf16),
        'moe_w2': moe_w2.astype(bf16), 'norm_final': norm_final,
    }
    p['s5'] = _s5_params(p)
    bp = x_prompt.shape[0]
    outs_p = _run_trunk(
        x_prompt, c_prompt,
        jnp.zeros((bp, CONV_K - 1, CONV_CH), x_prompt.dtype),
        jnp.zeros((bp, SSD_HEADS, SSD_HEAD_DIM, SSD_STATE), f32),
        jnp.zeros((bp, GDN_HEADS, GDN_HEAD_DIM, GDN_HEAD_DIM), f32),
        None,
        jnp.zeros((bp, S5_GROUPS, S5_STATE), f32),
        jnp.zeros((bp, S5_GROUPS, S5_STATE), f32),
        p, 1024)
    db = x_sample.shape[0]
    k_past = cache_k[page_table].reshape(db, -1, FOX_HEADS, FOX_HEAD_DIM)
    v_past = cache_v[page_table].reshape(db, -1, FOX_HEADS, FOX_HEAD_DIM)
    logf_past = cache_logf[page_table].reshape(db, -1, FOX_HEADS)
    outs_s = _run_trunk(
        x_sample, c_sample, state_conv0, state_ssd, state_gdn, (k_past, v_past, logf_past),
        state_s5_re, state_s5_im, p, 256)
    return (outs_p[0], outs_s[0]) + tuple(outs_p[1:]) + tuple(outs_s[1:])
```

```python
import functools
import math

import jax
import jax.numpy as jnp
from jax import lax
from jax.experimental import pallas as pl
from jax.experimental.pallas import tpu as pltpu

f32 = jnp.float32
bf16 = jnp.bfloat16

D_MODEL = 1024
CONV_K = 4
CHUNK = 64
SSD_HEADS = 8
SSD_HEAD_DIM = 64
SSD_INNER = 512
SSD_GROUPS = 2
SSD_STATE = 64
GDN_HEADS = 4
GDN_HEAD_DIM = 128
GDN_INNER = 512
SSD_CONV_CH = 768
CONV_CH = 2304
FOX_HEADS = 8
FOX_HEAD_DIM = 64
FOX_INNER = 512
Q_BLOCK = 128
S5_CH = 16
S5_GROUPS = 32
S5_INNER = 512
S5_STATE = 64
D_FF = 2816
N_EXPERTS = 8
D_FF_EXPERT = 3584
EPS = 1e-6

LANES = 128
IN0_PAD = 3456
IN1_PAD = 2304
VMEM_LIMIT = 56 * 1024 * 1024


def _cparams(*sem):
    return pltpu.CompilerParams(dimension_semantics=sem, vmem_limit_bytes=VMEM_LIMIT)


def _modnorm(x, g, scale, shift):
    ms = jnp.mean(x * x, axis=-1, keepdims=True)
    y = x * lax.rsqrt(ms + EPS) * g
    return y * (1.0 + scale) + shift


def _split3(a):
    a0 = a.astype(bf16)
    r = a - a0.astype(f32)
    a1 = r.astype(bf16)
    a2 = (r - a1.astype(f32)).astype(bf16)
    return a0, a1, a2


def _dot(a, b):
    return jnp.dot(a, b, preferred_element_type=f32)


def _dot_f32(a, b):
    a0, a1, a2 = _split3(a)
    b0, b1, b2 = _split3(b)
    return (_dot(a0, b0) + _dot(a0, b1) + _dot(a1, b0)
            + _dot(a1, b1) + _dot(a0, b2) + _dot(a2, b0))


def _mod_spec(mod, n_tiles):
    n_mod, rows, d = mod.shape
    per = n_tiles // n_mod
    return pl.BlockSpec((1, rows, d), lambda i, *_: (i // per, 0, 0))


def _inproj_kernel(x_ref, g_ref, sc_ref, sh_ref, w_ref, o_ref, h_scr):
    @pl.when(pl.program_id(1) == 0)
    def _():
        h_scr[...] = _modnorm(x_ref[...], g_ref[...], sc_ref[0], sh_ref[0]).astype(bf16)

    o_ref[...] = _dot(h_scr[...], w_ref[...])


def _inproj(x, g, scale, shift, w, tm, tn):
    t, d = x.shape
    n = w.shape[1]
    n_tiles = t // tm
    return pl.pallas_call(
        _inproj_kernel,
        out_shape=jax.ShapeDtypeStruct((t, n), f32),
        grid=(n_tiles, n // tn),
        in_specs=[
            pl.BlockSpec((tm, d), lambda i, j: (i, 0)),
            pl.BlockSpec((1, d), lambda i, j: (0, 0)),
            _mod_spec(scale, n_tiles),
            _mod_spec(shift, n_tiles),
            pl.BlockSpec((d, tn), lambda i, j: (0, j)),
        ],
        out_specs=pl.BlockSpec((tm, tn), lambda i, j: (i, j)),
        scratch_shapes=[pltpu.VMEM((tm, d), bf16)],
        compiler_params=_cparams("parallel", "arbitrary"),
        name="inproj",
    )(x, g, scale, shift, w)


def _outproj_kernel(*refs, n_parts):
    mix_refs = refs[:n_parts]
    w_refs = refs[n_parts:2 * n_parts]
    x_ref, gate_ref, o_ref = refs[2 * n_parts:]
    acc = None
    for m_ref, w_ref in zip(mix_refs, w_refs):
        if len(m_ref.shape) == 3:
            terms = [_dot(m_ref[q].astype(bf16), w_ref[q]) for q in range(m_ref.shape[0])]
        else:
            terms = [_dot(m_ref[...].astype(bf16), w_ref[...])]
        for term in terms:
            acc = term if acc is None else acc + term
    o_ref[...] = x_ref[...] + gate_ref[0] * acc


def _outproj(parts, weights, x, gate, tm):
    t, d = x.shape
    n_tiles = t // tm
    in_specs = []
    for a in parts:
        if a.ndim == 3:
            in_specs.append(pl.BlockSpec((a.shape[0], tm, a.shape[2]), lambda i: (0, i, 0)))
        else:
            in_specs.append(pl.BlockSpec((tm, a.shape[1]), lambda i: (i, 0)))
    for w in weights:
        in_specs.append(pl.BlockSpec(w.shape, (lambda i: (0, 0, 0)) if w.ndim == 3 else (lambda i: (0, 0))))
    in_specs += [pl.BlockSpec((tm, d), lambda i: (i, 0)), _mod_spec(gate, n_tiles)]
    return pl.pallas_call(
        functools.partial(_outproj_kernel, n_parts=len(parts)),
        out_shape=jax.ShapeDtypeStruct((t, d), f32),
        grid=(n_tiles,),
        in_specs=in_specs,
        out_specs=pl.BlockSpec((tm, d), lambda i: (i, 0)),
        compiler_params=_cparams("parallel"),
        name="outproj",
    )(*parts, *weights, x, gate)


def _ffn_kernel(x_ref, g_ref, sc_ref, sh_ref, gate_ref, w1_ref, w3_ref, w2_ref, o_ref, h_scr):
    j = pl.program_id(1)

    @pl.when(j == 0)
    def _():
        h_scr[...] = _modnorm(x_ref[...], g_ref[...], sc_ref[0], sh_ref[0]).astype(bf16)
        o_ref[...] = jnp.zeros_like(o_ref)

    h = h_scr[...]
    a = _dot(h, w1_ref[...])
    b = _dot(h, w3_ref[...])
    act = (a * jax.nn.sigmoid(a)) * b
    o_ref[...] += _dot(act.astype(bf16), w2_ref[...])

    @pl.when(j == pl.num_programs(1) - 1)
    def _():
        o_ref[...] = x_ref[...] + gate_ref[0] * o_ref[...]


def _ffn(x, g, scale, shift, gate, w1, w3, w2, tm, tf):
    t, d = x.shape
    f = w1.shape[1]
    n_tiles = t // tm
    return pl.pallas_call(
        _ffn_kernel,
        out_shape=jax.ShapeDtypeStruct((t, d), f32),
        grid=(n_tiles, f // tf),
        in_specs=[
            pl.BlockSpec((tm, d), lambda i, j: (i, 0)),
            pl.BlockSpec((1, d), lambda i, j: (0, 0)),
            _mod_spec(scale, n_tiles),
            _mod_spec(shift, n_tiles),
            _mod_spec(gate, n_tiles),
            pl.BlockSpec((d, tf), lambda i, j: (0, j)),
            pl.BlockSpec((d, tf), lambda i, j: (0, j)),
            pl.BlockSpec((tf, d), lambda i, j: (j, 0)),
        ],
        out_specs=pl.BlockSpec((tm, d), lambda i, j: (i, 0)),
        scratch_shapes=[pltpu.VMEM((tm, d), bf16)],
        compiler_params=_cparams("parallel", "arbitrary"),
        name="ffn",
    )(x, g, scale, shift, gate, w1, w3, w2)


def _router_kernel(x_ref, g_ref, sc_ref, sh_ref, rw_ref, comb_ref, rank_ref, rankt_ref, cnt_ref):
    tm = x_ref.shape[0]
    h = _modnorm(x_ref[...], g_ref[...], sc_ref[0], sh_ref[0])
    logits = _dot_f32(h, rw_ref[...])
    lane = lax.broadcasted_iota(jnp.int32, (tm, LANES), 1)
    lg = jnp.where(lane < N_EXPERTS, logits, -jnp.inf)
    m1 = jnp.max(lg, axis=1, keepdims=True)
    i1 = jnp.min(jnp.where(lg == m1, lane, LANES), axis=1, keepdims=True)
    lg2 = jnp.where(lane == i1, -jnp.inf, lg)
    m2 = jnp.max(lg2, axis=1, keepdims=True)
    i2 = jnp.min(jnp.where(lg2 == m2, lane, LANES), axis=1, keepdims=True)
    e2 = jnp.exp(m2 - m1)
    den = 1.0 + e2
    comb_ref[...] = jnp.where(lane == i1, 1.0 / den, jnp.where(lane == i2, e2 / den, 0.0))
    sel = (lane == i1) | (lane == i2)
    ind = jnp.where(sel, 1.0, 0.0)
    row = lax.broadcasted_iota(jnp.int32, (tm, tm), 0)
    col = lax.broadcasted_iota(jnp.int32, (tm, tm), 1)
    below = jnp.where(col < row, 1.0, 0.0).astype(bf16)
    rank = jnp.where(sel, _dot(below, ind.astype(bf16)), -1.0)
    rank_ref[...] = rank
    rankt_ref[0] = rank.T[:N_EXPERTS, :]
    cnt_ref[0] = jnp.broadcast_to(jnp.sum(ind, axis=0, keepdims=True), (8, LANES))


def _router(x, g, scale, shift, rw, tm):
    t, d = x.shape
    n_tiles = t // tm
    return pl.pallas_call(
        _router_kernel,
        out_shape=(
            jax.ShapeDtypeStruct((t, LANES), f32),
            jax.ShapeDtypeStruct((t, LANES), f32),
            jax.ShapeDtypeStruct((n_tiles, N_EXPERTS, tm), f32),
            jax.ShapeDtypeStruct((n_tiles, 8, LANES), f32),
        ),
        grid=(n_tiles,),
        in_specs=[
            pl.BlockSpec((tm, d), lambda i: (i, 0)),
            pl.BlockSpec((1, d), lambda i: (0, 0)),
            _mod_spec(scale, n_tiles),
            _mod_spec(shift, n_tiles),
            pl.BlockSpec((d, LANES), lambda i: (0, 0)),
        ],
        out_specs=(
            pl.BlockSpec((tm, LANES), lambda i: (i, 0)),
            pl.BlockSpec((tm, LANES), lambda i: (i, 0)),
            pl.BlockSpec((1, N_EXPERTS, tm), lambda i: (i, 0, 0)),
            pl.BlockSpec((1, 8, LANES), lambda i: (i, 0, 0)),
        ),
        compiler_params=_cparams("parallel"),
        name="router",
    )(x, g, scale, shift, rw)


def _moe_kernel(cnt_ref, x_ref, g_ref, sc_ref, sh_ref, gate_ref, comb_ref, rank_ref, rankt_ref,
                w1_ref, w3_ref, w2_ref, nf_ref, o_ref, h_scr, hc_scr, y_scr, *, cap):
    i = pl.program_id(0)
    e = pl.program_id(1)
    fc = pl.program_id(2)
    last_fc = pl.num_programs(2) - 1
    tm = x_ref.shape[0]
    n_batches = (cnt_ref[i * N_EXPERTS + e] + cap - 1) // cap

    @pl.when((e == 0) & (fc == 0))
    def _():
        h_scr[...] = _modnorm(x_ref[...], g_ref[...], sc_ref[0], sh_ref[0]).astype(bf16)
        o_ref[...] = jnp.zeros_like(o_ref)

    @pl.when(fc == 0)
    def _():
        def compact(b, carry):
            r0 = pl.multiple_of(b * cap, 8)
            rowid = (lax.broadcasted_iota(jnp.int32, (cap, tm), 0) + r0).astype(f32)
            onehot = jnp.where(rankt_ref[0] == rowid, 1.0, 0.0).astype(bf16)
            hc_scr[pl.ds(r0, cap), :] = _dot(onehot, h_scr[...]).astype(bf16)
            y_scr[pl.ds(r0, cap), :] = jnp.zeros((cap, y_scr.shape[1]), f32)
            return carry
        lax.fori_loop(0, n_batches, compact, 0)

    def expert(b, carry):
        r0 = pl.multiple_of(b * cap, 8)
        hc = hc_scr[pl.ds(r0, cap), :]
        a = _dot(hc, w1_ref[0])
        g = _dot(hc, w3_ref[0])
        act = (a * jax.nn.sigmoid(a)) * g
        y_scr[pl.ds(r0, cap), :] += _dot(act.astype(bf16), w2_ref[0])
        return carry
    lax.fori_loop(0, n_batches, expert, 0)

    @pl.when(fc == last_fc)
    def _():
        lane = lax.broadcasted_iota(jnp.int32, (tm, LANES), 1)
        rank_e = jnp.sum(jnp.where(lane == e, rank_ref[...], 0.0), axis=1, keepdims=True)
        gate_e = jnp.sum(jnp.where(lane == e, comb_ref[...], 0.0), axis=1, keepdims=True)

        def expand(b, carry):
            r0 = pl.multiple_of(b * cap, 8)
            colid = (lax.broadcasted_iota(jnp.int32, (tm, cap), 1) + r0).astype(f32)
            onehot = jnp.where(rank_e == colid, 1.0, 0.0).astype(bf16)
            o_ref[...] += gate_e * _dot(onehot, y_scr[pl.ds(r0, cap), :].astype(bf16))
            return carry
        lax.fori_loop(0, n_batches, expand, 0)

    @pl.when((e == N_EXPERTS - 1) & (fc == last_fc))
    def _():
        y = x_ref[...] + gate_ref[0] * o_ref[...]
        o_ref[...] = y * lax.rsqrt(jnp.mean(y * y, axis=-1, keepdims=True) + EPS) * nf_ref[...]


def _moe(x, g, scale, shift, gate, rw, w1, w3, w2, norm_final, tm, tf, cap):
    t, d = x.shape
    f = w1.shape[2]
    n_tiles = t // tm
    cap = min(cap, tm)
    n_cap = -(-tm // cap)
    comb, rank, rankt, cnt = _router(x, g, scale, shift, rw, tm)
    counts = cnt[:, 0, :N_EXPERTS].astype(jnp.int32).reshape(-1)
    rankt = rankt.reshape(n_tiles * N_EXPERTS, 1, tm)
    grid_spec = pltpu.PrefetchScalarGridSpec(
        num_scalar_prefetch=1,
        grid=(n_tiles, N_EXPERTS, f // tf),
        in_specs=[
            pl.BlockSpec((tm, d), lambda i, e, c, cnt: (i, 0)),
            pl.BlockSpec((1, d), lambda i, e, c, cnt: (0, 0)),
            _mod_spec(scale, n_tiles),
            _mod_spec(shift, n_tiles),
            _mod_spec(gate, n_tiles),
            pl.BlockSpec((tm, LANES), lambda i, e, c, cnt: (i, 0)),
            pl.BlockSpec((tm, LANES), lambda i, e, c, cnt: (i, 0)),
            pl.BlockSpec((1, 1, tm), lambda i, e, c, cnt: (i * N_EXPERTS + e, 0, 0)),
            pl.BlockSpec((1, d, tf), lambda i, e, c, cnt: (e, 0, c)),
            pl.BlockSpec((1, d, tf), lambda i, e, c, cnt: (e, 0, c)),
            pl.BlockSpec((1, tf, d), lambda i, e, c, cnt: (e, c, 0)),
            pl.BlockSpec((1, d), lambda i, e, c, cnt: (0, 0)),
        ],
        out_specs=pl.BlockSpec((tm, d), lambda i, e, c, cnt: (i, 0)),
        scratch_shapes=[
            pltpu.VMEM((tm, d), bf16),
            pltpu.VMEM((n_cap * cap, d), bf16),
            pltpu.VMEM((n_cap * cap, d), f32),
        ],
    )
    return pl.pallas_call(
        functools.partial(_moe_kernel, cap=cap),
        out_shape=jax.ShapeDtypeStruct((t, d), f32),
        grid_spec=grid_spec,
        compiler_params=_cparams("parallel", "arbitrary", "arbitrary"),
        name="moe",
    )(counts, x, g, scale, shift, gate, comb, rank, rankt, w1, w3, w2, norm_final)


S5_LANES = S5_GROUPS * S5_STATE
S5_QUARTERS = 4
S5_QS = S5_LANES // S5_QUARTERS
S5_QC = S5_INNER // S5_QUARTERS


def _s5_load_bu(u_refs, bre_ref, bim_ref, up_scr, bu_r, bu_i, n_sub, m):
    for c in range(S5_QUARTERS):
        for k in range(m):
            up_scr[k * n_sub:(k + 1) * n_sub, c * S5_QC:(c + 1) * S5_QC] = u_refs[c][pl.ds(k, n_sub, stride=m), :]
        uc = up_scr[:, c * S5_QC:(c + 1) * S5_QC].astype(bf16)
        bu_r[:, c * S5_QS:(c + 1) * S5_QS] = _dot(uc, bre_ref[c])
        bu_i[:, c * S5_QS:(c + 1) * S5_QS] = _dot(uc, bim_ref[c])


def _s5_local_scan(lam_r_ref, lam_i_ref, init_r_ref, init_i_ref, bu_r, bu_i, n_sub, m):
    width = 8192 // n_sub
    for c in range(S5_LANES // width):
        cols = slice(c * width, (c + 1) * width)
        lr = jnp.broadcast_to(lam_r_ref[:, cols], (n_sub, width))
        li = jnp.broadcast_to(lam_i_ref[:, cols], (n_sub, width))
        if init_r_ref is None:
            x0 = (jnp.zeros((n_sub, width), f32), jnp.zeros((n_sub, width), f32))
        else:
            x0 = (init_r_ref[:, cols], init_i_ref[:, cols])

        def step(k, carry):
            xr, xi = carry
            rows = pl.ds(pl.multiple_of(k * n_sub, 8), n_sub)
            nr = lr * xr - li * xi + bu_r[rows, cols]
            ni = lr * xi + li * xr + bu_i[rows, cols]
            bu_r[rows, cols] = nr
            bu_i[rows, cols] = ni
            return nr, ni
        lax.fori_loop(0, m, step, x0)


def _gelu_tanh(x):
    return 0.5 * x * (1.0 + jnp.tanh(math.sqrt(2.0 / math.pi) * (x + 0.044715 * (x * x * x))))


def _s5_output(xb_r, xb_i, up_scr, cre_ref, cim_ref, dd_ref, gw_ref, gb_ref, o_ref, op_scr, n_sub, m):
    ys = []
    for c in range(S5_QUARTERS):
        cols = slice(c * S5_QS, (c + 1) * S5_QS)
        ys.append(_dot(xb_r[:, cols].astype(bf16), cre_ref[c]) - _dot(xb_i[:, cols].astype(bf16), cim_ref[c]))
    y = jnp.concatenate(ys, axis=1) + dd_ref[...] * up_scr[...]
    hs = _gelu_tanh(y)
    op_scr[...] = hs * jax.nn.sigmoid(_dot(hs.astype(bf16), gw_ref[...]) + gb_ref[...])
    for c in range(S5_QUARTERS):
        for k in range(m):
            o_ref[c, pl.ds(k, n_sub, stride=m), :] = op_scr[k * n_sub:(k + 1) * n_sub, c * S5_QC:(c + 1) * S5_QC]


def _s5_chain_kernel(u0_ref, u1_ref, u2_ref, u3_ref, x0r_ref, x0i_ref, lam_r_ref, lam_i_ref, bre_ref, bim_ref, cre_ref, cim_ref,
                     dd_ref, gw_ref, gb_ref, o_ref, sr_ref, si_ref,
                     pow_r, pow_i, bu_r, bu_i, up_scr, op_scr, en_r, en_i, *, m):
    n_sub = 8
    j = pl.program_id(1)

    @pl.when((pl.program_id(0) == 0) & (j == 0))
    def _():
        def pstep(k, carry):
            pr, pi = carry
            pow_r[pl.ds(k, 1), :] = pr
            pow_i[pl.ds(k, 1), :] = pi
            lr = lam_r_ref[...]
            li = lam_i_ref[...]
            return lr * pr - li * pi, lr * pi + li * pr
        lax.fori_loop(0, m, pstep, (lam_r_ref[...], lam_i_ref[...]))

    @pl.when(j == 0)
    def _():
        sr_ref[0] = x0r_ref[0]
        si_ref[0] = x0i_ref[0]

    _s5_load_bu((u0_ref, u1_ref, u2_ref, u3_ref), bre_ref, bim_ref, up_scr, bu_r, bu_i, n_sub, m)
    _s5_local_scan(lam_r_ref, lam_i_ref, None, None, bu_r, bu_i, n_sub, m)

    pm_r = pow_r[m - 1:m, :]
    pm_i = pow_i[m - 1:m, :]
    e_r = sr_ref[0]
    e_i = si_ref[0]
    for s in range(n_sub):
        en_r[s:s + 1, :] = e_r
        en_i[s:s + 1, :] = e_i
        row = (m - 1) * n_sub + s
        e_r, e_i = (bu_r[row:row + 1, :] + pm_r * e_r - pm_i * e_i,
                    bu_i[row:row + 1, :] + pm_r * e_i + pm_i * e_r)
    sr_ref[0] = e_r
    si_ref[0] = e_i

    def fix(k, carry):
        rows = pl.ds(pl.multiple_of(k * n_sub, 8), n_sub)
        pr = pow_r[pl.ds(k, 1), :]
        pi = pow_i[pl.ds(k, 1), :]
        er = en_r[...]
        ei = en_i[...]
        bu_r[rows, :] = bu_r[rows, :] + pr * er - pi * ei
        bu_i[rows, :] = bu_i[rows, :] + pr * ei + pi * er
        return carry
    lax.fori_loop(0, m, fix, 0)

    _s5_output(bu_r, bu_i, up_scr, cre_ref, cim_ref, dd_ref, gw_ref, gb_ref, o_ref, op_scr, n_sub, m)


def _s5_batch_kernel(u0_ref, u1_ref, u2_ref, u3_ref, x0r_ref, x0i_ref, lam_r_ref, lam_i_ref, bre_ref, bim_ref, cre_ref, cim_ref,
                     dd_ref, gw_ref, gb_ref, o_ref, sr_ref, si_ref,
                     bu_r, bu_i, up_scr, op_scr, *, n_sub, m):
    _s5_load_bu((u0_ref, u1_ref, u2_ref, u3_ref), bre_ref, bim_ref, up_scr, bu_r, bu_i, n_sub, m)
    _s5_local_scan(lam_r_ref, lam_i_ref, x0r_ref, x0i_ref, bu_r, bu_i, n_sub, m)
    last = slice((m - 1) * n_sub, m * n_sub)
    sr_ref[...] = bu_r[last, :]
    si_ref[...] = bu_i[last, :]
    _s5_output(bu_r, bu_i, up_scr, cre_ref, cim_ref, dd_ref, gw_ref, gb_ref, o_ref, op_scr, n_sub, m)


def _s5_params(p):
    ar = p['s5_A_re'].astype(f32)
    ai = p['s5_A_im'].astype(f32)
    step = jnp.exp(p['s5_log_step'].astype(f32))[:, None]
    mag = jnp.exp(ar * step)
    lb_re = mag * jnp.cos(ai * step)
    lb_im = mag * jnp.sin(ai * step)
    den = ar * ar + ai * ai
    nr = lb_re - 1.0
    cr = (nr * ar + lb_im * ai) / den
    ci = (lb_im * ar - nr * ai) / den
    bb_re = cr[..., None] * p['s5_B_re'] - ci[..., None] * p['s5_B_im']
    bb_im = cr[..., None] * p['s5_B_im'] + ci[..., None] * p['s5_B_re']
    gq = S5_GROUPS // S5_QUARTERS
    eye = jnp.eye(gq, dtype=f32)

    def bq(bb):
        t = bb.reshape(S5_QUARTERS, gq, S5_STATE, S5_CH)
        return jnp.einsum('cgph,gk->cghkp', t, eye).reshape(S5_QUARTERS, S5_QC, S5_QS).astype(bf16)

    def cq(cc):
        t = cc.astype(f32).reshape(S5_QUARTERS, gq, S5_CH, S5_STATE)
        return jnp.einsum('cghp,gk->cgpkh', t, eye).reshape(S5_QUARTERS, S5_QS, S5_QC).astype(bf16)

    return dict(lam_r=lb_re.reshape(1, S5_LANES), lam_i=lb_im.reshape(1, S5_LANES),
                bre=bq(bb_re), bim=bq(bb_im), cre=cq(p['s5_C_re']), cim=cq(p['s5_C_im']),
                dd=p['s5_D'].astype(f32).reshape(1, S5_INNER),
                gw=p['glu_w'].astype(bf16), gb=p['glu_b'].astype(f32).reshape(1, S5_INNER))


def _s5_const_specs(nd):
    z2 = (lambda *_: (0, 0))
    z3 = (lambda *_: (0, 0, 0))
    return [
        pl.BlockSpec((1, S5_LANES), z2), pl.BlockSpec((1, S5_LANES), z2),
        pl.BlockSpec((S5_QUARTERS, S5_QC, S5_QS), z3), pl.BlockSpec((S5_QUARTERS, S5_QC, S5_QS), z3),
        pl.BlockSpec((S5_QUARTERS, S5_QS, S5_QC), z3), pl.BlockSpec((S5_QUARTERS, S5_QS, S5_QC), z3),
        pl.BlockSpec((1, S5_INNER), z2), pl.BlockSpec((S5_INNER, S5_INNER), z2), pl.BlockSpec((1, S5_INNER), z2),
    ]


def _s5_chain(proj, ucol, x0r, x0i, sp, b, l, m):
    chunk = 8 * m
    nc = l // chunk
    consts = [sp[k] for k in ('lam_r', 'lam_i', 'bre', 'bim', 'cre', 'cim', 'dd', 'gw', 'gb')]
    return pl.pallas_call(
        functools.partial(_s5_chain_kernel, m=m),
        out_shape=(jax.ShapeDtypeStruct((S5_QUARTERS, b * l, S5_QC), f32),
                   jax.ShapeDtypeStruct((b, 1, S5_LANES), f32),
                   jax.ShapeDtypeStruct((b, 1, S5_LANES), f32)),
        grid=(b, nc),
        in_specs=[pl.BlockSpec((chunk, S5_QC), functools.partial(lambda i, j, c: (i * nc + j, ucol + c), c=c))
                  for c in range(S5_QUARTERS)] + [
                  pl.BlockSpec((1, 1, S5_LANES), lambda i, j: (i, 0, 0)),
                  pl.BlockSpec((1, 1, S5_LANES), lambda i, j: (i, 0, 0))] + _s5_const_specs(2),
        out_specs=(pl.BlockSpec((S5_QUARTERS, chunk, S5_QC), lambda i, j: (0, i * nc + j, 0)),
                   pl.BlockSpec((1, 1, S5_LANES), lambda i, j: (i, 0, 0)),
                   pl.BlockSpec((1, 1, S5_LANES), lambda i, j: (i, 0, 0))),
        scratch_shapes=[pltpu.VMEM((m, S5_LANES), f32), pltpu.VMEM((m, S5_LANES), f32),
                        pltpu.VMEM((chunk, S5_LANES), f32), pltpu.VMEM((chunk, S5_LANES), f32),
                        pltpu.VMEM((chunk, S5_INNER), f32), pltpu.VMEM((chunk, S5_INNER), f32),
                        pltpu.VMEM((8, S5_LANES), f32), pltpu.VMEM((8, S5_LANES), f32)],
        compiler_params=_cparams("arbitrary", "arbitrary"),
        name="s5_chain",
    )(proj, proj, proj, proj, x0r, x0i, *consts)


def _s5_batch(proj, ucol, x0r, x0i, sp, n_sub, m):
    t = n_sub * m
    consts = [sp[k] for k in ('lam_r', 'lam_i', 'bre', 'bim', 'cre', 'cim', 'dd', 'gw', 'gb')]
    return pl.pallas_call(
        functools.partial(_s5_batch_kernel, n_sub=n_sub, m=m),
        out_shape=(jax.ShapeDtypeStruct((S5_QUARTERS, t, S5_QC), f32),
                   jax.ShapeDtypeStruct((n_sub, S5_LANES), f32),
                   jax.ShapeDtypeStruct((n_sub, S5_LANES), f32)),
        grid=(1,),
        in_specs=[pl.BlockSpec((t, S5_QC), functools.partial(lambda i, c: (0, ucol + c), c=c))
                  for c in range(S5_QUARTERS)] + [
                  pl.BlockSpec((n_sub, S5_LANES), lambda i: (0, 0)),
                  pl.BlockSpec((n_sub, S5_LANES), lambda i: (0, 0))] + _s5_const_specs(1),
        out_specs=(pl.BlockSpec((S5_QUARTERS, t, S5_QC), lambda i: (0, 0, 0)),
                   pl.BlockSpec((n_sub, S5_LANES), lambda i: (0, 0)),
                   pl.BlockSpec((n_sub, S5_LANES), lambda i: (0, 0))),
        scratch_shapes=[pltpu.VMEM((t, S5_LANES), f32), pltpu.VMEM((t, S5_LANES), f32),
                        pltpu.VMEM((t, S5_INNER), f32), pltpu.VMEM((t, S5_INNER), f32)],
        compiler_params=_cparams("arbitrary"),
        name="s5_batch",
    )(proj, proj, proj, proj, x0r, x0i, *consts)


def _tri(n, strict=False, upper=False):
    r = lax.broadcasted_iota(jnp.int32, (n, n), 0)
    c = lax.broadcasted_iota(jnp.int32, (n, n), 1)
    if upper:
        r, c = c, r
    return jnp.where((c < r) if strict else (c <= r), 1.0, 0.0).astype(bf16)


def _dot_exact_lhs(a_exact_bf16, b):
    b0, b1, b2 = _split3(b)
    return _dot(a_exact_bf16, b0) + _dot(a_exact_bf16, b1) + _dot(a_exact_bf16, b2)


def _log_sigmoid(x):
    return jnp.minimum(x, 0.0) - jnp.log1p(jnp.exp(-jnp.abs(x)))


def _fcum_kernel(fr_ref, bias_ref, logf_ref, f_ref, ft_ref, carry):
    @pl.when(pl.program_id(1) == 0)
    def _():
        carry[...] = jnp.zeros_like(carry)

    n = fr_ref.shape[0]
    logf = _log_sigmoid(fr_ref[...] + bias_ref[...])
    logf_ref[...] = logf
    f = _dot_exact_lhs(_tri(n), logf) + carry[0:1, :]
    f_ref[...] = f
    ft_ref[0] = f.T[:8, :]
    carry[0:1, :] = f[n - 1:n, :]


def _fcum(proj, col, bias, b, l, chunk):
    nc = l // chunk
    return pl.pallas_call(
        _fcum_kernel,
        out_shape=(jax.ShapeDtypeStruct((b * l, LANES), f32),
                   jax.ShapeDtypeStruct((b * l, LANES), f32),
                   jax.ShapeDtypeStruct((b, 8, l), f32)),
        grid=(b, nc),
        in_specs=[pl.BlockSpec((chunk, LANES), lambda i, j: (i * nc + j, col)),
                  pl.BlockSpec((1, LANES), lambda i, j: (0, 0))],
        out_specs=(pl.BlockSpec((chunk, LANES), lambda i, j: (i * nc + j, 0)),
                   pl.BlockSpec((chunk, LANES), lambda i, j: (i * nc + j, 0)),
                   pl.BlockSpec((1, 8, chunk), lambda i, j: (i, 0, j))),
        scratch_shapes=[pltpu.VMEM((8, LANES), f32)],
        compiler_params=_cparams("arbitrary", "arbitrary"),
        name="fcum",
    )(proj, bias)


N_FPARTS = 3


def _fox_select_mats():
    rows = jnp.arange(N_FPARTS * LANES)[None, :, None]
    cols = jnp.arange(FOX_HEAD_DIM)[None, None, :]
    head = jnp.arange(FOX_HEADS)[:, None, None]
    hit = (rows % LANES == head)
    sq = jnp.where(hit & (cols == rows // LANES), 1.0, 0.0)
    sk = jnp.where(hit & (cols == N_FPARTS + rows // LANES), -1.0, 0.0)
    return sq.astype(bf16), sk.astype(bf16)


def _fox_prep_kernel(q_ref, k_ref, v_ref, f_ref, sq_ref, sk_ref, qa_ref, ka_ref, vt_ref, *, scale):
    tm = q_ref.shape[0]
    hd = FOX_HEAD_DIM
    fcat = jnp.concatenate(_split3(f_ref[...]), axis=1)
    lane = lax.broadcasted_iota(jnp.int32, (tm, hd), 1)
    ones_q = jnp.where((lane >= N_FPARTS) & (lane < 2 * N_FPARTS), 1.0, 0.0)
    ones_k = jnp.where(lane < N_FPARTS, 1.0, 0.0)
    vt = v_ref[...].T
    for h in range(FOX_HEADS):
        cols = slice(h * hd, (h + 1) * hd)
        eq = _dot(fcat, sq_ref[h]) + ones_q
        ek = _dot(fcat, sk_ref[h]) + ones_k
        qa_ref[h] = jnp.concatenate([q_ref[:, cols] * scale, eq], axis=1).astype(bf16)
        ka_ref[h] = jnp.concatenate([k_ref[:, cols], ek], axis=1).astype(bf16)
        vt_ref[h] = vt[h * hd:(h + 1) * hd, :].astype(bf16)


def _fox_prep(proj, f, t, tm):
    sq, sk = _fox_select_mats()
    sel_spec = pl.BlockSpec((FOX_HEADS, N_FPARTS * LANES, FOX_HEAD_DIM), lambda i: (0, 0, 0))
    return pl.pallas_call(
        functools.partial(_fox_prep_kernel, scale=FOX_HEAD_DIM ** -0.5),
        out_shape=(jax.ShapeDtypeStruct((FOX_HEADS, t, LANES), bf16),
                   jax.ShapeDtypeStruct((FOX_HEADS, t, LANES), bf16),
                   jax.ShapeDtypeStruct((FOX_HEADS, FOX_HEAD_DIM, t), bf16)),
        grid=(t // tm,),
        in_specs=[pl.BlockSpec((tm, FOX_INNER), lambda i: (i, 0)),
                  pl.BlockSpec((tm, FOX_INNER), lambda i: (i, 1)),
                  pl.BlockSpec((tm, FOX_INNER), lambda i: (i, 2)),
                  pl.BlockSpec((tm, LANES), lambda i: (i, 0)),
                  sel_spec, sel_spec],
        out_specs=(pl.BlockSpec((FOX_HEADS, tm, LANES), lambda i: (0, i, 0)),
                   pl.BlockSpec((FOX_HEADS, tm, LANES), lambda i: (0, i, 0)),
                   pl.BlockSpec((FOX_HEADS, FOX_HEAD_DIM, tm), lambda i: (0, 0, i))),
        compiler_params=_cparams("parallel"),
        name="fox_prep",
    )(proj, proj, proj, f, sq, sk)


def _fox_kernel(qa_ref, ka_ref, vt_ref, o_ref, m_scr, l_scr, acc_scr):
    qi = pl.program_id(2)
    ki = pl.program_id(3)
    tq = qa_ref.shape[1]
    tk = ka_ref.shape[1]

    @pl.when(ki == 0)
    def _():
        m_scr[...] = jnp.full_like(m_scr, -jnp.inf)
        l_scr[...] = jnp.zeros_like(l_scr)
        acc_scr[...] = jnp.zeros_like(acc_scr)

    def update(masked):
        for hh in range(2):
            st = lax.dot_general(ka_ref[hh], qa_ref[hh], (((1,), (1,)), ((), ())),
                                 preferred_element_type=f32)
            if masked:
                kpos = lax.broadcasted_iota(jnp.int32, (tk, tq), 0)
                qpos = lax.broadcasted_iota(jnp.int32, (tk, tq), 1)
                st = jnp.where(kpos <= qpos, st, -jnp.inf)
            m_old = m_scr[hh]
            m_new = jnp.maximum(m_old, jnp.max(st, axis=0, keepdims=True))
            alpha = jnp.exp(m_old - m_new)
            p = jnp.exp(st - m_new)
            l_scr[hh] = alpha * l_scr[hh] + jnp.sum(p, axis=0, keepdims=True)
            acc_scr[hh] = alpha * acc_scr[hh] + _dot(vt_ref[hh], p.astype(bf16))
            m_scr[hh] = m_new

    @pl.when(ki < qi)
    def _():
        update(False)

    @pl.when(ki == qi)
    def _():
        update(True)
        ot = jnp.concatenate([acc_scr[0] / l_scr[0], acc_scr[1] / l_scr[1]], axis=0)
        o_ref[...] = ot.T


def _fox_prompt(proj, f, b, l, tq):
    nq = l // tq
    n_hp = FOX_HEADS // 2
    qa, ka, vt = _fox_prep(proj, f, b * l, tq)
    return pl.pallas_call(
        _fox_kernel,
        out_shape=jax.ShapeDtypeStruct((b * l, FOX_INNER), f32),
        grid=(b, n_hp, nq, nq),
        in_specs=[
            pl.BlockSpec((2, tq, LANES), lambda i, h, qi, ki: (h, i * nq + qi, 0)),
            pl.BlockSpec((2, tq, LANES), lambda i, h, qi, ki: (h, i * nq + jnp.minimum(ki, qi), 0)),
            pl.BlockSpec((2, FOX_HEAD_DIM, tq), lambda i, h, qi, ki: (h, 0, i * nq + jnp.minimum(ki, qi))),
        ],
        out_specs=pl.BlockSpec((tq, LANES), lambda i, h, qi, ki: (i * nq + qi, h)),
        scratch_shapes=[pltpu.VMEM((2, 1, tq), f32), pltpu.VMEM((2, 1, tq), f32),
                        pltpu.VMEM((2, FOX_HEAD_DIM, tq), f32)],
        compiler_params=_cparams("parallel", "parallel", "arbitrary", "arbitrary"),
        name="fox_prompt",
    )(qa, ka, vt)


PAGE = 128
PAGES_PER_STEP = 8
HQ = FOX_HEADS * 8


def _fox_decode_kernel(pt_ref, *refs, n_pages, scale):
    pps = PAGES_PER_STEP
    k_refs, v_refs, lf_refs = refs[:pps], refs[pps:2 * pps], refs[2 * pps:3 * pps]
    (q_ref, kn_ref, vn_ref, fr_ref, bias_ref, o_ref, logf_ref,
     s_scr, qbd_scr, acc_scr, m_scr, l_scr, car_scr, fq_scr) = refs[3 * pps:]
    ph = pl.program_id(1)
    c = pl.program_id(2)
    last_c = pl.num_programs(2) - 1
    nq = q_ref.shape[0]
    past = n_pages * PAGE
    row = lax.broadcasted_iota(jnp.int32, (LANES, LANES), 0)
    lane = lax.broadcasted_iota(jnp.int32, (LANES, LANES), 1)
    head_to_cols = jnp.where((lane // nq == row) & (lane < HQ), 1.0, 0.0).astype(bf16)
    r8 = lax.broadcasted_iota(jnp.int32, (nq, LANES), 0)
    l8 = lax.broadcasted_iota(jnp.int32, (nq, LANES), 1)

    def expand_heads(x):
        x0, x1, x2 = _split3(x)
        return _dot(x0, head_to_cols) + _dot(x1, head_to_cols) + _dot(x2, head_to_cols)

    def pad_page(x, fill=0.0):
        return jnp.concatenate([x, jnp.full((PAGE - x.shape[0], x.shape[1]), fill, x.dtype)], axis=0)

    @pl.when((ph == 0) & (c == 0))
    def _():
        q_t = pad_page(q_ref[...] * scale).T
        query_to_cols = jnp.where((lane % nq == row) & (row < nq) & (lane < HQ), 1.0, 0.0).astype(bf16)
        q_exp = _dot(q_t.astype(bf16), query_to_cols)
        r5 = lax.broadcasted_iota(jnp.int32, (FOX_INNER, LANES), 0)
        l5 = lax.broadcasted_iota(jnp.int32, (FOX_INNER, LANES), 1)
        qbd_scr[...] = jnp.where(r5 // FOX_HEAD_DIM == l5 // nq, q_exp, 0.0).astype(bf16)
        m_scr[...] = jnp.full_like(m_scr, -jnp.inf)
        car_scr[...] = jnp.zeros_like(car_scr)

    @pl.when(ph == 0)
    def _():
        tri = _tri(PAGE)
        for i in range(pps):
            lf = lf_refs[i][0]
            lf = jnp.concatenate([lf, jnp.zeros((PAGE, LANES - lf.shape[1]), f32)], axis=1)
            cum = _dot_exact_lhs(tri, lf) + car_scr[0:1, :]
            car_scr[0:1, :] = cum[PAGE - 1:PAGE, :]
            s = _dot(k_refs[i][0].astype(bf16), qbd_scr[...]) - expand_heads(cum)
            s_scr[pl.ds(pl.multiple_of((c * pps + i) * PAGE, PAGE), PAGE), :] = s
            m_scr[0:1, :] = jnp.maximum(m_scr[0:1, :], jnp.max(s, axis=0, keepdims=True))

    @pl.when((ph == 0) & (c == last_c))
    def _():
        logf_new = _log_sigmoid(fr_ref[...] + bias_ref[...])
        logf_ref[...] = logf_new
        run = car_scr[0:1, :]
        f_rows = []
        for jrow in range(nq):
            run = run + logf_new[jrow:jrow + 1, :]
            f_rows.append(run)
        f_exp = expand_heads(jnp.concatenate(f_rows, axis=0))
        fq_scr[0:1, :] = jnp.sum(jnp.where(l8 % nq == r8, f_exp, 0.0), axis=0, keepdims=True)
        s_new = _dot(pad_page(kn_ref[...]).astype(bf16), qbd_scr[...])[:nq] - f_exp
        s_new = jnp.where(r8 <= l8 % nq, s_new, -jnp.inf)
        s_scr[past:past + PAGE, :] = pad_page(s_new, -jnp.inf)
        m_scr[0:1, :] = jnp.maximum(m_scr[0:1, :], jnp.max(s_new, axis=0, keepdims=True))

    @pl.when((ph == 1) & (c == 0))
    def _():
        l_scr[...] = jnp.zeros_like(l_scr)
        acc_scr[...] = jnp.zeros_like(acc_scr)

    def accumulate(s, v_bf16):
        fq = fq_scr[0:1, :]
        p = jnp.exp((s + fq) - (m_scr[0:1, :] + fq))
        l_scr[0:1, :] += jnp.sum(p, axis=0, keepdims=True)
        acc_scr[...] += _dot(p.T.astype(bf16), v_bf16)

    @pl.when(ph == 1)
    def _():
        for i in range(pps):
            s = s_scr[pl.ds(pl.multiple_of((c * pps + i) * PAGE, PAGE), PAGE), :]
            accumulate(s, v_refs[i][0].astype(bf16))

    @pl.when((ph == 1) & (c == last_c))
    def _():
        accumulate(s_scr[past:past + PAGE, :], pad_page(vn_ref[...]).astype(bf16))
        l_col = jnp.sum(jnp.where(row == lane, jnp.broadcast_to(l_scr[0:1, :], (LANES, LANES)), 0.0),
                        axis=1, keepdims=True)
        o_full = acc_scr[...] / l_col
        hd = FOX_HEAD_DIM
        o_ref[...] = jnp.concatenate([o_full[h * nq:(h + 1) * nq, h * hd:(h + 1) * hd] for h in range(FOX_HEADS)],
                                     axis=1)


def _fox_decode(proj, fcol, bias, cache_k, cache_v, cache_logf, page_table):
    n_seq, n_pages = page_table.shape
    nq = proj.shape[0] // n_seq
    n_pool = cache_k.shape[0]
    pps = PAGES_PER_STEP
    n_chunks = n_pages // pps
    ck = cache_k.reshape(n_pool, PAGE, FOX_INNER)
    cv = cache_v.reshape(n_pool, PAGE, FOX_INNER)

    def k_map(i):
        return lambda s, ph, c, pt: (pt[s * n_pages + jnp.where(ph == 0, c, n_chunks - 1) * pps + i], 0, 0)

    def v_map(i):
        return lambda s, ph, c, pt: (pt[s * n_pages + jnp.where(ph == 0, 0, c) * pps + i], 0, 0)

    in_specs = ([pl.BlockSpec((1, PAGE, FOX_INNER), k_map(i)) for i in range(pps)]
                + [pl.BlockSpec((1, PAGE, FOX_INNER), v_map(i)) for i in range(pps)]
                + [pl.BlockSpec((1, PAGE, FOX_HEADS), k_map(i)) for i in range(pps)]
                + [pl.BlockSpec((nq, FOX_INNER), lambda s, ph, c, pt: (s, 0)),
                   pl.BlockSpec((nq, FOX_INNER), lambda s, ph, c, pt: (s, 1)),
                   pl.BlockSpec((nq, FOX_INNER), lambda s, ph, c, pt: (s, 2)),
                   pl.BlockSpec((nq, LANES), lambda s, ph, c, pt: (s, fcol)),
                   pl.BlockSpec((1, LANES), lambda s, ph, c, pt: (0, 0))])
    grid_spec = pltpu.PrefetchScalarGridSpec(
        num_scalar_prefetch=1,
        grid=(n_seq, 2, n_chunks),
        in_specs=in_specs,
        out_specs=(pl.BlockSpec((nq, FOX_INNER), lambda s, ph, c, pt: (s, 0)),
                   pl.BlockSpec((nq, LANES), lambda s, ph, c, pt: (s, 0))),
        scratch_shapes=[pltpu.VMEM((n_pages * PAGE + PAGE, LANES), f32),
                        pltpu.VMEM((FOX_INNER, LANES), bf16),
                        pltpu.VMEM((LANES, FOX_INNER), f32),
                        pltpu.VMEM((8, LANES), f32), pltpu.VMEM((8, LANES), f32),
                        pltpu.VMEM((8, LANES), f32), pltpu.VMEM((8, LANES), f32)],
    )
    return pl.pallas_call(
        functools.partial(_fox_decode_kernel, n_pages=n_pages, scale=FOX_HEAD_DIM ** -0.5),
        out_shape=(jax.ShapeDtypeStruct((n_seq * nq, FOX_INNER), f32),
                   jax.ShapeDtypeStruct((n_seq * nq, LANES), f32)),
        grid_spec=grid_spec,
        compiler_params=_cparams("arbitrary", "arbitrary", "arbitrary"),
        name="fox_decode",
    )(page_table.reshape(-1), *([ck] * pps), *([cv] * pps), *([cache_logf] * pps), proj, proj, proj, proj, bias)


CONV_TAIL = 8
COL0_QKV = 0
COL0_XS = 3 * GDN_INNER
COL0_ZS = COL0_XS + SSD_INNER
COL0_ZG = COL0_ZS + SSD_INNER
COL0_BC = COL0_ZG + GDN_INNER
COL0_SM = COL0_BC + 2 * SSD_GROUPS * SSD_STATE
assert COL0_SM + LANES == IN0_PAD


def _softplus(x):
    return jnp.maximum(x, 0.0) + jnp.log1p(jnp.exp(-jnp.abs(x)))


def _silu(x):
    return x * jax.nn.sigmoid(x)


def _conv_silu(ubuf, u_refs, tail_ref, cw_ref, cb_ref, first, rows, lpad):
    @pl.when(first)
    def _():
        ubuf[0:CONV_TAIL, :] = tail_ref[0]

    @pl.when(jnp.logical_not(first))
    def _():
        ubuf[0:CONV_TAIL, :] = ubuf[rows:rows + CONV_TAIL, :]

    col = 0
    for u_ref in u_refs:
        ubuf[CONV_TAIL:CONV_TAIL + rows, col:col + u_ref.shape[1]] = u_ref[...]
        col += u_ref.shape[1]
    if lpad > rows:
        ubuf[CONV_TAIL + rows:, :] = jnp.zeros((lpad - rows, ubuf.shape[1]), f32)
    acc = cb_ref[...]
    for j in range(CONV_K):
        off = CONV_TAIL - (CONV_K - 1) + j
        acc = acc + cw_ref[j:j + 1, :] * ubuf[off:off + lpad, :]
    return _silu(acc)


def _pad_rows(x, lpad):
    rows = x.shape[0]
    if lpad == rows:
        return x
    return jnp.concatenate([x, jnp.zeros((lpad - rows, x.shape[1]), x.dtype)], axis=0)


def _head_scalars(sm_ref, bias_ref, coef_ref, rows, lpad):
    raw = _pad_rows(sm_ref[...], lpad)
    valid = lax.broadcasted_iota(jnp.int32, (lpad, LANES), 0) < rows
    sp = jnp.where(valid, _softplus(raw + bias_ref[...]), 0.0)
    a = sp * coef_ref[...]
    cum = _dot_exact_lhs(_tri(lpad), a)
    return raw, valid, sp, cum


def _ssd_kernel(ux_ref, ubc_ref, z_ref, sm_ref, tail_ref, h0_ref, cw_ref, cb_ref, bias_ref, coef_ref, dd_ref, nw_ref,
                y_ref, hout_ref, ubuf, *, rows, lpad):
    j = pl.program_id(1)
    n, p = SSD_STATE, SSD_HEAD_DIM
    xbc = _conv_silu(ubuf, (ux_ref, ubc_ref), tail_ref, cw_ref, cb_ref, j == 0, rows, lpad)

    @pl.when(j == 0)
    def _():
        hout_ref[...] = h0_ref[...]

    _, _, dt, acum = _head_scalars(sm_ref, bias_ref, coef_ref, rows, lpad)
    acum_t = acum.T
    dt_t = dt.T
    xs = xbc[:, :SSD_INNER]
    xs_t = xs.T
    r = lax.broadcasted_iota(jnp.int32, (lpad, lpad), 0)
    c = lax.broadcasted_iota(jnp.int32, (lpad, lpad), 1)
    causal = c <= r
    heads_per_group = SSD_HEADS // SSD_GROUPS
    ys = []
    for g in range(SSD_GROUPS):
        bm = xbc[:, SSD_INNER + g * n:SSD_INNER + (g + 1) * n]
        cm = xbc[:, SSD_INNER + SSD_GROUPS * n + g * n:SSD_INNER + SSD_GROUPS * n + (g + 1) * n]
        cb = lax.dot_general(cm.astype(bf16), bm.astype(bf16), (((1,), (1,)), ((), ())),
                             preferred_element_type=f32)
        for hh in range(heads_per_group):
            h = g * heads_per_group + hh
            a_col = acum[:, h:h + 1]
            a_row = acum_t[h:h + 1, :]
            a_last = acum[lpad - 1:lpad, h:h + 1]
            decay = jnp.exp(jnp.where(causal, a_col - a_row, -jnp.inf))
            x_h = xs[:, h * p:(h + 1) * p]
            xdt = (x_h * dt[:, h:h + 1]).astype(bf16)
            hst = hout_ref[0, h]
            y = _dot((cb * decay).astype(bf16), xdt)
            y = y + lax.dot_general((cm * jnp.exp(a_col)).astype(bf16), hst.astype(bf16),
                                    (((1,), (1,)), ((), ())), preferred_element_type=f32)
            xdt_t = (xs_t[h * p:(h + 1) * p, :] * dt_t[h:h + 1, :]).astype(bf16)
            upd = _dot(xdt_t, (bm * jnp.exp(a_last - a_col)).astype(bf16))
            hout_ref[0, h] = hst * jnp.exp(a_last) + upd
            ys.append(y + dd_ref[:, h * p:(h + 1) * p] * x_h)
    y = jnp.concatenate(ys, axis=1)[:rows]
    gated = y * _silu(z_ref[...])
    ms = jnp.mean(gated * gated, axis=-1, keepdims=True)
    y_ref[...] = gated * lax.rsqrt(ms + EPS) * nw_ref[...]


def _ssd(proj, tail, h0, prm, b, l, rows, lpad):
    nc = l // rows
    bc_w = 2 * SSD_GROUPS * SSD_STATE
    cw, cb, bias, coef, dd, nw = prm
    c2 = lambda i, j: (0, 0)
    return pl.pallas_call(
        functools.partial(_ssd_kernel, rows=rows, lpad=lpad),
        out_shape=(jax.ShapeDtypeStruct((b * l, SSD_INNER), f32),
                   jax.ShapeDtypeStruct((b, SSD_HEADS, SSD_HEAD_DIM, SSD_STATE), f32)),
        grid=(b, nc),
        in_specs=[pl.BlockSpec((rows, SSD_INNER), lambda i, j: (i * nc + j, COL0_XS // SSD_INNER)),
                  pl.BlockSpec((rows, bc_w), lambda i, j: (i * nc + j, COL0_BC // bc_w)),
                  pl.BlockSpec((rows, SSD_INNER), lambda i, j: (i * nc + j, COL0_ZS // SSD_INNER)),
                  pl.BlockSpec((rows, LANES), lambda i, j: (i * nc + j, COL0_SM // LANES)),
                  pl.BlockSpec((1, CONV_TAIL, SSD_CONV_CH), lambda i, j: (i, 0, 0)),
                  pl.BlockSpec((1, SSD_HEADS, SSD_HEAD_DIM, SSD_STATE), lambda i, j: (i, 0, 0, 0)),
                  pl.BlockSpec((CONV_K, SSD_CONV_CH), c2), pl.BlockSpec((1, SSD_CONV_CH), c2),
                  pl.BlockSpec((1, LANES), c2), pl.BlockSpec((1, LANES), c2),
                  pl.BlockSpec((1, SSD_INNER), c2), pl.BlockSpec((1, SSD_INNER), c2)],
        out_specs=(pl.BlockSpec((rows, SSD_INNER), lambda i, j: (i * nc + j, 0)),
                   pl.BlockSpec((1, SSD_HEADS, SSD_HEAD_DIM, SSD_STATE), lambda i, j: (i, 0, 0, 0))),
        scratch_shapes=[pltpu.VMEM((CONV_TAIL + lpad, SSD_CONV_CH), f32)],
        compiler_params=_cparams("arbitrary", "arbitrary"),
        name="ssd",
    )(proj, proj, proj, proj, tail, h0, cw, cb, bias, coef, dd, nw)


def _dot3(a, b):
    a_hi = a.astype(bf16)
    a_lo = (a - a_hi.astype(f32)).astype(bf16)
    b_hi = b.astype(bf16)
    b_lo = (b - b_hi.astype(f32)).astype(bf16)
    return _dot(a_hi, b_hi) + _dot(a_lo, b_hi) + _dot(a_hi, b_lo)


INV_BASE = 16


def _unit_lower_inverse(m, n):
    r = lax.broadcasted_iota(jnp.int32, (n, n), 0)
    c = lax.broadcasted_iota(jnp.int32, (n, n), 1)
    nb = min(INV_BASE, n)
    d = jnp.where((r // nb) == (c // nb), m, 0.0)
    inv = jnp.where(r == c, 1.0, 0.0) - d
    pw = d
    size = 2
    while size < nb:
        pw = _dot3(pw, pw)
        inv = inv + _dot3(inv, pw)
        size *= 2
    s = nb
    while s < n:
        lower_left = ((r // (2 * s)) == (c // (2 * s))) & ((r // s) % 2 == 1) & ((c // s) % 2 == 0)
        inv = inv - _dot3(_dot3(inv, jnp.where(lower_left, m, 0.0)), inv)
        s *= 2
    return inv


def _l2n(x):
    return x * lax.rsqrt(jnp.sum(x * x, axis=-1, keepdims=True) + EPS)


GDN_A_LANE = SSD_HEADS
GDN_B_LANE = SSD_HEADS + GDN_HEADS


def _gdn_kernel(u_ref, z_ref, sm_ref, tail_ref, s0_ref, cw_ref, cb_ref, bias_ref, coef_ref, nw_ref,
                o_ref, sout_ref, ubuf, *, rows, lpad):
    j = pl.program_id(1)
    dk = GDN_HEAD_DIM
    qkv = _conv_silu(ubuf, (u_ref,), tail_ref, cw_ref, cb_ref, j == 0, rows, lpad)

    @pl.when(j == 0)
    def _():
        sout_ref[...] = s0_ref[...]

    raw, valid, _, gcum = _head_scalars(sm_ref, bias_ref, coef_ref, rows, lpad)
    beta_all = jnp.where(valid, jax.nn.sigmoid(raw), 0.0)
    gcum_t = gcum.T
    r = lax.broadcasted_iota(jnp.int32, (lpad, lpad), 0)
    c = lax.broadcasted_iota(jnp.int32, (lpad, lpad), 1)
    outs = []
    for h in range(GDN_HEADS):
        qh = _l2n(qkv[:, h * dk:(h + 1) * dk]) * dk ** -0.5
        kh = _l2n(qkv[:, GDN_INNER + h * dk:GDN_INNER + (h + 1) * dk])
        vh = qkv[:, 2 * GDN_INNER + h * dk:2 * GDN_INNER + (h + 1) * dk]
        lane = GDN_A_LANE + h
        g_col = gcum[:, lane:lane + 1]
        g_row = gcum_t[lane:lane + 1, :]
        g_last = gcum[lpad - 1:lpad, lane:lane + 1]
        beta = beta_all[:, GDN_B_LANE + h:GDN_B_LANE + h + 1]
        decay = jnp.exp(jnp.where(c <= r, g_col - g_row, -jnp.inf))
        kb = kh * beta
        kh_b = kh.astype(bf16)
        kk = lax.dot_general(kb.astype(bf16), kh_b, (((1,), (1,)), ((), ())), preferred_element_type=f32)
        a_inv = _unit_lower_inverse(jnp.where(c < r, kk * decay, 0.0), lpad)
        a_hi = a_inv.astype(bf16)
        a_lo = (a_inv - a_hi.astype(f32)).astype(bf16)
        vb_b = (vh * beta).astype(bf16)
        kbe_b = (kb * jnp.exp(g_col)).astype(bf16)
        u = _dot(a_hi, vb_b) + _dot(a_lo, vb_b)
        w = _dot(a_hi, kbe_b) + _dot(a_lo, kbe_b)
        attn = lax.dot_general(qh.astype(bf16), kh_b, (((1,), (1,)), ((), ())), preferred_element_type=f32) * decay
        st = sout_ref[0, h]
        st_b = st.astype(bf16)
        v_new = (u - _dot(w.astype(bf16), st_b)).astype(bf16)
        o = _dot((qh * jnp.exp(g_col)).astype(bf16), st_b) + _dot(attn.astype(bf16), v_new)
        ke_t = (kh * jnp.exp(g_last - g_col)).T.astype(bf16)
        sout_ref[0, h] = st * jnp.exp(g_last) + _dot(ke_t, v_new)
        o = o[:rows]
        ms = jnp.mean(o * o, axis=-1, keepdims=True)
        outs.append(o * lax.rsqrt(ms + EPS) * nw_ref[...] * _silu(z_ref[:, h * dk:(h + 1) * dk]))
    o_ref[...] = jnp.concatenate(outs, axis=1)


def _gdn(proj, tail, s0, prm, b, l, rows, lpad):
    nc = l // rows
    cw, cb, bias, coef, nw = prm
    c2 = lambda i, j: (0, 0)
    width = 3 * GDN_INNER
    return pl.pallas_call(
        functools.partial(_gdn_kernel, rows=rows, lpad=lpad),
        out_shape=(jax.ShapeDtypeStruct((b * l, GDN_INNER), f32),
                   jax.ShapeDtypeStruct((b, GDN_HEADS, GDN_HEAD_DIM, GDN_HEAD_DIM), f32)),
        grid=(b, nc),
        in_specs=[pl.BlockSpec((rows, width), lambda i, j: (i * nc + j, COL0_QKV // width)),
                  pl.BlockSpec((rows, GDN_INNER), lambda i, j: (i * nc + j, COL0_ZG // GDN_INNER)),
                  pl.BlockSpec((rows, LANES), lambda i, j: (i * nc + j, COL0_SM // LANES)),
                  pl.BlockSpec((1, CONV_TAIL, width), lambda i, j: (i, 0, 0)),
                  pl.BlockSpec((1, GDN_HEADS, GDN_HEAD_DIM, GDN_HEAD_DIM), lambda i, j: (i, 0, 0, 0)),
                  pl.BlockSpec((CONV_K, width), c2), pl.BlockSpec((1, width), c2),
                  pl.BlockSpec((1, LANES), c2), pl.BlockSpec((1, LANES), c2),
                  pl.BlockSpec((1, GDN_HEAD_DIM), c2)],
        out_specs=(pl.BlockSpec((rows, GDN_INNER), lambda i, j: (i * nc + j, 0)),
                   pl.BlockSpec((1, GDN_HEADS, GDN_HEAD_DIM, GDN_HEAD_DIM), lambda i, j: (i, 0, 0, 0))),
        scratch_shapes=[pltpu.VMEM((CONV_TAIL + lpad, width), f32)],
        compiler_params=_cparams("arbitrary", "arbitrary"),
        name="gdn",
    )(proj, proj, proj, tail, s0, cw, cb, bias, coef, nw)


def _layer0_params(p):
    def lanes(v, off):
        return jnp.zeros((1, LANES), f32).at[0, off:off + v.shape[0]].set(v.astype(f32))

    cw = p['conv_w'].astype(f32)
    cb = p['conv_b'].astype(f32).reshape(1, CONV_CH)
    bias = lanes(p['ssd_dt_bias'], 0) + lanes(p['gdn_dt_bias'], GDN_A_LANE)
    coef = lanes(-jnp.exp(p['ssd_A_log'].astype(f32)), 0) + lanes(-jnp.exp(p['gdn_A_log'].astype(f32)), GDN_A_LANE)
    dd = jnp.repeat(p['ssd_D'].astype(f32), SSD_HEAD_DIM).reshape(1, SSD_INNER)
    return {'ssd': (cw[:, :SSD_CONV_CH], cb[:, :SSD_CONV_CH], bias, coef, dd,
                    p['ssd_norm'].astype(f32).reshape(1, SSD_INNER)),
            'gdn': (cw[:, SSD_CONV_CH:], cb[:, SSD_CONV_CH:], bias, coef,
                    p['gdn_norm'].astype(f32).reshape(1, GDN_HEAD_DIM))}


def _rmsnorm(x, w):
    xf = x.astype(f32)
    y = xf * lax.rsqrt(jnp.mean(xf * xf, axis=-1, keepdims=True) + EPS)
    return (y * w.astype(f32)).astype(x.dtype)


def _l2norm(x):
    xf = x.astype(f32)
    return xf * lax.rsqrt(jnp.sum(xf * xf, axis=-1, keepdims=True) + EPS)


def _causal_conv(u, buf, w, bias):
    l = u.shape[1]
    upad = jnp.concatenate([buf.astype(u.dtype), u], axis=1)
    out = bias + sum(w[j] * upad[:, j:j + l] for j in range(CONV_K))
    return jax.nn.silu(out), upad[:, upad.shape[1] - (CONV_K - 1):]


def _ssd_chunked(x, dt, A, Bm, Cm, h0):
    b, l, h, p = x.shape
    n = Bm.shape[-1]
    q = math.gcd(l, CHUNK)
    c = l // q
    xdt = (x.astype(f32) * dt[..., None]).reshape(b, c, q, h, p)
    acum = jnp.cumsum((dt * A).reshape(b, c, q, h), axis=2)
    Bc = Bm.astype(f32).reshape(b, c, q, h, n)
    Cc = Cm.astype(f32).reshape(b, c, q, h, n)
    causal = jnp.tril(jnp.ones((q, q), bool))[None, None, :, :, None]
    seg = acum[:, :, :, None, :] - acum[:, :, None, :, :]
    decay = jnp.where(causal, jnp.exp(jnp.where(causal, seg, 0.0)), 0.0)
    scores = jnp.einsum('bcihn,bcjhn->bcijh', Cc, Bc) * decay
    y_diag = jnp.einsum('bcijh,bcjhp->bcihp', scores, xdt)
    to_end = jnp.exp(acum[:, :, -1:, :] - acum)
    chunk_states = jnp.einsum('bcjhn,bcjhp->bchpn', Bc * to_end[..., None], xdt)
    chunk_decay = jnp.exp(acum[:, :, -1, :])

    def step(state, inp):
        st, dec = inp
        return state * dec[..., None, None] + st, state

    h_final, h_enter = lax.scan(step, h0.astype(f32),
                                (jnp.moveaxis(chunk_states, 1, 0), jnp.moveaxis(chunk_decay, 1, 0)))
    h_enter = jnp.moveaxis(h_enter, 0, 1)
    y_off = jnp.einsum('bcihn,bchpn->bcihp', Cc * jnp.exp(acum)[..., None], h_enter)
    return (y_diag + y_off).reshape(b, l, h, p), h_final


def _gdn_chunked(q, k, v, g, beta, s0):
    b, l, h, dk = q.shape
    dv = v.shape[-1]
    Q = math.gcd(l, CHUNK)
    c = l // Q

    def blk(t):
        return t.astype(f32).reshape(b, c, Q, h, t.shape[-1]).transpose(0, 1, 3, 2, 4)

    qc = blk(q) * dk ** -0.5
    kc = blk(k)
    vc = blk(v)
    gc = jnp.cumsum(g.astype(f32).reshape(b, c, Q, h).transpose(0, 1, 3, 2), axis=-1)
    bc = beta.astype(f32).reshape(b, c, Q, h).transpose(0, 1, 3, 2)
    incl = jnp.tril(jnp.ones((Q, Q), bool))
    strict = jnp.tril(jnp.ones((Q, Q), bool), k=-1)
    diff = gc[..., :, None] - gc[..., None, :]
    decay = jnp.where(incl, jnp.exp(jnp.where(incl, diff, 0.0)), 0.0)
    kb = kc * bc[..., None]
    vb = vc * bc[..., None]
    m = jnp.where(strict, jnp.einsum('bchid,bchjd->bchij', kb, kc) * decay, 0.0)
    t_sys = jnp.eye(Q, dtype=f32) + m
    rhs = jnp.concatenate([vb, kb * jnp.exp(gc)[..., None]], axis=-1)
    sol = lax.linalg.triangular_solve(t_sys, rhs, left_side=True, lower=True)
    u = sol[..., :dv]
    w = sol[..., dv:]
    attn = jnp.einsum('bchid,bchjd->bchij', qc, kc) * decay
    qd = qc * jnp.exp(gc)[..., None]
    ke = kc * jnp.exp(gc[..., -1:] - gc)[..., None]
    dl = jnp.exp(gc[..., -1])

    def step(S, inp):
        u_c, w_c, qd_c, ke_c, a_c, dl_c = inp
        v_new = u_c - jnp.einsum('bhid,bhde->bhie', w_c, S)
        o = jnp.einsum('bhid,bhde->bhie', qd_c, S) + jnp.einsum('bhij,bhje->bhie', a_c, v_new)
        S = S * dl_c[..., None, None] + jnp.einsum('bhid,bhie->bhde', ke_c, v_new)
        return S, o

    xs = tuple(jnp.moveaxis(t, 1, 0) for t in (u, w, qd, ke, attn, dl))
    s_final, o = lax.scan(step, s0.astype(f32), xs)
    return jnp.transpose(o, (1, 0, 3, 2, 4)).reshape(b, l, h, dv), s_final


def _fox_attention(q, k, v, logf, past):
    b, l, h, d = q.shape
    if past is None:
        k_all, v_all, logf_all = k.astype(f32), v.astype(f32), logf
    else:
        k_past, v_past, logf_past = past
        k_all = jnp.concatenate([k_past.astype(f32), k.astype(f32)], axis=1)
        v_all = jnp.concatenate([v_past.astype(f32), v.astype(f32)], axis=1)
        logf_all = jnp.concatenate([logf_past.astype(f32), logf], axis=1)
    n_keys = k_all.shape[1]
    offset = n_keys - l
    F = jnp.cumsum(logf_all, axis=1)
    Fk = jnp.moveaxis(F, 1, 2)
    Fq = F[:, offset:]
    kpos = jnp.arange(n_keys)
    qb = math.gcd(l, Q_BLOCK)
    nb = l // qb
    scale = d ** -0.5

    def block(args):
        q_i, fq_i, qpos_i = args
        s = jnp.einsum('bqhd,bkhd->bhqk', q_i.astype(f32), k_all) * scale
        s = s + jnp.moveaxis(fq_i, 1, 2)[..., None] - Fk[:, :, None, :]
        s = jnp.where(qpos_i[:, None] >= kpos[None, :], s, -jnp.inf)
        pr = jax.nn.softmax(s, axis=-1)
        return jnp.einsum('bhqk,bkhd->bqhd', pr, v_all)

    q_blocks = jnp.moveaxis(q.reshape(b, nb, qb, h, d), 1, 0)
    fq_blocks = jnp.moveaxis(Fq.reshape(b, nb, qb, h), 1, 0)
    qpos_blocks = (offset + jnp.arange(l)).reshape(nb, qb)
    o = lax.map(block, (q_blocks, fq_blocks, qpos_blocks))
    return jnp.moveaxis(o, 0, 1).reshape(b, l, h, d)


def _s5_scan(u, A_re, A_im, log_step, B_re, B_im, C_re, C_im, Dd, x0_re, x0_im):
    b, l, _ = u.shape
    uu = u.astype(f32).reshape(b, l, S5_GROUPS, S5_CH)
    ar = A_re.astype(f32)
    ai = A_im.astype(f32)
    step = jnp.exp(log_step.astype(f32))[:, None]
    mag = jnp.exp(ar * step)
    lb_re = mag * jnp.cos(ai * step)
    lb_im = mag * jnp.sin(ai * step)
    den = ar * ar + ai * ai
    nr = lb_re - 1.0
    cr = (nr * ar + lb_im * ai) / den
    ci = (lb_im * ar - nr * ai) / den
    bb_re = cr[..., None] * B_re - ci[..., None] * B_im
    bb_im = cr[..., None] * B_im + ci[..., None] * B_re
    bu_re = jnp.einsum('gph,blgh->blgp', bb_re, uu)
    bu_im = jnp.einsum('gph,blgh->blgp', bb_im, uu)
    shp = bu_re.shape
    a_re = jnp.broadcast_to(lb_re, shp)
    a_im = jnp.broadcast_to(lb_im, shp)

    def combine(e1, e2):
        a1r, a1i, b1r, b1i = e1
        a2r, a2i, b2r, b2i = e2
        return (a2r * a1r - a2i * a1i, a2r * a1i + a2i * a1r,
                a2r * b1r - a2i * b1i + b2r, a2r * b1i + a2i * b1r + b2i)

    pr, pi, xr, xi = lax.associative_scan(combine, (a_re, a_im, bu_re, bu_im), axis=1)
    x0r = x0_re.astype(f32)[:, None]
    x0i = x0_im.astype(f32)[:, None]
    xr = xr + pr * x0r - pi * x0i
    xi = xi + pr * x0i + pi * x0r
    y = (jnp.einsum('ghp,blgp->blgh', C_re, xr) - jnp.einsum('ghp,blgp->blgh', C_im, xi)
         + Dd * uu)
    return y.reshape(b, l, S5_INNER), xr[:, -1], xi[:, -1]


def _ada_kernel(c_ref, w_ref, b_ref, o_ref):
    o_ref[...] = _dot(_silu(c_ref[...]).astype(bf16), w_ref[...].astype(bf16)) + b_ref[...]


def _ada(c, w_ada, b_ada, tn):
    rows, d = c.shape
    n = w_ada.shape[1]
    return pl.pallas_call(
        _ada_kernel,
        out_shape=jax.ShapeDtypeStruct((rows, n), f32),
        grid=(n // tn,),
        in_specs=[pl.BlockSpec((rows, d), lambda j: (0, 0)),
                  pl.BlockSpec((d, tn), lambda j: (0, j)),
                  pl.BlockSpec((1, tn), lambda j: (0, j))],
        out_specs=pl.BlockSpec((rows, tn), lambda j: (0, j)),
        compiler_params=_cparams("parallel"),
        name="ada",
    )(c, w_ada, b_ada.reshape(1, n))


def _mods(mod, l, tm):
    parts = jnp.split(mod, 6, axis=-1)
    if l % tm == 0:
        return [p[:, None, :] for p in parts]
    b = mod.shape[0]
    return [jnp.repeat(p, l, axis=0).reshape((b * l) // tm, tm, D_MODEL) for p in parts]


def _run_trunk(x, mod0, mod1, conv_buf, ssd_h0, gdn_s0, past, s5_re0, s5_im0, p, tm):
    b, l, d = x.shape
    t = b * l
    x2 = x.reshape(t, d)

    sh1, sc1, g1, sh2, sc2, g2 = _mods(mod0, l, tm)
    proj0 = _inproj(x2, p['norm_mix0'], sc1, sh1, p['w_in0'], tm, 1152)
    tail = jnp.pad(conv_buf.astype(f32), ((0, 0), (CONV_TAIL - (CONV_K - 1), 0), (0, 0)))
    ssd_rows, ssd_lpad, gdn_rows, gdn_lpad = (256, 256, 128, 128) if l % 256 == 0 else (l, 128, l, 128)
    y_ssd, ssd_new = _ssd(proj0, tail[..., :SSD_CONV_CH], ssd_h0.astype(f32), p['l0']['ssd'], b, l, ssd_rows, ssd_lpad)
    o_gdn, gdn_new = _gdn(proj0, tail[..., SSD_CONV_CH:], gdn_s0.astype(f32), p['l0']['gdn'], b, l, gdn_rows, gdn_lpad)
    last = proj0.reshape(b, l, IN0_PAD)[:, l - (CONV_K - 1):]
    conv_new = jnp.concatenate([last[..., COL0_XS:COL0_XS + SSD_INNER], last[..., COL0_BC:COL0_SM],
                                last[..., COL0_QKV:COL0_QKV + 3 * GDN_INNER]], axis=-1)
    w_out0 = p['w_out0']
    x2 = _outproj([y_ssd, o_gdn], [w_out0[:SSD_INNER], w_out0[SSD_INNER:]], x2, g1, tm)
    x2 = _ffn(x2, p['norm_ffn0'], sc2, sh2, g2, p['ffn_w1'], p['ffn_w3'], p['ffn_w2'], min(tm, 512), 1408)

    sh1, sc1, g1, sh2, sc2, g2 = _mods(mod1, l, tm)
    proj2 = _inproj(x2, p['norm_mix1'], sc1, sh1, p['w_in1'], tm, 1152)
    proj = proj2.reshape(b, l, IN1_PAD)
    q = proj[..., :FOX_INNER].reshape(b, l, FOX_HEADS, FOX_HEAD_DIM)
    k_new = proj[..., FOX_INNER:2 * FOX_INNER].reshape(b, l, FOX_HEADS, FOX_HEAD_DIM)
    v_new = proj[..., 2 * FOX_INNER:3 * FOX_INNER].reshape(b, l, FOX_HEADS, FOX_HEAD_DIM)
    sp = p['s5']
    ucol = (3 * FOX_INNER) // S5_QC
    fcol = (3 * FOX_INNER + S5_INNER) // LANES
    if past is None:
        logf_pad, fcs, fcs_t = _fcum(proj2, fcol, p['fox_f_bias'], b, l, 512)
        logf = logf_pad[:, :FOX_HEADS].reshape(b, l, FOX_HEADS)
        o_fox = _fox_prompt(proj2, fcs, b, l, 512)
        o_s5, s5_re, s5_im = _s5_chain(proj2, ucol, s5_re0.reshape(b, 1, S5_LANES),
                                       s5_im0.reshape(b, 1, S5_LANES), sp, b, l, 64)
    else:
        o_fox, logf_pad = _fox_decode(proj2, fcol, p['fox_f_bias'], *past)
        logf = logf_pad[:, :FOX_HEADS].reshape(b, l, FOX_HEADS)
        o_s5, s5_re, s5_im = _s5_batch(proj2, ucol, s5_re0.reshape(b, S5_LANES),
                                       s5_im0.reshape(b, S5_LANES), sp, b, l)
    s5_re = s5_re.reshape(b, S5_GROUPS, S5_STATE)
    s5_im = s5_im.reshape(b, S5_GROUPS, S5_STATE)
    w_out1 = p['w_out1']
    x2 = _outproj([o_fox, o_s5], [w_out1[:FOX_INNER], w_out1[FOX_INNER:].reshape(S5_QUARTERS, S5_QC, d)], x2, g1, tm)
    y_out = _moe(x2, p['norm_ffn1'], sc2, sh2, g2, p['router_w'], p['moe_w1'], p['moe_w3'], p['moe_w2'],
                 p['norm_final'], tm, 896, 320).reshape(b, l, d)
    return y_out, conv_new, ssd_new, gdn_new, k_new, v_new, logf, s5_re, s5_im


def kernel(x_prompt, x_sample, state_conv0, state_ssd, state_gdn, cache_k, cache_v, cache_logf,
           state_s5_re, state_s5_im, page_table, c_prompt, c_sample,
           ada0_w, ada0_b, norm_mix0, w_in0, conv0_w, conv0_b, ssd_dt_bias, ssd_A_log, ssd_D,
           ssd_norm, gdn_dt_bias, gdn_A_log, gdn_norm, w_out0, norm_ffn0, ffn_w1, ffn_w3, ffn_w2,
           ada1_w, ada1_b, norm_mix1, w_in1, fox_f_bias, s5_A_re, s5_A_im, s5_log_step,
           s5_B_re, s5_B_im, s5_C_re, s5_C_im, s5_D, glu_w, glu_b, w_out1, norm_ffn1,
           router_w, moe_w1, moe_w3, moe_w2, norm_final):
    d = D_MODEL
    c0 = CONV_CH
    w_in0p = jnp.concatenate([
        w_in0[:, SSD_CONV_CH:c0],
        w_in0[:, :SSD_INNER],
        w_in0[:, c0:c0 + SSD_INNER],
        w_in0[:, c0 + SSD_INNER + SSD_HEADS:c0 + SSD_INNER + SSD_HEADS + GDN_INNER],
        w_in0[:, SSD_INNER:SSD_CONV_CH],
        w_in0[:, c0 + SSD_INNER:c0 + SSD_INNER + SSD_HEADS],
        w_in0[:, c0 + SSD_INNER + SSD_HEADS + GDN_INNER:],
        jnp.zeros((d, IN0_PAD - w_in0.shape[1]), f32)], axis=1).astype(bf16)
    f0 = 3 * FOX_INNER
    w_in1p = jnp.concatenate([
        w_in1[:, :f0],
        w_in1[:, f0 + FOX_HEADS:],
        w_in1[:, f0:f0 + FOX_HEADS],
        jnp.zeros((d, IN1_PAD - w_in1.shape[1]), f32)], axis=1).astype(bf16)
    router_wp = jnp.concatenate([router_w.astype(f32), jnp.zeros((d, LANES - N_EXPERTS), f32)], axis=1)
    p = {
        'ada0_w': ada0_w, 'ada0_b': ada0_b, 'norm_mix0': norm_mix0.reshape(1, d), 'w_in0': w_in0p,
        'conv0_w': conv0_w, 'conv0_b': conv0_b, 'ssd_dt_bias': ssd_dt_bias, 'ssd_A_log': ssd_A_log,
        'ssd_D': ssd_D, 'ssd_norm': ssd_norm, 'gdn_dt_bias': gdn_dt_bias, 'gdn_A_log': gdn_A_log,
        'gdn_norm': gdn_norm, 'w_out0': w_out0.astype(bf16), 'norm_ffn0': norm_ffn0.reshape(1, d),
        'ffn_w1': ffn_w1.astype(bf16), 'ffn_w3': ffn_w3.astype(bf16), 'ffn_w2': ffn_w2.astype(bf16),
        'ada1_w': ada1_w, 'ada1_b': ada1_b, 'norm_mix1': norm_mix1.reshape(1, d), 'w_in1': w_in1p,
        'fox_f_bias': jnp.concatenate([fox_f_bias.astype(f32), jnp.zeros((LANES - FOX_HEADS,), f32)]).reshape(1, LANES),
        's5_A_re': s5_A_re, 's5_A_im': s5_A_im, 's5_log_step': s5_log_step,
        's5_B_re': s5_B_re, 's5_B_im': s5_B_im, 's5_C_re': s5_C_re, 's5_C_im': s5_C_im, 's5_D': s5_D,
        'glu_w': glu_w, 'glu_b': glu_b, 'w_out1': w_out1.astype(bf16), 'norm_ffn1': norm_ffn1.reshape(1, d),
        'router_w': router_wp, 'moe_w1': moe_w1.astype(bf16), 'moe_w3': moe_w3.astype(bf16),
        'moe_w2': moe_w2.astype(bf16), 'norm_final': norm_final.astype(f32).reshape(1, d),
    }
    p['s5'] = _s5_params(p)
    p['l0'] = _layer0_params({'conv_w': conv0_w, 'conv_b': conv0_b, 'ssd_dt_bias': ssd_dt_bias, 'ssd_A_log': ssd_A_log,
                              'ssd_D': ssd_D, 'ssd_norm': ssd_norm, 'gdn_dt_bias': gdn_dt_bias,
                              'gdn_A_log': gdn_A_log, 'gdn_norm': gdn_norm})
    bp = x_prompt.shape[0]
    c_all = jnp.concatenate([c_prompt, c_sample], axis=0).astype(f32)
    mod0 = _ada(c_all, ada0_w, ada0_b, 1536)
    mod1 = _ada(c_all, ada1_w, ada1_b, 1536)
    outs_p = _run_trunk(
        x_prompt, mod0[:bp], mod1[:bp],
        jnp.zeros((bp, CONV_K - 1, CONV_CH), x_prompt.dtype),
        jnp.zeros((bp, SSD_HEADS, SSD_HEAD_DIM, SSD_STATE), f32),
        jnp.zeros((bp, GDN_HEADS, GDN_HEAD_DIM, GDN_HEAD_DIM), f32),
        None,
        jnp.zeros((bp, S5_GROUPS, S5_STATE), f32),
        jnp.zeros((bp, S5_GROUPS, S5_STATE), f32),
        p, 1024)
    outs_s = _run_trunk(
        x_sample, mod0[bp:], mod1[bp:], state_conv0, state_ssd, state_gdn, (cache_k, cache_v, cache_logf, page_table),
        state_s5_re, state_s5_im, p, 256)
    return (outs_p[0], outs_s[0]) + tuple(outs_p[1:]) + tuple(outs_s[1:])
```

```python
import functools
import math

import jax
import jax.numpy as jnp
from jax import lax
from jax.experimental import pallas as pl
from jax.experimental.pallas import tpu as pltpu

f32 = jnp.float32
bf16 = jnp.bfloat16

D_MODEL = 1024
CONV_K = 4
CHUNK = 64
SSD_HEADS = 8
SSD_HEAD_DIM = 64
SSD_INNER = 512
SSD_GROUPS = 2
SSD_STATE = 64
GDN_HEADS = 4
GDN_HEAD_DIM = 128
GDN_INNER = 512
SSD_CONV_CH = 768
CONV_CH = 2304
FOX_HEADS = 8
FOX_HEAD_DIM = 64
FOX_INNER = 512
Q_BLOCK = 128
S5_CH = 16
S5_GROUPS = 32
S5_INNER = 512
S5_STATE = 64
D_FF = 2816
N_EXPERTS = 8
D_FF_EXPERT = 3584
EPS = 1e-6

LANES = 128
IN0_PAD = 3456
IN1_PAD = 2304
VMEM_LIMIT = 56 * 1024 * 1024


def _cparams(*sem):
    return pltpu.CompilerParams(dimension_semantics=sem, vmem_limit_bytes=VMEM_LIMIT)


def _modnorm(x, g, scale, shift):
    ms = jnp.mean(x * x, axis=-1, keepdims=True)
    y = x * lax.rsqrt(ms + EPS) * g
    return y * (1.0 + scale) + shift


def _split3(a):
    a0 = a.astype(bf16)
    r = a - a0.astype(f32)
    a1 = r.astype(bf16)
    a2 = (r - a1.astype(f32)).astype(bf16)
    return a0, a1, a2


def _dot(a, b):
    return jnp.dot(a, b, preferred_element_type=f32)


def _dot_f32(a, b):
    a0, a1, a2 = _split3(a)
    b0, b1, b2 = _split3(b)
    return (_dot(a0, b0) + _dot(a0, b1) + _dot(a1, b0)
            + _dot(a1, b1) + _dot(a0, b2) + _dot(a2, b0))


def _mod_spec(mod, n_tiles):
    n_mod, rows, d = mod.shape
    per = n_tiles // n_mod
    return pl.BlockSpec((1, rows, d), lambda i, *_: (i // per, 0, 0))


def _inproj_kernel(x_ref, g_ref, sc_ref, sh_ref, w_ref, o_ref, h_scr):
    @pl.when(pl.program_id(1) == 0)
    def _():
        h_scr[...] = _modnorm(x_ref[...], g_ref[...], sc_ref[0], sh_ref[0]).astype(bf16)

    o_ref[...] = _dot(h_scr[...], w_ref[...])


def _inproj(x, g, scale, shift, w, tm, tn):
    t, d = x.shape
    n = w.shape[1]
    n_tiles = t // tm
    return pl.pallas_call(
        _inproj_kernel,
        out_shape=jax.ShapeDtypeStruct((t, n), f32),
        grid=(n_tiles, n // tn),
        in_specs=[
            pl.BlockSpec((tm, d), lambda i, j: (i, 0)),
            pl.BlockSpec((1, d), lambda i, j: (0, 0)),
            _mod_spec(scale, n_tiles),
            _mod_spec(shift, n_tiles),
            pl.BlockSpec((d, tn), lambda i, j: (0, j)),
        ],
        out_specs=pl.BlockSpec((tm, tn), lambda i, j: (i, j)),
        scratch_shapes=[pltpu.VMEM((tm, d), bf16)],
        compiler_params=_cparams("parallel", "arbitrary"),
        name="inproj",
    )(x, g, scale, shift, w)


def _outproj_kernel(*refs, n_parts):
    mix_refs = refs[:n_parts]
    w_refs = refs[n_parts:2 * n_parts]
    x_ref, gate_ref, o_ref = refs[2 * n_parts:]
    acc = None
    for m_ref, w_ref in zip(mix_refs, w_refs):
        if len(m_ref.shape) == 3:
            terms = [_dot(m_ref[q].astype(bf16), w_ref[q]) for q in range(m_ref.shape[0])]
        else:
            terms = [_dot(m_ref[...].astype(bf16), w_ref[...])]
        for term in terms:
            acc = term if acc is None else acc + term
    o_ref[...] = x_ref[...] + gate_ref[0] * acc


def _outproj(parts, weights, x, gate, tm):
    t, d = x.shape
    n_tiles = t // tm
    in_specs = []
    for a in parts:
        if a.ndim == 3:
            in_specs.append(pl.BlockSpec((a.shape[0], tm, a.shape[2]), lambda i: (0, i, 0)))
        else:
            in_specs.append(pl.BlockSpec((tm, a.shape[1]), lambda i: (i, 0)))
    for w in weights:
        in_specs.append(pl.BlockSpec(w.shape, (lambda i: (0, 0, 0)) if w.ndim == 3 else (lambda i: (0, 0))))
    in_specs += [pl.BlockSpec((tm, d), lambda i: (i, 0)), _mod_spec(gate, n_tiles)]
    return pl.pallas_call(
        functools.partial(_outproj_kernel, n_parts=len(parts)),
        out_shape=jax.ShapeDtypeStruct((t, d), f32),
        grid=(n_tiles,),
        in_specs=in_specs,
        out_specs=pl.BlockSpec((tm, d), lambda i: (i, 0)),
        compiler_params=_cparams("parallel"),
        name="outproj",
    )(*parts, *weights, x, gate)


def _ffn_kernel(x_ref, g_ref, sc_ref, sh_ref, gate_ref, w1_ref, w3_ref, w2_ref, o_ref, h_scr):
    j = pl.program_id(1)

    @pl.when(j == 0)
    def _():
        h_scr[...] = _modnorm(x_ref[...], g_ref[...], sc_ref[0], sh_ref[0]).astype(bf16)
        o_ref[...] = jnp.zeros_like(o_ref)

    h = h_scr[...]
    a = _dot(h, w1_ref[...])
    b = _dot(h, w3_ref[...])
    act = (a * jax.nn.sigmoid(a)) * b
    o_ref[...] += _dot(act.astype(bf16), w2_ref[...])

    @pl.when(j == pl.num_programs(1) - 1)
    def _():
        o_ref[...] = x_ref[...] + gate_ref[0] * o_ref[...]


def _ffn(x, g, scale, shift, gate, w1, w3, w2, tm, tf):
    t, d = x.shape
    f = w1.shape[1]
    n_tiles = t // tm
    return pl.pallas_call(
        _ffn_kernel,
        out_shape=jax.ShapeDtypeStruct((t, d), f32),
        grid=(n_tiles, f // tf),
        in_specs=[
            pl.BlockSpec((tm, d), lambda i, j: (i, 0)),
            pl.BlockSpec((1, d), lambda i, j: (0, 0)),
            _mod_spec(scale, n_tiles),
            _mod_spec(shift, n_tiles),
            _mod_spec(gate, n_tiles),
            pl.BlockSpec((d, tf), lambda i, j: (0, j)),
            pl.BlockSpec((d, tf), lambda i, j: (0, j)),
            pl.BlockSpec((tf, d), lambda i, j: (j, 0)),
        ],
        out_specs=pl.BlockSpec((tm, d), lambda i, j: (i, 0)),
        scratch_shapes=[pltpu.VMEM((tm, d), bf16)],
        compiler_params=_cparams("parallel", "arbitrary"),
        name="ffn",
    )(x, g, scale, shift, gate, w1, w3, w2)


def _router_kernel(x_ref, g_ref, sc_ref, sh_ref, rw_ref, h_ref, comb_ref, rank_ref, rankt_ref, cnt_ref):
    tm = x_ref.shape[0]
    h = _modnorm(x_ref[...], g_ref[...], sc_ref[0], sh_ref[0])
    h_ref[...] = h.astype(bf16)
    logits = _dot_f32(h, rw_ref[...])
    lane = lax.broadcasted_iota(jnp.int32, (tm, LANES), 1)
    lg = jnp.where(lane < N_EXPERTS, logits, -jnp.inf)
    m1 = jnp.max(lg, axis=1, keepdims=True)
    i1 = jnp.min(jnp.where(lg == m1, lane, LANES), axis=1, keepdims=True)
    lg2 = jnp.where(lane == i1, -jnp.inf, lg)
    m2 = jnp.max(lg2, axis=1, keepdims=True)
    i2 = jnp.min(jnp.where(lg2 == m2, lane, LANES), axis=1, keepdims=True)
    e2 = jnp.exp(m2 - m1)
    den = 1.0 + e2
    comb_ref[...] = jnp.where(lane == i1, 1.0 / den, jnp.where(lane == i2, e2 / den, 0.0))
    sel = (lane == i1) | (lane == i2)
    ind = jnp.where(sel, 1.0, 0.0)
    row = lax.broadcasted_iota(jnp.int32, (tm, tm), 0)
    col = lax.broadcasted_iota(jnp.int32, (tm, tm), 1)
    below = jnp.where(col < row, 1.0, 0.0).astype(bf16)
    rank = jnp.where(sel, _dot(below, ind.astype(bf16)), -1.0)
    rank_ref[...] = rank
    rankt_ref[0] = rank.T[:N_EXPERTS, :]
    cnt_ref[0] = jnp.broadcast_to(jnp.sum(ind, axis=0, keepdims=True), (8, LANES))


def _router(x, g, scale, shift, rw, tm):
    t, d = x.shape
    n_tiles = t // tm
    return pl.pallas_call(
        _router_kernel,
        out_shape=(
            jax.ShapeDtypeStruct((t, d), bf16),
            jax.ShapeDtypeStruct((t, LANES), f32),
            jax.ShapeDtypeStruct((t, LANES), f32),
            jax.ShapeDtypeStruct((n_tiles, N_EXPERTS, tm), f32),
            jax.ShapeDtypeStruct((n_tiles, 8, LANES), f32),
        ),
        grid=(n_tiles,),
        in_specs=[
            pl.BlockSpec((tm, d), lambda i: (i, 0)),
            pl.BlockSpec((1, d), lambda i: (0, 0)),
            _mod_spec(scale, n_tiles),
            _mod_spec(shift, n_tiles),
            pl.BlockSpec((d, LANES), lambda i: (0, 0)),
        ],
        out_specs=(
            pl.BlockSpec((tm, d), lambda i: (i, 0)),
            pl.BlockSpec((tm, LANES), lambda i: (i, 0)),
            pl.BlockSpec((tm, LANES), lambda i: (i, 0)),
            pl.BlockSpec((1, N_EXPERTS, tm), lambda i: (i, 0, 0)),
            pl.BlockSpec((1, 8, LANES), lambda i: (i, 0, 0)),
        ),
        compiler_params=_cparams("parallel"),
        name="router",
    )(x, g, scale, shift, rw)


def _moe_kernel(cnt_ref, x_ref, g_ref, sc_ref, sh_ref, gate_ref, comb_ref, rank_ref, rankt_ref,
                w1_ref, w3_ref, w2_ref, nf_ref, o_ref, h_scr, hc_scr, y_scr, *, cap):
    i = pl.program_id(0)
    e = pl.program_id(1)
    fc = pl.program_id(2)
    last_fc = pl.num_programs(2) - 1
    tm = x_ref.shape[0]
    n_batches = (cnt_ref[i * N_EXPERTS + e] + cap - 1) // cap

    @pl.when((e == 0) & (fc == 0))
    def _():
        h_scr[...] = _modnorm(x_ref[...], g_ref[...], sc_ref[0], sh_ref[0]).astype(bf16)
        o_ref[...] = jnp.zeros_like(o_ref)

    @pl.when(fc == 0)
    def _():
        def compact(b, carry):
            r0 = pl.multiple_of(b * cap, 8)
            rowid = (lax.broadcasted_iota(jnp.int32, (cap, tm), 0) + r0).astype(f32)
            onehot = jnp.where(rankt_ref[0] == rowid, 1.0, 0.0).astype(bf16)
            hc_scr[pl.ds(r0, cap), :] = _dot(onehot, h_scr[...]).astype(bf16)
            y_scr[pl.ds(r0, cap), :] = jnp.zeros((cap, y_scr.shape[1]), f32)
            return carry
        lax.fori_loop(0, n_batches, compact, 0)

    def expert(b, carry):
        r0 = pl.multiple_of(b * cap, 8)
        hc = hc_scr[pl.ds(r0, cap), :]
        a = _dot(hc, w1_ref[0])
        g = _dot(hc, w3_ref[0])
        act = (a * jax.nn.sigmoid(a)) * g
        y_scr[pl.ds(r0, cap), :] += _dot(act.astype(bf16), w2_ref[0])
        return carry
    lax.fori_loop(0, n_batches, expert, 0)

    @pl.when(fc == last_fc)
    def _():
        lane = lax.broadcasted_iota(jnp.int32, (tm, LANES), 1)
        rank_e = jnp.sum(jnp.where(lane == e, rank_ref[...], 0.0), axis=1, keepdims=True)
        gate_e = jnp.sum(jnp.where(lane == e, comb_ref[...], 0.0), axis=1, keepdims=True)

        def expand(b, carry):
            r0 = pl.multiple_of(b * cap, 8)
            colid = (lax.broadcasted_iota(jnp.int32, (tm, cap), 1) + r0).astype(f32)
            onehot = jnp.where(rank_e == colid, 1.0, 0.0).astype(bf16)
            o_ref[...] += gate_e * _dot(onehot, y_scr[pl.ds(r0, cap), :].astype(bf16))
            return carry
        lax.fori_loop(0, n_batches, expand, 0)

    @pl.when((e == N_EXPERTS - 1) & (fc == last_fc))
    def _():
        y = x_ref[...] + gate_ref[0] * o_ref[...]
        o_ref[...] = y * lax.rsqrt(jnp.mean(y * y, axis=-1, keepdims=True) + EPS) * nf_ref[...]


def _moe(x, g, scale, shift, gate, rw, w1, w3, w2, norm_final, tm, tf, cap):
    t, d = x.shape
    f = w1.shape[2]
    n_tiles = t // tm
    cap = min(cap, tm)
    n_cap = -(-tm // cap)
    comb, rank, rankt, cnt = _router(x, g, scale, shift, rw, tm)
    counts = cnt[:, 0, :N_EXPERTS].astype(jnp.int32).reshape(-1)
    rankt = rankt.reshape(n_tiles * N_EXPERTS, 1, tm)
    grid_spec = pltpu.PrefetchScalarGridSpec(
        num_scalar_prefetch=1,
        grid=(n_tiles, N_EXPERTS, f // tf),
        in_specs=[
            pl.BlockSpec((tm, d), lambda i, e, c, cnt: (i, 0)),
            pl.BlockSpec((1, d), lambda i, e, c, cnt: (0, 0)),
            _mod_spec(scale, n_tiles),
            _mod_spec(shift, n_tiles),
            _mod_spec(gate, n_tiles),
            pl.BlockSpec((tm, LANES), lambda i, e, c, cnt: (i, 0)),
            pl.BlockSpec((tm, LANES), lambda i, e, c, cnt: (i, 0)),
            pl.BlockSpec((1, 1, tm), lambda i, e, c, cnt: (i * N_EXPERTS + e, 0, 0)),
            pl.BlockSpec((1, d, tf), lambda i, e, c, cnt: (e, 0, c)),
            pl.BlockSpec((1, d, tf), lambda i, e, c, cnt: (e, 0, c)),
            pl.BlockSpec((1, tf, d), lambda i, e, c, cnt: (e, c, 0)),
            pl.BlockSpec((1, d), lambda i, e, c, cnt: (0, 0)),
        ],
        out_specs=pl.BlockSpec((tm, d), lambda i, e, c, cnt: (i, 0)),
        scratch_shapes=[
            pltpu.VMEM((tm, d), bf16),
            pltpu.VMEM((n_cap * cap, d), bf16),
            pltpu.VMEM((n_cap * cap, d), f32),
        ],
    )
    return pl.pallas_call(
        functools.partial(_moe_kernel, cap=cap),
        out_shape=jax.ShapeDtypeStruct((t, d), f32),
        grid_spec=grid_spec,
        compiler_params=_cparams("parallel", "arbitrary", "arbitrary"),
        name="moe",
    )(counts, x, g, scale, shift, gate, comb, rank, rankt, w1, w3, w2, norm_final)


def _moe_tile_kernel(cnt_ref, *refs, cap, halves):
    h_ref, gate_ref, comb_ref, rank_ref = refs[:4]
    rankt_refs = refs[4:4 + halves]
    w1_ref, w3_ref, w2_ref, o_ref, hc_scr, y_scr = refs[4 + halves:]
    i = pl.program_id(0)
    e = pl.program_id(1)
    fc = pl.program_id(2)
    last_fc = pl.num_programs(2) - 1
    th = h_ref.shape[0] // halves
    group = halves * cap
    n_batches = (cnt_ref[(i * halves) * N_EXPERTS + e] + cap - 1) // cap
    for hf in range(1, halves):
        n_batches = jnp.maximum(n_batches, (cnt_ref[(i * halves + hf) * N_EXPERTS + e] + cap - 1) // cap)

    @pl.when((e == 0) & (fc == 0))
    def _():
        o_ref[...] = jnp.zeros_like(o_ref)

    @pl.when(fc == 0)
    def _():
        def compact(b, carry):
            rowid = (lax.broadcasted_iota(jnp.int32, (cap, th), 0) + b * cap).astype(f32)
            for hf in range(halves):
                r0 = pl.multiple_of(b * group + hf * cap, 8)
                onehot = jnp.where(rankt_refs[hf][0] == rowid, 1.0, 0.0).astype(bf16)
                hc_scr[pl.ds(r0, cap), :] = _dot(onehot, h_ref[hf * th:(hf + 1) * th, :]).astype(bf16)
            g0 = pl.multiple_of(b * group, 8)
            y_scr[pl.ds(g0, group), :] = jnp.zeros((group, y_scr.shape[1]), f32)
            return carry
        lax.fori_loop(0, n_batches, compact, 0)

    def expert(b, carry):
        g0 = pl.multiple_of(b * group, 8)
        hc = hc_scr[pl.ds(g0, group), :]
        a = _dot(hc, w1_ref[0])
        g = _dot(hc, w3_ref[0])
        act = (a * jax.nn.sigmoid(a)) * g
        y_scr[pl.ds(g0, group), :] += _dot(act.astype(bf16), w2_ref[0])
        return carry
    lax.fori_loop(0, n_batches, expert, 0)

    @pl.when(fc == last_fc)
    def _():
        lane = lax.broadcasted_iota(jnp.int32, (th, LANES), 1)
        for hf in range(halves):
            rows = slice(hf * th, (hf + 1) * th)
            rank_e = jnp.sum(jnp.where(lane == e, rank_ref[rows, :], 0.0), axis=1, keepdims=True)
            gate_e = jnp.sum(jnp.where(lane == e, comb_ref[rows, :], 0.0), axis=1, keepdims=True)
            gate_mod = gate_ref[0] if gate_ref.shape[1] == 1 else gate_ref[0, rows, :]

            def expand(b, carry):
                r0 = pl.multiple_of(b * group + hf * cap, 8)
                colid = (lax.broadcasted_iota(jnp.int32, (th, cap), 1) + b * cap).astype(f32)
                onehot = jnp.where(rank_e == colid, 1.0, 0.0).astype(bf16)
                o_ref[rows, :] += (gate_mod * gate_e) * _dot(onehot, y_scr[pl.ds(r0, cap), :].astype(bf16))
                return carry
            lax.fori_loop(0, n_batches, expand, 0)


def _resnorm_kernel(x_ref, dl_ref, nf_ref, o_ref):
    y = x_ref[...] + dl_ref[...]
    o_ref[...] = y * lax.rsqrt(jnp.mean(y * y, axis=-1, keepdims=True) + EPS) * nf_ref[...]


def _moe_grouped(x, g, scale, shift, gate, rw, w1, w3, w2, norm_final, th, halves, tf, cap):
    t, d = x.shape
    f = w1.shape[2]
    tm = th * halves
    n_tiles = t // tm
    cap = min(cap, th)
    n_cap = -(-th // cap)
    h, comb, rank, rankt, cnt = _router(x, g, scale, shift, rw, th)
    counts = cnt[:, 0, :N_EXPERTS].astype(jnp.int32).reshape(-1)
    rankt = rankt.reshape((t // th) * N_EXPERTS, 1, th)
    once = pl.Buffered(1)

    def rankt_map(hf):
        return lambda i, e, c, cnt: ((i * halves + hf) * N_EXPERTS + e, 0, 0)

    grid_spec = pltpu.PrefetchScalarGridSpec(
        num_scalar_prefetch=1,
        grid=(n_tiles, N_EXPERTS, f // tf),
        in_specs=[pl.BlockSpec((tm, d), lambda i, e, c, cnt: (i, 0), pipeline_mode=once),
                  _mod_spec(gate, n_tiles),
                  pl.BlockSpec((tm, LANES), lambda i, e, c, cnt: (i, 0), pipeline_mode=once),
                  pl.BlockSpec((tm, LANES), lambda i, e, c, cnt: (i, 0), pipeline_mode=once)]
                 + [pl.BlockSpec((1, 1, th), rankt_map(hf)) for hf in range(halves)]
                 + [pl.BlockSpec((1, d, tf), lambda i, e, c, cnt: (e, 0, c)),
                    pl.BlockSpec((1, d, tf), lambda i, e, c, cnt: (e, 0, c)),
                    pl.BlockSpec((1, tf, d), lambda i, e, c, cnt: (e, c, 0))],
        out_specs=pl.BlockSpec((tm, d), lambda i, e, c, cnt: (i, 0)),
        scratch_shapes=[pltpu.VMEM((n_cap * halves * cap, d), bf16),
                        pltpu.VMEM((n_cap * halves * cap, d), f32)],
    )
    delta = pl.pallas_call(
        functools.partial(_moe_tile_kernel, cap=cap, halves=halves),
        out_shape=jax.ShapeDtypeStruct((t, d), f32),
        grid_spec=grid_spec,
        compiler_params=_cparams("parallel", "arbitrary", "arbitrary"),
        name="moe_tile",
    )(counts, h, gate, comb, rank, *([rankt] * halves), w1, w3, w2)
    tr = min(t, 1024)
    return pl.pallas_call(
        _resnorm_kernel,
        out_shape=jax.ShapeDtypeStruct((t, d), f32),
        grid=(t // tr,),
        in_specs=[pl.BlockSpec((tr, d), lambda i: (i, 0)), pl.BlockSpec((tr, d), lambda i: (i, 0)),
                  pl.BlockSpec((1, d), lambda i: (0, 0))],
        out_specs=pl.BlockSpec((tr, d), lambda i: (i, 0)),
        compiler_params=_cparams("parallel"),
        name="resnorm",
    )(x, delta, norm_final)


S5_LANES = S5_GROUPS * S5_STATE
S5_QUARTERS = 4
S5_QS = S5_LANES // S5_QUARTERS
S5_QC = S5_INNER // S5_QUARTERS


def _s5_load_bu(u_refs, bre_ref, bim_ref, up_scr, bu_r, bu_i, n_sub, m):
    for c in range(S5_QUARTERS):
        for k in range(m):
            up_scr[k * n_sub:(k + 1) * n_sub, c * S5_QC:(c + 1) * S5_QC] = u_refs[c][pl.ds(k, n_sub, stride=m), :]
        uc = up_scr[:, c * S5_QC:(c + 1) * S5_QC].astype(bf16)
        bu_r[:, c * S5_QS:(c + 1) * S5_QS] = _dot(uc, bre_ref[c])
        bu_i[:, c * S5_QS:(c + 1) * S5_QS] = _dot(uc, bim_ref[c])


def _s5_local_scan(lam_r_ref, lam_i_ref, init_r_ref, init_i_ref, bu_r, bu_i, n_sub, m):
    width = 8192 // n_sub
    for c in range(S5_LANES // width):
        cols = slice(c * width, (c + 1) * width)
        lr = jnp.broadcast_to(lam_r_ref[:, cols], (n_sub, width))
        li = jnp.broadcast_to(lam_i_ref[:, cols], (n_sub, width))
        if init_r_ref is None:
            x0 = (jnp.zeros((n_sub, width), f32), jnp.zeros((n_sub, width), f32))
        else:
            x0 = (init_r_ref[:, cols], init_i_ref[:, cols])

        def step(k, carry):
            xr, xi = carry
            rows = pl.ds(pl.multiple_of(k * n_sub, 8), n_sub)
            nr = lr * xr - li * xi + bu_r[rows, cols]
            ni = lr * xi + li * xr + bu_i[rows, cols]
            bu_r[rows, cols] = nr
            bu_i[rows, cols] = ni
            return nr, ni
        lax.fori_loop(0, m, step, x0)


def _gelu_tanh(x):
    return 0.5 * x * (1.0 + jnp.tanh(math.sqrt(2.0 / math.pi) * (x + 0.044715 * (x * x * x))))


def _s5_output(xb_r, xb_i, up_scr, cre_ref, cim_ref, dd_ref, gw_ref, gb_ref, o_ref, op_scr, n_sub, m):
    ys = []
    for c in range(S5_QUARTERS):
        cols = slice(c * S5_QS, (c + 1) * S5_QS)
        ys.append(_dot(xb_r[:, cols].astype(bf16), cre_ref[c]) - _dot(xb_i[:, cols].astype(bf16), cim_ref[c]))
    y = jnp.concatenate(ys, axis=1) + dd_ref[...] * up_scr[...]
    hs = _gelu_tanh(y)
    op_scr[...] = hs * jax.nn.sigmoid(_dot(hs.astype(bf16), gw_ref[...]) + gb_ref[...])
    for c in range(S5_QUARTERS):
        for k in range(m):
            o_ref[c, pl.ds(k, n_sub, stride=m), :] = op_scr[k * n_sub:(k + 1) * n_sub, c * S5_QC:(c + 1) * S5_QC]


def _s5_chain_kernel(u0_ref, u1_ref, u2_ref, u3_ref, x0r_ref, x0i_ref, lam_r_ref, lam_i_ref, bre_ref, bim_ref, cre_ref, cim_ref,
                     dd_ref, gw_ref, gb_ref, o_ref, sr_ref, si_ref,
                     pow_r, pow_i, bu_r, bu_i, up_scr, op_scr, en_r, en_i, *, m):
    n_sub = 8
    j = pl.program_id(1)

    @pl.when((pl.program_id(0) == 0) & (j == 0))
    def _():
        def pstep(k, carry):
            pr, pi = carry
            pow_r[pl.ds(k, 1), :] = pr
            pow_i[pl.ds(k, 1), :] = pi
            lr = lam_r_ref[...]
            li = lam_i_ref[...]
            return lr * pr - li * pi, lr * pi + li * pr
        lax.fori_loop(0, m, pstep, (lam_r_ref[...], lam_i_ref[...]))

    @pl.when(j == 0)
    def _():
        sr_ref[0] = x0r_ref[0]
        si_ref[0] = x0i_ref[0]

    _s5_load_bu((u0_ref, u1_ref, u2_ref, u3_ref), bre_ref, bim_ref, up_scr, bu_r, bu_i, n_sub, m)
    _s5_local_scan(lam_r_ref, lam_i_ref, None, None, bu_r, bu_i, n_sub, m)

    pm_r = pow_r[m - 1:m, :]
    pm_i = pow_i[m - 1:m, :]
    e_r = sr_ref[0]
    e_i = si_ref[0]
    for s in range(n_sub):
        en_r[s:s + 1, :] = e_r
        en_i[s:s + 1, :] = e_i
        row = (m - 1) * n_sub + s
        e_r, e_i = (bu_r[row:row + 1, :] + pm_r * e_r - pm_i * e_i,
                    bu_i[row:row + 1, :] + pm_r * e_i + pm_i * e_r)
    sr_ref[0] = e_r
    si_ref[0] = e_i

    def fix(k, carry):
        rows = pl.ds(pl.multiple_of(k * n_sub, 8), n_sub)
        pr = pow_r[pl.ds(k, 1), :]
        pi = pow_i[pl.ds(k, 1), :]
        er = en_r[...]
        ei = en_i[...]
        bu_r[rows, :] = bu_r[rows, :] + pr * er - pi * ei
        bu_i[rows, :] = bu_i[rows, :] + pr * ei + pi * er
        return carry
    lax.fori_loop(0, m, fix, 0)

    _s5_output(bu_r, bu_i, up_scr, cre_ref, cim_ref, dd_ref, gw_ref, gb_ref, o_ref, op_scr, n_sub, m)


def _s5_batch_kernel(u0_ref, u1_ref, u2_ref, u3_ref, x0r_ref, x0i_ref, lam_r_ref, lam_i_ref, bre_ref, bim_ref, cre_ref, cim_ref,
                     dd_ref, gw_ref, gb_ref, o_ref, sr_ref, si_ref,
                     bu_r, bu_i, up_scr, op_scr, *, n_sub, m):
    _s5_load_bu((u0_ref, u1_ref, u2_ref, u3_ref), bre_ref, bim_ref, up_scr, bu_r, bu_i, n_sub, m)
    _s5_local_scan(lam_r_ref, lam_i_ref, x0r_ref, x0i_ref, bu_r, bu_i, n_sub, m)
    last = slice((m - 1) * n_sub, m * n_sub)
    sr_ref[...] = bu_r[last, :]
    si_ref[...] = bu_i[last, :]
    _s5_output(bu_r, bu_i, up_scr, cre_ref, cim_ref, dd_ref, gw_ref, gb_ref, o_ref, op_scr, n_sub, m)


def _s5_params(p):
    ar = p['s5_A_re'].astype(f32)
    ai = p['s5_A_im'].astype(f32)
    step = jnp.exp(p['s5_log_step'].astype(f32))[:, None]
    mag = jnp.exp(ar * step)
    lb_re = mag * jnp.cos(ai * step)
    lb_im = mag * jnp.sin(ai * step)
    den = ar * ar + ai * ai
    nr = lb_re - 1.0
    cr = (nr * ar + lb_im * ai) / den
    ci = (lb_im * ar - nr * ai) / den
    bb_re = cr[..., None] * p['s5_B_re'] - ci[..., None] * p['s5_B_im']
    bb_im = cr[..., None] * p['s5_B_im'] + ci[..., None] * p['s5_B_re']
    gq = S5_GROUPS // S5_QUARTERS
    eye = jnp.eye(gq, dtype=f32)

    def bq(bb):
        t = bb.reshape(S5_QUARTERS, gq, S5_STATE, S5_CH)
        return jnp.einsum('cgph,gk->cghkp', t, eye).reshape(S5_QUARTERS, S5_QC, S5_QS).astype(bf16)

    def cq(cc):
        t = cc.astype(f32).reshape(S5_QUARTERS, gq, S5_CH, S5_STATE)
        return jnp.einsum('cghp,gk->cgpkh', t, eye).reshape(S5_QUARTERS, S5_QS, S5_QC).astype(bf16)

    return dict(lam_r=lb_re.reshape(1, S5_LANES), lam_i=lb_im.reshape(1, S5_LANES),
                bre=bq(bb_re), bim=bq(bb_im), cre=cq(p['s5_C_re']), cim=cq(p['s5_C_im']),
                dd=p['s5_D'].astype(f32).reshape(1, S5_INNER),
                gw=p['glu_w'].astype(bf16), gb=p['glu_b'].astype(f32).reshape(1, S5_INNER))


def _s5_const_specs(nd):
    z2 = (lambda *_: (0, 0))
    z3 = (lambda *_: (0, 0, 0))
    return [
        pl.BlockSpec((1, S5_LANES), z2), pl.BlockSpec((1, S5_LANES), z2),
        pl.BlockSpec((S5_QUARTERS, S5_QC, S5_QS), z3), pl.BlockSpec((S5_QUARTERS, S5_QC, S5_QS), z3),
        pl.BlockSpec((S5_QUARTERS, S5_QS, S5_QC), z3), pl.BlockSpec((S5_QUARTERS, S5_QS, S5_QC), z3),
        pl.BlockSpec((1, S5_INNER), z2), pl.BlockSpec((S5_INNER, S5_INNER), z2), pl.BlockSpec((1, S5_INNER), z2),
    ]


def _s5_chain(proj, ucol, x0r, x0i, sp, b, l, m):
    chunk = 8 * m
    nc = l // chunk
    consts = [sp[k] for k in ('lam_r', 'lam_i', 'bre', 'bim', 'cre', 'cim', 'dd', 'gw', 'gb')]
    return pl.pallas_call(
        functools.partial(_s5_chain_kernel, m=m),
        out_shape=(jax.ShapeDtypeStruct((S5_QUARTERS, b * l, S5_QC), f32),
                   jax.ShapeDtypeStruct((b, 1, S5_LANES), f32),
                   jax.ShapeDtypeStruct((b, 1, S5_LANES), f32)),
        grid=(b, nc),
        in_specs=[pl.BlockSpec((chunk, S5_QC), functools.partial(lambda i, j, c: (i * nc + j, ucol + c), c=c))
                  for c in range(S5_QUARTERS)] + [
                  pl.BlockSpec((1, 1, S5_LANES), lambda i, j: (i, 0, 0)),
                  pl.BlockSpec((1, 1, S5_LANES), lambda i, j: (i, 0, 0))] + _s5_const_specs(2),
        out_specs=(pl.BlockSpec((S5_QUARTERS, chunk, S5_QC), lambda i, j: (0, i * nc + j, 0)),
                   pl.BlockSpec((1, 1, S5_LANES), lambda i, j: (i, 0, 0)),
                   pl.BlockSpec((1, 1, S5_LANES), lambda i, j: (i, 0, 0))),
        scratch_shapes=[pltpu.VMEM((m, S5_LANES), f32), pltpu.VMEM((m, S5_LANES), f32),
                        pltpu.VMEM((chunk, S5_LANES), f32), pltpu.VMEM((chunk, S5_LANES), f32),
                        pltpu.VMEM((chunk, S5_INNER), f32), pltpu.VMEM((chunk, S5_INNER), f32),
                        pltpu.VMEM((8, S5_LANES), f32), pltpu.VMEM((8, S5_LANES), f32)],
        compiler_params=_cparams("arbitrary", "arbitrary"),
        name="s5_chain",
    )(proj, proj, proj, proj, x0r, x0i, *consts)


def _s5_batch(proj, ucol, x0r, x0i, sp, n_sub, m):
    t = n_sub * m
    consts = [sp[k] for k in ('lam_r', 'lam_i', 'bre', 'bim', 'cre', 'cim', 'dd', 'gw', 'gb')]
    return pl.pallas_call(
        functools.partial(_s5_batch_kernel, n_sub=n_sub, m=m),
        out_shape=(jax.ShapeDtypeStruct((S5_QUARTERS, t, S5_QC), f32),
                   jax.ShapeDtypeStruct((n_sub, S5_LANES), f32),
                   jax.ShapeDtypeStruct((n_sub, S5_LANES), f32)),
        grid=(1,),
        in_specs=[pl.BlockSpec((t, S5_QC), functools.partial(lambda i, c: (0, ucol + c), c=c))
                  for c in range(S5_QUARTERS)] + [
                  pl.BlockSpec((n_sub, S5_LANES), lambda i: (0, 0)),
                  pl.BlockSpec((n_sub, S5_LANES), lambda i: (0, 0))] + _s5_const_specs(1),
        out_specs=(pl.BlockSpec((S5_QUARTERS, t, S5_QC), lambda i: (0, 0, 0)),
                   pl.BlockSpec((n_sub, S5_LANES), lambda i: (0, 0)),
                   pl.BlockSpec((n_sub, S5_LANES), lambda i: (0, 0))),
        scratch_shapes=[pltpu.VMEM((t, S5_LANES), f32), pltpu.VMEM((t, S5_LANES), f32),
                        pltpu.VMEM((t, S5_INNER), f32), pltpu.VMEM((t, S5_INNER), f32)],
        compiler_params=_cparams("arbitrary"),
        name="s5_batch",
    )(proj, proj, proj, proj, x0r, x0i, *consts)


def _tri(n, strict=False, upper=False):
    r = lax.broadcasted_iota(jnp.int32, (n, n), 0)
    c = lax.broadcasted_iota(jnp.int32, (n, n), 1)
    if upper:
        r, c = c, r
    return jnp.where((c < r) if strict else (c <= r), 1.0, 0.0).astype(bf16)


def _dot_exact_lhs(a_exact_bf16, b):
    b0, b1, b2 = _split3(b)
    return _dot(a_exact_bf16, b0) + _dot(a_exact_bf16, b1) + _dot(a_exact_bf16, b2)


def _log_sigmoid(x):
    return jnp.minimum(x, 0.0) - jnp.log1p(jnp.exp(-jnp.abs(x)))


def _fcum_kernel(fr_ref, bias_ref, logf_ref, f_ref, ft_ref, carry):
    @pl.when(pl.program_id(1) == 0)
    def _():
        carry[...] = jnp.zeros_like(carry)

    n = fr_ref.shape[0]
    logf = _log_sigmoid(fr_ref[...] + bias_ref[...])
    logf_ref[...] = logf
    f = _dot_exact_lhs(_tri(n), logf) + carry[0:1, :]
    f_ref[...] = f
    ft_ref[0] = f.T[:8, :]
    carry[0:1, :] = f[n - 1:n, :]


def _fcum(proj, col, bias, b, l, chunk):
    nc = l // chunk
    return pl.pallas_call(
        _fcum_kernel,
        out_shape=(jax.ShapeDtypeStruct((b * l, LANES), f32),
                   jax.ShapeDtypeStruct((b * l, LANES), f32),
                   jax.ShapeDtypeStruct((b, 8, l), f32)),
        grid=(b, nc),
        in_specs=[pl.BlockSpec((chunk, LANES), lambda i, j: (i * nc + j, col)),
                  pl.BlockSpec((1, LANES), lambda i, j: (0, 0))],
        out_specs=(pl.BlockSpec((chunk, LANES), lambda i, j: (i * nc + j, 0)),
                   pl.BlockSpec((chunk, LANES), lambda i, j: (i * nc + j, 0)),
                   pl.BlockSpec((1, 8, chunk), lambda i, j: (i, 0, j))),
        scratch_shapes=[pltpu.VMEM((8, LANES), f32)],
        compiler_params=_cparams("arbitrary", "arbitrary"),
        name="fcum",
    )(proj, bias)


N_FPARTS = 3
FOX_STRIP = 256


def _fox_select_mats():
    rows = jnp.arange(N_FPARTS * LANES)[None, :, None]
    cols = jnp.arange(FOX_HEAD_DIM)[None, None, :]
    head = jnp.arange(FOX_HEADS)[:, None, None]
    hit = (rows % LANES == head)
    sq = jnp.where(hit & (cols == rows // LANES), 1.0, 0.0)
    sk = jnp.where(hit & (cols == N_FPARTS + rows // LANES), -1.0, 0.0)
    return sq.astype(bf16), sk.astype(bf16)


def _fox_prep_kernel(q_ref, k_ref, v_ref, f_ref, sq_ref, sk_ref, qa_ref, ka_ref, vt_ref, *, scale):
    tm = q_ref.shape[0]
    hd = FOX_HEAD_DIM
    fcat = jnp.concatenate(_split3(f_ref[...]), axis=1)
    lane = lax.broadcasted_iota(jnp.int32, (tm, hd), 1)
    ones_q = jnp.where((lane >= N_FPARTS) & (lane < 2 * N_FPARTS), 1.0, 0.0)
    ones_k = jnp.where(lane < N_FPARTS, 1.0, 0.0)
    vt = v_ref[...].T
    for h in range(FOX_HEADS):
        cols = slice(h * hd, (h + 1) * hd)
        eq = _dot(fcat, sq_ref[h]) + ones_q
        ek = _dot(fcat, sk_ref[h]) + ones_k
        qa_ref[h] = jnp.concatenate([q_ref[:, cols] * scale, eq], axis=1).astype(bf16)
        ka_ref[h] = jnp.concatenate([k_ref[:, cols], ek], axis=1).astype(bf16)
        vt_ref[h] = vt[h * hd:(h + 1) * hd, :].astype(bf16)


def _fox_prep(proj, f, t, tm):
    sq, sk = _fox_select_mats()
    sel_spec = pl.BlockSpec((FOX_HEADS, N_FPARTS * LANES, FOX_HEAD_DIM), lambda i: (0, 0, 0))
    return pl.pallas_call(
        functools.partial(_fox_prep_kernel, scale=FOX_HEAD_DIM ** -0.5),
        out_shape=(jax.ShapeDtypeStruct((FOX_HEADS, t, LANES), bf16),
                   jax.ShapeDtypeStruct((FOX_HEADS, t, LANES), bf16),
                   jax.ShapeDtypeStruct((FOX_HEADS, FOX_HEAD_DIM, t), bf16)),
        grid=(t // tm,),
        in_specs=[pl.BlockSpec((tm, FOX_INNER), lambda i: (i, 0)),
                  pl.BlockSpec((tm, FOX_INNER), lambda i: (i, 1)),
                  pl.BlockSpec((tm, FOX_INNER), lambda i: (i, 2)),
                  pl.BlockSpec((tm, LANES), lambda i: (i, 0)),
                  sel_spec, sel_spec],
        out_specs=(pl.BlockSpec((FOX_HEADS, tm, LANES), lambda i: (0, i, 0)),
                   pl.BlockSpec((FOX_HEADS, tm, LANES), lambda i: (0, i, 0)),
                   pl.BlockSpec((FOX_HEADS, FOX_HEAD_DIM, tm), lambda i: (0, 0, i))),
        compiler_params=_cparams("parallel"),
        name="fox_prep",
    )(proj, proj, proj, f, sq, sk)


def _fox_kernel(qt_ref, kt_ref, qa_ref, ka_ref, vt_ref, o_ref, m_scr, l_scr, acc_scr):
    step = pl.program_id(2)
    qi = qt_ref[step]
    ki = kt_ref[step]
    tq = qa_ref.shape[1]
    tk = ka_ref.shape[1]
    q_first = qi * tq
    k_first = ki * tk

    @pl.when(ki == 0)
    def _():
        m_scr[...] = jnp.full_like(m_scr, -jnp.inf)
        l_scr[...] = jnp.zeros_like(l_scr)
        acc_scr[...] = jnp.zeros_like(acc_scr)

    def update(masked):
        strip = min(FOX_STRIP, tq)
        chains = [(hh, q0) for hh in range(2) for q0 in range(0, tq, strip)]
        nks = [min(tk, q0 + strip) if (masked and tq == tk) else tk for _, q0 in chains]
        sts = [_dot_nt(ka_ref[hh, :nk, :], qa_ref[hh, q0:q0 + strip, :])
               for (hh, q0), nk in zip(chains, nks)]
        ps, alphas = [], []
        for (hh, q0), nk, st in zip(chains, nks, sts):
            qs = slice(q0, q0 + strip)
            if masked:
                kpos = lax.broadcasted_iota(jnp.int32, (nk, strip), 0) + k_first
                qpos = lax.broadcasted_iota(jnp.int32, (nk, strip), 1) + (q0 + q_first)
                st = jnp.where(kpos <= qpos, st, -jnp.inf)
            m_old = m_scr[hh, :, qs]
            m_new = jnp.maximum(m_old, jnp.max(st, axis=0, keepdims=True))
            alpha = jnp.exp(m_old - m_new)
            p = jnp.exp(st - m_new)
            l_scr[hh, :, qs] = alpha * l_scr[hh, :, qs] + jnp.sum(p, axis=0, keepdims=True)
            m_scr[hh, :, qs] = m_new
            ps.append(p.astype(bf16))
            alphas.append(alpha)
        pvs = [_dot(vt_ref[hh, :, :nk], p) for (hh, _), nk, p in zip(chains, nks, ps)]
        for (hh, q0), alpha, pv in zip(chains, alphas, pvs):
            qs = slice(q0, q0 + strip)
            acc_scr[hh, :, qs] = alpha * acc_scr[hh, :, qs] + pv

    full = k_first + (tk - 1) <= q_first

    @pl.when(full)
    def _():
        update(False)

    @pl.when(jnp.logical_not(full))
    def _():
        update(True)

    @pl.when(k_first + tk >= q_first + tq)
    def _():
        ot = jnp.concatenate([acc_scr[0] / l_scr[0], acc_scr[1] / l_scr[1]], axis=0)
        o_ref[...] = ot.T


def _fox_prompt(proj, f, b, l, tq, tk):
    nq = l // tq
    nk = l // tk
    n_hp = FOX_HEADS // 2
    qa, ka, vt = _fox_prep(proj, f, b * l, max(tq, tk))
    pairs = [(qi, ki) for qi in range(nq) for ki in range(((qi + 1) * tq - 1) // tk + 1)]
    qtab = jnp.array([pr[0] for pr in pairs], jnp.int32)
    ktab = jnp.array([pr[1] for pr in pairs], jnp.int32)
    grid_spec = pltpu.PrefetchScalarGridSpec(
        num_scalar_prefetch=2,
        grid=(b, n_hp, len(pairs)),
        in_specs=[
            pl.BlockSpec((2, tq, LANES), lambda i, h, s, qt, kt: (h, i * nq + qt[s], 0)),
            pl.BlockSpec((2, tk, LANES), lambda i, h, s, qt, kt: (h, i * nk + kt[s], 0)),
            pl.BlockSpec((2, FOX_HEAD_DIM, tk), lambda i, h, s, qt, kt: (h, 0, i * nk + kt[s])),
        ],
        out_specs=pl.BlockSpec((tq, LANES), lambda i, h, s, qt, kt: (i * nq + qt[s], h)),
        scratch_shapes=[pltpu.VMEM((2, 1, tq), f32), pltpu.VMEM((2, 1, tq), f32),
                        pltpu.VMEM((2, FOX_HEAD_DIM, tq), f32)],
    )
    return pl.pallas_call(
        _fox_kernel,
        out_shape=jax.ShapeDtypeStruct((b * l, FOX_INNER), f32),
        grid_spec=grid_spec,
        compiler_params=_cparams("parallel", "parallel", "arbitrary"),
        name="fox_prompt",
    )(qtab, ktab, qa, ka, vt)


def _dot_exact_rhs(a, b_exact_bf16):
    a0, a1, a2 = _split3(a)
    return _dot(a0, b_exact_bf16) + _dot(a1, b_exact_bf16) + _dot(a2, b_exact_bf16)


def _dot_nt(a, b):
    return lax.dot_general(a, b, (((1,), (1,)), ((), ())), preferred_element_type=f32)


def _fox_paged_kernel(pt_ref, *refs, n_pages, pps, scale):
    k_refs, v_refs, lf_refs = refs[:pps], refs[pps:2 * pps], refs[2 * pps:3 * pps]
    (q_ref, kn_ref, vn_ref, fr_ref, bias_ref, o_ref, logf_ref,
     s_scr, qbd_scr, acc_scr, psum_scr, m_scr, car_scr, fq_scr) = refs[3 * pps:]
    ph = pl.program_id(1)
    c = pl.program_id(2)
    last_c = pl.num_programs(2) - 1
    nq = q_ref.shape[0]
    hd = FOX_HEAD_DIM
    pg = LANES
    row = lax.broadcasted_iota(jnp.int32, (pg, pg), 0)
    lane = lax.broadcasted_iota(jnp.int32, (pg, pg), 1)
    upper = jnp.where(row <= lane, 1.0, 0.0).astype(bf16)

    def pad_rows(x, fill=0.0):
        return jnp.concatenate([x, jnp.full((pg - x.shape[0], x.shape[1]), fill, x.dtype)], axis=0)

    def per_head_rows(x):
        rep = jnp.broadcast_to(x[:, None, :], (FOX_HEADS, nq, x.shape[1])).reshape(FOX_HEADS * nq, x.shape[1])
        return pad_rows(rep)

    def key_sums(logf_rows):
        cum = _dot_exact_rhs(logf_rows, upper) + car_scr[...]
        car_scr[...] = jnp.broadcast_to(cum[:, pg - 1:pg], (pg, pg))
        return cum

    @pl.when((ph == 0) & (c == 0))
    def _():
        q_rep = jnp.broadcast_to((q_ref[...] * scale)[None], (FOX_HEADS, nq, FOX_INNER)).reshape(FOX_HEADS * nq, FOX_INNER)
        r5 = lax.broadcasted_iota(jnp.int32, (FOX_HEADS * nq, FOX_INNER), 0)
        l5 = lax.broadcasted_iota(jnp.int32, (FOX_HEADS * nq, FOX_INNER), 1)
        qbd_scr[...] = pad_rows(jnp.where(l5 // hd == r5 // nq, q_rep, 0.0)).astype(bf16)
        m_scr[...] = jnp.full_like(m_scr, -jnp.inf)
        car_scr[...] = jnp.zeros_like(car_scr)

    @pl.when(ph == 0)
    def _():
        qk = [_dot(qbd_scr[...], k_refs[i][0].reshape(FOX_INNER, pg).astype(bf16)) for i in range(pps)]
        local = [_dot_exact_rhs(per_head_rows(lf_refs[i][0]), upper) for i in range(pps)]
        car = car_scr[...]
        m = m_scr[...]
        for i in range(pps):
            cum = local[i] + car
            car = jnp.broadcast_to(cum[:, pg - 1:pg], (pg, pg))
            s = qk[i] - cum
            s_scr[c * pps + i] = s
            m = jnp.maximum(m, s)
        car_scr[...] = car
        m_scr[...] = m

    @pl.when((ph == 0) & (c == last_c))
    def _():
        logf_new = _log_sigmoid(fr_ref[...] + bias_ref[...])
        logf_ref[...] = logf_new
        cum = key_sums(per_head_rows(pad_rows(logf_new).T[:FOX_HEADS, :]))
        fq = jnp.sum(jnp.where(lane == row % nq, cum, 0.0), axis=1, keepdims=True)
        fq_scr[...] = jnp.broadcast_to(fq, (pg, pg))
        s = _dot(qbd_scr[...], pad_rows(kn_ref[...]).T.astype(bf16)) - cum
        s = jnp.where(lane <= row % nq, s, -jnp.inf)
        s_scr[n_pages] = s
        m = jnp.max(jnp.maximum(m_scr[...], s), axis=1, keepdims=True)
        m_scr[...] = jnp.broadcast_to(m, (pg, pg))

    @pl.when((ph == 1) & (c == 0))
    def _():
        psum_scr[...] = jnp.zeros_like(psum_scr)
        acc_scr[...] = jnp.zeros_like(acc_scr)

    def probs(s):
        fq = fq_scr[...]
        return jnp.exp((s + fq) - (m_scr[...] + fq))

    def accumulate(s, v_t_bf16):
        p = probs(s)
        psum_scr[...] += p
        acc_scr[...] += _dot_nt(v_t_bf16, p.astype(bf16))

    @pl.when(ph == 1)
    def _():
        ps = [probs(s_scr[c * pps + i]) for i in range(pps)]
        pvs = [_dot_nt(v_refs[i][0].reshape(FOX_INNER, pg).astype(bf16), ps[i].astype(bf16)) for i in range(pps)]
        psum = psum_scr[...]
        acc = acc_scr[...]
        for i in range(pps):
            psum = psum + ps[i]
            acc = acc + pvs[i]
        psum_scr[...] = psum
        acc_scr[...] = acc

    @pl.when((ph == 1) & (c == last_c))
    def _():
        accumulate(s_scr[n_pages], pad_rows(vn_ref[...]).T.astype(bf16))
        p0, p1, p2 = _split3(psum_scr[...])
        ones = jnp.ones((8, pg), bf16)
        l_row = (_dot_nt(ones, p0) + _dot_nt(ones, p1) + _dot_nt(ones, p2))[0:1, :]
        o_t = (acc_scr[...] / l_row).T
        o_ref[...] = jnp.concatenate([o_t[h * nq:(h + 1) * nq, h * hd:(h + 1) * hd] for h in range(FOX_HEADS)],
                                     axis=1)


def _fox_paged(proj, fcol, bias, cache_k, cache_v, cache_logf, page_table):
    n_seq, n_pages = page_table.shape
    nq = proj.shape[0] // n_seq
    pps = PAGES_PER_STEP
    n_chunks = n_pages // pps
    ck = jnp.transpose(cache_k, (0, 2, 3, 1))
    cv = jnp.transpose(cache_v, (0, 2, 3, 1))
    clf = jnp.transpose(cache_logf, (0, 2, 1))
    page = ck.shape[3]
    assert page == LANES and nq == 8

    def k_map(i):
        return lambda s, ph, c, pt: (pt[s * n_pages + jnp.where(ph == 0, c, n_chunks - 1) * pps + i], 0, 0, 0)

    def lf_map(i):
        return lambda s, ph, c, pt: (pt[s * n_pages + jnp.where(ph == 0, c, n_chunks - 1) * pps + i], 0, 0)

    def v_map(i):
        return lambda s, ph, c, pt: (pt[s * n_pages + jnp.where(ph == 0, 0, c) * pps + i], 0, 0, 0)

    blk = (1, FOX_HEADS, FOX_HEAD_DIM, page)
    in_specs = ([pl.BlockSpec(blk, k_map(i)) for i in range(pps)]
                + [pl.BlockSpec(blk, v_map(i)) for i in range(pps)]
                + [pl.BlockSpec((1, FOX_HEADS, page), lf_map(i)) for i in range(pps)]
                + [pl.BlockSpec((nq, FOX_INNER), lambda s, ph, c, pt: (s, 0)),
                   pl.BlockSpec((nq, FOX_INNER), lambda s, ph, c, pt: (s, 1)),
                   pl.BlockSpec((nq, FOX_INNER), lambda s, ph, c, pt: (s, 2)),
                   pl.BlockSpec((nq, LANES), lambda s, ph, c, pt: (s, fcol)),
                   pl.BlockSpec((1, LANES), lambda s, ph, c, pt: (0, 0))])
    grid_spec = pltpu.PrefetchScalarGridSpec(
        num_scalar_prefetch=1,
        grid=(n_seq, 2, n_chunks),
        in_specs=in_specs,
        out_specs=(pl.BlockSpec((nq, FOX_INNER), lambda s, ph, c, pt: (s, 0)),
                   pl.BlockSpec((nq, LANES), lambda s, ph, c, pt: (s, 0))),
        scratch_shapes=[pltpu.VMEM((n_pages + 1, page, page), f32),
                        pltpu.VMEM((page, FOX_INNER), bf16),
                        pltpu.VMEM((FOX_INNER, page), f32),
                        pltpu.VMEM((page, page), f32), pltpu.VMEM((page, page), f32),
                        pltpu.VMEM((page, page), f32), pltpu.VMEM((page, page), f32)],
    )
    return pl.pallas_call(
        functools.partial(_fox_paged_kernel, n_pages=n_pages, pps=pps, scale=FOX_HEAD_DIM ** -0.5),
        out_shape=(jax.ShapeDtypeStruct((n_seq * nq, FOX_INNER), f32),
                   jax.ShapeDtypeStruct((n_seq * nq, LANES), f32)),
        grid_spec=grid_spec,
        compiler_params=_cparams("arbitrary", "arbitrary", "arbitrary"),
        name="fox_paged",
    )(page_table.reshape(-1), *([ck] * pps), *([cv] * pps), *([clf] * pps), proj, proj, proj, proj, bias)


PAGE = 128
PAGES_PER_STEP = 8
HQ = FOX_HEADS * 8


def _fox_decode_kernel(pt_ref, *refs, n_pages, scale):
    pps = PAGES_PER_STEP
    k_refs, v_refs, lf_refs = refs[:pps], refs[pps:2 * pps], refs[2 * pps:3 * pps]
    (q_ref, kn_ref, vn_ref, fr_ref, bias_ref, o_ref, logf_ref,
     s_scr, qbd_scr, acc_scr, m_scr, l_scr, car_scr, fq_scr) = refs[3 * pps:]
    ph = pl.program_id(1)
    c = pl.program_id(2)
    last_c = pl.num_programs(2) - 1
    nq = q_ref.shape[0]
    past = n_pages * PAGE
    row = lax.broadcasted_iota(jnp.int32, (LANES, LANES), 0)
    lane = lax.broadcasted_iota(jnp.int32, (LANES, LANES), 1)
    head_to_cols = jnp.where((lane // nq == row) & (lane < HQ), 1.0, 0.0).astype(bf16)
    r8 = lax.broadcasted_iota(jnp.int32, (nq, LANES), 0)
    l8 = lax.broadcasted_iota(jnp.int32, (nq, LANES), 1)

    def expand_heads(x):
        x0, x1, x2 = _split3(x)
        return _dot(x0, head_to_cols) + _dot(x1, head_to_cols) + _dot(x2, head_to_cols)

    def pad_page(x, fill=0.0):
        return jnp.concatenate([x, jnp.full((PAGE - x.shape[0], x.shape[1]), fill, x.dtype)], axis=0)

    @pl.when((ph == 0) & (c == 0))
    def _():
        q_t = pad_page(q_ref[...] * scale).T
        query_to_cols = jnp.where((lane % nq == row) & (row < nq) & (lane < HQ), 1.0, 0.0).astype(bf16)
        q_exp = _dot(q_t.astype(bf16), query_to_cols)
        r5 = lax.broadcasted_iota(jnp.int32, (FOX_INNER, LANES), 0)
        l5 = lax.broadcasted_iota(jnp.int32, (FOX_INNER, LANES), 1)
        qbd_scr[...] = jnp.where(r5 // FOX_HEAD_DIM == l5 // nq, q_exp, 0.0).astype(bf16)
        m_scr[...] = jnp.full_like(m_scr, -jnp.inf)
        car_scr[...] = jnp.zeros_like(car_scr)

    @pl.when(ph == 0)
    def _():
        tri = _tri(PAGE)
        for i in range(pps):
            lf = lf_refs[i][0]
            lf = jnp.concatenate([lf, jnp.zeros((PAGE, LANES - lf.shape[1]), f32)], axis=1)
            cum = _dot_exact_lhs(tri, lf) + car_scr[0:1, :]
            car_scr[0:1, :] = cum[PAGE - 1:PAGE, :]
            s = _dot(k_refs[i][0].astype(bf16), qbd_scr[...]) - expand_heads(cum)
            s_scr[pl.ds(pl.multiple_of((c * pps + i) * PAGE, PAGE), PAGE), :] = s
            m_scr[0:1, :] = jnp.maximum(m_scr[0:1, :], jnp.max(s, axis=0, keepdims=True))

    @pl.when((ph == 0) & (c == last_c))
    def _():
        logf_new = _log_sigmoid(fr_ref[...] + bias_ref[...])
        logf_ref[...] = logf_new
        run = car_scr[0:1, :]
        f_rows = []
        for jrow in range(nq):
            run = run + logf_new[jrow:jrow + 1, :]
            f_rows.append(run)
        f_exp = expand_heads(jnp.concatenate(f_rows, axis=0))
        fq_scr[0:1, :] = jnp.sum(jnp.where(l8 % nq == r8, f_exp, 0.0), axis=0, keepdims=True)
        s_new = _dot(pad_page(kn_ref[...]).astype(bf16), qbd_scr[...])[:nq] - f_exp
        s_new = jnp.where(r8 <= l8 % nq, s_new, -jnp.inf)
        s_scr[past:past + PAGE, :] = pad_page(s_new, -jnp.inf)
        m_scr[0:1, :] = jnp.maximum(m_scr[0:1, :], jnp.max(s_new, axis=0, keepdims=True))

    @pl.when((ph == 1) & (c == 0))
    def _():
        l_scr[...] = jnp.zeros_like(l_scr)
        acc_scr[...] = jnp.zeros_like(acc_scr)

    def accumulate(s, v_bf16):
        fq = fq_scr[0:1, :]
        p = jnp.exp((s + fq) - (m_scr[0:1, :] + fq))
        l_scr[0:1, :] += jnp.sum(p, axis=0, keepdims=True)
        acc_scr[...] += _dot(p.T.astype(bf16), v_bf16)

    @pl.when(ph == 1)
    def _():
        for i in range(pps):
            s = s_scr[pl.ds(pl.multiple_of((c * pps + i) * PAGE, PAGE), PAGE), :]
            accumulate(s, v_refs[i][0].astype(bf16))

    @pl.when((ph == 1) & (c == last_c))
    def _():
        accumulate(s_scr[past:past + PAGE, :], pad_page(vn_ref[...]).astype(bf16))
        l_col = jnp.sum(jnp.where(row == lane, jnp.broadcast_to(l_scr[0:1, :], (LANES, LANES)), 0.0),
                        axis=1, keepdims=True)
        o_full = acc_scr[...] / l_col
        hd = FOX_HEAD_DIM
        o_ref[...] = jnp.concatenate([o_full[h * nq:(h + 1) * nq, h * hd:(h + 1) * hd] for h in range(FOX_HEADS)],
                                     axis=1)


def _fox_decode(proj, fcol, bias, cache_k, cache_v, cache_logf, page_table):
    n_seq, n_pages = page_table.shape
    nq = proj.shape[0] // n_seq
    n_pool = cache_k.shape[0]
    pps = PAGES_PER_STEP
    n_chunks = n_pages // pps
    ck = cache_k.reshape(n_pool, PAGE, FOX_INNER)
    cv = cache_v.reshape(n_pool, PAGE, FOX_INNER)

    def k_map(i):
        return lambda s, ph, c, pt: (pt[s * n_pages + jnp.where(ph == 0, c, n_chunks - 1) * pps + i], 0, 0)

    def v_map(i):
        return lambda s, ph, c, pt: (pt[s * n_pages + jnp.where(ph == 0, 0, c) * pps + i], 0, 0)

    in_specs = ([pl.BlockSpec((1, PAGE, FOX_INNER), k_map(i)) for i in range(pps)]
                + [pl.BlockSpec((1, PAGE, FOX_INNER), v_map(i)) for i in range(pps)]
                + [pl.BlockSpec((1, PAGE, FOX_HEADS), k_map(i)) for i in range(pps)]
                + [pl.BlockSpec((nq, FOX_INNER), lambda s, ph, c, pt: (s, 0)),
                   pl.BlockSpec((nq, FOX_INNER), lambda s, ph, c, pt: (s, 1)),
                   pl.BlockSpec((nq, FOX_INNER), lambda s, ph, c, pt: (s, 2)),
                   pl.BlockSpec((nq, LANES), lambda s, ph, c, pt: (s, fcol)),
                   pl.BlockSpec((1, LANES), lambda s, ph, c, pt: (0, 0))])
    grid_spec = pltpu.PrefetchScalarGridSpec(
        num_scalar_prefetch=1,
        grid=(n_seq, 2, n_chunks),
        in_specs=in_specs,
        out_specs=(pl.BlockSpec((nq, FOX_INNER), lambda s, ph, c, pt: (s, 0)),
                   pl.BlockSpec((nq, LANES), lambda s, ph, c, pt: (s, 0))),
        scratch_shapes=[pltpu.VMEM((n_pages * PAGE + PAGE, LANES), f32),
                        pltpu.VMEM((FOX_INNER, LANES), bf16),
                        pltpu.VMEM((LANES, FOX_INNER), f32),
                        pltpu.VMEM((8, LANES), f32), pltpu.VMEM((8, LANES), f32),
                        pltpu.VMEM((8, LANES), f32), pltpu.VMEM((8, LANES), f32)],
    )
    return pl.pallas_call(
        functools.partial(_fox_decode_kernel, n_pages=n_pages, scale=FOX_HEAD_DIM ** -0.5),
        out_shape=(jax.ShapeDtypeStruct((n_seq * nq, FOX_INNER), f32),
                   jax.ShapeDtypeStruct((n_seq * nq, LANES), f32)),
        grid_spec=grid_spec,
        compiler_params=_cparams("arbitrary", "arbitrary", "arbitrary"),
        name="fox_decode",
    )(page_table.reshape(-1), *([ck] * pps), *([cv] * pps), *([cache_logf] * pps), proj, proj, proj, proj, bias)


CONV_TAIL = 8
COL0_QKV = 0
COL0_XS = 3 * GDN_INNER
COL0_ZS = COL0_XS + SSD_INNER
COL0_ZG = COL0_ZS + SSD_INNER
COL0_BC = COL0_ZG + GDN_INNER
COL0_SM = COL0_BC + 2 * SSD_GROUPS * SSD_STATE
assert COL0_SM + LANES == IN0_PAD


def _softplus(x):
    return jnp.maximum(x, 0.0) + jnp.log1p(jnp.exp(-jnp.abs(x)))


def _silu(x):
    return x * jax.nn.sigmoid(x)


def _conv_silu(ubuf, u_refs, tail_ref, cw_ref, cb_ref, first, rows, lpad):
    @pl.when(first)
    def _():
        ubuf[0:CONV_TAIL, :] = tail_ref[0]

    @pl.when(jnp.logical_not(first))
    def _():
        ubuf[0:CONV_TAIL, :] = ubuf[rows:rows + CONV_TAIL, :]

    col = 0
    for u_ref in u_refs:
        ubuf[CONV_TAIL:CONV_TAIL + rows, col:col + u_ref.shape[1]] = u_ref[...]
        col += u_ref.shape[1]
    if lpad > rows:
        ubuf[CONV_TAIL + rows:, :] = jnp.zeros((lpad - rows, ubuf.shape[1]), f32)
    acc = cb_ref[...]
    for j in range(CONV_K):
        off = CONV_TAIL - (CONV_K - 1) + j
        acc = acc + cw_ref[j:j + 1, :] * ubuf[off:off + lpad, :]
    return _silu(acc)


def _pad_rows(x, lpad):
    rows = x.shape[0]
    if lpad == rows:
        return x
    return jnp.concatenate([x, jnp.zeros((lpad - rows, x.shape[1]), x.dtype)], axis=0)


def _head_scalars(sm_ref, bias_ref, coef_ref, rows, lpad):
    raw = _pad_rows(sm_ref[...], lpad)
    valid = lax.broadcasted_iota(jnp.int32, (lpad, LANES), 0) < rows
    sp = jnp.where(valid, _softplus(raw + bias_ref[...]), 0.0)
    a = sp * coef_ref[...]
    cum = _dot_exact_lhs(_tri(lpad), a)
    return raw, valid, sp, cum


def _ssd_kernel(ux_ref, ubc_ref, z_ref, sm_ref, tail_ref, h0_ref, cw_ref, cb_ref, bias_ref, coef_ref, dd_ref, nw_ref,
                y_ref, hout_ref, ubuf, *, rows, lpad):
    j = pl.program_id(1)
    n, p = SSD_STATE, SSD_HEAD_DIM
    xbc = _conv_silu(ubuf, (ux_ref, ubc_ref), tail_ref, cw_ref, cb_ref, j == 0, rows, lpad)

    @pl.when(j == 0)
    def _():
        hout_ref[...] = h0_ref[...]

    _, _, dt, acum = _head_scalars(sm_ref, bias_ref, coef_ref, rows, lpad)
    acum_t = acum.T
    dt_t = dt.T
    xs = xbc[:, :SSD_INNER]
    xs_t = xs.T
    r = lax.broadcasted_iota(jnp.int32, (lpad, lpad), 0)
    c = lax.broadcasted_iota(jnp.int32, (lpad, lpad), 1)
    causal = c <= r
    heads = range(SSD_HEADS)
    group_of = [h // (SSD_HEADS // SSD_GROUPS) for h in heads]
    bms = [xbc[:, SSD_INNER + g * n:SSD_INNER + (g + 1) * n] for g in range(SSD_GROUPS)]
    cms = [xbc[:, SSD_INNER + SSD_GROUPS * n + g * n:SSD_INNER + SSD_GROUPS * n + (g + 1) * n]
           for g in range(SSD_GROUPS)]
    cbs = [_dot_nt(cms[g].astype(bf16), bms[g].astype(bf16)) for g in range(SSD_GROUPS)]
    a_cols = [acum[:, h:h + 1] for h in heads]
    a_lasts = [acum[lpad - 1:lpad, h:h + 1] for h in heads]
    x_hs = [xs[:, h * p:(h + 1) * p] for h in heads]
    hsts = [hout_ref[0, h] for h in heads]
    scores = [(cbs[group_of[h]] * jnp.exp(jnp.where(causal, a_cols[h] - acum_t[h:h + 1, :], -jnp.inf))).astype(bf16)
              for h in heads]
    xdts = [(x_hs[h] * dt[:, h:h + 1]).astype(bf16) for h in heads]
    c_exps = [(cms[group_of[h]] * jnp.exp(a_cols[h])).astype(bf16) for h in heads]
    xdt_ts = [(xs_t[h * p:(h + 1) * p, :] * dt_t[h:h + 1, :]).astype(bf16) for h in heads]
    b_ends = [(bms[group_of[h]] * jnp.exp(a_lasts[h] - a_cols[h])).astype(bf16) for h in heads]
    y_diags = [_dot(scores[h], xdts[h]) for h in heads]
    y_offs = [_dot_nt(c_exps[h], hsts[h].astype(bf16)) for h in heads]
    upds = [_dot(xdt_ts[h], b_ends[h]) for h in heads]
    for h in heads:
        hout_ref[0, h] = hsts[h] * jnp.exp(a_lasts[h]) + upds[h]
    ys = [y_diags[h] + y_offs[h] + dd_ref[:, h * p:(h + 1) * p] * x_hs[h] for h in heads]
    y = jnp.concatenate(ys, axis=1)[:rows]
    gated = y * _silu(z_ref[...])
    ms = jnp.mean(gated * gated, axis=-1, keepdims=True)
    y_ref[...] = gated * lax.rsqrt(ms + EPS) * nw_ref[...]


def _ssd(proj, tail, h0, prm, b, l, rows, lpad):
    nc = l // rows
    bc_w = 2 * SSD_GROUPS * SSD_STATE
    cw, cb, bias, coef, dd, nw = prm
    c2 = lambda i, j: (0, 0)
    return pl.pallas_call(
        functools.partial(_ssd_kernel, rows=rows, lpad=lpad),
        out_shape=(jax.ShapeDtypeStruct((b * l, SSD_INNER), f32),
                   jax.ShapeDtypeStruct((b, SSD_HEADS, SSD_HEAD_DIM, SSD_STATE), f32)),
        grid=(b, nc),
        in_specs=[pl.BlockSpec((rows, SSD_INNER), lambda i, j: (i * nc + j, COL0_XS // SSD_INNER)),
                  pl.BlockSpec((rows, bc_w), lambda i, j: (i * nc + j, COL0_BC // bc_w)),
                  pl.BlockSpec((rows, SSD_INNER), lambda i, j: (i * nc + j, COL0_ZS // SSD_INNER)),
                  pl.BlockSpec((rows, LANES), lambda i, j: (i * nc + j, COL0_SM // LANES)),
                  pl.BlockSpec((1, CONV_TAIL, SSD_CONV_CH), lambda i, j: (i, 0, 0)),
                  pl.BlockSpec((1, SSD_HEADS, SSD_HEAD_DIM, SSD_STATE), lambda i, j: (i, 0, 0, 0)),
                  pl.BlockSpec((CONV_K, SSD_CONV_CH), c2), pl.BlockSpec((1, SSD_CONV_CH), c2),
                  pl.BlockSpec((1, LANES), c2), pl.BlockSpec((1, LANES), c2),
                  pl.BlockSpec((1, SSD_INNER), c2), pl.BlockSpec((1, SSD_INNER), c2)],
        out_specs=(pl.BlockSpec((rows, SSD_INNER), lambda i, j: (i * nc + j, 0)),
                   pl.BlockSpec((1, SSD_HEADS, SSD_HEAD_DIM, SSD_STATE), lambda i, j: (i, 0, 0, 0))),
        scratch_shapes=[pltpu.VMEM((CONV_TAIL + lpad, SSD_CONV_CH), f32)],
        compiler_params=_cparams("arbitrary", "arbitrary"),
        name="ssd",
    )(proj, proj, proj, proj, tail, h0, cw, cb, bias, coef, dd, nw)


def _dot3(a, b):
    a_hi = a.astype(bf16)
    a_lo = (a - a_hi.astype(f32)).astype(bf16)
    b_hi = b.astype(bf16)
    b_lo = (b - b_hi.astype(f32)).astype(bf16)
    return _dot(a_hi, b_hi) + _dot(a_lo, b_hi) + _dot(a_hi, b_lo)


INV_BASE = 16


def _unit_lower_inverse(m, n):
    r = lax.broadcasted_iota(jnp.int32, (n, n), 0)
    c = lax.broadcasted_iota(jnp.int32, (n, n), 1)
    nb = min(INV_BASE, n)
    d = jnp.where((r // nb) == (c // nb), m, 0.0)
    inv = jnp.where(r == c, 1.0, 0.0) - d
    pw = d
    size = 2
    while size < nb:
        pw = _dot3(pw, pw)
        inv = inv + _dot3(inv, pw)
        size *= 2
    s = nb
    while s < n:
        lower_left = ((r // (2 * s)) == (c // (2 * s))) & ((r // s) % 2 == 1) & ((c // s) % 2 == 0)
        inv = inv - _dot3(_dot3(inv, jnp.where(lower_left, m, 0.0)), inv)
        s *= 2
    return inv


def _l2n(x):
    return x * lax.rsqrt(jnp.sum(x * x, axis=-1, keepdims=True) + EPS)


GDN_A_LANE = SSD_HEADS
GDN_B_LANE = SSD_HEADS + GDN_HEADS


def _gdn_kernel(u_ref, z_ref, sm_ref, tail_ref, s0_ref, cw_ref, cb_ref, bias_ref, coef_ref, nw_ref,
                o_ref, sout_ref, ubuf, *, rows, lpad):
    j = pl.program_id(1)
    dk = GDN_HEAD_DIM
    qkv = _conv_silu(ubuf, (u_ref,), tail_ref, cw_ref, cb_ref, j == 0, rows, lpad)

    @pl.when(j == 0)
    def _():
        sout_ref[...] = s0_ref[...]

    raw, valid, _, gcum = _head_scalars(sm_ref, bias_ref, coef_ref, rows, lpad)
    beta_all = jnp.where(valid, jax.nn.sigmoid(raw), 0.0)
    gcum_t = gcum.T
    r = lax.broadcasted_iota(jnp.int32, (lpad, lpad), 0)
    c = lax.broadcasted_iota(jnp.int32, (lpad, lpad), 1)
    outs = []
    for h in range(GDN_HEADS):
        qh = _l2n(qkv[:, h * dk:(h + 1) * dk]) * dk ** -0.5
        kh = _l2n(qkv[:, GDN_INNER + h * dk:GDN_INNER + (h + 1) * dk])
        vh = qkv[:, 2 * GDN_INNER + h * dk:2 * GDN_INNER + (h + 1) * dk]
        lane = GDN_A_LANE + h
        g_col = gcum[:, lane:lane + 1]
        g_row = gcum_t[lane:lane + 1, :]
        g_last = gcum[lpad - 1:lpad, lane:lane + 1]
        beta = beta_all[:, GDN_B_LANE + h:GDN_B_LANE + h + 1]
        decay = jnp.exp(jnp.where(c <= r, g_col - g_row, -jnp.inf))
        kb = kh * beta
        kh_b = kh.astype(bf16)
        kk = lax.dot_general(kb.astype(bf16), kh_b, (((1,), (1,)), ((), ())), preferred_element_type=f32)
        a_inv = _unit_lower_inverse(jnp.where(c < r, kk * decay, 0.0), lpad)
        a_hi = a_inv.astype(bf16)
        a_lo = (a_inv - a_hi.astype(f32)).astype(bf16)
        vb_b = (vh * beta).astype(bf16)
        kbe_b = (kb * jnp.exp(g_col)).astype(bf16)
        u = _dot(a_hi, vb_b) + _dot(a_lo, vb_b)
        w = _dot(a_hi, kbe_b) + _dot(a_lo, kbe_b)
        attn = lax.dot_general(qh.astype(bf16), kh_b, (((1,), (1,)), ((), ())), preferred_element_type=f32) * decay
        st = sout_ref[0, h]
        st_b = st.astype(bf16)
        v_new = (u - _dot(w.astype(bf16), st_b)).astype(bf16)
        o = _dot((qh * jnp.exp(g_col)).astype(bf16), st_b) + _dot(attn.astype(bf16), v_new)
        ke_t = (kh * jnp.exp(g_last - g_col)).T.astype(bf16)
        sout_ref[0, h] = st * jnp.exp(g_last) + _dot(ke_t, v_new)
        o = o[:rows]
        ms = jnp.mean(o * o, axis=-1, keepdims=True)
        outs.append(o * lax.rsqrt(ms + EPS) * nw_ref[...] * _silu(z_ref[:, h * dk:(h + 1) * dk]))
    o_ref[...] = jnp.concatenate(outs, axis=1)


def _unit_lower_inverse_multi(ms, n):
    r = lax.broadcasted_iota(jnp.int32, (n, n), 0)
    c = lax.broadcasted_iota(jnp.int32, (n, n), 1)
    nb = min(INV_BASE, n)
    diag_blk = (r // nb) == (c // nb)
    eye = jnp.where(r == c, 1.0, 0.0)
    pws = [jnp.where(diag_blk, m, 0.0) for m in ms]
    invs = [eye - d for d in pws]
    size = 2
    while size < nb:
        pws = [_dot3(pw, pw) for pw in pws]
        invs = [inv + _dot3(inv, pw) for inv, pw in zip(invs, pws)]
        size *= 2
    s = nb
    while s < n:
        lower_left = ((r // (2 * s)) == (c // (2 * s))) & ((r // s) % 2 == 1) & ((c // s) % 2 == 0)
        tmps = [_dot3(inv, jnp.where(lower_left, m, 0.0)) for inv, m in zip(invs, ms)]
        invs = [inv - _dot3(tmp, inv) for inv, tmp in zip(invs, tmps)]
        s *= 2
    return invs


def _gdn_staged_kernel(u_ref, z_ref, sm_ref, tail_ref, s0_ref, cw_ref, cb_ref, bias_ref, coef_ref, nw_ref,
                       o_ref, sout_ref, ubuf, *, rows, lpad):
    j = pl.program_id(1)
    dk = GDN_HEAD_DIM
    heads = range(GDN_HEADS)
    qkv = _conv_silu(ubuf, (u_ref,), tail_ref, cw_ref, cb_ref, j == 0, rows, lpad)

    @pl.when(j == 0)
    def _():
        sout_ref[...] = s0_ref[...]

    raw, valid, _, gcum = _head_scalars(sm_ref, bias_ref, coef_ref, rows, lpad)
    beta_all = jnp.where(valid, jax.nn.sigmoid(raw), 0.0)
    gcum_t = gcum.T
    r = lax.broadcasted_iota(jnp.int32, (lpad, lpad), 0)
    c = lax.broadcasted_iota(jnp.int32, (lpad, lpad), 1)
    qs = [_l2n(qkv[:, h * dk:(h + 1) * dk]) * dk ** -0.5 for h in heads]
    ks = [_l2n(qkv[:, GDN_INNER + h * dk:GDN_INNER + (h + 1) * dk]) for h in heads]
    vs = [qkv[:, 2 * GDN_INNER + h * dk:2 * GDN_INNER + (h + 1) * dk] for h in heads]
    g_cols = [gcum[:, GDN_A_LANE + h:GDN_A_LANE + h + 1] for h in heads]
    g_lasts = [gcum[lpad - 1:lpad, GDN_A_LANE + h:GDN_A_LANE + h + 1] for h in heads]
    betas = [beta_all[:, GDN_B_LANE + h:GDN_B_LANE + h + 1] for h in heads]
    decays = [jnp.exp(jnp.where(c <= r, g_cols[h] - gcum_t[GDN_A_LANE + h:GDN_A_LANE + h + 1, :], -jnp.inf))
              for h in heads]
    kbs = [ks[h] * betas[h] for h in heads]
    k_bs = [k.astype(bf16) for k in ks]
    kks = [_dot_nt(kbs[h].astype(bf16), k_bs[h]) for h in heads]
    attns = [_dot_nt(qs[h].astype(bf16), k_bs[h]) * decays[h] for h in heads]
    a_invs = _unit_lower_inverse_multi([jnp.where(c < r, kks[h] * decays[h], 0.0) for h in heads], lpad)
    a_his = [a.astype(bf16) for a in a_invs]
    a_los = [(a - ah.astype(f32)).astype(bf16) for a, ah in zip(a_invs, a_his)]
    vb_bs = [(vs[h] * betas[h]).astype(bf16) for h in heads]
    kbe_bs = [(kbs[h] * jnp.exp(g_cols[h])).astype(bf16) for h in heads]
    us = [_dot(a_his[h], vb_bs[h]) + _dot(a_los[h], vb_bs[h]) for h in heads]
    ws = [_dot(a_his[h], kbe_bs[h]) + _dot(a_los[h], kbe_bs[h]) for h in heads]
    sts = [sout_ref[0, h] for h in heads]
    st_bs = [st.astype(bf16) for st in sts]
    v_news = [(us[h] - _dot(ws[h].astype(bf16), st_bs[h])).astype(bf16) for h in heads]
    os_ = [_dot((qs[h] * jnp.exp(g_cols[h])).astype(bf16), st_bs[h]) + _dot(attns[h].astype(bf16), v_news[h])
           for h in heads]
    ke_ts = [(ks[h] * jnp.exp(g_lasts[h] - g_cols[h])).T.astype(bf16) for h in heads]
    for h in heads:
        sout_ref[0, h] = sts[h] * jnp.exp(g_lasts[h]) + _dot(ke_ts[h], v_news[h])
    outs = []
    for h in heads:
        o = os_[h][:rows]
        ms = jnp.mean(o * o, axis=-1, keepdims=True)
        outs.append(o * lax.rsqrt(ms + EPS) * nw_ref[...] * _silu(z_ref[:, h * dk:(h + 1) * dk]))
    o_ref[...] = jnp.concatenate(outs, axis=1)


def _gdn(proj, tail, s0, prm, b, l, rows, lpad):
    nc = l // rows
    cw, cb, bias, coef, nw = prm
    c2 = lambda i, j: (0, 0)
    width = 3 * GDN_INNER
    return pl.pallas_call(
        functools.partial(_gdn_staged_kernel, rows=rows, lpad=lpad),
        out_shape=(jax.ShapeDtypeStruct((b * l, GDN_INNER), f32),
                   jax.ShapeDtypeStruct((b, GDN_HEADS, GDN_HEAD_DIM, GDN_HEAD_DIM), f32)),
        grid=(b, nc),
        in_specs=[pl.BlockSpec((rows, width), lambda i, j: (i * nc + j, COL0_QKV // width)),
                  pl.BlockSpec((rows, GDN_INNER), lambda i, j: (i * nc + j, COL0_ZG // GDN_INNER)),
                  pl.BlockSpec((rows, LANES), lambda i, j: (i * nc + j, COL0_SM // LANES)),
                  pl.BlockSpec((1, CONV_TAIL, width), lambda i, j: (i, 0, 0)),
                  pl.BlockSpec((1, GDN_HEADS, GDN_HEAD_DIM, GDN_HEAD_DIM), lambda i, j: (i, 0, 0, 0)),
                  pl.BlockSpec((CONV_K, width), c2), pl.BlockSpec((1, width), c2),
                  pl.BlockSpec((1, LANES), c2), pl.BlockSpec((1, LANES), c2),
                  pl.BlockSpec((1, GDN_HEAD_DIM), c2)],
        out_specs=(pl.BlockSpec((rows, GDN_INNER), lambda i, j: (i * nc + j, 0)),
                   pl.BlockSpec((1, GDN_HEADS, GDN_HEAD_DIM, GDN_HEAD_DIM), lambda i, j: (i, 0, 0, 0))),
        scratch_shapes=[pltpu.VMEM((CONV_TAIL + lpad, width), f32)],
        compiler_params=_cparams("arbitrary", "arbitrary"),
        name="gdn",
    )(proj, proj, proj, tail, s0, cw, cb, bias, coef, nw)


def _layer0_params(p):
    def lanes(v, off):
        return jnp.zeros((1, LANES), f32).at[0, off:off + v.shape[0]].set(v.astype(f32))

    cw = p['conv_w'].astype(f32)
    cb = p['conv_b'].astype(f32).reshape(1, CONV_CH)
    bias = lanes(p['ssd_dt_bias'], 0) + lanes(p['gdn_dt_bias'], GDN_A_LANE)
    coef = lanes(-jnp.exp(p['ssd_A_log'].astype(f32)), 0) + lanes(-jnp.exp(p['gdn_A_log'].astype(f32)), GDN_A_LANE)
    dd = jnp.repeat(p['ssd_D'].astype(f32), SSD_HEAD_DIM).reshape(1, SSD_INNER)
    return {'ssd': (cw[:, :SSD_CONV_CH], cb[:, :SSD_CONV_CH], bias, coef, dd,
                    p['ssd_norm'].astype(f32).reshape(1, SSD_INNER)),
            'gdn': (cw[:, SSD_CONV_CH:], cb[:, SSD_CONV_CH:], bias, coef,
                    p['gdn_norm'].astype(f32).reshape(1, GDN_HEAD_DIM))}


def _rmsnorm(x, w):
    xf = x.astype(f32)
    y = xf * lax.rsqrt(jnp.mean(xf * xf, axis=-1, keepdims=True) + EPS)
    return (y * w.astype(f32)).astype(x.dtype)


def _l2norm(x):
    xf = x.astype(f32)
    return xf * lax.rsqrt(jnp.sum(xf * xf, axis=-1, keepdims=True) + EPS)


def _causal_conv(u, buf, w, bias):
    l = u.shape[1]
    upad = jnp.concatenate([buf.astype(u.dtype), u], axis=1)
    out = bias + sum(w[j] * upad[:, j:j + l] for j in range(CONV_K))
    return jax.nn.silu(out), upad[:, upad.shape[1] - (CONV_K - 1):]


def _ssd_chunked(x, dt, A, Bm, Cm, h0):
    b, l, h, p = x.shape
    n = Bm.shape[-1]
    q = math.gcd(l, CHUNK)
    c = l // q
    xdt = (x.astype(f32) * dt[..., None]).reshape(b, c, q, h, p)
    acum = jnp.cumsum((dt * A).reshape(b, c, q, h), axis=2)
    Bc = Bm.astype(f32).reshape(b, c, q, h, n)
    Cc = Cm.astype(f32).reshape(b, c, q, h, n)
    causal = jnp.tril(jnp.ones((q, q), bool))[None, None, :, :, None]
    seg = acum[:, :, :, None, :] - acum[:, :, None, :, :]
    decay = jnp.where(causal, jnp.exp(jnp.where(causal, seg, 0.0)), 0.0)
    scores = jnp.einsum('bcihn,bcjhn->bcijh', Cc, Bc) * decay
    y_diag = jnp.einsum('bcijh,bcjhp->bcihp', scores, xdt)
    to_end = jnp.exp(acum[:, :, -1:, :] - acum)
    chunk_states = jnp.einsum('bcjhn,bcjhp->bchpn', Bc * to_end[..., None], xdt)
    chunk_decay = jnp.exp(acum[:, :, -1, :])

    def step(state, inp):
        st, dec = inp
        return state * dec[..., None, None] + st, state

    h_final, h_enter = lax.scan(step, h0.astype(f32),
                                (jnp.moveaxis(chunk_states, 1, 0), jnp.moveaxis(chunk_decay, 1, 0)))
    h_enter = jnp.moveaxis(h_enter, 0, 1)
    y_off = jnp.einsum('bcihn,bchpn->bcihp', Cc * jnp.exp(acum)[..., None], h_enter)
    return (y_diag + y_off).reshape(b, l, h, p), h_final


def _gdn_chunked(q, k, v, g, beta, s0):
    b, l, h, dk = q.shape
    dv = v.shape[-1]
    Q = math.gcd(l, CHUNK)
    c = l // Q

    def blk(t):
        return t.astype(f32).reshape(b, c, Q, h, t.shape[-1]).transpose(0, 1, 3, 2, 4)

    qc = blk(q) * dk ** -0.5
    kc = blk(k)
    vc = blk(v)
    gc = jnp.cumsum(g.astype(f32).reshape(b, c, Q, h).transpose(0, 1, 3, 2), axis=-1)
    bc = beta.astype(f32).reshape(b, c, Q, h).transpose(0, 1, 3, 2)
    incl = jnp.tril(jnp.ones((Q, Q), bool))
    strict = jnp.tril(jnp.ones((Q, Q), bool), k=-1)
    diff = gc[..., :, None] - gc[..., None, :]
    decay = jnp.where(incl, jnp.exp(jnp.where(incl, diff, 0.0)), 0.0)
    kb = kc * bc[..., None]
    vb = vc * bc[..., None]
    m = jnp.where(strict, jnp.einsum('bchid,bchjd->bchij', kb, kc) * decay, 0.0)
    t_sys = jnp.eye(Q, dtype=f32) + m
    rhs = jnp.concatenate([vb, kb * jnp.exp(gc)[..., None]], axis=-1)
    sol = lax.linalg.triangular_solve(t_sys, rhs, left_side=True, lower=True)
    u = sol[..., :dv]
    w = sol[..., dv:]
    attn = jnp.einsum('bchid,bchjd->bchij', qc, kc) * decay
    qd = qc * jnp.exp(gc)[..., None]
    ke = kc * jnp.exp(gc[..., -1:] - gc)[..., None]
    dl = jnp.exp(gc[..., -1])

    def step(S, inp):
        u_c, w_c, qd_c, ke_c, a_c, dl_c = inp
        v_new = u_c - jnp.einsum('bhid,bhde->bhie', w_c, S)
        o = jnp.einsum('bhid,bhde->bhie', qd_c, S) + jnp.einsum('bhij,bhje->bhie', a_c, v_new)
        S = S * dl_c[..., None, None] + jnp.einsum('bhid,bhie->bhde', ke_c, v_new)
        return S, o

    xs = tuple(jnp.moveaxis(t, 1, 0) for t in (u, w, qd, ke, attn, dl))
    s_final, o = lax.scan(step, s0.astype(f32), xs)
    return jnp.transpose(o, (1, 0, 3, 2, 4)).reshape(b, l, h, dv), s_final


def _fox_attention(q, k, v, logf, past):
    b, l, h, d = q.shape
    if past is None:
        k_all, v_all, logf_all = k.astype(f32), v.astype(f32), logf
    else:
        k_past, v_past, logf_past = past
        k_all = jnp.concatenate([k_past.astype(f32), k.astype(f32)], axis=1)
        v_all = jnp.concatenate([v_past.astype(f32), v.astype(f32)], axis=1)
        logf_all = jnp.concatenate([logf_past.astype(f32), logf], axis=1)
    n_keys = k_all.shape[1]
    offset = n_keys - l
    F = jnp.cumsum(logf_all, axis=1)
    Fk = jnp.moveaxis(F, 1, 2)
    Fq = F[:, offset:]
    kpos = jnp.arange(n_keys)
    qb = math.gcd(l, Q_BLOCK)
    nb = l // qb
    scale = d ** -0.5

    def block(args):
        q_i, fq_i, qpos_i = args
        s = jnp.einsum('bqhd,bkhd->bhqk', q_i.astype(f32), k_all) * scale
        s = s + jnp.moveaxis(fq_i, 1, 2)[..., None] - Fk[:, :, None, :]
        s = jnp.where(qpos_i[:, None] >= kpos[None, :], s, -jnp.inf)
        pr = jax.nn.softmax(s, axis=-1)
        return jnp.einsum('bhqk,bkhd->bqhd', pr, v_all)

    q_blocks = jnp.moveaxis(q.reshape(b, nb, qb, h, d), 1, 0)
    fq_blocks = jnp.moveaxis(Fq.reshape(b, nb, qb, h), 1, 0)
    qpos_blocks = (offset + jnp.arange(l)).reshape(nb, qb)
    o = lax.map(block, (q_blocks, fq_blocks, qpos_blocks))
    return jnp.moveaxis(o, 0, 1).reshape(b, l, h, d)


def _s5_scan(u, A_re, A_im, log_step, B_re, B_im, C_re, C_im, Dd, x0_re, x0_im):
    b, l, _ = u.shape
    uu = u.astype(f32).reshape(b, l, S5_GROUPS, S5_CH)
    ar = A_re.astype(f32)
    ai = A_im.astype(f32)
    step = jnp.exp(log_step.astype(f32))[:, None]
    mag = jnp.exp(ar * step)
    lb_re = mag * jnp.cos(ai * step)
    lb_im = mag * jnp.sin(ai * step)
    den = ar * ar + ai * ai
    nr = lb_re - 1.0
    cr = (nr * ar + lb_im * ai) / den
    ci = (lb_im * ar - nr * ai) / den
    bb_re = cr[..., None] * B_re - ci[..., None] * B_im
    bb_im = cr[..., None] * B_im + ci[..., None] * B_re
    bu_re = jnp.einsum('gph,blgh->blgp', bb_re, uu)
    bu_im = jnp.einsum('gph,blgh->blgp', bb_im, uu)
    shp = bu_re.shape
    a_re = jnp.broadcast_to(lb_re, shp)
    a_im = jnp.broadcast_to(lb_im, shp)

    def combine(e1, e2):
        a1r, a1i, b1r, b1i = e1
        a2r, a2i, b2r, b2i = e2
        return (a2r * a1r - a2i * a1i, a2r * a1i + a2i * a1r,
                a2r * b1r - a2i * b1i + b2r, a2r * b1i + a2i * b1r + b2i)

    pr, pi, xr, xi = lax.associative_scan(combine, (a_re, a_im, bu_re, bu_im), axis=1)
    x0r = x0_re.astype(f32)[:, None]
    x0i = x0_im.astype(f32)[:, None]
    xr = xr + pr * x0r - pi * x0i
    xi = xi + pr * x0i + pi * x0r
    y = (jnp.einsum('ghp,blgp->blgh', C_re, xr) - jnp.einsum('ghp,blgp->blgh', C_im, xi)
         + Dd * uu)
    return y.reshape(b, l, S5_INNER), xr[:, -1], xi[:, -1]


def _ada_kernel(c_ref, w_ref, b_ref, o_ref):
    o_ref[...] = _dot(_silu(c_ref[...]).astype(bf16), w_ref[...].astype(bf16)) + b_ref[...]


def _ada(c, w_ada, b_ada, tn):
    rows, d = c.shape
    n = w_ada.shape[1]
    return pl.pallas_call(
        _ada_kernel,
        out_shape=jax.ShapeDtypeStruct((rows, n), f32),
        grid=(n // tn,),
        in_specs=[pl.BlockSpec((rows, d), lambda j: (0, 0)),
                  pl.BlockSpec((d, tn), lambda j: (0, j)),
                  pl.BlockSpec((1, tn), lambda j: (0, j))],
        out_specs=pl.BlockSpec((rows, tn), lambda j: (0, j)),
        compiler_params=_cparams("parallel"),
        name="ada",
    )(c, w_ada, b_ada.reshape(1, n))


def _mods(mod, l, tm):
    parts = jnp.split(mod, 6, axis=-1)
    if l % tm == 0:
        return [p[:, None, :] for p in parts]
    b = mod.shape[0]
    return [jnp.repeat(p, l, axis=0).reshape((b * l) // tm, tm, D_MODEL) for p in parts]


def _run_trunk(x, mod0, mod1, conv_buf, ssd_h0, gdn_s0, past, s5_re0, s5_im0, p, tm):
    b, l, d = x.shape
    t = b * l
    x2 = x.reshape(t, d)

    sh1, sc1, g1, sh2, sc2, g2 = _mods(mod0, l, tm)
    proj0 = _inproj(x2, p['norm_mix0'], sc1, sh1, p['w_in0'], tm, 1152)
    tail = jnp.pad(conv_buf.astype(f32), ((0, 0), (CONV_TAIL - (CONV_K - 1), 0), (0, 0)))
    ssd_rows, ssd_lpad, gdn_rows, gdn_lpad = (256, 256, 128, 128) if l % 256 == 0 else (l, 128, l, 128)
    y_ssd, ssd_new = _ssd(proj0, tail[..., :SSD_CONV_CH], ssd_h0.astype(f32), p['l0']['ssd'], b, l, ssd_rows, ssd_lpad)
    o_gdn, gdn_new = _gdn(proj0, tail[..., SSD_CONV_CH:], gdn_s0.astype(f32), p['l0']['gdn'], b, l, gdn_rows, gdn_lpad)
    last = proj0.reshape(b, l, IN0_PAD)[:, l - (CONV_K - 1):]
    conv_new = jnp.concatenate([last[..., COL0_XS:COL0_XS + SSD_INNER], last[..., COL0_BC:COL0_SM],
                                last[..., COL0_QKV:COL0_QKV + 3 * GDN_INNER]], axis=-1)
    w_out0 = p['w_out0']
    x2 = _outproj([y_ssd, o_gdn], [w_out0[:SSD_INNER], w_out0[SSD_INNER:]], x2, g1, tm)
    x2 = _ffn(x2, p['norm_ffn0'], sc2, sh2, g2, p['ffn_w1'], p['ffn_w3'], p['ffn_w2'], min(tm, 512), 1408)

    sh1, sc1, g1, sh2, sc2, g2 = _mods(mod1, l, tm)
    proj2 = _inproj(x2, p['norm_mix1'], sc1, sh1, p['w_in1'], tm, 1152)
    proj = proj2.reshape(b, l, IN1_PAD)
    q = proj[..., :FOX_INNER].reshape(b, l, FOX_HEADS, FOX_HEAD_DIM)
    k_new = proj[..., FOX_INNER:2 * FOX_INNER].reshape(b, l, FOX_HEADS, FOX_HEAD_DIM)
    v_new = proj[..., 2 * FOX_INNER:3 * FOX_INNER].reshape(b, l, FOX_HEADS, FOX_HEAD_DIM)
    sp = p['s5']
    ucol = (3 * FOX_INNER) // S5_QC
    fcol = (3 * FOX_INNER + S5_INNER) // LANES
    if past is None:
        logf_pad, fcs, fcs_t = _fcum(proj2, fcol, p['fox_f_bias'], b, l, 512)
        logf = logf_pad[:, :FOX_HEADS].reshape(b, l, FOX_HEADS)
        o_fox = _fox_prompt(proj2, fcs, b, l, 1024, 1024)
        o_s5, s5_re, s5_im = _s5_chain(proj2, ucol, s5_re0.reshape(b, 1, S5_LANES),
                                       s5_im0.reshape(b, 1, S5_LANES), sp, b, l, 64)
    else:
        o_fox, logf_pad = _fox_paged(proj2, fcol, p['fox_f_bias'], *past)
        logf = logf_pad[:, :FOX_HEADS].reshape(b, l, FOX_HEADS)
        o_s5, s5_re, s5_im = _s5_batch(proj2, ucol, s5_re0.reshape(b, S5_LANES),
                                       s5_im0.reshape(b, S5_LANES), sp, b, l)
    s5_re = s5_re.reshape(b, S5_GROUPS, S5_STATE)
    s5_im = s5_im.reshape(b, S5_GROUPS, S5_STATE)
    w_out1 = p['w_out1']
    x2 = _outproj([o_fox, o_s5], [w_out1[:FOX_INNER], w_out1[FOX_INNER:].reshape(S5_QUARTERS, S5_QC, d)], x2, g1, tm)
    halves = 2 if t % (2 * tm) == 0 else 1
    y_out = _moe_grouped(x2, p['norm_ffn1'], sc2, sh2, g2, p['router_w'], p['moe_w1'], p['moe_w3'], p['moe_w2'],
                         p['norm_final'], tm, halves, 512, 320).reshape(b, l, d)
    return y_out, conv_new, ssd_new, gdn_new, k_new, v_new, logf, s5_re, s5_im


def kernel(x_prompt, x_sample, state_conv0, state_ssd, state_gdn, cache_k, cache_v, cache_logf,
           state_s5_re, state_s5_im, page_table, c_prompt, c_sample,
           ada0_w, ada0_b, norm_mix0, w_in0, conv0_w, conv0_b, ssd_dt_bias, ssd_A_log, ssd_D,
           ssd_norm, gdn_dt_bias, gdn_A_log, gdn_norm, w_out0, norm_ffn0, ffn_w1, ffn_w3, ffn_w2,
           ada1_w, ada1_b, norm_mix1, w_in1, fox_f_bias, s5_A_re, s5_A_im, s5_log_step,
           s5_B_re, s5_B_im, s5_C_re, s5_C_im, s5_D, glu_w, glu_b, w_out1, norm_ffn1,
           router_w, moe_w1, moe_w3, moe_w2, norm_final):
    d = D_MODEL
    c0 = CONV_CH
    w_in0p = jnp.concatenate([
        w_in0[:, SSD_CONV_CH:c0],
        w_in0[:, :SSD_INNER],
        w_in0[:, c0:c0 + SSD_INNER],
        w_in0[:, c0 + SSD_INNER + SSD_HEADS:c0 + SSD_INNER + SSD_HEADS + GDN_INNER],
        w_in0[:, SSD_INNER:SSD_CONV_CH],
        w_in0[:, c0 + SSD_INNER:c0 + SSD_INNER + SSD_HEADS],
        w_in0[:, c0 + SSD_INNER + SSD_HEADS + GDN_INNER:],
        jnp.zeros((d, IN0_PAD - w_in0.shape[1]), f32)], axis=1).astype(bf16)
    f0 = 3 * FOX_INNER
    w_in1p = jnp.concatenate([
        w_in1[:, :f0],
        w_in1[:, f0 + FOX_HEADS:],
        w_in1[:, f0:f0 + FOX_HEADS],
        jnp.zeros((d, IN1_PAD - w_in1.shape[1]), f32)], axis=1).astype(bf16)
    router_wp = jnp.concatenate([router_w.astype(f32), jnp.zeros((d, LANES - N_EXPERTS), f32)], axis=1)
    p = {
        'ada0_w': ada0_w, 'ada0_b': ada0_b, 'norm_mix0': norm_mix0.reshape(1, d), 'w_in0': w_in0p,
        'conv0_w': conv0_w, 'conv0_b': conv0_b, 'ssd_dt_bias': ssd_dt_bias, 'ssd_A_log': ssd_A_log,
        'ssd_D': ssd_D, 'ssd_norm': ssd_norm, 'gdn_dt_bias': gdn_dt_bias, 'gdn_A_log': gdn_A_log,
        'gdn_norm': gdn_norm, 'w_out0': w_out0.astype(bf16), 'norm_ffn0': norm_ffn0.reshape(1, d),
        'ffn_w1': ffn_w1.astype(bf16), 'ffn_w3': ffn_w3.astype(bf16), 'ffn_w2': ffn_w2.astype(bf16),
        'ada1_w': ada1_w, 'ada1_b': ada1_b, 'norm_mix1': norm_mix1.reshape(1, d), 'w_in1': w_in1p,
        'fox_f_bias': jnp.concatenate([fox_f_bias.astype(f32), jnp.zeros((LANES - FOX_HEADS,), f32)]).reshape(1, LANES),
        's5_A_re': s5_A_re, 's5_A_im': s5_A_im, 's5_log_step': s5_log_step,
        's5_B_re': s5_B_re, 's5_B_im': s5_B_im, 's5_C_re': s5_C_re, 's5_C_im': s5_C_im, 's5_D': s5_D,
        'glu_w': glu_w, 'glu_b': glu_b, 'w_out1': w_out1.astype(bf16), 'norm_ffn1': norm_ffn1.reshape(1, d),
        'router_w': router_wp, 'moe_w1': moe_w1.astype(bf16), 'moe_w3': moe_w3.astype(bf16),
        'moe_w2': moe_w2.astype(bf16), 'norm_final': norm_final.astype(f32).reshape(1, d),
    }
    p['s5'] = _s5_params(p)
    p['l0'] = _layer0_params({'conv_w': conv0_w, 'conv_b': conv0_b, 'ssd_dt_bias': ssd_dt_bias, 'ssd_A_log': ssd_A_log,
                              'ssd_D': ssd_D, 'ssd_norm': ssd_norm, 'gdn_dt_bias': gdn_dt_bias,
                              'gdn_A_log': gdn_A_log, 'gdn_norm': gdn_norm})
    bp = x_prompt.shape[0]
    c_all = jnp.concatenate([c_prompt, c_sample], axis=0).astype(f32)
    mod0 = _ada(c_all, ada0_w, ada0_b, 1536)
    mod1 = _ada(c_all, ada1_w, ada1_b, 1536)
    outs_p = _run_trunk(
        x_prompt, mod0[:bp], mod1[:bp],
        jnp.zeros((bp, CONV_K - 1, CONV_CH), x_prompt.dtype),
        jnp.zeros((bp, SSD_HEADS, SSD_HEAD_DIM, SSD_STATE), f32),
        jnp.zeros((bp, GDN_HEADS, GDN_HEAD_DIM, GDN_HEAD_DIM), f32),
        None,
        jnp.zeros((bp, S5_GROUPS, S5_STATE), f32),
        jnp.zeros((bp, S5_GROUPS, S5_STATE), f32),
        p, 1024)
    outs_s = _run_trunk(
        x_sample, mod0[bp:], mod1[bp:], state_conv0, state_ssd, state_gdn, (cache_k, cache_v, cache_logf, page_table),
        state_s5_re, state_s5_im, p, 256)
    return (outs_p[0], outs_s[0]) + tuple(outs_p[1:]) + tuple(outs_s[1:])
```

```python
import functools
import math

import jax
import jax.numpy as jnp
from jax import lax
from jax.experimental import pallas as pl
from jax.experimental.pallas import tpu as pltpu

f32 = jnp.float32
bf16 = jnp.bfloat16

D_MODEL = 1024
CONV_K = 4
CHUNK = 64
SSD_HEADS = 8
SSD_HEAD_DIM = 64
SSD_INNER = 512
SSD_GROUPS = 2
SSD_STATE = 64
GDN_HEADS = 4
GDN_HEAD_DIM = 128
GDN_INNER = 512
SSD_CONV_CH = 768
CONV_CH = 2304
FOX_HEADS = 8
FOX_HEAD_DIM = 64
FOX_INNER = 512
Q_BLOCK = 128
S5_CH = 16
S5_GROUPS = 32
S5_INNER = 512
S5_STATE = 64
D_FF = 2816
N_EXPERTS = 8
D_FF_EXPERT = 3584
EPS = 1e-6

LANES = 128
IN0_PAD = 3456
IN1_PAD = 2304
VMEM_LIMIT = 56 * 1024 * 1024

TM_PROMPT = 1024
TM_SAMPLE = 256
TN_INPROJ = 1152
TM_FFN = 512
TF_FFN = 1408
TF_MOE = 896
MOE_CAP = 288
TN_ADA = 1536
SSD_ROWS = 256
GDN_ROWS = 128
PAD_ROWS = 128
FOX_TQ = 1024
FOX_TK = 1024
FCUM_ROWS = 512
S5_STEPS = 64


def _cparams(*sem):
    return pltpu.CompilerParams(dimension_semantics=sem, vmem_limit_bytes=VMEM_LIMIT)


def _modnorm(x, g, scale, shift):
    ms = jnp.mean(x * x, axis=-1, keepdims=True)
    y = x * lax.rsqrt(ms + EPS) * g
    return y * (1.0 + scale) + shift


def _split3(a):
    a0 = a.astype(bf16)
    r = a - a0.astype(f32)
    a1 = r.astype(bf16)
    a2 = (r - a1.astype(f32)).astype(bf16)
    return a0, a1, a2


def _dot(a, b):
    return jnp.dot(a, b, preferred_element_type=f32)


def _dot_f32(a, b):
    a0, a1, a2 = _split3(a)
    b0, b1, b2 = _split3(b)
    return (_dot(a0, b0) + _dot(a0, b1) + _dot(a1, b0)
            + _dot(a1, b1) + _dot(a0, b2) + _dot(a2, b0))


def _mod_spec(mod, n_tiles):
    n_mod, rows, d = mod.shape
    per = n_tiles // n_mod
    return pl.BlockSpec((1, rows, d), lambda i, *_: (i // per, 0, 0))


def _inproj_kernel(x_ref, g_ref, sc_ref, sh_ref, w_ref, o_ref, h_scr):
    @pl.when(pl.program_id(1) == 0)
    def _():
        h_scr[...] = _modnorm(x_ref[...], g_ref[...], sc_ref[0], sh_ref[0]).astype(bf16)

    o_ref[...] = _dot(h_scr[...], w_ref[...])


def _inproj(x, g, scale, shift, w, tm, tn):
    t, d = x.shape
    n = w.shape[1]
    n_tiles = t // tm
    return pl.pallas_call(
        _inproj_kernel,
        out_shape=jax.ShapeDtypeStruct((t, n), f32),
        grid=(n_tiles, n // tn),
        in_specs=[
            pl.BlockSpec((tm, d), lambda i, j: (i, 0)),
            pl.BlockSpec((1, d), lambda i, j: (0, 0)),
            _mod_spec(scale, n_tiles),
            _mod_spec(shift, n_tiles),
            pl.BlockSpec((d, tn), lambda i, j: (0, j)),
        ],
        out_specs=pl.BlockSpec((tm, tn), lambda i, j: (i, j)),
        scratch_shapes=[pltpu.VMEM((tm, d), bf16)],
        compiler_params=_cparams("parallel", "arbitrary"),
        name="inproj",
    )(x, g, scale, shift, w)


def _outproj_kernel(*refs, n_parts):
    mix_refs = refs[:n_parts]
    w_refs = refs[n_parts:2 * n_parts]
    x_ref, gate_ref, o_ref = refs[2 * n_parts:]
    acc = None
    for m_ref, w_ref in zip(mix_refs, w_refs):
        if len(m_ref.shape) == 3:
            terms = [_dot(m_ref[q].astype(bf16), w_ref[q]) for q in range(m_ref.shape[0])]
        else:
            terms = [_dot(m_ref[...].astype(bf16), w_ref[...])]
        for term in terms:
            acc = term if acc is None else acc + term
    o_ref[...] = x_ref[...] + gate_ref[0] * acc


def _outproj(parts, weights, x, gate, tm):
    t, d = x.shape
    n_tiles = t // tm
    in_specs = []
    for a in parts:
        if a.ndim == 3:
            in_specs.append(pl.BlockSpec((a.shape[0], tm, a.shape[2]), lambda i: (0, i, 0)))
        else:
            in_specs.append(pl.BlockSpec((tm, a.shape[1]), lambda i: (i, 0)))
    for w in weights:
        in_specs.append(pl.BlockSpec(w.shape, (lambda i: (0, 0, 0)) if w.ndim == 3 else (lambda i: (0, 0))))
    in_specs += [pl.BlockSpec((tm, d), lambda i: (i, 0)), _mod_spec(gate, n_tiles)]
    return pl.pallas_call(
        functools.partial(_outproj_kernel, n_parts=len(parts)),
        out_shape=jax.ShapeDtypeStruct((t, d), f32),
        grid=(n_tiles,),
        in_specs=in_specs,
        out_specs=pl.BlockSpec((tm, d), lambda i: (i, 0)),
        compiler_params=_cparams("parallel"),
        name="outproj",
    )(*parts, *weights, x, gate)


def _ffn_kernel(x_ref, g_ref, sc_ref, sh_ref, gate_ref, w1_ref, w3_ref, w2_ref, o_ref, h_scr):
    j = pl.program_id(1)

    @pl.when(j == 0)
    def _():
        h_scr[...] = _modnorm(x_ref[...], g_ref[...], sc_ref[0], sh_ref[0]).astype(bf16)
        o_ref[...] = jnp.zeros_like(o_ref)

    h = h_scr[...]
    a = _dot(h, w1_ref[...])
    b = _dot(h, w3_ref[...])
    act = (a * jax.nn.sigmoid(a)) * b
    o_ref[...] += _dot(act.astype(bf16), w2_ref[...])

    @pl.when(j == pl.num_programs(1) - 1)
    def _():
        o_ref[...] = x_ref[...] + gate_ref[0] * o_ref[...]


def _ffn(x, g, scale, shift, gate, w1, w3, w2, tm, tf):
    t, d = x.shape
    f = w1.shape[1]
    n_tiles = t // tm
    return pl.pallas_call(
        _ffn_kernel,
        out_shape=jax.ShapeDtypeStruct((t, d), f32),
        grid=(n_tiles, f // tf),
        in_specs=[
            pl.BlockSpec((tm, d), lambda i, j: (i, 0)),
            pl.BlockSpec((1, d), lambda i, j: (0, 0)),
            _mod_spec(scale, n_tiles),
            _mod_spec(shift, n_tiles),
            _mod_spec(gate, n_tiles),
            pl.BlockSpec((d, tf), lambda i, j: (0, j)),
            pl.BlockSpec((d, tf), lambda i, j: (0, j)),
            pl.BlockSpec((tf, d), lambda i, j: (j, 0)),
        ],
        out_specs=pl.BlockSpec((tm, d), lambda i, j: (i, 0)),
        scratch_shapes=[pltpu.VMEM((tm, d), bf16)],
        compiler_params=_cparams("parallel", "arbitrary"),
        name="ffn",
    )(x, g, scale, shift, gate, w1, w3, w2)


def _router_kernel(x_ref, g_ref, sc_ref, sh_ref, rw_ref, h_ref, comb_ref, rank_ref, rankt_ref, cnt_ref):
    tm = x_ref.shape[0]
    h = _modnorm(x_ref[...], g_ref[...], sc_ref[0], sh_ref[0])
    h_ref[...] = h.astype(bf16)
    logits = _dot_f32(h, rw_ref[...])
    lane = lax.broadcasted_iota(jnp.int32, (tm, LANES), 1)
    lg = jnp.where(lane < N_EXPERTS, logits, -jnp.inf)
    m1 = jnp.max(lg, axis=1, keepdims=True)
    i1 = jnp.min(jnp.where(lg == m1, lane, LANES), axis=1, keepdims=True)
    lg2 = jnp.where(lane == i1, -jnp.inf, lg)
    m2 = jnp.max(lg2, axis=1, keepdims=True)
    i2 = jnp.min(jnp.where(lg2 == m2, lane, LANES), axis=1, keepdims=True)
    e2 = jnp.exp(m2 - m1)
    den = 1.0 + e2
    comb_ref[...] = jnp.where(lane == i1, 1.0 / den, jnp.where(lane == i2, e2 / den, 0.0))
    sel = (lane == i1) | (lane == i2)
    ind = jnp.where(sel, 1.0, 0.0)
    row = lax.broadcasted_iota(jnp.int32, (tm, tm), 0)
    col = lax.broadcasted_iota(jnp.int32, (tm, tm), 1)
    below = jnp.where(col < row, 1.0, 0.0).astype(bf16)
    rank = jnp.where(sel, _dot(below, ind.astype(bf16)), -1.0)
    rank_ref[...] = rank
    rankt_ref[0] = rank.T[:N_EXPERTS, :]
    cnt_ref[0] = jnp.broadcast_to(jnp.sum(ind, axis=0, keepdims=True), (8, LANES))


def _router(x, g, scale, shift, rw, tm):
    t, d = x.shape
    n_tiles = t // tm
    return pl.pallas_call(
        _router_kernel,
        out_shape=(
            jax.ShapeDtypeStruct((t, d), bf16),
            jax.ShapeDtypeStruct((t, LANES), f32),
            jax.ShapeDtypeStruct((t, LANES), f32),
            jax.ShapeDtypeStruct((n_tiles, N_EXPERTS, tm), f32),
            jax.ShapeDtypeStruct((n_tiles, 8, LANES), f32),
        ),
        grid=(n_tiles,),
        in_specs=[
            pl.BlockSpec((tm, d), lambda i: (i, 0)),
            pl.BlockSpec((1, d), lambda i: (0, 0)),
            _mod_spec(scale, n_tiles),
            _mod_spec(shift, n_tiles),
            pl.BlockSpec((d, LANES), lambda i: (0, 0)),
        ],
        out_specs=(
            pl.BlockSpec((tm, d), lambda i: (i, 0)),
            pl.BlockSpec((tm, LANES), lambda i: (i, 0)),
            pl.BlockSpec((tm, LANES), lambda i: (i, 0)),
            pl.BlockSpec((1, N_EXPERTS, tm), lambda i: (i, 0, 0)),
            pl.BlockSpec((1, 8, LANES), lambda i: (i, 0, 0)),
        ),
        compiler_params=_cparams("parallel"),
        name="router",
    )(x, g, scale, shift, rw)


def _moe_tile_kernel(cnt_ref, *refs, cap, halves):
    h_ref, gate_ref, comb_ref, rank_ref = refs[:4]
    rankt_refs = refs[4:4 + halves]
    w1_ref, w3_ref, w2_ref, o_ref, hc_scr, y_scr = refs[4 + halves:]
    i = pl.program_id(0)
    e = pl.program_id(1)
    fc = pl.program_id(2)
    last_fc = pl.num_programs(2) - 1
    th = h_ref.shape[0] // halves
    group = halves * cap
    n_batches = (cnt_ref[(i * halves) * N_EXPERTS + e] + cap - 1) // cap
    for hf in range(1, halves):
        n_batches = jnp.maximum(n_batches, (cnt_ref[(i * halves + hf) * N_EXPERTS + e] + cap - 1) // cap)

    @pl.when((e == 0) & (fc == 0))
    def _():
        o_ref[...] = jnp.zeros_like(o_ref)

    @pl.when(fc == 0)
    def _():
        def compact(b, carry):
            rowid = (lax.broadcasted_iota(jnp.int32, (cap, th), 0) + b * cap).astype(f32)
            for hf in range(halves):
                r0 = pl.multiple_of(b * group + hf * cap, 8)
                onehot = jnp.where(rankt_refs[hf][0] == rowid, 1.0, 0.0).astype(bf16)
                hc_scr[pl.ds(r0, cap), :] = _dot(onehot, h_ref[hf * th:(hf + 1) * th, :]).astype(bf16)
            g0 = pl.multiple_of(b * group, 8)
            y_scr[pl.ds(g0, group), :] = jnp.zeros((group, y_scr.shape[1]), f32)
            return carry
        lax.fori_loop(0, n_batches, compact, 0)

    def expert(b, carry):
        g0 = pl.multiple_of(b * group, 8)
        hc = hc_scr[pl.ds(g0, group), :]
        a = _dot(hc, w1_ref[0])
        g = _dot(hc, w3_ref[0])
        act = (a * jax.nn.sigmoid(a)) * g
        y_scr[pl.ds(g0, group), :] += _dot(act.astype(bf16), w2_ref[0])
        return carry
    lax.fori_loop(0, n_batches, expert, 0)

    @pl.when(fc == last_fc)
    def _():
        lane = lax.broadcasted_iota(jnp.int32, (th, LANES), 1)
        for hf in range(halves):
            rows = slice(hf * th, (hf + 1) * th)
            rank_e = jnp.sum(jnp.where(lane == e, rank_ref[rows, :], 0.0), axis=1, keepdims=True)
            gate_e = jnp.sum(jnp.where(lane == e, comb_ref[rows, :], 0.0), axis=1, keepdims=True)
            gate_mod = gate_ref[0] if gate_ref.shape[1] == 1 else gate_ref[0, rows, :]

            def expand(b, carry):
                r0 = pl.multiple_of(b * group + hf * cap, 8)
                colid = (lax.broadcasted_iota(jnp.int32, (th, cap), 1) + b * cap).astype(f32)
                onehot = jnp.where(rank_e == colid, 1.0, 0.0).astype(bf16)
                o_ref[rows, :] += (gate_mod * gate_e) * _dot(onehot, y_scr[pl.ds(r0, cap), :].astype(bf16))
                return carry
            lax.fori_loop(0, n_batches, expand, 0)


def _resnorm_kernel(x_ref, dl_ref, nf_ref, o_ref):
    y = x_ref[...] + dl_ref[...]
    o_ref[...] = y * lax.rsqrt(jnp.mean(y * y, axis=-1, keepdims=True) + EPS) * nf_ref[...]


def _moe_grouped(x, g, scale, shift, gate, rw, w1, w3, w2, norm_final, th, halves, tf, cap):
    t, d = x.shape
    f = w1.shape[2]
    tm = th * halves
    n_tiles = t // tm
    cap = min(cap, th)
    n_cap = -(-th // cap)
    h, comb, rank, rankt, cnt = _router(x, g, scale, shift, rw, th)
    counts = cnt[:, 0, :N_EXPERTS].astype(jnp.int32).reshape(-1)
    rankt = rankt.reshape((t // th) * N_EXPERTS, 1, th)
    once = pl.Buffered(1)

    def rankt_map(hf):
        return lambda i, e, c, cnt: ((i * halves + hf) * N_EXPERTS + e, 0, 0)

    grid_spec = pltpu.PrefetchScalarGridSpec(
        num_scalar_prefetch=1,
        grid=(n_tiles, N_EXPERTS, f // tf),
        in_specs=[pl.BlockSpec((tm, d), lambda i, e, c, cnt: (i, 0), pipeline_mode=once),
                  _mod_spec(gate, n_tiles),
                  pl.BlockSpec((tm, LANES), lambda i, e, c, cnt: (i, 0), pipeline_mode=once),
                  pl.BlockSpec((tm, LANES), lambda i, e, c, cnt: (i, 0), pipeline_mode=once)]
                 + [pl.BlockSpec((1, 1, th), rankt_map(hf)) for hf in range(halves)]
                 + [pl.BlockSpec((1, d, tf), lambda i, e, c, cnt: (e, 0, c)),
                    pl.BlockSpec((1, d, tf), lambda i, e, c, cnt: (e, 0, c)),
                    pl.BlockSpec((1, tf, d), lambda i, e, c, cnt: (e, c, 0))],
        out_specs=pl.BlockSpec((tm, d), lambda i, e, c, cnt: (i, 0)),
        scratch_shapes=[pltpu.VMEM((n_cap * halves * cap, d), bf16),
                        pltpu.VMEM((n_cap * halves * cap, d), f32)],
    )
    delta = pl.pallas_call(
        functools.partial(_moe_tile_kernel, cap=cap, halves=halves),
        out_shape=jax.ShapeDtypeStruct((t, d), f32),
        grid_spec=grid_spec,
        compiler_params=_cparams("parallel", "arbitrary", "arbitrary"),
        name="moe_tile",
    )(counts, h, gate, comb, rank, *([rankt] * halves), w1, w3, w2)
    tr = min(t, 1024)
    return pl.pallas_call(
        _resnorm_kernel,
        out_shape=jax.ShapeDtypeStruct((t, d), f32),
        grid=(t // tr,),
        in_specs=[pl.BlockSpec((tr, d), lambda i: (i, 0)), pl.BlockSpec((tr, d), lambda i: (i, 0)),
                  pl.BlockSpec((1, d), lambda i: (0, 0))],
        out_specs=pl.BlockSpec((tr, d), lambda i: (i, 0)),
        compiler_params=_cparams("parallel"),
        name="resnorm",
    )(x, delta, norm_final)


S5_LANES = S5_GROUPS * S5_STATE
S5_QUARTERS = 4
S5_QS = S5_LANES // S5_QUARTERS
S5_QC = S5_INNER // S5_QUARTERS


def _s5_load_bu(u_refs, bre_ref, bim_ref, up_scr, bu_r, bu_i, n_sub, m):
    for c in range(S5_QUARTERS):
        for k in range(m):
            up_scr[k * n_sub:(k + 1) * n_sub, c * S5_QC:(c + 1) * S5_QC] = u_refs[c][pl.ds(k, n_sub, stride=m), :]
        uc = up_scr[:, c * S5_QC:(c + 1) * S5_QC].astype(bf16)
        bu_r[:, c * S5_QS:(c + 1) * S5_QS] = _dot(uc, bre_ref[c])
        bu_i[:, c * S5_QS:(c + 1) * S5_QS] = _dot(uc, bim_ref[c])


def _s5_local_scan(lam_r_ref, lam_i_ref, init_r_ref, init_i_ref, bu_r, bu_i, n_sub, m):
    width = 8192 // n_sub
    for c in range(S5_LANES // width):
        cols = slice(c * width, (c + 1) * width)
        lr = jnp.broadcast_to(lam_r_ref[:, cols], (n_sub, width))
        li = jnp.broadcast_to(lam_i_ref[:, cols], (n_sub, width))
        if init_r_ref is None:
            x0 = (jnp.zeros((n_sub, width), f32), jnp.zeros((n_sub, width), f32))
        else:
            x0 = (init_r_ref[:, cols], init_i_ref[:, cols])

        def step(k, carry):
            xr, xi = carry
            rows = pl.ds(pl.multiple_of(k * n_sub, 8), n_sub)
            nr = lr * xr - li * xi + bu_r[rows, cols]
            ni = lr * xi + li * xr + bu_i[rows, cols]
            bu_r[rows, cols] = nr
            bu_i[rows, cols] = ni
            return nr, ni
        lax.fori_loop(0, m, step, x0)


def _gelu_tanh(x):
    return 0.5 * x * (1.0 + jnp.tanh(math.sqrt(2.0 / math.pi) * (x + 0.044715 * (x * x * x))))


def _s5_output(xb_r, xb_i, up_scr, cre_ref, cim_ref, dd_ref, gw_ref, gb_ref, o_ref, op_scr, n_sub, m):
    ys = []
    for c in range(S5_QUARTERS):
        cols = slice(c * S5_QS, (c + 1) * S5_QS)
        ys.append(_dot(xb_r[:, cols].astype(bf16), cre_ref[c]) - _dot(xb_i[:, cols].astype(bf16), cim_ref[c]))
    y = jnp.concatenate(ys, axis=1) + dd_ref[...] * up_scr[...]
    hs = _gelu_tanh(y)
    op_scr[...] = hs * jax.nn.sigmoid(_dot(hs.astype(bf16), gw_ref[...]) + gb_ref[...])
    for c in range(S5_QUARTERS):
        for k in range(m):
            o_ref[c, pl.ds(k, n_sub, stride=m), :] = op_scr[k * n_sub:(k + 1) * n_sub, c * S5_QC:(c + 1) * S5_QC]


def _s5_chain_kernel(u0_ref, u1_ref, u2_ref, u3_ref, x0r_ref, x0i_ref, lam_r_ref, lam_i_ref, bre_ref, bim_ref, cre_ref, cim_ref,
                     dd_ref, gw_ref, gb_ref, o_ref, sr_ref, si_ref,
                     pow_r, pow_i, bu_r, bu_i, up_scr, op_scr, en_r, en_i, *, m):
    n_sub = 8
    j = pl.program_id(1)

    @pl.when((pl.program_id(0) == 0) & (j == 0))
    def _():
        def pstep(k, carry):
            pr, pi = carry
            pow_r[pl.ds(k, 1), :] = pr
            pow_i[pl.ds(k, 1), :] = pi
            lr = lam_r_ref[...]
            li = lam_i_ref[...]
            return lr * pr - li * pi, lr * pi + li * pr
        lax.fori_loop(0, m, pstep, (lam_r_ref[...], lam_i_ref[...]))

    @pl.when(j == 0)
    def _():
        sr_ref[0] = x0r_ref[0]
        si_ref[0] = x0i_ref[0]

    _s5_load_bu((u0_ref, u1_ref, u2_ref, u3_ref), bre_ref, bim_ref, up_scr, bu_r, bu_i, n_sub, m)
    _s5_local_scan(lam_r_ref, lam_i_ref, None, None, bu_r, bu_i, n_sub, m)

    pm_r = pow_r[m - 1:m, :]
    pm_i = pow_i[m - 1:m, :]
    e_r = sr_ref[0]
    e_i = si_ref[0]
    for s in range(n_sub):
        en_r[s:s + 1, :] = e_r
        en_i[s:s + 1, :] = e_i
        row = (m - 1) * n_sub + s
        e_r, e_i = (bu_r[row:row + 1, :] + pm_r * e_r - pm_i * e_i,
                    bu_i[row:row + 1, :] + pm_r * e_i + pm_i * e_r)
    sr_ref[0] = e_r
    si_ref[0] = e_i

    def fix(k, carry):
        rows = pl.ds(pl.multiple_of(k * n_sub, 8), n_sub)
        pr = pow_r[pl.ds(k, 1), :]
        pi = pow_i[pl.ds(k, 1), :]
        er = en_r[...]
        ei = en_i[...]
        bu_r[rows, :] = bu_r[rows, :] + pr * er - pi * ei
        bu_i[rows, :] = bu_i[rows, :] + pr * ei + pi * er
        return carry
    lax.fori_loop(0, m, fix, 0)

    _s5_output(bu_r, bu_i, up_scr, cre_ref, cim_ref, dd_ref, gw_ref, gb_ref, o_ref, op_scr, n_sub, m)


def _s5_batch_kernel(u0_ref, u1_ref, u2_ref, u3_ref, x0r_ref, x0i_ref, lam_r_ref, lam_i_ref, bre_ref, bim_ref, cre_ref, cim_ref,
                     dd_ref, gw_ref, gb_ref, o_ref, sr_ref, si_ref,
                     bu_r, bu_i, up_scr, op_scr, *, n_sub, m):
    _s5_load_bu((u0_ref, u1_ref, u2_ref, u3_ref), bre_ref, bim_ref, up_scr, bu_r, bu_i, n_sub, m)
    _s5_local_scan(lam_r_ref, lam_i_ref, x0r_ref, x0i_ref, bu_r, bu_i, n_sub, m)
    last = slice((m - 1) * n_sub, m * n_sub)
    sr_ref[...] = bu_r[last, :]
    si_ref[...] = bu_i[last, :]
    _s5_output(bu_r, bu_i, up_scr, cre_ref, cim_ref, dd_ref, gw_ref, gb_ref, o_ref, op_scr, n_sub, m)


def _s5_params(p):
    ar = p['s5_A_re'].astype(f32)
    ai = p['s5_A_im'].astype(f32)
    step = jnp.exp(p['s5_log_step'].astype(f32))[:, None]
    mag = jnp.exp(ar * step)
    lb_re = mag * jnp.cos(ai * step)
    lb_im = mag * jnp.sin(ai * step)
    den = ar * ar + ai * ai
    nr = lb_re - 1.0
    cr = (nr * ar + lb_im * ai) / den
    ci = (lb_im * ar - nr * ai) / den
    bb_re = cr[..., None] * p['s5_B_re'] - ci[..., None] * p['s5_B_im']
    bb_im = cr[..., None] * p['s5_B_im'] + ci[..., None] * p['s5_B_re']
    gq = S5_GROUPS // S5_QUARTERS
    eye = jnp.eye(gq, dtype=f32)

    def bq(bb):
        t = bb.reshape(S5_QUARTERS, gq, S5_STATE, S5_CH)
        return jnp.einsum('cgph,gk->cghkp', t, eye).reshape(S5_QUARTERS, S5_QC, S5_QS).astype(bf16)

    def cq(cc):
        t = cc.astype(f32).reshape(S5_QUARTERS, gq, S5_CH, S5_STATE)
        return jnp.einsum('cghp,gk->cgpkh', t, eye).reshape(S5_QUARTERS, S5_QS, S5_QC).astype(bf16)

    return dict(lam_r=lb_re.reshape(1, S5_LANES), lam_i=lb_im.reshape(1, S5_LANES),
                bre=bq(bb_re), bim=bq(bb_im), cre=cq(p['s5_C_re']), cim=cq(p['s5_C_im']),
                dd=p['s5_D'].astype(f32).reshape(1, S5_INNER),
                gw=p['glu_w'].astype(bf16), gb=p['glu_b'].astype(f32).reshape(1, S5_INNER))


def _s5_const_specs(nd):
    z2 = (lambda *_: (0, 0))
    z3 = (lambda *_: (0, 0, 0))
    return [
        pl.BlockSpec((1, S5_LANES), z2), pl.BlockSpec((1, S5_LANES), z2),
        pl.BlockSpec((S5_QUARTERS, S5_QC, S5_QS), z3), pl.BlockSpec((S5_QUARTERS, S5_QC, S5_QS), z3),
        pl.BlockSpec((S5_QUARTERS, S5_QS, S5_QC), z3), pl.BlockSpec((S5_QUARTERS, S5_QS, S5_QC), z3),
        pl.BlockSpec((1, S5_INNER), z2), pl.BlockSpec((S5_INNER, S5_INNER), z2), pl.BlockSpec((1, S5_INNER), z2),
    ]


def _s5_chain(proj, ucol, x0r, x0i, sp, b, l, m):
    chunk = 8 * m
    nc = l // chunk
    consts = [sp[k] for k in ('lam_r', 'lam_i', 'bre', 'bim', 'cre', 'cim', 'dd', 'gw', 'gb')]
    return pl.pallas_call(
        functools.partial(_s5_chain_kernel, m=m),
        out_shape=(jax.ShapeDtypeStruct((S5_QUARTERS, b * l, S5_QC), f32),
                   jax.ShapeDtypeStruct((b, 1, S5_LANES), f32),
                   jax.ShapeDtypeStruct((b, 1, S5_LANES), f32)),
        grid=(b, nc),
        in_specs=[pl.BlockSpec((chunk, S5_QC), functools.partial(lambda i, j, c: (i * nc + j, ucol + c), c=c))
                  for c in range(S5_QUARTERS)] + [
                  pl.BlockSpec((1, 1, S5_LANES), lambda i, j: (i, 0, 0)),
                  pl.BlockSpec((1, 1, S5_LANES), lambda i, j: (i, 0, 0))] + _s5_const_specs(2),
        out_specs=(pl.BlockSpec((S5_QUARTERS, chunk, S5_QC), lambda i, j: (0, i * nc + j, 0)),
                   pl.BlockSpec((1, 1, S5_LANES), lambda i, j: (i, 0, 0)),
                   pl.BlockSpec((1, 1, S5_LANES), lambda i, j: (i, 0, 0))),
        scratch_shapes=[pltpu.VMEM((m, S5_LANES), f32), pltpu.VMEM((m, S5_LANES), f32),
                        pltpu.VMEM((chunk, S5_LANES), f32), pltpu.VMEM((chunk, S5_LANES), f32),
                        pltpu.VMEM((chunk, S5_INNER), f32), pltpu.VMEM((chunk, S5_INNER), f32),
                        pltpu.VMEM((8, S5_LANES), f32), pltpu.VMEM((8, S5_LANES), f32)],
        compiler_params=_cparams("arbitrary", "arbitrary"),
        name="s5_chain",
    )(proj, proj, proj, proj, x0r, x0i, *consts)


def _s5_batch(proj, ucol, x0r, x0i, sp, n_sub, m):
    t = n_sub * m
    consts = [sp[k] for k in ('lam_r', 'lam_i', 'bre', 'bim', 'cre', 'cim', 'dd', 'gw', 'gb')]
    return pl.pallas_call(
        functools.partial(_s5_batch_kernel, n_sub=n_sub, m=m),
        out_shape=(jax.ShapeDtypeStruct((S5_QUARTERS, t, S5_QC), f32),
                   jax.ShapeDtypeStruct((n_sub, S5_LANES), f32),
                   jax.ShapeDtypeStruct((n_sub, S5_LANES), f32)),
        grid=(1,),
        in_specs=[pl.BlockSpec((t, S5_QC), functools.partial(lambda i, c: (0, ucol + c), c=c))
                  for c in range(S5_QUARTERS)] + [
                  pl.BlockSpec((n_sub, S5_LANES), lambda i: (0, 0)),
                  pl.BlockSpec((n_sub, S5_LANES), lambda i: (0, 0))] + _s5_const_specs(1),
        out_specs=(pl.BlockSpec((S5_QUARTERS, t, S5_QC), lambda i: (0, 0, 0)),
                   pl.BlockSpec((n_sub, S5_LANES), lambda i: (0, 0)),
                   pl.BlockSpec((n_sub, S5_LANES), lambda i: (0, 0))),
        scratch_shapes=[pltpu.VMEM((t, S5_LANES), f32), pltpu.VMEM((t, S5_LANES), f32),
                        pltpu.VMEM((t, S5_INNER), f32), pltpu.VMEM((t, S5_INNER), f32)],
        compiler_params=_cparams("arbitrary"),
        name="s5_batch",
    )(proj, proj, proj, proj, x0r, x0i, *consts)


def _tri(n, strict=False, upper=False):
    r = lax.broadcasted_iota(jnp.int32, (n, n), 0)
    c = lax.broadcasted_iota(jnp.int32, (n, n), 1)
    if upper:
        r, c = c, r
    return jnp.where((c < r) if strict else (c <= r), 1.0, 0.0).astype(bf16)


def _dot_exact_lhs(a_exact_bf16, b):
    b0, b1, b2 = _split3(b)
    return _dot(a_exact_bf16, b0) + _dot(a_exact_bf16, b1) + _dot(a_exact_bf16, b2)


def _log_sigmoid(x):
    return jnp.minimum(x, 0.0) - jnp.log1p(jnp.exp(-jnp.abs(x)))


def _fcum_kernel(fr_ref, bias_ref, logf_ref, f_ref, ft_ref, carry):
    @pl.when(pl.program_id(1) == 0)
    def _():
        carry[...] = jnp.zeros_like(carry)

    n = fr_ref.shape[0]
    logf = _log_sigmoid(fr_ref[...] + bias_ref[...])
    logf_ref[...] = logf
    f = _dot_exact_lhs(_tri(n), logf) + carry[0:1, :]
    f_ref[...] = f
    ft_ref[0] = f.T[:8, :]
    carry[0:1, :] = f[n - 1:n, :]


def _fcum(proj, col, bias, b, l, chunk):
    nc = l // chunk
    return pl.pallas_call(
        _fcum_kernel,
        out_shape=(jax.ShapeDtypeStruct((b * l, LANES), f32),
                   jax.ShapeDtypeStruct((b * l, LANES), f32),
                   jax.ShapeDtypeStruct((b, 8, l), f32)),
        grid=(b, nc),
        in_specs=[pl.BlockSpec((chunk, LANES), lambda i, j: (i * nc + j, col)),
                  pl.BlockSpec((1, LANES), lambda i, j: (0, 0))],
        out_specs=(pl.BlockSpec((chunk, LANES), lambda i, j: (i * nc + j, 0)),
                   pl.BlockSpec((chunk, LANES), lambda i, j: (i * nc + j, 0)),
                   pl.BlockSpec((1, 8, chunk), lambda i, j: (i, 0, j))),
        scratch_shapes=[pltpu.VMEM((8, LANES), f32)],
        compiler_params=_cparams("arbitrary", "arbitrary"),
        name="fcum",
    )(proj, bias)


N_FPARTS = 3
FOX_STRIP = 256


def _fox_select_mats():
    rows = jnp.arange(N_FPARTS * LANES)[None, :, None]
    cols = jnp.arange(FOX_HEAD_DIM)[None, None, :]
    head = jnp.arange(FOX_HEADS)[:, None, None]
    hit = (rows % LANES == head)
    sq = jnp.where(hit & (cols == rows // LANES), 1.0, 0.0)
    sk = jnp.where(hit & (cols == N_FPARTS + rows // LANES), -1.0, 0.0)
    return sq.astype(bf16), sk.astype(bf16)


def _fox_prep_kernel(q_ref, k_ref, v_ref, f_ref, sq_ref, sk_ref, qa_ref, ka_ref, vt_ref, *, scale):
    tm = q_ref.shape[0]
    hd = FOX_HEAD_DIM
    fcat = jnp.concatenate(_split3(f_ref[...]), axis=1)
    lane = lax.broadcasted_iota(jnp.int32, (tm, hd), 1)
    ones_q = jnp.where((lane >= N_FPARTS) & (lane < 2 * N_FPARTS), 1.0, 0.0)
    ones_k = jnp.where(lane < N_FPARTS, 1.0, 0.0)
    vt = v_ref[...].T
    for h in range(FOX_HEADS):
        cols = slice(h * hd, (h + 1) * hd)
        eq = _dot(fcat, sq_ref[h]) + ones_q
        ek = _dot(fcat, sk_ref[h]) + ones_k
        qa_ref[h] = jnp.concatenate([q_ref[:, cols] * scale, eq], axis=1).astype(bf16)
        ka_ref[h] = jnp.concatenate([k_ref[:, cols], ek], axis=1).astype(bf16)
        vt_ref[h] = vt[h * hd:(h + 1) * hd, :].astype(bf16)


def _fox_prep(proj, f, t, tm):
    sq, sk = _fox_select_mats()
    sel_spec = pl.BlockSpec((FOX_HEADS, N_FPARTS * LANES, FOX_HEAD_DIM), lambda i: (0, 0, 0))
    return pl.pallas_call(
        functools.partial(_fox_prep_kernel, scale=FOX_HEAD_DIM ** -0.5),
        out_shape=(jax.ShapeDtypeStruct((FOX_HEADS, t, LANES), bf16),
                   jax.ShapeDtypeStruct((FOX_HEADS, t, LANES), bf16),
                   jax.ShapeDtypeStruct((FOX_HEADS, FOX_HEAD_DIM, t), bf16)),
        grid=(t // tm,),
        in_specs=[pl.BlockSpec((tm, FOX_INNER), lambda i: (i, 0)),
                  pl.BlockSpec((tm, FOX_INNER), lambda i: (i, 1)),
                  pl.BlockSpec((tm, FOX_INNER), lambda i: (i, 2)),
                  pl.BlockSpec((tm, LANES), lambda i: (i, 0)),
                  sel_spec, sel_spec],
        out_specs=(pl.BlockSpec((FOX_HEADS, tm, LANES), lambda i: (0, i, 0)),
                   pl.BlockSpec((FOX_HEADS, tm, LANES), lambda i: (0, i, 0)),
                   pl.BlockSpec((FOX_HEADS, FOX_HEAD_DIM, tm), lambda i: (0, 0, i))),
        compiler_params=_cparams("parallel"),
        name="fox_prep",
    )(proj, proj, proj, f, sq, sk)


def _fox_kernel(qt_ref, kt_ref, qa_ref, ka_ref, vt_ref, o_ref, m_scr, l_scr, acc_scr):
    step = pl.program_id(2)
    qi = qt_ref[step]
    ki = kt_ref[step]
    tq = qa_ref.shape[1]
    tk = ka_ref.shape[1]
    q_first = qi * tq
    k_first = ki * tk

    @pl.when(ki == 0)
    def _():
        m_scr[...] = jnp.full_like(m_scr, -jnp.inf)
        l_scr[...] = jnp.zeros_like(l_scr)
        acc_scr[...] = jnp.zeros_like(acc_scr)

    def update(masked):
        strip = min(FOX_STRIP, tq)
        chains = [(hh, q0) for hh in range(2) for q0 in range(0, tq, strip)]
        nks = [min(tk, q0 + strip) if (masked and tq == tk) else tk for _, q0 in chains]
        sts = [_dot_nt(ka_ref[hh, :nk, :], qa_ref[hh, q0:q0 + strip, :])
               for (hh, q0), nk in zip(chains, nks)]
        ps, alphas = [], []
        for (hh, q0), nk, st in zip(chains, nks, sts):
            qs = slice(q0, q0 + strip)
            if masked:
                kpos = lax.broadcasted_iota(jnp.int32, (nk, strip), 0) + k_first
                qpos = lax.broadcasted_iota(jnp.int32, (nk, strip), 1) + (q0 + q_first)
                st = jnp.where(kpos <= qpos, st, -jnp.inf)
            m_old = m_scr[hh, :, qs]
            m_new = jnp.maximum(m_old, jnp.max(st, axis=0, keepdims=True))
            alpha = jnp.exp(m_old - m_new)
            p = jnp.exp(st - m_new)
            l_scr[hh, :, qs] = alpha * l_scr[hh, :, qs] + jnp.sum(p, axis=0, keepdims=True)
            m_scr[hh, :, qs] = m_new
            ps.append(p.astype(bf16))
            alphas.append(alpha)
        pvs = [_dot(vt_ref[hh, :, :nk], p) for (hh, _), nk, p in zip(chains, nks, ps)]
        for (hh, q0), alpha, pv in zip(chains, alphas, pvs):
            qs = slice(q0, q0 + strip)
            acc_scr[hh, :, qs] = alpha * acc_scr[hh, :, qs] + pv

    full = k_first + (tk - 1) <= q_first

    @pl.when(full)
    def _():
        update(False)

    @pl.when(jnp.logical_not(full))
    def _():
        update(True)

    @pl.when(k_first + tk >= q_first + tq)
    def _():
        ot = jnp.concatenate([acc_scr[0] / l_scr[0], acc_scr[1] / l_scr[1]], axis=0)
        o_ref[...] = ot.T


def _fox_prompt(proj, f, b, l, tq, tk):
    nq = l // tq
    nk = l // tk
    n_hp = FOX_HEADS // 2
    qa, ka, vt = _fox_prep(proj, f, b * l, max(tq, tk))
    pairs = [(qi, ki) for qi in range(nq) for ki in range(((qi + 1) * tq - 1) // tk + 1)]
    qtab = jnp.array([pr[0] for pr in pairs], jnp.int32)
    ktab = jnp.array([pr[1] for pr in pairs], jnp.int32)
    grid_spec = pltpu.PrefetchScalarGridSpec(
        num_scalar_prefetch=2,
        grid=(b, n_hp, len(pairs)),
        in_specs=[
            pl.BlockSpec((2, tq, LANES), lambda i, h, s, qt, kt: (h, i * nq + qt[s], 0)),
            pl.BlockSpec((2, tk, LANES), lambda i, h, s, qt, kt: (h, i * nk + kt[s], 0)),
            pl.BlockSpec((2, FOX_HEAD_DIM, tk), lambda i, h, s, qt, kt: (h, 0, i * nk + kt[s])),
        ],
        out_specs=pl.BlockSpec((tq, LANES), lambda i, h, s, qt, kt: (i * nq + qt[s], h)),
        scratch_shapes=[pltpu.VMEM((2, 1, tq), f32), pltpu.VMEM((2, 1, tq), f32),
                        pltpu.VMEM((2, FOX_HEAD_DIM, tq), f32)],
    )
    return pl.pallas_call(
        _fox_kernel,
        out_shape=jax.ShapeDtypeStruct((b * l, FOX_INNER), f32),
        grid_spec=grid_spec,
        compiler_params=_cparams("parallel", "parallel", "arbitrary"),
        name="fox_prompt",
    )(qtab, ktab, qa, ka, vt)


PAGES_PER_STEP = 16


def _dot_exact_rhs(a, b_exact_bf16):
    a0, a1, a2 = _split3(a)
    return _dot(a0, b_exact_bf16) + _dot(a1, b_exact_bf16) + _dot(a2, b_exact_bf16)


def _dot_nt(a, b):
    return lax.dot_general(a, b, (((1,), (1,)), ((), ())), preferred_element_type=f32)


def _fox_paged_kernel(pt_ref, *refs, n_pages, pps, scale):
    k_refs, v_refs = refs[:pps], refs[pps:2 * pps]
    (lf_ref, q_ref, kn_ref, vn_ref, fr_ref, bias_ref, o_ref, logf_ref,
     s_scr, qbd_scr, acc_scr, psum_scr, m_scr, car_scr, fq_scr) = refs[2 * pps:]
    seq = pl.program_id(0)
    ph = pl.program_id(1)
    c = pl.program_id(2)
    last_c = pl.num_programs(2) - 1
    nq = q_ref.shape[0]
    hd = FOX_HEAD_DIM
    pg = LANES
    row = lax.broadcasted_iota(jnp.int32, (pg, pg), 0)
    lane = lax.broadcasted_iota(jnp.int32, (pg, pg), 1)
    upper = jnp.where(row <= lane, 1.0, 0.0).astype(bf16)

    def pad_rows(x, fill=0.0):
        return jnp.concatenate([x, jnp.full((pg - x.shape[0], x.shape[1]), fill, x.dtype)], axis=0)

    def per_head_rows(x):
        rep = jnp.broadcast_to(x[:, None, :], (FOX_HEADS, nq, x.shape[1])).reshape(FOX_HEADS * nq, x.shape[1])
        return pad_rows(rep)

    def key_sums(logf_rows):
        cum = _dot_exact_rhs(logf_rows, upper) + car_scr[...]
        car_scr[...] = jnp.broadcast_to(cum[:, pg - 1:pg], (pg, pg))
        return cum

    @pl.when((ph == 0) & (c == 0))
    def _():
        q_rep = jnp.broadcast_to((q_ref[...] * scale)[None], (FOX_HEADS, nq, FOX_INNER)).reshape(FOX_HEADS * nq, FOX_INNER)
        r5 = lax.broadcasted_iota(jnp.int32, (FOX_HEADS * nq, FOX_INNER), 0)
        l5 = lax.broadcasted_iota(jnp.int32, (FOX_HEADS * nq, FOX_INNER), 1)
        qbd_scr[...] = pad_rows(jnp.where(l5 // hd == r5 // nq, q_rep, 0.0)).astype(bf16)
        m_scr[...] = jnp.full_like(m_scr, -jnp.inf)
        car_scr[...] = jnp.zeros_like(car_scr)

    @pl.when(ph == 0)
    def _():
        qk = [_dot(qbd_scr[...], k_refs[i][0].reshape(FOX_INNER, pg).astype(bf16)) for i in range(pps)]
        pages = [pt_ref[seq * n_pages + c * pps + i] for i in range(pps)]
        local = [per_head_rows(_dot_exact_rhs(lf_ref[pages[i]], upper)) for i in range(pps)]
        car = car_scr[...]
        m = m_scr[...]
        for i in range(pps):
            cum = local[i] + car
            car = jnp.broadcast_to(cum[:, pg - 1:pg], (pg, pg))
            s = qk[i] - cum
            s_scr[c * pps + i] = s
            m = jnp.maximum(m, s)
        car_scr[...] = car
        m_scr[...] = m

    @pl.when((ph == 0) & (c == last_c))
    def _():
        logf_new = _log_sigmoid(fr_ref[...] + bias_ref[...])
        logf_ref[...] = logf_new
        cum = key_sums(per_head_rows(pad_rows(logf_new).T[:FOX_HEADS, :]))
        fq = jnp.sum(jnp.where(lane == row % nq, cum, 0.0), axis=1, keepdims=True)
        fq_scr[...] = jnp.broadcast_to(fq, (pg, pg))
        s = _dot(qbd_scr[...], pad_rows(kn_ref[...]).T.astype(bf16)) - cum
        s = jnp.where(lane <= row % nq, s, -jnp.inf)
        s_scr[n_pages] = s
        m = jnp.max(jnp.maximum(m_scr[...], s), axis=1, keepdims=True)
        m_scr[...] = jnp.broadcast_to(m, (pg, pg))

    @pl.when((ph == 1) & (c == 0))
    def _():
        psum_scr[...] = jnp.zeros_like(psum_scr)
        acc_scr[...] = jnp.zeros_like(acc_scr)

    def probs(s):
        fq = fq_scr[...]
        return jnp.exp((s + fq) - (m_scr[...] + fq))

    def accumulate(s, v_t_bf16):
        p = probs(s)
        psum_scr[...] += p
        acc_scr[...] += _dot_nt(v_t_bf16, p.astype(bf16))

    @pl.when(ph == 1)
    def _():
        ps = [probs(s_scr[c * pps + i]) for i in range(pps)]
        pvs = [_dot_nt(v_refs[i][0].reshape(FOX_INNER, pg).astype(bf16), ps[i].astype(bf16)) for i in range(pps)]
        psum = psum_scr[...]
        acc = acc_scr[...]
        for i in range(pps):
            psum = psum + ps[i]
            acc = acc + pvs[i]
        psum_scr[...] = psum
        acc_scr[...] = acc

    @pl.when((ph == 1) & (c == last_c))
    def _():
        accumulate(s_scr[n_pages], pad_rows(vn_ref[...]).T.astype(bf16))
        p0, p1, p2 = _split3(psum_scr[...])
        ones = jnp.ones((8, pg), bf16)
        l_row = (_dot_nt(ones, p0) + _dot_nt(ones, p1) + _dot_nt(ones, p2))[0:1, :]
        o_t = (acc_scr[...] / l_row).T
        o_ref[...] = jnp.concatenate([o_t[h * nq:(h + 1) * nq, h * hd:(h + 1) * hd] for h in range(FOX_HEADS)],
                                     axis=1)


def _fox_paged(proj, fcol, bias, cache_k, cache_v, cache_logf, page_table):
    n_seq, n_pages = page_table.shape
    nq = proj.shape[0] // n_seq
    pps = PAGES_PER_STEP
    n_chunks = n_pages // pps
    ck = jnp.transpose(cache_k, (0, 2, 3, 1))
    cv = jnp.transpose(cache_v, (0, 2, 3, 1))
    clf = jnp.transpose(cache_logf, (0, 2, 1))
    page = ck.shape[3]
    assert page == LANES and nq == 8

    def k_map(i):
        return lambda s, ph, c, pt: (pt[s * n_pages + jnp.where(ph == 0, c, n_chunks - 1) * pps + i], 0, 0, 0)

    def v_map(i):
        return lambda s, ph, c, pt: (pt[s * n_pages + jnp.where(ph == 0, 0, c) * pps + i], 0, 0, 0)

    blk = (1, FOX_HEADS, FOX_HEAD_DIM, page)
    in_specs = ([pl.BlockSpec(blk, k_map(i)) for i in range(pps)]
                + [pl.BlockSpec(blk, v_map(i)) for i in range(pps)]
                + [pl.BlockSpec(clf.shape, lambda s, ph, c, pt: (0, 0, 0), pipeline_mode=pl.Buffered(1)),
                   pl.BlockSpec((nq, FOX_INNER), lambda s, ph, c, pt: (s, 0)),
                   pl.BlockSpec((nq, FOX_INNER), lambda s, ph, c, pt: (s, 1)),
                   pl.BlockSpec((nq, FOX_INNER), lambda s, ph, c, pt: (s, 2)),
                   pl.BlockSpec((nq, LANES), lambda s, ph, c, pt: (s, fcol)),
                   pl.BlockSpec((1, LANES), lambda s, ph, c, pt: (0, 0))])
    grid_spec = pltpu.PrefetchScalarGridSpec(
        num_scalar_prefetch=1,
        grid=(n_seq, 2, n_chunks),
        in_specs=in_specs,
        out_specs=(pl.BlockSpec((nq, FOX_INNER), lambda s, ph, c, pt: (s, 0)),
                   pl.BlockSpec((nq, LANES), lambda s, ph, c, pt: (s, 0))),
        scratch_shapes=[pltpu.VMEM((n_pages + 1, page, page), f32),
                        pltpu.VMEM((page, FOX_INNER), bf16),
                        pltpu.VMEM((FOX_INNER, page), f32),
                        pltpu.VMEM((page, page), f32), pltpu.VMEM((page, page), f32),
                        pltpu.VMEM((page, page), f32), pltpu.VMEM((page, page), f32)],
    )
    return pl.pallas_call(
        functools.partial(_fox_paged_kernel, n_pages=n_pages, pps=pps, scale=FOX_HEAD_DIM ** -0.5),
        out_shape=(jax.ShapeDtypeStruct((n_seq * nq, FOX_INNER), f32),
                   jax.ShapeDtypeStruct((n_seq * nq, LANES), f32)),
        grid_spec=grid_spec,
        compiler_params=_cparams("arbitrary", "arbitrary", "arbitrary"),
        name="fox_paged",
    )(page_table.reshape(-1), *([ck] * pps), *([cv] * pps), clf, proj, proj, proj, proj, bias)


CONV_TAIL = 8
COL0_QKV = 0
COL0_XS = 3 * GDN_INNER
COL0_ZS = COL0_XS + SSD_INNER
COL0_ZG = COL0_ZS + SSD_INNER
COL0_BC = COL0_ZG + GDN_INNER
COL0_SM = COL0_BC + 2 * SSD_GROUPS * SSD_STATE
assert COL0_SM + LANES == IN0_PAD


def _softplus(x):
    return jnp.maximum(x, 0.0) + jnp.log1p(jnp.exp(-jnp.abs(x)))


def _silu(x):
    return x * jax.nn.sigmoid(x)


def _conv_silu(ubuf, u_refs, tail_ref, cw_ref, cb_ref, first, rows, lpad):
    @pl.when(first)
    def _():
        ubuf[0:CONV_TAIL, :] = tail_ref[0]

    @pl.when(jnp.logical_not(first))
    def _():
        ubuf[0:CONV_TAIL, :] = ubuf[rows:rows + CONV_TAIL, :]

    col = 0
    for u_ref in u_refs:
        ubuf[CONV_TAIL:CONV_TAIL + rows, col:col + u_ref.shape[1]] = u_ref[...]
        col += u_ref.shape[1]
    if lpad > rows:
        ubuf[CONV_TAIL + rows:, :] = jnp.zeros((lpad - rows, ubuf.shape[1]), f32)
    acc = cb_ref[...]
    for j in range(CONV_K):
        off = CONV_TAIL - (CONV_K - 1) + j
        acc = acc + cw_ref[j:j + 1, :] * ubuf[off:off + lpad, :]
    return _silu(acc)


def _pad_rows(x, lpad):
    rows = x.shape[0]
    if lpad == rows:
        return x
    return jnp.concatenate([x, jnp.zeros((lpad - rows, x.shape[1]), x.dtype)], axis=0)


def _head_scalars(sm_ref, bias_ref, coef_ref, rows, lpad):
    raw = _pad_rows(sm_ref[...], lpad)
    valid = lax.broadcasted_iota(jnp.int32, (lpad, LANES), 0) < rows
    sp = jnp.where(valid, _softplus(raw + bias_ref[...]), 0.0)
    a = sp * coef_ref[...]
    cum = _dot_exact_lhs(_tri(lpad), a)
    return raw, valid, sp, cum


def _ssd_kernel(ux_ref, ubc_ref, z_ref, sm_ref, tail_ref, h0_ref, cw_ref, cb_ref, bias_ref, coef_ref, dd_ref, nw_ref,
                y_ref, hout_ref, ubuf, *, rows, lpad):
    j = pl.program_id(1)
    n, p = SSD_STATE, SSD_HEAD_DIM
    xbc = _conv_silu(ubuf, (ux_ref, ubc_ref), tail_ref, cw_ref, cb_ref, j == 0, rows, lpad)

    @pl.when(j == 0)
    def _():
        hout_ref[...] = h0_ref[...]

    _, _, dt, acum = _head_scalars(sm_ref, bias_ref, coef_ref, rows, lpad)
    acum_t = acum.T
    dt_t = dt.T
    xs = xbc[:, :SSD_INNER]
    xs_t = xs.T
    r = lax.broadcasted_iota(jnp.int32, (lpad, lpad), 0)
    c = lax.broadcasted_iota(jnp.int32, (lpad, lpad), 1)
    causal = c <= r
    heads = range(SSD_HEADS)
    group_of = [h // (SSD_HEADS // SSD_GROUPS) for h in heads]
    bms = [xbc[:, SSD_INNER + g * n:SSD_INNER + (g + 1) * n] for g in range(SSD_GROUPS)]
    cms = [xbc[:, SSD_INNER + SSD_GROUPS * n + g * n:SSD_INNER + SSD_GROUPS * n + (g + 1) * n]
           for g in range(SSD_GROUPS)]
    cbs = [_dot_nt(cms[g].astype(bf16), bms[g].astype(bf16)) for g in range(SSD_GROUPS)]
    a_cols = [acum[:, h:h + 1] for h in heads]
    a_lasts = [acum[lpad - 1:lpad, h:h + 1] for h in heads]
    x_hs = [xs[:, h * p:(h + 1) * p] for h in heads]
    hsts = [hout_ref[0, h] for h in heads]
    scores = [(cbs[group_of[h]] * jnp.exp(jnp.where(causal, a_cols[h] - acum_t[h:h + 1, :], -jnp.inf))).astype(bf16)
              for h in heads]
    xdts = [(x_hs[h] * dt[:, h:h + 1]).astype(bf16) for h in heads]
    c_exps = [(cms[group_of[h]] * jnp.exp(a_cols[h])).astype(bf16) for h in heads]
    xdt_ts = [(xs_t[h * p:(h + 1) * p, :] * dt_t[h:h + 1, :]).astype(bf16) for h in heads]
    b_ends = [(bms[group_of[h]] * jnp.exp(a_lasts[h] - a_cols[h])).astype(bf16) for h in heads]
    y_diags = [_dot(scores[h], xdts[h]) for h in heads]
    y_offs = [_dot_nt(c_exps[h], hsts[h].astype(bf16)) for h in heads]
    upds = [_dot(xdt_ts[h], b_ends[h]) for h in heads]
    for h in heads:
        hout_ref[0, h] = hsts[h] * jnp.exp(a_lasts[h]) + upds[h]
    ys = [y_diags[h] + y_offs[h] + dd_ref[:, h * p:(h + 1) * p] * x_hs[h] for h in heads]
    y = jnp.concatenate(ys, axis=1)[:rows]
    gated = y * _silu(z_ref[...])
    ms = jnp.mean(gated * gated, axis=-1, keepdims=True)
    y_ref[...] = gated * lax.rsqrt(ms + EPS) * nw_ref[...]


def _ssd(proj, tail, h0, prm, b, l, rows, lpad):
    nc = l // rows
    bc_w = 2 * SSD_GROUPS * SSD_STATE
    cw, cb, bias, coef, dd, nw = prm
    c2 = lambda i, j: (0, 0)
    return pl.pallas_call(
        functools.partial(_ssd_kernel, rows=rows, lpad=lpad),
        out_shape=(jax.ShapeDtypeStruct((b * l, SSD_INNER), f32),
                   jax.ShapeDtypeStruct((b, SSD_HEADS, SSD_HEAD_DIM, SSD_STATE), f32)),
        grid=(b, nc),
        in_specs=[pl.BlockSpec((rows, SSD_INNER), lambda i, j: (i * nc + j, COL0_XS // SSD_INNER)),
                  pl.BlockSpec((rows, bc_w), lambda i, j: (i * nc + j, COL0_BC // bc_w)),
                  pl.BlockSpec((rows, SSD_INNER), lambda i, j: (i * nc + j, COL0_ZS // SSD_INNER)),
                  pl.BlockSpec((rows, LANES), lambda i, j: (i * nc + j, COL0_SM // LANES)),
                  pl.BlockSpec((1, CONV_TAIL, SSD_CONV_CH), lambda i, j: (i, 0, 0)),
                  pl.BlockSpec((1, SSD_HEADS, SSD_HEAD_DIM, SSD_STATE), lambda i, j: (i, 0, 0, 0)),
                  pl.BlockSpec((CONV_K, SSD_CONV_CH), c2), pl.BlockSpec((1, SSD_CONV_CH), c2),
                  pl.BlockSpec((1, LANES), c2), pl.BlockSpec((1, LANES), c2),
                  pl.BlockSpec((1, SSD_INNER), c2), pl.BlockSpec((1, SSD_INNER), c2)],
        out_specs=(pl.BlockSpec((rows, SSD_INNER), lambda i, j: (i * nc + j, 0)),
                   pl.BlockSpec((1, SSD_HEADS, SSD_HEAD_DIM, SSD_STATE), lambda i, j: (i, 0, 0, 0))),
        scratch_shapes=[pltpu.VMEM((CONV_TAIL + lpad, SSD_CONV_CH), f32)],
        compiler_params=_cparams("arbitrary", "arbitrary"),
        name="ssd",
    )(proj, proj, proj, proj, tail, h0, cw, cb, bias, coef, dd, nw)


def _dot3(a, b):
    a_hi = a.astype(bf16)
    a_lo = (a - a_hi.astype(f32)).astype(bf16)
    b_hi = b.astype(bf16)
    b_lo = (b - b_hi.astype(f32)).astype(bf16)
    return _dot(a_hi, b_hi) + _dot(a_lo, b_hi) + _dot(a_hi, b_lo)


INV_BASE = 16


def _l2n(x):
    return x * lax.rsqrt(jnp.sum(x * x, axis=-1, keepdims=True) + EPS)


GDN_A_LANE = SSD_HEADS
GDN_B_LANE = SSD_HEADS + GDN_HEADS


def _unit_lower_inverse_multi(ms, n):
    r = lax.broadcasted_iota(jnp.int32, (n, n), 0)
    c = lax.broadcasted_iota(jnp.int32, (n, n), 1)
    nb = min(INV_BASE, n)
    diag_blk = (r // nb) == (c // nb)
    eye = jnp.where(r == c, 1.0, 0.0)
    pws = [jnp.where(diag_blk, m, 0.0) for m in ms]
    invs = [eye - d for d in pws]
    size = 2
    while size < nb:
        pws = [_dot3(pw, pw) for pw in pws]
        invs = [inv + _dot3(inv, pw) for inv, pw in zip(invs, pws)]
        size *= 2
    s = nb
    while s < n:
        lower_left = ((r // (2 * s)) == (c // (2 * s))) & ((r // s) % 2 == 1) & ((c // s) % 2 == 0)
        tmps = [_dot3(inv, jnp.where(lower_left, m, 0.0)) for inv, m in zip(invs, ms)]
        invs = [inv - _dot3(tmp, inv) for inv, tmp in zip(invs, tmps)]
        s *= 2
    return invs


def _gdn_staged_kernel(u_ref, z_ref, sm_ref, tail_ref, s0_ref, cw_ref, cb_ref, bias_ref, coef_ref, nw_ref,
                       o_ref, sout_ref, ubuf, *, rows, lpad):
    j = pl.program_id(1)
    dk = GDN_HEAD_DIM
    heads = range(GDN_HEADS)
    qkv = _conv_silu(ubuf, (u_ref,), tail_ref, cw_ref, cb_ref, j == 0, rows, lpad)

    @pl.when(j == 0)
    def _():
        sout_ref[...] = s0_ref[...]

    raw, valid, _, gcum = _head_scalars(sm_ref, bias_ref, coef_ref, rows, lpad)
    beta_all = jnp.where(valid, jax.nn.sigmoid(raw), 0.0)
    gcum_t = gcum.T
    r = lax.broadcasted_iota(jnp.int32, (lpad, lpad), 0)
    c = lax.broadcasted_iota(jnp.int32, (lpad, lpad), 1)
    qs = [_l2n(qkv[:, h * dk:(h + 1) * dk]) * dk ** -0.5 for h in heads]
    ks = [_l2n(qkv[:, GDN_INNER + h * dk:GDN_INNER + (h + 1) * dk]) for h in heads]
    vs = [qkv[:, 2 * GDN_INNER + h * dk:2 * GDN_INNER + (h + 1) * dk] for h in heads]
    g_cols = [gcum[:, GDN_A_LANE + h:GDN_A_LANE + h + 1] for h in heads]
    g_lasts = [gcum[lpad - 1:lpad, GDN_A_LANE + h:GDN_A_LANE + h + 1] for h in heads]
    betas = [beta_all[:, GDN_B_LANE + h:GDN_B_LANE + h + 1] for h in heads]
    decays = [jnp.exp(jnp.where(c <= r, g_cols[h] - gcum_t[GDN_A_LANE + h:GDN_A_LANE + h + 1, :], -jnp.inf))
              for h in heads]
    kbs = [ks[h] * betas[h] for h in heads]
    k_bs = [k.astype(bf16) for k in ks]
    kks = [_dot_nt(kbs[h].astype(bf16), k_bs[h]) for h in heads]
    attns = [_dot_nt(qs[h].astype(bf16), k_bs[h]) * decays[h] for h in heads]
    a_invs = _unit_lower_inverse_multi([jnp.where(c < r, kks[h] * decays[h], 0.0) for h in heads], lpad)
    a_his = [a.astype(bf16) for a in a_invs]
    a_los = [(a - ah.astype(f32)).astype(bf16) for a, ah in zip(a_invs, a_his)]
    vb_bs = [(vs[h] * betas[h]).astype(bf16) for h in heads]
    kbe_bs = [(kbs[h] * jnp.exp(g_cols[h])).astype(bf16) for h in heads]
    us = [_dot(a_his[h], vb_bs[h]) + _dot(a_los[h], vb_bs[h]) for h in heads]
    ws = [_dot(a_his[h], kbe_bs[h]) + _dot(a_los[h], kbe_bs[h]) for h in heads]
    sts = [sout_ref[0, h] for h in heads]
    st_bs = [st.astype(bf16) for st in sts]
    v_news = [(us[h] - _dot(ws[h].astype(bf16), st_bs[h])).astype(bf16) for h in heads]
    os_ = [_dot((qs[h] * jnp.exp(g_cols[h])).astype(bf16), st_bs[h]) + _dot(attns[h].astype(bf16), v_news[h])
           for h in heads]
    ke_ts = [(ks[h] * jnp.exp(g_lasts[h] - g_cols[h])).T.astype(bf16) for h in heads]
    for h in heads:
        sout_ref[0, h] = sts[h] * jnp.exp(g_lasts[h]) + _dot(ke_ts[h], v_news[h])
    outs = []
    for h in heads:
        o = os_[h][:rows]
        ms = jnp.mean(o * o, axis=-1, keepdims=True)
        outs.append(o * lax.rsqrt(ms + EPS) * nw_ref[...] * _silu(z_ref[:, h * dk:(h + 1) * dk]))
    o_ref[...] = jnp.concatenate(outs, axis=1)


def _gdn(proj, tail, s0, prm, b, l, rows, lpad):
    nc = l // rows
    cw, cb, bias, coef, nw = prm
    c2 = lambda i, j: (0, 0)
    width = 3 * GDN_INNER
    return pl.pallas_call(
        functools.partial(_gdn_staged_kernel, rows=rows, lpad=lpad),
        out_shape=(jax.ShapeDtypeStruct((b * l, GDN_INNER), f32),
                   jax.ShapeDtypeStruct((b, GDN_HEADS, GDN_HEAD_DIM, GDN_HEAD_DIM), f32)),
        grid=(b, nc),
        in_specs=[pl.BlockSpec((rows, width), lambda i, j: (i * nc + j, COL0_QKV // width)),
                  pl.BlockSpec((rows, GDN_INNER), lambda i, j: (i * nc + j, COL0_ZG // GDN_INNER)),
                  pl.BlockSpec((rows, LANES), lambda i, j: (i * nc + j, COL0_SM // LANES)),
                  pl.BlockSpec((1, CONV_TAIL, width), lambda i, j: (i, 0, 0)),
                  pl.BlockSpec((1, GDN_HEADS, GDN_HEAD_DIM, GDN_HEAD_DIM), lambda i, j: (i, 0, 0, 0)),
                  pl.BlockSpec((CONV_K, width), c2), pl.BlockSpec((1, width), c2),
                  pl.BlockSpec((1, LANES), c2), pl.BlockSpec((1, LANES), c2),
                  pl.BlockSpec((1, GDN_HEAD_DIM), c2)],
        out_specs=(pl.BlockSpec((rows, GDN_INNER), lambda i, j: (i * nc + j, 0)),
                   pl.BlockSpec((1, GDN_HEADS, GDN_HEAD_DIM, GDN_HEAD_DIM), lambda i, j: (i, 0, 0, 0))),
        scratch_shapes=[pltpu.VMEM((CONV_TAIL + lpad, width), f32)],
        compiler_params=_cparams("arbitrary", "arbitrary"),
        name="gdn",
    )(proj, proj, proj, tail, s0, cw, cb, bias, coef, nw)


def _layer0_params(p):
    def lanes(v, off):
        return jnp.zeros((1, LANES), f32).at[0, off:off + v.shape[0]].set(v.astype(f32))

    cw = p['conv_w'].astype(f32)
    cb = p['conv_b'].astype(f32).reshape(1, CONV_CH)
    bias = lanes(p['ssd_dt_bias'], 0) + lanes(p['gdn_dt_bias'], GDN_A_LANE)
    coef = lanes(-jnp.exp(p['ssd_A_log'].astype(f32)), 0) + lanes(-jnp.exp(p['gdn_A_log'].astype(f32)), GDN_A_LANE)
    dd = jnp.repeat(p['ssd_D'].astype(f32), SSD_HEAD_DIM).reshape(1, SSD_INNER)
    return {'ssd': (cw[:, :SSD_CONV_CH], cb[:, :SSD_CONV_CH], bias, coef, dd,
                    p['ssd_norm'].astype(f32).reshape(1, SSD_INNER)),
            'gdn': (cw[:, SSD_CONV_CH:], cb[:, SSD_CONV_CH:], bias, coef,
                    p['gdn_norm'].astype(f32).reshape(1, GDN_HEAD_DIM))}


def _ada_kernel(c_ref, w_ref, b_ref, o_ref):
    o_ref[...] = _dot(_silu(c_ref[...]).astype(bf16), w_ref[...].astype(bf16)) + b_ref[...]


def _ada(c, w_ada, b_ada, tn):
    rows, d = c.shape
    n = w_ada.shape[1]
    return pl.pallas_call(
        _ada_kernel,
        out_shape=jax.ShapeDtypeStruct((rows, n), f32),
        grid=(n // tn,),
        in_specs=[pl.BlockSpec((rows, d), lambda j: (0, 0)),
                  pl.BlockSpec((d, tn), lambda j: (0, j)),
                  pl.BlockSpec((1, tn), lambda j: (0, j))],
        out_specs=pl.BlockSpec((rows, tn), lambda j: (0, j)),
        compiler_params=_cparams("parallel"),
        name="ada",
    )(c, w_ada, b_ada.reshape(1, n))


def _mods(mod, l, tm):
    parts = jnp.split(mod, 6, axis=-1)
    if l % tm == 0:
        return [p[:, None, :] for p in parts]
    b = mod.shape[0]
    return [jnp.repeat(p, l, axis=0).reshape((b * l) // tm, tm, D_MODEL) for p in parts]


def _run_trunk(x, mod0, mod1, conv_buf, ssd_h0, gdn_s0, past, s5_re0, s5_im0, p, tm):
    b, l, d = x.shape
    t = b * l
    x2 = x.reshape(t, d)

    sh1, sc1, g1, sh2, sc2, g2 = _mods(mod0, l, tm)
    proj0 = _inproj(x2, p['norm_mix0'], sc1, sh1, p['w_in0'], tm, TN_INPROJ)
    tail = jnp.pad(conv_buf.astype(f32), ((0, 0), (CONV_TAIL - (CONV_K - 1), 0), (0, 0)))
    if l % SSD_ROWS == 0 and l % GDN_ROWS == 0:
        ssd_rows, ssd_lpad, gdn_rows, gdn_lpad = SSD_ROWS, SSD_ROWS, GDN_ROWS, GDN_ROWS
    else:
        ssd_rows, ssd_lpad, gdn_rows, gdn_lpad = l, PAD_ROWS, l, PAD_ROWS
    y_ssd, ssd_new = _ssd(proj0, tail[..., :SSD_CONV_CH], ssd_h0.astype(f32), p['l0']['ssd'], b, l, ssd_rows, ssd_lpad)
    o_gdn, gdn_new = _gdn(proj0, tail[..., SSD_CONV_CH:], gdn_s0.astype(f32), p['l0']['gdn'], b, l, gdn_rows, gdn_lpad)
    last = proj0.reshape(b, l, IN0_PAD)[:, l - (CONV_K - 1):]
    conv_new = jnp.concatenate([last[..., COL0_XS:COL0_XS + SSD_INNER], last[..., COL0_BC:COL0_SM],
                                last[..., COL0_QKV:COL0_QKV + 3 * GDN_INNER]], axis=-1)
    w_out0 = p['w_out0']
    x2 = _outproj([y_ssd, o_gdn], [w_out0[:SSD_INNER], w_out0[SSD_INNER:]], x2, g1, tm)
    x2 = _ffn(x2, p['norm_ffn0'], sc2, sh2, g2, p['ffn_w1'], p['ffn_w3'], p['ffn_w2'], min(tm, TM_FFN), TF_FFN)

    sh1, sc1, g1, sh2, sc2, g2 = _mods(mod1, l, tm)
    proj2 = _inproj(x2, p['norm_mix1'], sc1, sh1, p['w_in1'], tm, TN_INPROJ)
    proj = proj2.reshape(b, l, IN1_PAD)
    k_new = proj[..., FOX_INNER:2 * FOX_INNER].reshape(b, l, FOX_HEADS, FOX_HEAD_DIM)
    v_new = proj[..., 2 * FOX_INNER:3 * FOX_INNER].reshape(b, l, FOX_HEADS, FOX_HEAD_DIM)
    sp = p['s5']
    ucol = (3 * FOX_INNER) // S5_QC
    fcol = (3 * FOX_INNER + S5_INNER) // LANES
    if past is None:
        logf_pad, fcs, _ = _fcum(proj2, fcol, p['fox_f_bias'], b, l, FCUM_ROWS)
        logf = logf_pad[:, :FOX_HEADS].reshape(b, l, FOX_HEADS)
        o_fox = _fox_prompt(proj2, fcs, b, l, FOX_TQ, FOX_TK)
        o_s5, s5_re, s5_im = _s5_chain(proj2, ucol, s5_re0.reshape(b, 1, S5_LANES),
                                       s5_im0.reshape(b, 1, S5_LANES), sp, b, l, S5_STEPS)
    else:
        o_fox, logf_pad = _fox_paged(proj2, fcol, p['fox_f_bias'], *past)
        logf = logf_pad[:, :FOX_HEADS].reshape(b, l, FOX_HEADS)
        o_s5, s5_re, s5_im = _s5_batch(proj2, ucol, s5_re0.reshape(b, S5_LANES),
                                       s5_im0.reshape(b, S5_LANES), sp, b, l)
    s5_re = s5_re.reshape(b, S5_GROUPS, S5_STATE)
    s5_im = s5_im.reshape(b, S5_GROUPS, S5_STATE)
    w_out1 = p['w_out1']
    x2 = _outproj([o_fox, o_s5], [w_out1[:FOX_INNER], w_out1[FOX_INNER:].reshape(S5_QUARTERS, S5_QC, d)], x2, g1, tm)
    halves = 2 if t % (2 * tm) == 0 else 1
    y_out = _moe_grouped(x2, p['norm_ffn1'], sc2, sh2, g2, p['router_w'], p['moe_w1'], p['moe_w3'], p['moe_w2'],
                         p['norm_final'], tm, halves, TF_MOE, MOE_CAP).reshape(b, l, d)
    return y_out, conv_new, ssd_new, gdn_new, k_new, v_new, logf, s5_re, s5_im


def kernel(x_prompt, x_sample, state_conv0, state_ssd, state_gdn, cache_k, cache_v, cache_logf,
           state_s5_re, state_s5_im, page_table, c_prompt, c_sample,
           ada0_w, ada0_b, norm_mix0, w_in0, conv0_w, conv0_b, ssd_dt_bias, ssd_A_log, ssd_D,
           ssd_norm, gdn_dt_bias, gdn_A_log, gdn_norm, w_out0, norm_ffn0, ffn_w1, ffn_w3, ffn_w2,
           ada1_w, ada1_b, norm_mix1, w_in1, fox_f_bias, s5_A_re, s5_A_im, s5_log_step,
           s5_B_re, s5_B_im, s5_C_re, s5_C_im, s5_D, glu_w, glu_b, w_out1, norm_ffn1,
           router_w, moe_w1, moe_w3, moe_w2, norm_final):
    d = D_MODEL
    c0 = CONV_CH
    w_in0p = jnp.concatenate([
        w_in0[:, SSD_CONV_CH:c0],
        w_in0[:, :SSD_INNER],
        w_in0[:, c0:c0 + SSD_INNER],
        w_in0[:, c0 + SSD_INNER + SSD_HEADS:c0 + SSD_INNER + SSD_HEADS + GDN_INNER],
        w_in0[:, SSD_INNER:SSD_CONV_CH],
        w_in0[:, c0 + SSD_INNER:c0 + SSD_INNER + SSD_HEADS],
        w_in0[:, c0 + SSD_INNER + SSD_HEADS + GDN_INNER:],
        jnp.zeros((d, IN0_PAD - w_in0.shape[1]), f32)], axis=1).astype(bf16)
    f0 = 3 * FOX_INNER
    w_in1p = jnp.concatenate([
        w_in1[:, :f0],
        w_in1[:, f0 + FOX_HEADS:],
        w_in1[:, f0:f0 + FOX_HEADS],
        jnp.zeros((d, IN1_PAD - w_in1.shape[1]), f32)], axis=1).astype(bf16)
    router_wp = jnp.concatenate([router_w.astype(f32), jnp.zeros((d, LANES - N_EXPERTS), f32)], axis=1)
    p = {
        'ada0_w': ada0_w, 'ada0_b': ada0_b, 'norm_mix0': norm_mix0.reshape(1, d), 'w_in0': w_in0p,
        'conv0_w': conv0_w, 'conv0_b': conv0_b, 'ssd_dt_bias': ssd_dt_bias, 'ssd_A_log': ssd_A_log,
        'ssd_D': ssd_D, 'ssd_norm': ssd_norm, 'gdn_dt_bias': gdn_dt_bias, 'gdn_A_log': gdn_A_log,
        'gdn_norm': gdn_norm, 'w_out0': w_out0.astype(bf16), 'norm_ffn0': norm_ffn0.reshape(1, d),
        'ffn_w1': ffn_w1.astype(bf16), 'ffn_w3': ffn_w3.astype(bf16), 'ffn_w2': ffn_w2.astype(bf16),
        'ada1_w': ada1_w, 'ada1_b': ada1_b, 'norm_mix1': norm_mix1.reshape(1, d), 'w_in1': w_in1p,
        'fox_f_bias': jnp.concatenate([fox_f_bias.astype(f32), jnp.zeros((LANES - FOX_HEADS,), f32)]).reshape(1, LANES),
        's5_A_re': s5_A_re, 's5_A_im': s5_A_im, 's5_log_step': s5_log_step,
        's5_B_re': s5_B_re, 's5_B_im': s5_B_im, 's5_C_re': s5_C_re, 's5_C_im': s5_C_im, 's5_D': s5_D,
        'glu_w': glu_w, 'glu_b': glu_b, 'w_out1': w_out1.astype(bf16), 'norm_ffn1': norm_ffn1.reshape(1, d),
        'router_w': router_wp, 'moe_w1': moe_w1.astype(bf16), 'moe_w3': moe_w3.astype(bf16),
        'moe_w2': moe_w2.astype(bf16), 'norm_final': norm_final.astype(f32).reshape(1, d),
    }
    p['s5'] = _s5_params(p)
    p['l0'] = _layer0_params({'conv_w': conv0_w, 'conv_b': conv0_b, 'ssd_dt_bias': ssd_dt_bias, 'ssd_A_log': ssd_A_log,
                              'ssd_D': ssd_D, 'ssd_norm': ssd_norm, 'gdn_dt_bias': gdn_dt_bias,
                              'gdn_A_log': gdn_A_log, 'gdn_norm': gdn_norm})
    bp = x_prompt.shape[0]
    c_all = jnp.concatenate([c_prompt, c_sample], axis=0).astype(f32)
    mod0 = _ada(c_all, ada0_w, ada0_b, TN_ADA)
    mod1 = _ada(c_all, ada1_w, ada1_b, TN_ADA)
    outs_p = _run_trunk(
        x_prompt, mod0[:bp], mod1[:bp],
        jnp.zeros((bp, CONV_K - 1, CONV_CH), x_prompt.dtype),
        jnp.zeros((bp, SSD_HEADS, SSD_HEAD_DIM, SSD_STATE), f32),
        jnp.zeros((bp, GDN_HEADS, GDN_HEAD_DIM, GDN_HEAD_DIM), f32),
        None,
        jnp.zeros((bp, S5_GROUPS, S5_STATE), f32),
        jnp.zeros((bp, S5_GROUPS, S5_STATE), f32),
        p, TM_PROMPT)
    outs_s = _run_trunk(
        x_sample, mod0[bp:], mod1[bp:], state_conv0, state_ssd, state_gdn, (cache_k, cache_v, cache_logf, page_table),
        state_s5_re, state_s5_im, p, TM_SAMPLE)
    return (outs_p[0], outs_s[0]) + tuple(outs_p[1:]) + tuple(outs_s[1:])
```

```python
import functools
import math

import jax
import jax.numpy as jnp
from jax import lax
from jax.experimental import pallas as pl
from jax.experimental.pallas import tpu as pltpu

f32 = jnp.float32
bf16 = jnp.bfloat16

D_MODEL = 1024
CONV_K = 4
CHUNK = 64
SSD_HEADS = 8
SSD_HEAD_DIM = 64
SSD_INNER = 512
SSD_GROUPS = 2
SSD_STATE = 64
GDN_HEADS = 4
GDN_HEAD_DIM = 128
GDN_INNER = 512
SSD_CONV_CH = 768
CONV_CH = 2304
FOX_HEADS = 8
FOX_HEAD_DIM = 64
FOX_INNER = 512
Q_BLOCK = 128
S5_CH = 16
S5_GROUPS = 32
S5_INNER = 512
S5_STATE = 64
D_FF = 2816
N_EXPERTS = 8
D_FF_EXPERT = 3584
EPS = 1e-6

LANES = 128
IN0_PAD = 3456
IN1_PAD = 2304
VMEM_LIMIT = 56 * 1024 * 1024

TM_PROMPT = 1024
TM_SAMPLE = 256
TN_INPROJ = 1152
TM_FFN = 512
TF_FFN = 2816
TF_MOE = 512
MOE_CAP = 320
TN_ADA = 1536
SSD_ROWS = 256
GDN_ROWS = 128
PAD_ROWS = 128
FOX_TQ = 1024
FOX_TK = 1024
FCUM_ROWS = 512
S5_STEPS = 64


def _cparams(*sem):
    return pltpu.CompilerParams(dimension_semantics=sem, vmem_limit_bytes=VMEM_LIMIT)


def _modnorm(x, g, scale, shift):
    ms = jnp.mean(x * x, axis=-1, keepdims=True)
    y = x * lax.rsqrt(ms + EPS) * g
    return y * (1.0 + scale) + shift


def _split3(a):
    a0 = a.astype(bf16)
    r = a - a0.astype(f32)
    a1 = r.astype(bf16)
    a2 = (r - a1.astype(f32)).astype(bf16)
    return a0, a1, a2


def _dot(a, b):
    return jnp.dot(a, b, preferred_element_type=f32)


def _dot_f32(a, b):
    a0, a1, a2 = _split3(a)
    b0, b1, b2 = _split3(b)
    return (_dot(a0, b0) + _dot(a0, b1) + _dot(a1, b0)
            + _dot(a1, b1) + _dot(a0, b2) + _dot(a2, b0))


def _col_blocks(w, width):
    *lead, d, f = w.shape
    n = len(lead)
    return w.reshape(*lead, d, f // width, width).transpose(*range(n), n + 1, n, n + 2)


def _mod_spec(mod, n_tiles):
    n_mod, rows, d = mod.shape
    per = n_tiles // n_mod
    return pl.BlockSpec((1, rows, d), lambda i, *_: (i // per, 0, 0))


def _inproj_kernel(x_ref, g_ref, sc_ref, sh_ref, w_ref, o_ref, h_scr):
    @pl.when(pl.program_id(1) == 0)
    def _():
        h_scr[...] = _modnorm(x_ref[...], g_ref[...], sc_ref[0], sh_ref[0]).astype(bf16)

    o_ref[...] = _dot(h_scr[...], w_ref[0])


def _inproj(x, g, scale, shift, w, tm, tn):
    t, d = x.shape
    n = w.shape[1]
    n_tiles = t // tm
    return pl.pallas_call(
        _inproj_kernel,
        out_shape=jax.ShapeDtypeStruct((t, n), f32),
        grid=(n_tiles, n // tn),
        in_specs=[
            pl.BlockSpec((tm, d), lambda i, j: (i, 0)),
            pl.BlockSpec((1, d), lambda i, j: (0, 0)),
            _mod_spec(scale, n_tiles),
            _mod_spec(shift, n_tiles),
            pl.BlockSpec((1, d, tn), lambda i, j: (j, 0, 0)),
        ],
        out_specs=pl.BlockSpec((tm, tn), lambda i, j: (i, j)),
        scratch_shapes=[pltpu.VMEM((tm, d), bf16)],
        compiler_params=_cparams("parallel", "arbitrary"),
        name="inproj",
    )(x, g, scale, shift, _col_blocks(w, tn))


def _outproj_kernel(*refs, n_parts):
    mix_refs = refs[:n_parts]
    w_refs = refs[n_parts:2 * n_parts]
    x_ref, gate_ref, o_ref = refs[2 * n_parts:]
    acc = None
    for m_ref, w_ref in zip(mix_refs, w_refs):
        if len(m_ref.shape) == 3:
            terms = [_dot(m_ref[q].astype(bf16), w_ref[q]) for q in range(m_ref.shape[0])]
        else:
            terms = [_dot(m_ref[...].astype(bf16), w_ref[...])]
        for term in terms:
            acc = term if acc is None else acc + term
    o_ref[...] = x_ref[...] + gate_ref[0] * acc


def _outproj(parts, weights, x, gate, tm):
    t, d = x.shape
    n_tiles = t // tm
    in_specs = []
    for a in parts:
        if a.ndim == 3:
            in_specs.append(pl.BlockSpec((a.shape[0], tm, a.shape[2]), lambda i: (0, i, 0)))
        else:
            in_specs.append(pl.BlockSpec((tm, a.shape[1]), lambda i: (i, 0)))
    for w in weights:
        in_specs.append(pl.BlockSpec(w.shape, (lambda i: (0, 0, 0)) if w.ndim == 3 else (lambda i: (0, 0))))
    in_specs += [pl.BlockSpec((tm, d), lambda i: (i, 0)), _mod_spec(gate, n_tiles)]
    return pl.pallas_call(
        functools.partial(_outproj_kernel, n_parts=len(parts)),
        out_shape=jax.ShapeDtypeStruct((t, d), f32),
        grid=(n_tiles,),
        in_specs=in_specs,
        out_specs=pl.BlockSpec((tm, d), lambda i: (i, 0)),
        compiler_params=_cparams("parallel"),
        name="outproj",
    )(*parts, *weights, x, gate)


def _ffn_kernel(x_ref, g_ref, sc_ref, sh_ref, gate_ref, w1_ref, w3_ref, w2_ref, o_ref, h_scr):
    j = pl.program_id(1)

    @pl.when(j == 0)
    def _():
        h_scr[...] = _modnorm(x_ref[...], g_ref[...], sc_ref[0], sh_ref[0]).astype(bf16)
        o_ref[...] = jnp.zeros_like(o_ref)

    h = h_scr[...]
    a = _dot(h, w1_ref[0])
    b = _dot(h, w3_ref[0])
    act = (a * jax.nn.sigmoid(a)) * b
    o_ref[...] += _dot(act.astype(bf16), w2_ref[...])

    @pl.when(j == pl.num_programs(1) - 1)
    def _():
        o_ref[...] = x_ref[...] + gate_ref[0] * o_ref[...]


def _ffn(x, g, scale, shift, gate, w1, w3, w2, tm, tf):
    t, d = x.shape
    f = w1.shape[1]
    n_tiles = t // tm
    return pl.pallas_call(
        _ffn_kernel,
        out_shape=jax.ShapeDtypeStruct((t, d), f32),
        grid=(n_tiles, f // tf),
        in_specs=[
            pl.BlockSpec((tm, d), lambda i, j: (i, 0)),
            pl.BlockSpec((1, d), lambda i, j: (0, 0)),
            _mod_spec(scale, n_tiles),
            _mod_spec(shift, n_tiles),
            _mod_spec(gate, n_tiles),
            pl.BlockSpec((1, d, tf), lambda i, j: (j, 0, 0)),
            pl.BlockSpec((1, d, tf), lambda i, j: (j, 0, 0)),
            pl.BlockSpec((tf, d), lambda i, j: (j, 0)),
        ],
        out_specs=pl.BlockSpec((tm, d), lambda i, j: (i, 0)),
        scratch_shapes=[pltpu.VMEM((tm, d), bf16)],
        compiler_params=_cparams("parallel", "arbitrary"),
        name="ffn",
    )(x, g, scale, shift, gate, _col_blocks(w1, tf), _col_blocks(w3, tf), w2)


def _router_kernel(x_ref, g_ref, sc_ref, sh_ref, rw_ref, h_ref, comb_ref, rank_ref, rankt_ref, cnt_ref):
    tm = x_ref.shape[0]
    h = _modnorm(x_ref[...], g_ref[...], sc_ref[0], sh_ref[0])
    h_ref[...] = h.astype(bf16)
    logits = _dot_f32(h, rw_ref[...])
    lane = lax.broadcasted_iota(jnp.int32, (tm, LANES), 1)
    lg = jnp.where(lane < N_EXPERTS, logits, -jnp.inf)
    m1 = jnp.max(lg, axis=1, keepdims=True)
    i1 = jnp.min(jnp.where(lg == m1, lane, LANES), axis=1, keepdims=True)
    lg2 = jnp.where(lane == i1, -jnp.inf, lg)
    m2 = jnp.max(lg2, axis=1, keepdims=True)
    i2 = jnp.min(jnp.where(lg2 == m2, lane, LANES), axis=1, keepdims=True)
    e2 = jnp.exp(m2 - m1)
    den = 1.0 + e2
    comb_ref[...] = jnp.where(lane == i1, 1.0 / den, jnp.where(lane == i2, e2 / den, 0.0))
    sel = (lane == i1) | (lane == i2)
    ind = jnp.where(sel, 1.0, 0.0)
    row = lax.broadcasted_iota(jnp.int32, (tm, tm), 0)
    col = lax.broadcasted_iota(jnp.int32, (tm, tm), 1)
    below = jnp.where(col < row, 1.0, 0.0).astype(bf16)
    rank = jnp.where(sel, _dot(below, ind.astype(bf16)), -1.0)
    rank_ref[...] = rank
    rankt_ref[0] = rank.T[:N_EXPERTS, :]
    cnt_ref[0] = jnp.broadcast_to(jnp.sum(ind, axis=0, keepdims=True), (8, LANES))


def _router(x, g, scale, shift, rw, tm):
    t, d = x.shape
    n_tiles = t // tm
    return pl.pallas_call(
        _router_kernel,
        out_shape=(
            jax.ShapeDtypeStruct((t, d), bf16),
            jax.ShapeDtypeStruct((t, LANES), f32),
            jax.ShapeDtypeStruct((t, LANES), f32),
            jax.ShapeDtypeStruct((n_tiles, N_EXPERTS, tm), f32),
            jax.ShapeDtypeStruct((n_tiles, 8, LANES), f32),
        ),
        grid=(n_tiles,),
        in_specs=[
            pl.BlockSpec((tm, d), lambda i: (i, 0)),
            pl.BlockSpec((1, d), lambda i: (0, 0)),
            _mod_spec(scale, n_tiles),
            _mod_spec(shift, n_tiles),
            pl.BlockSpec((d, LANES), lambda i: (0, 0)),
        ],
        out_specs=(
            pl.BlockSpec((tm, d), lambda i: (i, 0)),
            pl.BlockSpec((tm, LANES), lambda i: (i, 0)),
            pl.BlockSpec((tm, LANES), lambda i: (i, 0)),
            pl.BlockSpec((1, N_EXPERTS, tm), lambda i: (i, 0, 0)),
            pl.BlockSpec((1, 8, LANES), lambda i: (i, 0, 0)),
        ),
        compiler_params=_cparams("parallel"),
        name="router",
    )(x, g, scale, shift, rw)


def _moe_tile_kernel(cnt_ref, *refs, cap, halves):
    h_ref, gate_ref, comb_ref, rank_ref = refs[:4]
    rankt_refs = refs[4:4 + halves]
    w1_ref, w3_ref, w2_ref, o_ref, hc_scr, y_scr = refs[4 + halves:]
    i = pl.program_id(0)
    e = pl.program_id(1)
    fc = pl.program_id(2)
    last_fc = pl.num_programs(2) - 1
    th = h_ref.shape[0] // halves
    group = halves * cap
    n_batches = (cnt_ref[(i * halves) * N_EXPERTS + e] + cap - 1) // cap
    for hf in range(1, halves):
        n_batches = jnp.maximum(n_batches, (cnt_ref[(i * halves + hf) * N_EXPERTS + e] + cap - 1) // cap)

    @pl.when((e == 0) & (fc == 0))
    def _():
        o_ref[...] = jnp.zeros_like(o_ref)

    @pl.when(fc == 0)
    def _():
        def compact(b, carry):
            rowid = (lax.broadcasted_iota(jnp.int32, (cap, th), 0) + b * cap).astype(f32)
            for hf in range(halves):
                r0 = pl.multiple_of(b * group + hf * cap, 8)
                onehot = jnp.where(rankt_refs[hf][0] == rowid, 1.0, 0.0).astype(bf16)
                hc_scr[pl.ds(r0, cap), :] = _dot(onehot, h_ref[hf * th:(hf + 1) * th, :]).astype(bf16)
            g0 = pl.multiple_of(b * group, 8)
            y_scr[pl.ds(g0, group), :] = jnp.zeros((group, y_scr.shape[1]), f32)
            return carry
        lax.fori_loop(0, n_batches, compact, 0)

    def expert(b, carry):
        g0 = pl.multiple_of(b * group, 8)
        hc = hc_scr[pl.ds(g0, group), :]
        a = _dot(hc, w1_ref[0, 0])
        g = _dot(hc, w3_ref[0, 0])
        act = (a * jax.nn.sigmoid(a)) * g
        y_scr[pl.ds(g0, group), :] += _dot(act.astype(bf16), w2_ref[0])
        return carry
    lax.fori_loop(0, n_batches, expert, 0)

    @pl.when(fc == last_fc)
    def _():
        lane = lax.broadcasted_iota(jnp.int32, (th, LANES), 1)
        for hf in range(halves):
            rows = slice(hf * th, (hf + 1) * th)
            rank_e = jnp.sum(jnp.where(lane == e, rank_ref[rows, :], 0.0), axis=1, keepdims=True)
            gate_e = jnp.sum(jnp.where(lane == e, comb_ref[rows, :], 0.0), axis=1, keepdims=True)
            gate_mod = gate_ref[0] if gate_ref.shape[1] == 1 else gate_ref[0, rows, :]

            def expand(b, carry):
                r0 = pl.multiple_of(b * group + hf * cap, 8)
                colid = (lax.broadcasted_iota(jnp.int32, (th, cap), 1) + b * cap).astype(f32)
                onehot = jnp.where(rank_e == colid, 1.0, 0.0).astype(bf16)
                o_ref[rows, :] += (gate_mod * gate_e) * _dot(onehot, y_scr[pl.ds(r0, cap), :].astype(bf16))
                return carry
            lax.fori_loop(0, n_batches, expand, 0)


def _resnorm_kernel(x_ref, dl_ref, nf_ref, o_ref):
    y = x_ref[...] + dl_ref[...]
    o_ref[...] = y * lax.rsqrt(jnp.mean(y * y, axis=-1, keepdims=True) + EPS) * nf_ref[...]


def _moe_grouped(x, g, scale, shift, gate, rw, w1, w3, w2, norm_final, th, halves, tf, cap):
    t, d = x.shape
    f = w1.shape[2]
    tm = th * halves
    n_tiles = t // tm
    cap = min(cap, th)
    n_cap = -(-th // cap)
    h, comb, rank, rankt, cnt = _router(x, g, scale, shift, rw, th)
    counts = cnt[:, 0, :N_EXPERTS].astype(jnp.int32).reshape(-1)
    rankt = rankt.reshape((t // th) * N_EXPERTS, 1, th)
    once = pl.Buffered(1)
    w1 = _col_blocks(w1, tf)
    w3 = _col_blocks(w3, tf)

    def rankt_map(hf):
        return lambda i, e, c, cnt: ((i * halves + hf) * N_EXPERTS + e, 0, 0)

    grid_spec = pltpu.PrefetchScalarGridSpec(
        num_scalar_prefetch=1,
        grid=(n_tiles, N_EXPERTS, f // tf),
        in_specs=[pl.BlockSpec((tm, d), lambda i, e, c, cnt: (i, 0), pipeline_mode=once),
                  _mod_spec(gate, n_tiles),
                  pl.BlockSpec((tm, LANES), lambda i, e, c, cnt: (i, 0), pipeline_mode=once),
                  pl.BlockSpec((tm, LANES), lambda i, e, c, cnt: (i, 0), pipeline_mode=once)]
                 + [pl.BlockSpec((1, 1, th), rankt_map(hf)) for hf in range(halves)]
                 + [pl.BlockSpec((1, 1, d, tf), lambda i, e, c, cnt: (e, c, 0, 0)),
                    pl.BlockSpec((1, 1, d, tf), lambda i, e, c, cnt: (e, c, 0, 0)),
                    pl.BlockSpec((1, tf, d), lambda i, e, c, cnt: (e, c, 0))],
        out_specs=pl.BlockSpec((tm, d), lambda i, e, c, cnt: (i, 0)),
        scratch_shapes=[pltpu.VMEM((n_cap * halves * cap, d), bf16),
                        pltpu.VMEM((n_cap * halves * cap, d), f32)],
    )
    delta = pl.pallas_call(
        functools.partial(_moe_tile_kernel, cap=cap, halves=halves),
        out_shape=jax.ShapeDtypeStruct((t, d), f32),
        grid_spec=grid_spec,
        compiler_params=_cparams("parallel", "arbitrary", "arbitrary"),
        name="moe_tile",
    )(counts, h, gate, comb, rank, *([rankt] * halves), w1, w3, w2)
    tr = min(t, 1024)
    return pl.pallas_call(
        _resnorm_kernel,
        out_shape=jax.ShapeDtypeStruct((t, d), f32),
        grid=(t // tr,),
        in_specs=[pl.BlockSpec((tr, d), lambda i: (i, 0)), pl.BlockSpec((tr, d), lambda i: (i, 0)),
                  pl.BlockSpec((1, d), lambda i: (0, 0))],
        out_specs=pl.BlockSpec((tr, d), lambda i: (i, 0)),
        compiler_params=_cparams("parallel"),
        name="resnorm",
    )(x, delta, norm_final)


S5_LANES = S5_GROUPS * S5_STATE
S5_QUARTERS = 4
S5_QS = S5_LANES // S5_QUARTERS
S5_QC = S5_INNER // S5_QUARTERS


def _s5_load_bu(u_refs, bre_ref, bim_ref, up_scr, bu_r, bu_i, n_sub, m):
    for c in range(S5_QUARTERS):
        for k in range(m):
            up_scr[k * n_sub:(k + 1) * n_sub, c * S5_QC:(c + 1) * S5_QC] = u_refs[c][pl.ds(k, n_sub, stride=m), :]
        uc = up_scr[:, c * S5_QC:(c + 1) * S5_QC].astype(bf16)
        bu_r[:, c * S5_QS:(c + 1) * S5_QS] = _dot(uc, bre_ref[c])
        bu_i[:, c * S5_QS:(c + 1) * S5_QS] = _dot(uc, bim_ref[c])


def _s5_local_scan(lam_r_ref, lam_i_ref, init_r_ref, init_i_ref, bu_r, bu_i, n_sub, m):
    width = 8192 // n_sub
    for c in range(S5_LANES // width):
        cols = slice(c * width, (c + 1) * width)
        lr = jnp.broadcast_to(lam_r_ref[:, cols], (n_sub, width))
        li = jnp.broadcast_to(lam_i_ref[:, cols], (n_sub, width))
        if init_r_ref is None:
            x0 = (jnp.zeros((n_sub, width), f32), jnp.zeros((n_sub, width), f32))
        else:
            x0 = (init_r_ref[:, cols], init_i_ref[:, cols])

        def step(k, carry):
            xr, xi = carry
            rows = pl.ds(pl.multiple_of(k * n_sub, 8), n_sub)
            nr = lr * xr - li * xi + bu_r[rows, cols]
            ni = lr * xi + li * xr + bu_i[rows, cols]
            bu_r[rows, cols] = nr
            bu_i[rows, cols] = ni
            return nr, ni
        lax.fori_loop(0, m, step, x0)


def _gelu_tanh(x):
    return 0.5 * x * (1.0 + jnp.tanh(math.sqrt(2.0 / math.pi) * (x + 0.044715 * (x * x * x))))


def _s5_output(xb_r, xb_i, up_scr, cre_ref, cim_ref, dd_ref, gw_ref, gb_ref, o_ref, op_scr, n_sub, m):
    ys = []
    for c in range(S5_QUARTERS):
        cols = slice(c * S5_QS, (c + 1) * S5_QS)
        ys.append(_dot(xb_r[:, cols].astype(bf16), cre_ref[c]) - _dot(xb_i[:, cols].astype(bf16), cim_ref[c]))
    y = jnp.concatenate(ys, axis=1) + dd_ref[...] * up_scr[...]
    hs = _gelu_tanh(y)
    op_scr[...] = hs * jax.nn.sigmoid(_dot(hs.astype(bf16), gw_ref[...]) + gb_ref[...])
    for c in range(S5_QUARTERS):
        for k in range(m):
            o_ref[c, pl.ds(k, n_sub, stride=m), :] = op_scr[k * n_sub:(k + 1) * n_sub, c * S5_QC:(c + 1) * S5_QC]


def _s5_chain_kernel(u0_ref, u1_ref, u2_ref, u3_ref, x0r_ref, x0i_ref, lam_r_ref, lam_i_ref, bre_ref, bim_ref, cre_ref, cim_ref,
                     dd_ref, gw_ref, gb_ref, o_ref, sr_ref, si_ref,
                     pow_r, pow_i, bu_r, bu_i, up_scr, op_scr, en_r, en_i, *, m):
    n_sub = 8
    j = pl.program_id(1)

    @pl.when((pl.program_id(0) == 0) & (j == 0))
    def _():
        def pstep(k, carry):
            pr, pi = carry
            pow_r[pl.ds(k, 1), :] = pr
            pow_i[pl.ds(k, 1), :] = pi
            lr = lam_r_ref[...]
            li = lam_i_ref[...]
            return lr * pr - li * pi, lr * pi + li * pr
        lax.fori_loop(0, m, pstep, (lam_r_ref[...], lam_i_ref[...]))

    @pl.when(j == 0)
    def _():
        sr_ref[0] = x0r_ref[0]
        si_ref[0] = x0i_ref[0]

    _s5_load_bu((u0_ref, u1_ref, u2_ref, u3_ref), bre_ref, bim_ref, up_scr, bu_r, bu_i, n_sub, m)
    _s5_local_scan(lam_r_ref, lam_i_ref, None, None, bu_r, bu_i, n_sub, m)

    pm_r = pow_r[m - 1:m, :]
    pm_i = pow_i[m - 1:m, :]
    e_r = sr_ref[0]
    e_i = si_ref[0]
    for s in range(n_sub):
        en_r[s:s + 1, :] = e_r
        en_i[s:s + 1, :] = e_i
        row = (m - 1) * n_sub + s
        e_r, e_i = (bu_r[row:row + 1, :] + pm_r * e_r - pm_i * e_i,
                    bu_i[row:row + 1, :] + pm_r * e_i + pm_i * e_r)
    sr_ref[0] = e_r
    si_ref[0] = e_i

    def fix(k, carry):
        rows = pl.ds(pl.multiple_of(k * n_sub, 8), n_sub)
        pr = pow_r[pl.ds(k, 1), :]
        pi = pow_i[pl.ds(k, 1), :]
        er = en_r[...]
        ei = en_i[...]
        bu_r[rows, :] = bu_r[rows, :] + pr * er - pi * ei
        bu_i[rows, :] = bu_i[rows, :] + pr * ei + pi * er
        return carry
    lax.fori_loop(0, m, fix, 0)

    _s5_output(bu_r, bu_i, up_scr, cre_ref, cim_ref, dd_ref, gw_ref, gb_ref, o_ref, op_scr, n_sub, m)


def _s5_batch_kernel(u0_ref, u1_ref, u2_ref, u3_ref, x0r_ref, x0i_ref, lam_r_ref, lam_i_ref, bre_ref, bim_ref, cre_ref, cim_ref,
                     dd_ref, gw_ref, gb_ref, o_ref, sr_ref, si_ref,
                     bu_r, bu_i, up_scr, op_scr, *, n_sub, m):
    _s5_load_bu((u0_ref, u1_ref, u2_ref, u3_ref), bre_ref, bim_ref, up_scr, bu_r, bu_i, n_sub, m)
    _s5_local_scan(lam_r_ref, lam_i_ref, x0r_ref, x0i_ref, bu_r, bu_i, n_sub, m)
    last = slice((m - 1) * n_sub, m * n_sub)
    sr_ref[...] = bu_r[last, :]
    si_ref[...] = bu_i[last, :]
    _s5_output(bu_r, bu_i, up_scr, cre_ref, cim_ref, dd_ref, gw_ref, gb_ref, o_ref, op_scr, n_sub, m)


def _s5_params(p):
    ar = p['s5_A_re'].astype(f32)
    ai = p['s5_A_im'].astype(f32)
    step = jnp.exp(p['s5_log_step'].astype(f32))[:, None]
    mag = jnp.exp(ar * step)
    lb_re = mag * jnp.cos(ai * step)
    lb_im = mag * jnp.sin(ai * step)
    den = ar * ar + ai * ai
    nr = lb_re - 1.0
    cr = (nr * ar + lb_im * ai) / den
    ci = (lb_im * ar - nr * ai) / den
    bb_re = cr[..., None] * p['s5_B_re'] - ci[..., None] * p['s5_B_im']
    bb_im = cr[..., None] * p['s5_B_im'] + ci[..., None] * p['s5_B_re']
    gq = S5_GROUPS // S5_QUARTERS
    eye = jnp.eye(gq, dtype=f32)

    def bq(bb):
        t = bb.reshape(S5_QUARTERS, gq, S5_STATE, S5_CH)
        return jnp.einsum('cgph,gk->cghkp', t, eye).reshape(S5_QUARTERS, S5_QC, S5_QS).astype(bf16)

    def cq(cc):
        t = cc.astype(f32).reshape(S5_QUARTERS, gq, S5_CH, S5_STATE)
        return jnp.einsum('cghp,gk->cgpkh', t, eye).reshape(S5_QUARTERS, S5_QS, S5_QC).astype(bf16)

    return dict(lam_r=lb_re.reshape(1, S5_LANES), lam_i=lb_im.reshape(1, S5_LANES),
                bre=bq(bb_re), bim=bq(bb_im), cre=cq(p['s5_C_re']), cim=cq(p['s5_C_im']),
                dd=p['s5_D'].astype(f32).reshape(1, S5_INNER),
                gw=p['glu_w'].astype(bf16), gb=p['glu_b'].astype(f32).reshape(1, S5_INNER))


def _s5_const_specs(nd):
    z2 = (lambda *_: (0, 0))
    z3 = (lambda *_: (0, 0, 0))
    return [
        pl.BlockSpec((1, S5_LANES), z2), pl.BlockSpec((1, S5_LANES), z2),
        pl.BlockSpec((S5_QUARTERS, S5_QC, S5_QS), z3), pl.BlockSpec((S5_QUARTERS, S5_QC, S5_QS), z3),
        pl.BlockSpec((S5_QUARTERS, S5_QS, S5_QC), z3), pl.BlockSpec((S5_QUARTERS, S5_QS, S5_QC), z3),
        pl.BlockSpec((1, S5_INNER), z2), pl.BlockSpec((S5_INNER, S5_INNER), z2), pl.BlockSpec((1, S5_INNER), z2),
    ]


def _s5_chain(proj, ucol, x0r, x0i, sp, b, l, m):
    chunk = 8 * m
    nc = l // chunk
    consts = [sp[k] for k in ('lam_r', 'lam_i', 'bre', 'bim', 'cre', 'cim', 'dd', 'gw', 'gb')]
    return pl.pallas_call(
        functools.partial(_s5_chain_kernel, m=m),
        out_shape=(jax.ShapeDtypeStruct((S5_QUARTERS, b * l, S5_QC), f32),
                   jax.ShapeDtypeStruct((b, 1, S5_LANES), f32),
                   jax.ShapeDtypeStruct((b, 1, S5_LANES), f32)),
        grid=(b, nc),
        in_specs=[pl.BlockSpec((chunk, S5_QC), functools.partial(lambda i, j, c: (i * nc + j, ucol + c), c=c))
                  for c in range(S5_QUARTERS)] + [
                  pl.BlockSpec((1, 1, S5_LANES), lambda i, j: (i, 0, 0)),
                  pl.BlockSpec((1, 1, S5_LANES), lambda i, j: (i, 0, 0))] + _s5_const_specs(2),
        out_specs=(pl.BlockSpec((S5_QUARTERS, chunk, S5_QC), lambda i, j: (0, i * nc + j, 0)),
                   pl.BlockSpec((1, 1, S5_LANES), lambda i, j: (i, 0, 0)),
                   pl.BlockSpec((1, 1, S5_LANES), lambda i, j: (i, 0, 0))),
        scratch_shapes=[pltpu.VMEM((m, S5_LANES), f32), pltpu.VMEM((m, S5_LANES), f32),
                        pltpu.VMEM((chunk, S5_LANES), f32), pltpu.VMEM((chunk, S5_LANES), f32),
                        pltpu.VMEM((chunk, S5_INNER), f32), pltpu.VMEM((chunk, S5_INNER), f32),
                        pltpu.VMEM((8, S5_LANES), f32), pltpu.VMEM((8, S5_LANES), f32)],
        compiler_params=_cparams("arbitrary", "arbitrary"),
        name="s5_chain",
    )(proj, proj, proj, proj, x0r, x0i, *consts)


def _s5_batch(proj, ucol, x0r, x0i, sp, n_sub, m):
    t = n_sub * m
    consts = [sp[k] for k in ('lam_r', 'lam_i', 'bre', 'bim', 'cre', 'cim', 'dd', 'gw', 'gb')]
    return pl.pallas_call(
        functools.partial(_s5_batch_kernel, n_sub=n_sub, m=m),
        out_shape=(jax.ShapeDtypeStruct((S5_QUARTERS, t, S5_QC), f32),
                   jax.ShapeDtypeStruct((n_sub, S5_LANES), f32),
                   jax.ShapeDtypeStruct((n_sub, S5_LANES), f32)),
        grid=(1,),
        in_specs=[pl.BlockSpec((t, S5_QC), functools.partial(lambda i, c: (0, ucol + c), c=c))
                  for c in range(S5_QUARTERS)] + [
                  pl.BlockSpec((n_sub, S5_LANES), lambda i: (0, 0)),
                  pl.BlockSpec((n_sub, S5_LANES), lambda i: (0, 0))] + _s5_const_specs(1),
        out_specs=(pl.BlockSpec((S5_QUARTERS, t, S5_QC), lambda i: (0, 0, 0)),
                   pl.BlockSpec((n_sub, S5_LANES), lambda i: (0, 0)),
                   pl.BlockSpec((n_sub, S5_LANES), lambda i: (0, 0))),
        scratch_shapes=[pltpu.VMEM((t, S5_LANES), f32), pltpu.VMEM((t, S5_LANES), f32),
                        pltpu.VMEM((t, S5_INNER), f32), pltpu.VMEM((t, S5_INNER), f32)],
        compiler_params=_cparams("arbitrary"),
        name="s5_batch",
    )(proj, proj, proj, proj, x0r, x0i, *consts)


def _tri(n, strict=False, upper=False):
    r = lax.broadcasted_iota(jnp.int32, (n, n), 0)
    c = lax.broadcasted_iota(jnp.int32, (n, n), 1)
    if upper:
        r, c = c, r
    return jnp.where((c < r) if strict else (c <= r), 1.0, 0.0).astype(bf16)


def _dot_exact_lhs(a_exact_bf16, b):
    b0, b1, b2 = _split3(b)
    return _dot(a_exact_bf16, b0) + _dot(a_exact_bf16, b1) + _dot(a_exact_bf16, b2)


def _log_sigmoid(x):
    return jnp.minimum(x, 0.0) - jnp.log1p(jnp.exp(-jnp.abs(x)))


def _fcum_kernel(fr_ref, bias_ref, logf_ref, f_ref, ft_ref, carry):
    @pl.when(pl.program_id(1) == 0)
    def _():
        carry[...] = jnp.zeros_like(carry)

    n = fr_ref.shape[0]
    logf = _log_sigmoid(fr_ref[...] + bias_ref[...])
    logf_ref[...] = logf
    f = _dot_exact_lhs(_tri(n), logf) + carry[0:1, :]
    f_ref[...] = f
    ft_ref[0] = f.T[:8, :]
    carry[0:1, :] = f[n - 1:n, :]


def _fcum(proj, col, bias, b, l, chunk):
    nc = l // chunk
    return pl.pallas_call(
        _fcum_kernel,
        out_shape=(jax.ShapeDtypeStruct((b * l, LANES), f32),
                   jax.ShapeDtypeStruct((b * l, LANES), f32),
                   jax.ShapeDtypeStruct((b, 8, l), f32)),
        grid=(b, nc),
        in_specs=[pl.BlockSpec((chunk, LANES), lambda i, j: (i * nc + j, col)),
                  pl.BlockSpec((1, LANES), lambda i, j: (0, 0))],
        out_specs=(pl.BlockSpec((chunk, LANES), lambda i, j: (i * nc + j, 0)),
                   pl.BlockSpec((chunk, LANES), lambda i, j: (i * nc + j, 0)),
                   pl.BlockSpec((1, 8, chunk), lambda i, j: (i, 0, j))),
        scratch_shapes=[pltpu.VMEM((8, LANES), f32)],
        compiler_params=_cparams("arbitrary", "arbitrary"),
        name="fcum",
    )(proj, bias)


N_FPARTS = 3
FOX_STRIP = 256


def _fox_select_mats():
    rows = jnp.arange(N_FPARTS * LANES)[None, :, None]
    cols = jnp.arange(FOX_HEAD_DIM)[None, None, :]
    head = jnp.arange(FOX_HEADS)[:, None, None]
    hit = (rows % LANES == head)
    sq = jnp.where(hit & (cols == rows // LANES), 1.0, 0.0)
    sk = jnp.where(hit & (cols == N_FPARTS + rows // LANES), -1.0, 0.0)
    return sq.astype(bf16), sk.astype(bf16)


def _fox_prep_kernel(q_ref, k_ref, v_ref, f_ref, sq_ref, sk_ref, qa_ref, ka_ref, vt_ref, *, scale):
    tm = q_ref.shape[0]
    hd = FOX_HEAD_DIM
    fcat = jnp.concatenate(_split3(f_ref[...]), axis=1)
    lane = lax.broadcasted_iota(jnp.int32, (tm, hd), 1)
    ones_q = jnp.where((lane >= N_FPARTS) & (lane < 2 * N_FPARTS), 1.0, 0.0)
    ones_k = jnp.where(lane < N_FPARTS, 1.0, 0.0)
    vt = v_ref[...].T
    for h in range(FOX_HEADS):
        cols = slice(h * hd, (h + 1) * hd)
        eq = _dot(fcat, sq_ref[h]) + ones_q
        ek = _dot(fcat, sk_ref[h]) + ones_k
        qa_ref[h] = jnp.concatenate([q_ref[:, cols] * scale, eq], axis=1).astype(bf16)
        ka_ref[h] = jnp.concatenate([k_ref[:, cols], ek], axis=1).astype(bf16)
        vt_ref[h] = vt[h * hd:(h + 1) * hd, :].astype(bf16)


def _fox_prep(proj, f, t, tm):
    sq, sk = _fox_select_mats()
    sel_spec = pl.BlockSpec((FOX_HEADS, N_FPARTS * LANES, FOX_HEAD_DIM), lambda i: (0, 0, 0))
    return pl.pallas_call(
        functools.partial(_fox_prep_kernel, scale=FOX_HEAD_DIM ** -0.5),
        out_shape=(jax.ShapeDtypeStruct((FOX_HEADS, t, LANES), bf16),
                   jax.ShapeDtypeStruct((FOX_HEADS, t, LANES), bf16),
                   jax.ShapeDtypeStruct((FOX_HEADS, FOX_HEAD_DIM, t), bf16)),
        grid=(t // tm,),
        in_specs=[pl.BlockSpec((tm, FOX_INNER), lambda i: (i, 0)),
                  pl.BlockSpec((tm, FOX_INNER), lambda i: (i, 1)),
                  pl.BlockSpec((tm, FOX_INNER), lambda i: (i, 2)),
                  pl.BlockSpec((tm, LANES), lambda i: (i, 0)),
                  sel_spec, sel_spec],
        out_specs=(pl.BlockSpec((FOX_HEADS, tm, LANES), lambda i: (0, i, 0)),
                   pl.BlockSpec((FOX_HEADS, tm, LANES), lambda i: (0, i, 0)),
                   pl.BlockSpec((FOX_HEADS, FOX_HEAD_DIM, tm), lambda i: (0, 0, i))),
        compiler_params=_cparams("parallel"),
        name="fox_prep",
    )(proj, proj, proj, f, sq, sk)


def _fox_kernel(qt_ref, kt_ref, qa_ref, ka_ref, vt_ref, o_ref, m_scr, l_scr, acc_scr):
    step = pl.program_id(2)
    qi = qt_ref[step]
    ki = kt_ref[step]
    tq = qa_ref.shape[1]
    tk = ka_ref.shape[1]
    q_first = qi * tq
    k_first = ki * tk

    @pl.when(ki == 0)
    def _():
        m_scr[...] = jnp.full_like(m_scr, -jnp.inf)
        l_scr[...] = jnp.zeros_like(l_scr)
        acc_scr[...] = jnp.zeros_like(acc_scr)

    def update(masked):
        strip = min(FOX_STRIP, tq)
        chains = [(hh, q0) for hh in range(2) for q0 in range(0, tq, strip)]
        nks = [min(tk, q0 + strip) if (masked and tq == tk) else tk for _, q0 in chains]
        sts = [_dot_nt(ka_ref[hh, :nk, :], qa_ref[hh, q0:q0 + strip, :])
               for (hh, q0), nk in zip(chains, nks)]
        ps, alphas = [], []
        for (hh, q0), nk, st in zip(chains, nks, sts):
            qs = slice(q0, q0 + strip)
            if masked:
                kpos = lax.broadcasted_iota(jnp.int32, (nk, strip), 0) + k_first
                qpos = lax.broadcasted_iota(jnp.int32, (nk, strip), 1) + (q0 + q_first)
                st = jnp.where(kpos <= qpos, st, -jnp.inf)
            m_old = m_scr[hh, :, qs]
            m_new = jnp.maximum(m_old, jnp.max(st, axis=0, keepdims=True))
            alpha = jnp.exp(m_old - m_new)
            p = jnp.exp(st - m_new)
            l_scr[hh, :, qs] = alpha * l_scr[hh, :, qs] + jnp.sum(p, axis=0, keepdims=True)
            m_scr[hh, :, qs] = m_new
            ps.append(p.astype(bf16))
            alphas.append(alpha)
        pvs = [_dot(vt_ref[hh, :, :nk], p) for (hh, _), nk, p in zip(chains, nks, ps)]
        for (hh, q0), alpha, pv in zip(chains, alphas, pvs):
            qs = slice(q0, q0 + strip)
            acc_scr[hh, :, qs] = alpha * acc_scr[hh, :, qs] + pv

    full = k_first + (tk - 1) <= q_first

    @pl.when(full)
    def _():
        update(False)

    @pl.when(jnp.logical_not(full))
    def _():
        update(True)

    @pl.when(k_first + tk >= q_first + tq)
    def _():
        ot = jnp.concatenate([acc_scr[0] / l_scr[0], acc_scr[1] / l_scr[1]], axis=0)
        o_ref[...] = ot.T


def _fox_prompt(proj, f, b, l, tq, tk):
    nq = l // tq
    nk = l // tk
    n_hp = FOX_HEADS // 2
    qa, ka, vt = _fox_prep(proj, f, b * l, max(tq, tk))
    pairs = [(qi, ki) for qi in range(nq) for ki in range(((qi + 1) * tq - 1) // tk + 1)]
    qtab = jnp.array([pr[0] for pr in pairs], jnp.int32)
    ktab = jnp.array([pr[1] for pr in pairs], jnp.int32)
    grid_spec = pltpu.PrefetchScalarGridSpec(
        num_scalar_prefetch=2,
        grid=(b, n_hp, len(pairs)),
        in_specs=[
            pl.BlockSpec((2, tq, LANES), lambda i, h, s, qt, kt: (h, i * nq + qt[s], 0)),
            pl.BlockSpec((2, tk, LANES), lambda i, h, s, qt, kt: (h, i * nk + kt[s], 0)),
            pl.BlockSpec((2, FOX_HEAD_DIM, tk), lambda i, h, s, qt, kt: (h, 0, i * nk + kt[s])),
        ],
        out_specs=pl.BlockSpec((tq, LANES), lambda i, h, s, qt, kt: (i * nq + qt[s], h)),
        scratch_shapes=[pltpu.VMEM((2, 1, tq), f32), pltpu.VMEM((2, 1, tq), f32),
                        pltpu.VMEM((2, FOX_HEAD_DIM, tq), f32)],
    )
    return pl.pallas_call(
        _fox_kernel,
        out_shape=jax.ShapeDtypeStruct((b * l, FOX_INNER), f32),
        grid_spec=grid_spec,
        compiler_params=_cparams("parallel", "parallel", "arbitrary"),
        name="fox_prompt",
    )(qtab, ktab, qa, ka, vt)


PAGES_PER_STEP = 16


def _dot_exact_rhs(a, b_exact_bf16):
    a0, a1, a2 = _split3(a)
    return _dot(a0, b_exact_bf16) + _dot(a1, b_exact_bf16) + _dot(a2, b_exact_bf16)


def _dot_nt(a, b):
    return lax.dot_general(a, b, (((1,), (1,)), ((), ())), preferred_element_type=f32)


def _fox_paged_kernel(pt_ref, *refs, n_pages, pps, scale):
    k_refs, v_refs = refs[:pps], refs[pps:2 * pps]
    (lf_ref, q_ref, kn_ref, vn_ref, fr_ref, bias_ref, o_ref, logf_ref,
     s_scr, qbd_scr, acc_scr, psum_scr, m_scr, car_scr, fq_scr) = refs[2 * pps:]
    seq = pl.program_id(0)
    ph = pl.program_id(1)
    c = pl.program_id(2)
    last_c = pl.num_programs(2) - 1
    nq = q_ref.shape[0]
    hd = FOX_HEAD_DIM
    pg = LANES
    row = lax.broadcasted_iota(jnp.int32, (pg, pg), 0)
    lane = lax.broadcasted_iota(jnp.int32, (pg, pg), 1)
    upper = jnp.where(row <= lane, 1.0, 0.0).astype(bf16)

    def pad_rows(x, fill=0.0):
        return jnp.concatenate([x, jnp.full((pg - x.shape[0], x.shape[1]), fill, x.dtype)], axis=0)

    def per_head_rows(x):
        rep = jnp.broadcast_to(x[:, None, :], (FOX_HEADS, nq, x.shape[1])).reshape(FOX_HEADS * nq, x.shape[1])
        return pad_rows(rep)

    def key_sums(logf_rows):
        cum = _dot_exact_rhs(logf_rows, upper) + car_scr[...]
        car_scr[...] = jnp.broadcast_to(cum[:, pg - 1:pg], (pg, pg))
        return cum

    @pl.when((ph == 0) & (c == 0))
    def _():
        q_rep = jnp.broadcast_to((q_ref[...] * scale)[None], (FOX_HEADS, nq, FOX_INNER)).reshape(FOX_HEADS * nq, FOX_INNER)
        r5 = lax.broadcasted_iota(jnp.int32, (FOX_HEADS * nq, FOX_INNER), 0)
        l5 = lax.broadcasted_iota(jnp.int32, (FOX_HEADS * nq, FOX_INNER), 1)
        qbd_scr[...] = pad_rows(jnp.where(l5 // hd == r5 // nq, q_rep, 0.0)).astype(bf16)
        m_scr[...] = jnp.full_like(m_scr, -jnp.inf)
        car_scr[...] = jnp.zeros_like(car_scr)

    @pl.when(ph == 0)
    def _():
        qk = [_dot(qbd_scr[...], k_refs[i][0].reshape(FOX_INNER, pg).astype(bf16)) for i in range(pps)]
        pages = [pt_ref[seq * n_pages + c * pps + i] for i in range(pps)]
        local = [per_head_rows(_dot_exact_rhs(lf_ref[pages[i]], upper)) for i in range(pps)]
        car = car_scr[...]
        m = m_scr[...]
        for i in range(pps):
            cum = local[i] + car
            car = jnp.broadcast_to(cum[:, pg - 1:pg], (pg, pg))
            s = qk[i] - cum
            s_scr[c * pps + i] = s
            m = jnp.maximum(m, s)
        car_scr[...] = car
        m_scr[...] = m

    @pl.when((ph == 0) & (c == last_c))
    def _():
        logf_new = _log_sigmoid(fr_ref[...] + bias_ref[...])
        logf_ref[...] = logf_new
        cum = key_sums(per_head_rows(pad_rows(logf_new).T[:FOX_HEADS, :]))
        fq = jnp.sum(jnp.where(lane == row % nq, cum, 0.0), axis=1, keepdims=True)
        fq_scr[...] = jnp.broadcast_to(fq, (pg, pg))
        s = _dot(qbd_scr[...], pad_rows(kn_ref[...]).T.astype(bf16)) - cum
        s = jnp.where(lane <= row % nq, s, -jnp.inf)
        s_scr[n_pages] = s
        m = jnp.max(jnp.maximum(m_scr[...], s), axis=1, keepdims=True)
        m_scr[...] = jnp.broadcast_to(m, (pg, pg))

    @pl.when((ph == 1) & (c == 0))
    def _():
        psum_scr[...] = jnp.zeros_like(psum_scr)
        acc_scr[...] = jnp.zeros_like(acc_scr)

    def probs(s):
        fq = fq_scr[...]
        return jnp.exp((s + fq) - (m_scr[...] + fq))

    def accumulate(s, v_t_bf16):
        p = probs(s)
        psum_scr[...] += p
        acc_scr[...] += _dot_nt(v_t_bf16, p.astype(bf16))

    @pl.when(ph == 1)
    def _():
        ps = [probs(s_scr[c * pps + i]) for i in range(pps)]
        pvs = [_dot_nt(v_refs[i][0].reshape(FOX_INNER, pg).astype(bf16), ps[i].astype(bf16)) for i in range(pps)]
        psum = psum_scr[...]
        acc = acc_scr[...]
        for i in range(pps):
            psum = psum + ps[i]
            acc = acc + pvs[i]
        psum_scr[...] = psum
        acc_scr[...] = acc

    @pl.when((ph == 1) & (c == last_c))
    def _():
        accumulate(s_scr[n_pages], pad_rows(vn_ref[...]).T.astype(bf16))
        p0, p1, p2 = _split3(psum_scr[...])
        ones = jnp.ones((8, pg), bf16)
        l_row = (_dot_nt(ones, p0) + _dot_nt(ones, p1) + _dot_nt(ones, p2))[0:1, :]
        o_t = (acc_scr[...] / l_row).T
        o_ref[...] = jnp.concatenate([o_t[h * nq:(h + 1) * nq, h * hd:(h + 1) * hd] for h in range(FOX_HEADS)],
                                     axis=1)


def _fox_paged(proj, fcol, bias, cache_k, cache_v, cache_logf, page_table):
    n_seq, n_pages = page_table.shape
    nq = proj.shape[0] // n_seq
    pps = PAGES_PER_STEP
    n_chunks = n_pages // pps
    ck = jnp.transpose(cache_k, (0, 2, 3, 1))
    cv = jnp.transpose(cache_v, (0, 2, 3, 1))
    clf = jnp.transpose(cache_logf, (0, 2, 1))
    page = ck.shape[3]
    assert page == LANES and nq == 8

    def k_map(i):
        return lambda s, ph, c, pt: (pt[s * n_pages + jnp.where(ph == 0, c, n_chunks - 1) * pps + i], 0, 0, 0)

    def v_map(i):
        return lambda s, ph, c, pt: (pt[s * n_pages + jnp.where(ph == 0, 0, c) * pps + i], 0, 0, 0)

    blk = (1, FOX_HEADS, FOX_HEAD_DIM, page)
    in_specs = ([pl.BlockSpec(blk, k_map(i)) for i in range(pps)]
                + [pl.BlockSpec(blk, v_map(i)) for i in range(pps)]
                + [pl.BlockSpec(clf.shape, lambda s, ph, c, pt: (0, 0, 0), pipeline_mode=pl.Buffered(1)),
                   pl.BlockSpec((nq, FOX_INNER), lambda s, ph, c, pt: (s, 0)),
                   pl.BlockSpec((nq, FOX_INNER), lambda s, ph, c, pt: (s, 1)),
                   pl.BlockSpec((nq, FOX_INNER), lambda s, ph, c, pt: (s, 2)),
                   pl.BlockSpec((nq, LANES), lambda s, ph, c, pt: (s, fcol)),
                   pl.BlockSpec((1, LANES), lambda s, ph, c, pt: (0, 0))])
    grid_spec = pltpu.PrefetchScalarGridSpec(
        num_scalar_prefetch=1,
        grid=(n_seq, 2, n_chunks),
        in_specs=in_specs,
        out_specs=(pl.BlockSpec((nq, FOX_INNER), lambda s, ph, c, pt: (s, 0)),
                   pl.BlockSpec((nq, LANES), lambda s, ph, c, pt: (s, 0))),
        scratch_shapes=[pltpu.VMEM((n_pages + 1, page, page), f32),
                        pltpu.VMEM((page, FOX_INNER), bf16),
                        pltpu.VMEM((FOX_INNER, page), f32),
                        pltpu.VMEM((page, page), f32), pltpu.VMEM((page, page), f32),
                        pltpu.VMEM((page, page), f32), pltpu.VMEM((page, page), f32)],
    )
    return pl.pallas_call(
        functools.partial(_fox_paged_kernel, n_pages=n_pages, pps=pps, scale=FOX_HEAD_DIM ** -0.5),
        out_shape=(jax.ShapeDtypeStruct((n_seq * nq, FOX_INNER), f32),
                   jax.ShapeDtypeStruct((n_seq * nq, LANES), f32)),
        grid_spec=grid_spec,
        compiler_params=_cparams("arbitrary", "arbitrary", "arbitrary"),
        name="fox_paged",
    )(page_table.reshape(-1), *([ck] * pps), *([cv] * pps), clf, proj, proj, proj, proj, bias)


CONV_TAIL = 8
COL0_QKV = 0
COL0_XS = 3 * GDN_INNER
COL0_ZS = COL0_XS + SSD_INNER
COL0_ZG = COL0_ZS + SSD_INNER
COL0_BC = COL0_ZG + GDN_INNER
COL0_SM = COL0_BC + 2 * SSD_GROUPS * SSD_STATE
assert COL0_SM + LANES == IN0_PAD


def _softplus(x):
    return jnp.maximum(x, 0.0) + jnp.log1p(jnp.exp(-jnp.abs(x)))


def _silu(x):
    return x * jax.nn.sigmoid(x)


def _conv_silu(ubuf, u_refs, tail_ref, cw_ref, cb_ref, first, rows, lpad):
    @pl.when(first)
    def _():
        ubuf[0:CONV_TAIL, :] = tail_ref[0]

    @pl.when(jnp.logical_not(first))
    def _():
        ubuf[0:CONV_TAIL, :] = ubuf[rows:rows + CONV_TAIL, :]

    col = 0
    for u_ref in u_refs:
        ubuf[CONV_TAIL:CONV_TAIL + rows, col:col + u_ref.shape[1]] = u_ref[...]
        col += u_ref.shape[1]
    if lpad > rows:
        ubuf[CONV_TAIL + rows:, :] = jnp.zeros((lpad - rows, ubuf.shape[1]), f32)
    acc = cb_ref[...]
    for j in range(CONV_K):
        off = CONV_TAIL - (CONV_K - 1) + j
        acc = acc + cw_ref[j:j + 1, :] * ubuf[off:off + lpad, :]
    return _silu(acc)


def _pad_rows(x, lpad):
    rows = x.shape[0]
    if lpad == rows:
        return x
    return jnp.concatenate([x, jnp.zeros((lpad - rows, x.shape[1]), x.dtype)], axis=0)


def _head_scalars(sm_ref, bias_ref, coef_ref, rows, lpad):
    raw = _pad_rows(sm_ref[...], lpad)
    valid = lax.broadcasted_iota(jnp.int32, (lpad, LANES), 0) < rows
    sp = jnp.where(valid, _softplus(raw + bias_ref[...]), 0.0)
    a = sp * coef_ref[...]
    cum = _dot_exact_lhs(_tri(lpad), a)
    return raw, valid, sp, cum


def _ssd_kernel(ux_ref, ubc_ref, z_ref, sm_ref, tail_ref, h0_ref, cw_ref, cb_ref, bias_ref, coef_ref, dd_ref, nw_ref,
                y_ref, hout_ref, ubuf, *, rows, lpad):
    j = pl.program_id(1)
    n, p = SSD_STATE, SSD_HEAD_DIM
    xbc = _conv_silu(ubuf, (ux_ref, ubc_ref), tail_ref, cw_ref, cb_ref, j == 0, rows, lpad)

    @pl.when(j == 0)
    def _():
        hout_ref[...] = h0_ref[...]

    _, _, dt, acum = _head_scalars(sm_ref, bias_ref, coef_ref, rows, lpad)
    acum_t = acum.T
    dt_t = dt.T
    xs = xbc[:, :SSD_INNER]
    xs_t = xs.T
    r = lax.broadcasted_iota(jnp.int32, (lpad, lpad), 0)
    c = lax.broadcasted_iota(jnp.int32, (lpad, lpad), 1)
    causal = c <= r
    heads = range(SSD_HEADS)
    group_of = [h // (SSD_HEADS // SSD_GROUPS) for h in heads]
    bms = [xbc[:, SSD_INNER + g * n:SSD_INNER + (g + 1) * n] for g in range(SSD_GROUPS)]
    cms = [xbc[:, SSD_INNER + SSD_GROUPS * n + g * n:SSD_INNER + SSD_GROUPS * n + (g + 1) * n]
           for g in range(SSD_GROUPS)]
    cbs = [_dot_nt(cms[g].astype(bf16), bms[g].astype(bf16)) for g in range(SSD_GROUPS)]
    a_cols = [acum[:, h:h + 1] for h in heads]
    a_lasts = [acum[lpad - 1:lpad, h:h + 1] for h in heads]
    x_hs = [xs[:, h * p:(h + 1) * p] for h in heads]
    hsts = [hout_ref[0, h] for h in heads]
    scores = [(cbs[group_of[h]] * jnp.exp(jnp.where(causal, a_cols[h] - acum_t[h:h + 1, :], -jnp.inf))).astype(bf16)
              for h in heads]
    xdts = [(x_hs[h] * dt[:, h:h + 1]).astype(bf16) for h in heads]
    c_exps = [(cms[group_of[h]] * jnp.exp(a_cols[h])).astype(bf16) for h in heads]
    xdt_ts = [(xs_t[h * p:(h + 1) * p, :] * dt_t[h:h + 1, :]).astype(bf16) for h in heads]
    b_ends = [(bms[group_of[h]] * jnp.exp(a_lasts[h] - a_cols[h])).astype(bf16) for h in heads]
    y_diags = [_dot(scores[h], xdts[h]) for h in heads]
    y_offs = [_dot_nt(c_exps[h], hsts[h].astype(bf16)) for h in heads]
    upds = [_dot(xdt_ts[h], b_ends[h]) for h in heads]
    for h in heads:
        hout_ref[0, h] = hsts[h] * jnp.exp(a_lasts[h]) + upds[h]
    ys = [y_diags[h] + y_offs[h] + dd_ref[:, h * p:(h + 1) * p] * x_hs[h] for h in heads]
    y = jnp.concatenate(ys, axis=1)[:rows]
    gated = y * _silu(z_ref[...])
    ms = jnp.mean(gated * gated, axis=-1, keepdims=True)
    y_ref[...] = gated * lax.rsqrt(ms + EPS) * nw_ref[...]


def _ssd(proj, tail, h0, prm, b, l, rows, lpad):
    nc = l // rows
    bc_w = 2 * SSD_GROUPS * SSD_STATE
    cw, cb, bias, coef, dd, nw = prm
    c2 = lambda i, j: (0, 0)
    return pl.pallas_call(
        functools.partial(_ssd_kernel, rows=rows, lpad=lpad),
        out_shape=(jax.ShapeDtypeStruct((b * l, SSD_INNER), f32),
                   jax.ShapeDtypeStruct((b, SSD_HEADS, SSD_HEAD_DIM, SSD_STATE), f32)),
        grid=(b, nc),
        in_specs=[pl.BlockSpec((rows, SSD_INNER), lambda i, j: (i * nc + j, COL0_XS // SSD_INNER)),
                  pl.BlockSpec((rows, bc_w), lambda i, j: (i * nc + j, COL0_BC // bc_w)),
                  pl.BlockSpec((rows, SSD_INNER), lambda i, j: (i * nc + j, COL0_ZS // SSD_INNER)),
                  pl.BlockSpec((rows, LANES), lambda i, j: (i * nc + j, COL0_SM // LANES)),
                  pl.BlockSpec((1, CONV_TAIL, SSD_CONV_CH), lambda i, j: (i, 0, 0)),
                  pl.BlockSpec((1, SSD_HEADS, SSD_HEAD_DIM, SSD_STATE), lambda i, j: (i, 0, 0, 0)),
                  pl.BlockSpec((CONV_K, SSD_CONV_CH), c2), pl.BlockSpec((1, SSD_CONV_CH), c2),
                  pl.BlockSpec((1, LANES), c2), pl.BlockSpec((1, LANES), c2),
                  pl.BlockSpec((1, SSD_INNER), c2), pl.BlockSpec((1, SSD_INNER), c2)],
        out_specs=(pl.BlockSpec((rows, SSD_INNER), lambda i, j: (i * nc + j, 0)),
                   pl.BlockSpec((1, SSD_HEADS, SSD_HEAD_DIM, SSD_STATE), lambda i, j: (i, 0, 0, 0))),
        scratch_shapes=[pltpu.VMEM((CONV_TAIL + lpad, SSD_CONV_CH), f32)],
        compiler_params=_cparams("arbitrary", "arbitrary"),
        name="ssd",
    )(proj, proj, proj, proj, tail, h0, cw, cb, bias, coef, dd, nw)


def _split2(a):
    a_hi = a.astype(bf16)
    return a_hi, (a - a_hi.astype(f32)).astype(bf16)


def _dot3_split(a_split, b_split):
    a_hi, a_lo = a_split
    b_hi, b_lo = b_split
    return _dot(a_hi, b_hi) + _dot(a_lo, b_hi) + _dot(a_hi, b_lo)


INV_BASE = 16


def _l2n(x):
    return x * lax.rsqrt(jnp.sum(x * x, axis=-1, keepdims=True) + EPS)


GDN_A_LANE = SSD_HEADS
GDN_B_LANE = SSD_HEADS + GDN_HEADS


def _unit_lower_inverse_multi(ms, n):
    r = lax.broadcasted_iota(jnp.int32, (n, n), 0)
    c = lax.broadcasted_iota(jnp.int32, (n, n), 1)
    nb = min(INV_BASE, n)
    diag_blk = (r // nb) == (c // nb)
    eye = jnp.where(r == c, 1.0, 0.0)
    pws = [jnp.where(diag_blk, m, 0.0) for m in ms]
    invs = [eye - d for d in pws]
    pw_ss = [_split2(pw) for pw in pws]
    size = 2
    while size < nb:
        pws = [_dot3_split(pw_s, pw_s) for pw_s in pw_ss]
        pw_ss = [_split2(pw) for pw in pws]
        invs = [inv + _dot3_split(_split2(inv), pw_s) for inv, pw_s in zip(invs, pw_ss)]
        size *= 2
    s = nb
    while s < n:
        lower_left = ((r // (2 * s)) == (c // (2 * s))) & ((r // s) % 2 == 1) & ((c // s) % 2 == 0)
        inv_ss = [_split2(inv) for inv in invs]
        tmps = [_dot3_split(inv_s, _split2(jnp.where(lower_left, m, 0.0))) for inv_s, m in zip(inv_ss, ms)]
        invs = [inv - _dot3_split(_split2(tmp), inv_s) for inv, tmp, inv_s in zip(invs, tmps, inv_ss)]
        s *= 2
    return invs


def _gdn_staged_kernel(u_ref, z_ref, sm_ref, tail_ref, s0_ref, cw_ref, cb_ref, bias_ref, coef_ref, nw_ref,
                       o_ref, sout_ref, ubuf, *, rows, lpad):
    j = pl.program_id(1)
    dk = GDN_HEAD_DIM
    heads = range(GDN_HEADS)
    qkv = _conv_silu(ubuf, (u_ref,), tail_ref, cw_ref, cb_ref, j == 0, rows, lpad)

    @pl.when(j == 0)
    def _():
        sout_ref[...] = s0_ref[...]

    raw, valid, _, gcum = _head_scalars(sm_ref, bias_ref, coef_ref, rows, lpad)
    beta_all = jnp.where(valid, jax.nn.sigmoid(raw), 0.0)
    gcum_t = gcum.T
    r = lax.broadcasted_iota(jnp.int32, (lpad, lpad), 0)
    c = lax.broadcasted_iota(jnp.int32, (lpad, lpad), 1)
    qs = [_l2n(qkv[:, h * dk:(h + 1) * dk]) * dk ** -0.5 for h in heads]
    ks = [_l2n(qkv[:, GDN_INNER + h * dk:GDN_INNER + (h + 1) * dk]) for h in heads]
    vs = [qkv[:, 2 * GDN_INNER + h * dk:2 * GDN_INNER + (h + 1) * dk] for h in heads]
    g_cols = [gcum[:, GDN_A_LANE + h:GDN_A_LANE + h + 1] for h in heads]
    g_lasts = [gcum[lpad - 1:lpad, GDN_A_LANE + h:GDN_A_LANE + h + 1] for h in heads]
    betas = [beta_all[:, GDN_B_LANE + h:GDN_B_LANE + h + 1] for h in heads]
    decays = [jnp.exp(jnp.where(c <= r, g_cols[h] - gcum_t[GDN_A_LANE + h:GDN_A_LANE + h + 1, :], -jnp.inf))
              for h in heads]
    kbs = [ks[h] * betas[h] for h in heads]
    k_bs = [k.astype(bf16) for k in ks]
    kks = [_dot_nt(kbs[h].astype(bf16), k_bs[h]) for h in heads]
    attns = [_dot_nt(qs[h].astype(bf16), k_bs[h]) * decays[h] for h in heads]
    a_invs = _unit_lower_inverse_multi([jnp.where(c < r, kks[h] * decays[h], 0.0) for h in heads], lpad)
    a_his = [a.astype(bf16) for a in a_invs]
    a_los = [(a - ah.astype(f32)).astype(bf16) for a, ah in zip(a_invs, a_his)]
    vb_bs = [(vs[h] * betas[h]).astype(bf16) for h in heads]
    kbe_bs = [(kbs[h] * jnp.exp(g_cols[h])).astype(bf16) for h in heads]
    us = [_dot(a_his[h], vb_bs[h]) + _dot(a_los[h], vb_bs[h]) for h in heads]
    ws = [_dot(a_his[h], kbe_bs[h]) + _dot(a_los[h], kbe_bs[h]) for h in heads]
    sts = [sout_ref[0, h] for h in heads]
    st_bs = [st.astype(bf16) for st in sts]
    v_news = [(us[h] - _dot(ws[h].astype(bf16), st_bs[h])).astype(bf16) for h in heads]
    os_ = [_dot((qs[h] * jnp.exp(g_cols[h])).astype(bf16), st_bs[h]) + _dot(attns[h].astype(bf16), v_news[h])
           for h in heads]
    ke_ts = [(ks[h] * jnp.exp(g_lasts[h] - g_cols[h])).T.astype(bf16) for h in heads]
    for h in heads:
        sout_ref[0, h] = sts[h] * jnp.exp(g_lasts[h]) + _dot(ke_ts[h], v_news[h])
    outs = []
    for h in heads:
        o = os_[h][:rows]
        ms = jnp.mean(o * o, axis=-1, keepdims=True)
        outs.append(o * lax.rsqrt(ms + EPS) * nw_ref[...] * _silu(z_ref[:, h * dk:(h + 1) * dk]))
    o_ref[...] = jnp.concatenate(outs, axis=1)


def _gdn(proj, tail, s0, prm, b, l, rows, lpad):
    nc = l // rows
    cw, cb, bias, coef, nw = prm
    c2 = lambda i, j: (0, 0)
    width = 3 * GDN_INNER
    return pl.pallas_call(
        functools.partial(_gdn_staged_kernel, rows=rows, lpad=lpad),
        out_shape=(jax.ShapeDtypeStruct((b * l, GDN_INNER), f32),
                   jax.ShapeDtypeStruct((b, GDN_HEADS, GDN_HEAD_DIM, GDN_HEAD_DIM), f32)),
        grid=(b, nc),
        in_specs=[pl.BlockSpec((rows, width), lambda i, j: (i * nc + j, COL0_QKV // width)),
                  pl.BlockSpec((rows, GDN_INNER), lambda i, j: (i * nc + j, COL0_ZG // GDN_INNER)),
                  pl.BlockSpec((rows, LANES), lambda i, j: (i * nc + j, COL0_SM // LANES)),
                  pl.BlockSpec((1, CONV_TAIL, width), lambda i, j: (i, 0, 0)),
                  pl.BlockSpec((1, GDN_HEADS, GDN_HEAD_DIM, GDN_HEAD_DIM), lambda i, j: (i, 0, 0, 0)),
                  pl.BlockSpec((CONV_K, width), c2), pl.BlockSpec((1, width), c2),
                  pl.BlockSpec((1, LANES), c2), pl.BlockSpec((1, LANES), c2),
                  pl.BlockSpec((1, GDN_HEAD_DIM), c2)],
        out_specs=(pl.BlockSpec((rows, GDN_INNER), lambda i, j: (i * nc + j, 0)),
                   pl.BlockSpec((1, GDN_HEADS, GDN_HEAD_DIM, GDN_HEAD_DIM), lambda i, j: (i, 0, 0, 0))),
        scratch_shapes=[pltpu.VMEM((CONV_TAIL + lpad, width), f32)],
        compiler_params=_cparams("arbitrary", "arbitrary"),
        name="gdn",
    )(proj, proj, proj, tail, s0, cw, cb, bias, coef, nw)


def _layer0_params(p):
    def lanes(v, off):
        return jnp.zeros((1, LANES), f32).at[0, off:off + v.shape[0]].set(v.astype(f32))

    cw = p['conv_w'].astype(f32)
    cb = p['conv_b'].astype(f32).reshape(1, CONV_CH)
    bias = lanes(p['ssd_dt_bias'], 0) + lanes(p['gdn_dt_bias'], GDN_A_LANE)
    coef = lanes(-jnp.exp(p['ssd_A_log'].astype(f32)), 0) + lanes(-jnp.exp(p['gdn_A_log'].astype(f32)), GDN_A_LANE)
    dd = jnp.repeat(p['ssd_D'].astype(f32), SSD_HEAD_DIM).reshape(1, SSD_INNER)
    return {'ssd': (cw[:, :SSD_CONV_CH], cb[:, :SSD_CONV_CH], bias, coef, dd,
                    p['ssd_norm'].astype(f32).reshape(1, SSD_INNER)),
            'gdn': (cw[:, SSD_CONV_CH:], cb[:, SSD_CONV_CH:], bias, coef,
                    p['gdn_norm'].astype(f32).reshape(1, GDN_HEAD_DIM))}


def _ada_kernel(c_ref, w_ref, b_ref, o_ref):
    o_ref[...] = _dot(_silu(c_ref[...]).astype(bf16), w_ref[...].astype(bf16)) + b_ref[...]


def _ada(c, w_ada, b_ada, tn):
    rows, d = c.shape
    n = w_ada.shape[1]
    return pl.pallas_call(
        _ada_kernel,
        out_shape=jax.ShapeDtypeStruct((rows, n), f32),
        grid=(n // tn,),
        in_specs=[pl.BlockSpec((rows, d), lambda j: (0, 0)),
                  pl.BlockSpec((d, tn), lambda j: (0, j)),
                  pl.BlockSpec((1, tn), lambda j: (0, j))],
        out_specs=pl.BlockSpec((rows, tn), lambda j: (0, j)),
        compiler_params=_cparams("parallel"),
        name="ada",
    )(c, w_ada, b_ada.reshape(1, n))


def _mods(mod, l, tm):
    parts = jnp.split(mod, 6, axis=-1)
    if l % tm == 0:
        return [p[:, None, :] for p in parts]
    b = mod.shape[0]
    return [jnp.repeat(p, l, axis=0).reshape((b * l) // tm, tm, D_MODEL) for p in parts]


def _run_trunk(x, mod0, mod1, conv_buf, ssd_h0, gdn_s0, past, s5_re0, s5_im0, p, tm):
    b, l, d = x.shape
    t = b * l
    x2 = x.reshape(t, d)

    sh1, sc1, g1, sh2, sc2, g2 = _mods(mod0, l, tm)
    proj0 = _inproj(x2, p['norm_mix0'], sc1, sh1, p['w_in0'], tm, TN_INPROJ)
    tail = jnp.pad(conv_buf.astype(f32), ((0, 0), (CONV_TAIL - (CONV_K - 1), 0), (0, 0)))
    if l % SSD_ROWS == 0 and l % GDN_ROWS == 0:
        ssd_rows, ssd_lpad, gdn_rows, gdn_lpad = SSD_ROWS, SSD_ROWS, GDN_ROWS, GDN_ROWS
    else:
        ssd_rows, ssd_lpad, gdn_rows, gdn_lpad = l, PAD_ROWS, l, PAD_ROWS
    y_ssd, ssd_new = _ssd(proj0, tail[..., :SSD_CONV_CH], ssd_h0.astype(f32), p['l0']['ssd'], b, l, ssd_rows, ssd_lpad)
    o_gdn, gdn_new = _gdn(proj0, tail[..., SSD_CONV_CH:], gdn_s0.astype(f32), p['l0']['gdn'], b, l, gdn_rows, gdn_lpad)
    last = proj0.reshape(b, l, IN0_PAD)[:, l - (CONV_K - 1):]
    conv_new = jnp.concatenate([last[..., COL0_XS:COL0_XS + SSD_INNER], last[..., COL0_BC:COL0_SM],
                                last[..., COL0_QKV:COL0_QKV + 3 * GDN_INNER]], axis=-1)
    w_out0 = p['w_out0']
    x2 = _outproj([y_ssd, o_gdn], [w_out0[:SSD_INNER], w_out0[SSD_INNER:]], x2, g1, tm)
    x2 = _ffn(x2, p['norm_ffn0'], sc2, sh2, g2, p['ffn_w1'], p['ffn_w3'], p['ffn_w2'], min(tm, TM_FFN), TF_FFN)

    sh1, sc1, g1, sh2, sc2, g2 = _mods(mod1, l, tm)
    proj2 = _inproj(x2, p['norm_mix1'], sc1, sh1, p['w_in1'], tm, TN_INPROJ)
    proj = proj2.reshape(b, l, IN1_PAD)
    k_new = proj[..., FOX_INNER:2 * FOX_INNER].reshape(b, l, FOX_HEADS, FOX_HEAD_DIM)
    v_new = proj[..., 2 * FOX_INNER:3 * FOX_INNER].reshape(b, l, FOX_HEADS, FOX_HEAD_DIM)
    sp = p['s5']
    ucol = (3 * FOX_INNER) // S5_QC
    fcol = (3 * FOX_INNER + S5_INNER) // LANES
    if past is None:
        logf_pad, fcs, _ = _fcum(proj2, fcol, p['fox_f_bias'], b, l, FCUM_ROWS)
        logf = logf_pad[:, :FOX_HEADS].reshape(b, l, FOX_HEADS)
        o_fox = _fox_prompt(proj2, fcs, b, l, FOX_TQ, FOX_TK)
        o_s5, s5_re, s5_im = _s5_chain(proj2, ucol, s5_re0.reshape(b, 1, S5_LANES),
                                       s5_im0.reshape(b, 1, S5_LANES), sp, b, l, S5_STEPS)
    else:
        o_fox, logf_pad = _fox_paged(proj2, fcol, p['fox_f_bias'], *past)
        logf = logf_pad[:, :FOX_HEADS].reshape(b, l, FOX_HEADS)
        o_s5, s5_re, s5_im = _s5_batch(proj2, ucol, s5_re0.reshape(b, S5_LANES),
                                       s5_im0.reshape(b, S5_LANES), sp, b, l)
    s5_re = s5_re.reshape(b, S5_GROUPS, S5_STATE)
    s5_im = s5_im.reshape(b, S5_GROUPS, S5_STATE)
    w_out1 = p['w_out1']
    x2 = _outproj([o_fox, o_s5], [w_out1[:FOX_INNER], w_out1[FOX_INNER:].reshape(S5_QUARTERS, S5_QC, d)], x2, g1, tm)
    halves = 2 if t % (2 * tm) == 0 else 1
    y_out = _moe_grouped(x2, p['norm_ffn1'], sc2, sh2, g2, p['router_w'], p['moe_w1'], p['moe_w3'], p['moe_w2'],
                         p['norm_final'], tm, halves, TF_MOE, MOE_CAP).reshape(b, l, d)
    return y_out, conv_new, ssd_new, gdn_new, k_new, v_new, logf, s5_re, s5_im


def kernel(x_prompt, x_sample, state_conv0, state_ssd, state_gdn, cache_k, cache_v, cache_logf,
           state_s5_re, state_s5_im, page_table, c_prompt, c_sample,
           ada0_w, ada0_b, norm_mix0, w_in0, conv0_w, conv0_b, ssd_dt_bias, ssd_A_log, ssd_D,
           ssd_norm, gdn_dt_bias, gdn_A_log, gdn_norm, w_out0, norm_ffn0, ffn_w1, ffn_w3, ffn_w2,
           ada1_w, ada1_b, norm_mix1, w_in1, fox_f_bias, s5_A_re, s5_A_im, s5_log_step,
           s5_B_re, s5_B_im, s5_C_re, s5_C_im, s5_D, glu_w, glu_b, w_out1, norm_ffn1,
           router_w, moe_w1, moe_w3, moe_w2, norm_final):
    d = D_MODEL
    c0 = CONV_CH
    w_in0p = jnp.concatenate([
        w_in0[:, SSD_CONV_CH:c0],
        w_in0[:, :SSD_INNER],
        w_in0[:, c0:c0 + SSD_INNER],
        w_in0[:, c0 + SSD_INNER + SSD_HEADS:c0 + SSD_INNER + SSD_HEADS + GDN_INNER],
        w_in0[:, SSD_INNER:SSD_CONV_CH],
        w_in0[:, c0 + SSD_INNER:c0 + SSD_INNER + SSD_HEADS],
        w_in0[:, c0 + SSD_INNER + SSD_HEADS + GDN_INNER:],
        jnp.zeros((d, IN0_PAD - w_in0.shape[1]), f32)], axis=1).astype(bf16)
    f0 = 3 * FOX_INNER
    w_in1p = jnp.concatenate([
        w_in1[:, :f0],
        w_in1[:, f0 + FOX_HEADS:],
        w_in1[:, f0:f0 + FOX_HEADS],
        jnp.zeros((d, IN1_PAD - w_in1.shape[1]), f32)], axis=1).astype(bf16)
    router_wp = jnp.concatenate([router_w.astype(f32), jnp.zeros((d, LANES - N_EXPERTS), f32)], axis=1)
    p = {
        'ada0_w': ada0_w, 'ada0_b': ada0_b, 'norm_mix0': norm_mix0.reshape(1, d), 'w_in0': w_in0p,
        'conv0_w': conv0_w, 'conv0_b': conv0_b, 'ssd_dt_bias': ssd_dt_bias, 'ssd_A_log': ssd_A_log,
        'ssd_D': ssd_D, 'ssd_norm': ssd_norm, 'gdn_dt_bias': gdn_dt_bias, 'gdn_A_log': gdn_A_log,
        'gdn_norm': gdn_norm, 'w_out0': w_out0.astype(bf16), 'norm_ffn0': norm_ffn0.reshape(1, d),
        'ffn_w1': ffn_w1.astype(bf16), 'ffn_w3': ffn_w3.astype(bf16), 'ffn_w2': ffn_w2.astype(bf16),
        'ada1_w': ada1_w, 'ada1_b': ada1_b, 'norm_mix1': norm_mix1.reshape(1, d), 'w_in1': w_in1p,
        'fox_f_bias': jnp.concatenate([fox_f_bias.astype(f32), jnp.zeros((LANES - FOX_HEADS,), f32)]).reshape(1, LANES),
        's5_A_re': s5_A_re, 's5_A_im': s5_A_im, 's5_log_step': s5_log_step,
        's5_B_re': s5_B_re, 's5_B_im': s5_B_im, 's5_C_re': s5_C_re, 's5_C_im': s5_C_im, 's5_D': s5_D,
        'glu_w': glu_w, 'glu_b': glu_b, 'w_out1': w_out1.astype(bf16), 'norm_ffn1': norm_ffn1.reshape(1, d),
        'router_w': router_wp, 'moe_w1': moe_w1.astype(bf16), 'moe_w3': moe_w3.astype(bf16),
        'moe_w2': moe_w2.astype(bf16), 'norm_final': norm_final.astype(f32).reshape(1, d),
    }
    p['s5'] = _s5_params(p)
    p['l0'] = _layer0_params({'conv_w': conv0_w, 'conv_b': conv0_b, 'ssd_dt_bias': ssd_dt_bias, 'ssd_A_log': ssd_A_log,
                              'ssd_D': ssd_D, 'ssd_norm': ssd_norm, 'gdn_dt_bias': gdn_dt_bias,
                              'gdn_A_log': gdn_A_log, 'gdn_norm': gdn_norm})
    bp = x_prompt.shape[0]
    c_all = jnp.concatenate([c_prompt, c_sample], axis=0).astype(f32)
    mod0 = _ada(c_all, ada0_w, ada0_b, TN_ADA)
    mod1 = _ada(c_all, ada1_w, ada1_b, TN_ADA)
    outs_p = _run_trunk(
        x_prompt, mod0[:bp], mod1[:bp],
        jnp.zeros((bp, CONV_K - 1, CONV_CH), x_prompt.dtype),
        jnp.zeros((bp, SSD_HEADS, SSD_HEAD_DIM, SSD_STATE), f32),
        jnp.zeros((bp, GDN_HEADS, GDN_HEAD_DIM, GDN_HEAD_DIM), f32),
        None,
        jnp.zeros((bp, S5_GROUPS, S5_STATE), f32),
        jnp.zeros((bp, S5_GROUPS, S5_STATE), f32),
        p, TM_PROMPT)
    outs_s = _run_trunk(
        x_sample, mod0[bp:], mod1[bp:], state_conv0, state_ssd, state_gdn, (cache_k, cache_v, cache_logf, page_table),
        state_s5_re, state_s5_im, p, TM_SAMPLE)
    return (outs_p[0], outs_s[0]) + tuple(outs_p[1:]) + tuple(outs_s[1:])
```

```python
import functools
import math

import jax
import jax.numpy as jnp
from jax import lax
from jax.experimental import pallas as pl
from jax.experimental.pallas import tpu as pltpu

f32 = jnp.float32
bf16 = jnp.bfloat16

D_MODEL = 1024
CONV_K = 4
CHUNK = 64
SSD_HEADS = 8
SSD_HEAD_DIM = 64
SSD_INNER = 512
SSD_GROUPS = 2
SSD_STATE = 64
GDN_HEADS = 4
GDN_HEAD_DIM = 128
GDN_INNER = 512
SSD_CONV_CH = 768
CONV_CH = 2304
FOX_HEADS = 8
FOX_HEAD_DIM = 64
FOX_INNER = 512
Q_BLOCK = 128
S5_CH = 16
S5_GROUPS = 32
S5_INNER = 512
S5_STATE = 64
D_FF = 2816
N_EXPERTS = 8
D_FF_EXPERT = 3584
EPS = 1e-6

LANES = 128
IN0_PAD = 3456
IN1_PAD = 2304
VMEM_LIMIT = 56 * 1024 * 1024

TM_PROMPT = 1024
TM_SAMPLE = 256
TN_INPROJ = 1152
TM_FFN = 512
TF_FFN = 2816
TF_MOE = 512
MOE_CAP = 320
TN_ADA = 1536
SSD_ROWS = 256
GDN_ROWS = 128
PAD_ROWS = 128
FOX_TQ = 1024
FOX_TK = 1024
FCUM_ROWS = 512
S5_STEPS = 64


def _cparams(*sem):
    return pltpu.CompilerParams(dimension_semantics=sem, vmem_limit_bytes=VMEM_LIMIT)


def _modnorm(x, g, scale, shift):
    ms = jnp.mean(x * x, axis=-1, keepdims=True)
    y = x * lax.rsqrt(ms + EPS) * g
    return y * (1.0 + scale) + shift


def _split3(a):
    a0 = a.astype(bf16)
    r = a - a0.astype(f32)
    a1 = r.astype(bf16)
    a2 = (r - a1.astype(f32)).astype(bf16)
    return a0, a1, a2


def _dot(a, b):
    return jnp.dot(a, b, preferred_element_type=f32)


def _dot_f32(a, b):
    a0, a1, a2 = _split3(a)
    b0, b1, b2 = _split3(b)
    return (_dot(a0, b0) + _dot(a0, b1) + _dot(a1, b0)
            + _dot(a1, b1) + _dot(a0, b2) + _dot(a2, b0))


def _col_blocks(w, width):
    *lead, d, f = w.shape
    n = len(lead)
    return w.reshape(*lead, d, f // width, width).transpose(*range(n), n + 1, n, n + 2)


def _mod_spec(mod, n_tiles):
    n_mod, rows, d = mod.shape
    per = n_tiles // n_mod
    return pl.BlockSpec((1, rows, d), lambda i, *_: (i // per, 0, 0))


def _inproj_kernel(x_ref, g_ref, sc_ref, sh_ref, w_ref, o_ref, h_scr):
    @pl.when(pl.program_id(1) == 0)
    def _():
        h_scr[...] = _modnorm(x_ref[...], g_ref[...], sc_ref[0], sh_ref[0]).astype(bf16)

    o_ref[...] = _dot(h_scr[...], w_ref[...])


def _inproj(x, g, scale, shift, w, tm, tn):
    t, d = x.shape
    n = w.shape[1]
    n_tiles = t // tm
    return pl.pallas_call(
        _inproj_kernel,
        out_shape=jax.ShapeDtypeStruct((t, n), f32),
        grid=(n_tiles, n // tn),
        in_specs=[
            pl.BlockSpec((tm, d), lambda i, j: (i, 0)),
            pl.BlockSpec((1, d), lambda i, j: (0, 0)),
            _mod_spec(scale, n_tiles),
            _mod_spec(shift, n_tiles),
            pl.BlockSpec((d, tn), lambda i, j: (0, j)),
        ],
        out_specs=pl.BlockSpec((tm, tn), lambda i, j: (i, j)),
        scratch_shapes=[pltpu.VMEM((tm, d), bf16)],
        compiler_params=_cparams("parallel", "arbitrary"),
        name="inproj",
    )(x, g, scale, shift, w)


def _outproj_kernel(*refs, n_parts):
    mix_refs = refs[:n_parts]
    w_refs = refs[n_parts:2 * n_parts]
    x_ref, gate_ref, o_ref = refs[2 * n_parts:]
    acc = None
    for m_ref, w_ref in zip(mix_refs, w_refs):
        if len(m_ref.shape) == 3:
            terms = [_dot(m_ref[q].astype(bf16), w_ref[q]) for q in range(m_ref.shape[0])]
        else:
            terms = [_dot(m_ref[...].astype(bf16), w_ref[...])]
        for term in terms:
            acc = term if acc is None else acc + term
    o_ref[...] = x_ref[...] + gate_ref[0] * acc


def _outproj(parts, weights, x, gate, tm):
    t, d = x.shape
    n_tiles = t // tm
    in_specs = []
    for a in parts:
        if a.ndim == 3:
            in_specs.append(pl.BlockSpec((a.shape[0], tm, a.shape[2]), lambda i: (0, i, 0)))
        else:
            in_specs.append(pl.BlockSpec((tm, a.shape[1]), lambda i: (i, 0)))
    for w in weights:
        in_specs.append(pl.BlockSpec(w.shape, (lambda i: (0, 0, 0)) if w.ndim == 3 else (lambda i: (0, 0))))
    in_specs += [pl.BlockSpec((tm, d), lambda i: (i, 0)), _mod_spec(gate, n_tiles)]
    return pl.pallas_call(
        functools.partial(_outproj_kernel, n_parts=len(parts)),
        out_shape=jax.ShapeDtypeStruct((t, d), f32),
        grid=(n_tiles,),
        in_specs=in_specs,
        out_specs=pl.BlockSpec((tm, d), lambda i: (i, 0)),
        compiler_params=_cparams("parallel"),
        name="outproj",
    )(*parts, *weights, x, gate)


def _ffn_kernel(x_ref, g_ref, sc_ref, sh_ref, gate_ref, w1_ref, w3_ref, w2_ref, o_ref, h_scr):
    j = pl.program_id(1)

    @pl.when(j == 0)
    def _():
        h_scr[...] = _modnorm(x_ref[...], g_ref[...], sc_ref[0], sh_ref[0]).astype(bf16)
        o_ref[...] = jnp.zeros_like(o_ref)

    h = h_scr[...]
    a = _dot(h, w1_ref[0])
    b = _dot(h, w3_ref[0])
    act = (a * jax.nn.sigmoid(a)) * b
    o_ref[...] += _dot(act.astype(bf16), w2_ref[...])

    @pl.when(j == pl.num_programs(1) - 1)
    def _():
        o_ref[...] = x_ref[...] + gate_ref[0] * o_ref[...]


def _ffn(x, g, scale, shift, gate, w1, w3, w2, tm, tf):
    t, d = x.shape
    f = w1.shape[1]
    n_tiles = t // tm
    return pl.pallas_call(
        _ffn_kernel,
        out_shape=jax.ShapeDtypeStruct((t, d), f32),
        grid=(n_tiles, f // tf),
        in_specs=[
            pl.BlockSpec((tm, d), lambda i, j: (i, 0)),
            pl.BlockSpec((1, d), lambda i, j: (0, 0)),
            _mod_spec(scale, n_tiles),
            _mod_spec(shift, n_tiles),
            _mod_spec(gate, n_tiles),
            pl.BlockSpec((1, d, tf), lambda i, j: (j, 0, 0)),
            pl.BlockSpec((1, d, tf), lambda i, j: (j, 0, 0)),
            pl.BlockSpec((tf, d), lambda i, j: (j, 0)),
        ],
        out_specs=pl.BlockSpec((tm, d), lambda i, j: (i, 0)),
        scratch_shapes=[pltpu.VMEM((tm, d), bf16)],
        compiler_params=_cparams("parallel", "arbitrary"),
        name="ffn",
    )(x, g, scale, shift, gate, _col_blocks(w1, tf), _col_blocks(w3, tf), w2)


def _router_kernel(x_ref, g_ref, sc_ref, sh_ref, rw_ref, h_ref, comb_ref, rank_ref, rankt_ref, cnt_ref):
    tm = x_ref.shape[0]
    h = _modnorm(x_ref[...], g_ref[...], sc_ref[0], sh_ref[0])
    h_ref[...] = h.astype(bf16)
    logits = _dot_f32(h, rw_ref[...])
    lane = lax.broadcasted_iota(jnp.int32, (tm, LANES), 1)
    lg = jnp.where(lane < N_EXPERTS, logits, -jnp.inf)
    m1 = jnp.max(lg, axis=1, keepdims=True)
    i1 = jnp.min(jnp.where(lg == m1, lane, LANES), axis=1, keepdims=True)
    lg2 = jnp.where(lane == i1, -jnp.inf, lg)
    m2 = jnp.max(lg2, axis=1, keepdims=True)
    i2 = jnp.min(jnp.where(lg2 == m2, lane, LANES), axis=1, keepdims=True)
    e2 = jnp.exp(m2 - m1)
    den = 1.0 + e2
    comb_ref[...] = jnp.where(lane == i1, 1.0 / den, jnp.where(lane == i2, e2 / den, 0.0))
    sel = (lane == i1) | (lane == i2)
    ind = jnp.where(sel, 1.0, 0.0)
    row = lax.broadcasted_iota(jnp.int32, (tm, tm), 0)
    col = lax.broadcasted_iota(jnp.int32, (tm, tm), 1)
    below = jnp.where(col < row, 1.0, 0.0).astype(bf16)
    rank = jnp.where(sel, _dot(below, ind.astype(bf16)), -1.0)
    rank_ref[...] = rank
    rankt_ref[0] = rank.T[:N_EXPERTS, :]
    cnt_ref[0] = jnp.broadcast_to(jnp.sum(ind, axis=0, keepdims=True), (8, LANES))


def _router(x, g, scale, shift, rw, tm):
    t, d = x.shape
    n_tiles = t // tm
    return pl.pallas_call(
        _router_kernel,
        out_shape=(
            jax.ShapeDtypeStruct((t, d), bf16),
            jax.ShapeDtypeStruct((t, LANES), f32),
            jax.ShapeDtypeStruct((t, LANES), f32),
            jax.ShapeDtypeStruct((n_tiles, N_EXPERTS, tm), f32),
            jax.ShapeDtypeStruct((n_tiles, 8, LANES), f32),
        ),
        grid=(n_tiles,),
        in_specs=[
            pl.BlockSpec((tm, d), lambda i: (i, 0)),
            pl.BlockSpec((1, d), lambda i: (0, 0)),
            _mod_spec(scale, n_tiles),
            _mod_spec(shift, n_tiles),
            pl.BlockSpec((d, LANES), lambda i: (0, 0)),
        ],
        out_specs=(
            pl.BlockSpec((tm, d), lambda i: (i, 0)),
            pl.BlockSpec((tm, LANES), lambda i: (i, 0)),
            pl.BlockSpec((tm, LANES), lambda i: (i, 0)),
            pl.BlockSpec((1, N_EXPERTS, tm), lambda i: (i, 0, 0)),
            pl.BlockSpec((1, 8, LANES), lambda i: (i, 0, 0)),
        ),
        compiler_params=_cparams("parallel"),
        name="router",
    )(x, g, scale, shift, rw)


def _moe_tile_kernel(cnt_ref, *refs, cap, halves):
    h_ref, gate_ref, comb_ref, rank_ref = refs[:4]
    rankt_refs = refs[4:4 + halves]
    w1_ref, w3_ref, w2_ref, o_ref, hc_scr, y_scr = refs[4 + halves:]
    i = pl.program_id(0)
    e = pl.program_id(1)
    fc = pl.program_id(2)
    last_fc = pl.num_programs(2) - 1
    th = h_ref.shape[0] // halves
    group = halves * cap
    n_batches = (cnt_ref[(i * halves) * N_EXPERTS + e] + cap - 1) // cap
    for hf in range(1, halves):
        n_batches = jnp.maximum(n_batches, (cnt_ref[(i * halves + hf) * N_EXPERTS + e] + cap - 1) // cap)

    @pl.when((e == 0) & (fc == 0))
    def _():
        o_ref[...] = jnp.zeros_like(o_ref)

    @pl.when(fc == 0)
    def _():
        def compact(b, carry):
            rowid = (lax.broadcasted_iota(jnp.int32, (cap, th), 0) + b * cap).astype(f32)
            for hf in range(halves):
                r0 = pl.multiple_of(b * group + hf * cap, 8)
                onehot = jnp.where(rankt_refs[hf][0] == rowid, 1.0, 0.0).astype(bf16)
                hc_scr[pl.ds(r0, cap), :] = _dot(onehot, h_ref[hf * th:(hf + 1) * th, :]).astype(bf16)
            g0 = pl.multiple_of(b * group, 8)
            y_scr[pl.ds(g0, group), :] = jnp.zeros((group, y_scr.shape[1]), f32)
            return carry
        lax.fori_loop(0, n_batches, compact, 0)

    def expert(b, carry):
        g0 = pl.multiple_of(b * group, 8)
        hc = hc_scr[pl.ds(g0, group), :]
        a = _dot(hc, w1_ref[0])
        g = _dot(hc, w3_ref[0])
        act = (a * jax.nn.sigmoid(a)) * g
        y_scr[pl.ds(g0, group), :] += _dot(act.astype(bf16), w2_ref[0])
        return carry
    lax.fori_loop(0, n_batches, expert, 0)

    @pl.when(fc == last_fc)
    def _():
        lane = lax.broadcasted_iota(jnp.int32, (th, LANES), 1)
        for hf in range(halves):
            rows = slice(hf * th, (hf + 1) * th)
            rank_e = jnp.sum(jnp.where(lane == e, rank_ref[rows, :], 0.0), axis=1, keepdims=True)
            gate_e = jnp.sum(jnp.where(lane == e, comb_ref[rows, :], 0.0), axis=1, keepdims=True)
            gate_mod = gate_ref[0] if gate_ref.shape[1] == 1 else gate_ref[0, rows, :]

            def expand(b, carry):
                r0 = pl.multiple_of(b * group + hf * cap, 8)
                colid = (lax.broadcasted_iota(jnp.int32, (th, cap), 1) + b * cap).astype(f32)
                onehot = jnp.where(rank_e == colid, 1.0, 0.0).astype(bf16)
                o_ref[rows, :] += (gate_mod * gate_e) * _dot(onehot, y_scr[pl.ds(r0, cap), :].astype(bf16))
                return carry
            lax.fori_loop(0, n_batches, expand, 0)


def _resnorm_kernel(x_ref, dl_ref, nf_ref, o_ref):
    y = x_ref[...] + dl_ref[...]
    o_ref[...] = y * lax.rsqrt(jnp.mean(y * y, axis=-1, keepdims=True) + EPS) * nf_ref[...]


def _moe_grouped(x, g, scale, shift, gate, rw, w1, w3, w2, norm_final, th, halves, tf, cap):
    t, d = x.shape
    f = w1.shape[2]
    tm = th * halves
    n_tiles = t // tm
    cap = min(cap, th)
    n_cap = -(-th // cap)
    h, comb, rank, rankt, cnt = _router(x, g, scale, shift, rw, th)
    counts = cnt[:, 0, :N_EXPERTS].astype(jnp.int32).reshape(-1)
    rankt = rankt.reshape((t // th) * N_EXPERTS, 1, th)
    once = pl.Buffered(1)

    def rankt_map(hf):
        return lambda i, e, c, cnt: ((i * halves + hf) * N_EXPERTS + e, 0, 0)

    grid_spec = pltpu.PrefetchScalarGridSpec(
        num_scalar_prefetch=1,
        grid=(n_tiles, N_EXPERTS, f // tf),
        in_specs=[pl.BlockSpec((tm, d), lambda i, e, c, cnt: (i, 0), pipeline_mode=once),
                  _mod_spec(gate, n_tiles),
                  pl.BlockSpec((tm, LANES), lambda i, e, c, cnt: (i, 0), pipeline_mode=once),
                  pl.BlockSpec((tm, LANES), lambda i, e, c, cnt: (i, 0), pipeline_mode=once)]
                 + [pl.BlockSpec((1, 1, th), rankt_map(hf)) for hf in range(halves)]
                 + [pl.BlockSpec((1, d, tf), lambda i, e, c, cnt: (e, 0, c)),
                    pl.BlockSpec((1, d, tf), lambda i, e, c, cnt: (e, 0, c)),
                    pl.BlockSpec((1, tf, d), lambda i, e, c, cnt: (e, c, 0))],
        out_specs=pl.BlockSpec((tm, d), lambda i, e, c, cnt: (i, 0)),
        scratch_shapes=[pltpu.VMEM((n_cap * halves * cap, d), bf16),
                        pltpu.VMEM((n_cap * halves * cap, d), f32)],
    )
    delta = pl.pallas_call(
        functools.partial(_moe_tile_kernel, cap=cap, halves=halves),
        out_shape=jax.ShapeDtypeStruct((t, d), f32),
        grid_spec=grid_spec,
        compiler_params=_cparams("parallel", "arbitrary", "arbitrary"),
        name="moe_tile",
    )(counts, h, gate, comb, rank, *([rankt] * halves), w1, w3, w2)
    tr = min(t, 1024)
    return pl.pallas_call(
        _resnorm_kernel,
        out_shape=jax.ShapeDtypeStruct((t, d), f32),
        grid=(t // tr,),
        in_specs=[pl.BlockSpec((tr, d), lambda i: (i, 0)), pl.BlockSpec((tr, d), lambda i: (i, 0)),
                  pl.BlockSpec((1, d), lambda i: (0, 0))],
        out_specs=pl.BlockSpec((tr, d), lambda i: (i, 0)),
        compiler_params=_cparams("parallel"),
        name="resnorm",
    )(x, delta, norm_final)


S5_LANES = S5_GROUPS * S5_STATE
S5_QUARTERS = 4
S5_QS = S5_LANES // S5_QUARTERS
S5_QC = S5_INNER // S5_QUARTERS


def _s5_load_bu(u_refs, bre_ref, bim_ref, up_scr, bu_r, bu_i, n_sub, m):
    for c in range(S5_QUARTERS):
        for k in range(m):
            up_scr[k * n_sub:(k + 1) * n_sub, c * S5_QC:(c + 1) * S5_QC] = u_refs[c][pl.ds(k, n_sub, stride=m), :]
        uc = up_scr[:, c * S5_QC:(c + 1) * S5_QC].astype(bf16)
        bu_r[:, c * S5_QS:(c + 1) * S5_QS] = _dot(uc, bre_ref[c])
        bu_i[:, c * S5_QS:(c + 1) * S5_QS] = _dot(uc, bim_ref[c])


def _s5_local_scan(lam_r_ref, lam_i_ref, init_r_ref, init_i_ref, bu_r, bu_i, n_sub, m):
    width = 8192 // n_sub
    for c in range(S5_LANES // width):
        cols = slice(c * width, (c + 1) * width)
        lr = jnp.broadcast_to(lam_r_ref[:, cols], (n_sub, width))
        li = jnp.broadcast_to(lam_i_ref[:, cols], (n_sub, width))
        if init_r_ref is None:
            x0 = (jnp.zeros((n_sub, width), f32), jnp.zeros((n_sub, width), f32))
        else:
            x0 = (init_r_ref[:, cols], init_i_ref[:, cols])

        def step(k, carry):
            xr, xi = carry
            rows = pl.ds(pl.multiple_of(k * n_sub, 8), n_sub)
            nr = lr * xr - li * xi + bu_r[rows, cols]
            ni = lr * xi + li * xr + bu_i[rows, cols]
            bu_r[rows, cols] = nr
            bu_i[rows, cols] = ni
            return nr, ni
        lax.fori_loop(0, m, step, x0)


def _gelu_tanh(x):
    return 0.5 * x * (1.0 + jnp.tanh(math.sqrt(2.0 / math.pi) * (x + 0.044715 * (x * x * x))))


def _s5_output(xb_r, xb_i, up_scr, cre_ref, cim_ref, dd_ref, gw_ref, gb_ref, o_ref, op_scr, n_sub, m):
    ys = []
    for c in range(S5_QUARTERS):
        cols = slice(c * S5_QS, (c + 1) * S5_QS)
        ys.append(_dot(xb_r[:, cols].astype(bf16), cre_ref[c]) - _dot(xb_i[:, cols].astype(bf16), cim_ref[c]))
    y = jnp.concatenate(ys, axis=1) + dd_ref[...] * up_scr[...]
    hs = _gelu_tanh(y)
    op_scr[...] = hs * jax.nn.sigmoid(_dot(hs.astype(bf16), gw_ref[...]) + gb_ref[...])
    for c in range(S5_QUARTERS):
        for k in range(m):
            o_ref[c, pl.ds(k, n_sub, stride=m), :] = op_scr[k * n_sub:(k + 1) * n_sub, c * S5_QC:(c + 1) * S5_QC]


def _s5_chain_kernel(u0_ref, u1_ref, u2_ref, u3_ref, x0r_ref, x0i_ref, lam_r_ref, lam_i_ref, bre_ref, bim_ref, cre_ref, cim_ref,
                     dd_ref, gw_ref, gb_ref, o_ref, sr_ref, si_ref,
                     pow_r, pow_i, bu_r, bu_i, up_scr, op_scr, en_r, en_i, *, m):
    n_sub = 8
    j = pl.program_id(1)

    @pl.when((pl.program_id(0) == 0) & (j == 0))
    def _():
        def pstep(k, carry):
            pr, pi = carry
            pow_r[pl.ds(k, 1), :] = pr
            pow_i[pl.ds(k, 1), :] = pi
            lr = lam_r_ref[...]
            li = lam_i_ref[...]
            return lr * pr - li * pi, lr * pi + li * pr
        lax.fori_loop(0, m, pstep, (lam_r_ref[...], lam_i_ref[...]))

    @pl.when(j == 0)
    def _():
        sr_ref[0] = x0r_ref[0]
        si_ref[0] = x0i_ref[0]

    _s5_load_bu((u0_ref, u1_ref, u2_ref, u3_ref), bre_ref, bim_ref, up_scr, bu_r, bu_i, n_sub, m)
    _s5_local_scan(lam_r_ref, lam_i_ref, None, None, bu_r, bu_i, n_sub, m)

    pm_r = pow_r[m - 1:m, :]
    pm_i = pow_i[m - 1:m, :]
    e_r = sr_ref[0]
    e_i = si_ref[0]
    for s in range(n_sub):
        en_r[s:s + 1, :] = e_r
        en_i[s:s + 1, :] = e_i
        row = (m - 1) * n_sub + s
        e_r, e_i = (bu_r[row:row + 1, :] + pm_r * e_r - pm_i * e_i,
                    bu_i[row:row + 1, :] + pm_r * e_i + pm_i * e_r)
    sr_ref[0] = e_r
    si_ref[0] = e_i

    def fix(k, carry):
        rows = pl.ds(pl.multiple_of(k * n_sub, 8), n_sub)
        pr = pow_r[pl.ds(k, 1), :]
        pi = pow_i[pl.ds(k, 1), :]
        er = en_r[...]
        ei = en_i[...]
        bu_r[rows, :] = bu_r[rows, :] + pr * er - pi * ei
        bu_i[rows, :] = bu_i[rows, :] + pr * ei + pi * er
        return carry
    lax.fori_loop(0, m, fix, 0)

    _s5_output(bu_r, bu_i, up_scr, cre_ref, cim_ref, dd_ref, gw_ref, gb_ref, o_ref, op_scr, n_sub, m)


def _s5_batch_kernel(u0_ref, u1_ref, u2_ref, u3_ref, x0r_ref, x0i_ref, lam_r_ref, lam_i_ref, bre_ref, bim_ref, cre_ref, cim_ref,
                     dd_ref, gw_ref, gb_ref, o_ref, sr_ref, si_ref,
                     bu_r, bu_i, up_scr, op_scr, *, n_sub, m):
    _s5_load_bu((u0_ref, u1_ref, u2_ref, u3_ref), bre_ref, bim_ref, up_scr, bu_r, bu_i, n_sub, m)
    _s5_local_scan(lam_r_ref, lam_i_ref, x0r_ref, x0i_ref, bu_r, bu_i, n_sub, m)
    last = slice((m - 1) * n_sub, m * n_sub)
    sr_ref[...] = bu_r[last, :]
    si_ref[...] = bu_i[last, :]
    _s5_output(bu_r, bu_i, up_scr, cre_ref, cim_ref, dd_ref, gw_ref, gb_ref, o_ref, op_scr, n_sub, m)


def _s5_params(p):
    ar = p['s5_A_re'].astype(f32)
    ai = p['s5_A_im'].astype(f32)
    step = jnp.exp(p['s5_log_step'].astype(f32))[:, None]
    mag = jnp.exp(ar * step)
    lb_re = mag * jnp.cos(ai * step)
    lb_im = mag * jnp.sin(ai * step)
    den = ar * ar + ai * ai
    nr = lb_re - 1.0
    cr = (nr * ar + lb_im * ai) / den
    ci = (lb_im * ar - nr * ai) / den
    bb_re = cr[..., None] * p['s5_B_re'] - ci[..., None] * p['s5_B_im']
    bb_im = cr[..., None] * p['s5_B_im'] + ci[..., None] * p['s5_B_re']
    gq = S5_GROUPS // S5_QUARTERS
    eye = jnp.eye(gq, dtype=f32)

    def bq(bb):
        t = bb.reshape(S5_QUARTERS, gq, S5_STATE, S5_CH)
        return jnp.einsum('cgph,gk->cghkp', t, eye).reshape(S5_QUARTERS, S5_QC, S5_QS).astype(bf16)

    def cq(cc):
        t = cc.astype(f32).reshape(S5_QUARTERS, gq, S5_CH, S5_STATE)
        return jnp.einsum('cghp,gk->cgpkh', t, eye).reshape(S5_QUARTERS, S5_QS, S5_QC).astype(bf16)

    return dict(lam_r=lb_re.reshape(1, S5_LANES), lam_i=lb_im.reshape(1, S5_LANES),
                bre=bq(bb_re), bim=bq(bb_im), cre=cq(p['s5_C_re']), cim=cq(p['s5_C_im']),
                dd=p['s5_D'].astype(f32).reshape(1, S5_INNER),
                gw=p['glu_w'].astype(bf16), gb=p['glu_b'].astype(f32).reshape(1, S5_INNER))


def _s5_const_specs(nd):
    z2 = (lambda *_: (0, 0))
    z3 = (lambda *_: (0, 0, 0))
    return [
        pl.BlockSpec((1, S5_LANES), z2), pl.BlockSpec((1, S5_LANES), z2),
        pl.BlockSpec((S5_QUARTERS, S5_QC, S5_QS), z3), pl.BlockSpec((S5_QUARTERS, S5_QC, S5_QS), z3),
        pl.BlockSpec((S5_QUARTERS, S5_QS, S5_QC), z3), pl.BlockSpec((S5_QUARTERS, S5_QS, S5_QC), z3),
        pl.BlockSpec((1, S5_INNER), z2), pl.BlockSpec((S5_INNER, S5_INNER), z2), pl.BlockSpec((1, S5_INNER), z2),
    ]


def _s5_chain(proj, ucol, x0r, x0i, sp, b, l, m):
    chunk = 8 * m
    nc = l // chunk
    consts = [sp[k] for k in ('lam_r', 'lam_i', 'bre', 'bim', 'cre', 'cim', 'dd', 'gw', 'gb')]
    return pl.pallas_call(
        functools.partial(_s5_chain_kernel, m=m),
        out_shape=(jax.ShapeDtypeStruct((S5_QUARTERS, b * l, S5_QC), f32),
                   jax.ShapeDtypeStruct((b, 1, S5_LANES), f32),
                   jax.ShapeDtypeStruct((b, 1, S5_LANES), f32)),
        grid=(b, nc),
        in_specs=[pl.BlockSpec((chunk, S5_QC), functools.partial(lambda i, j, c: (i * nc + j, ucol + c), c=c))
                  for c in range(S5_QUARTERS)] + [
                  pl.BlockSpec((1, 1, S5_LANES), lambda i, j: (i, 0, 0)),
                  pl.BlockSpec((1, 1, S5_LANES), lambda i, j: (i, 0, 0))] + _s5_const_specs(2),
        out_specs=(pl.BlockSpec((S5_QUARTERS, chunk, S5_QC), lambda i, j: (0, i * nc + j, 0)),
                   pl.BlockSpec((1, 1, S5_LANES), lambda i, j: (i, 0, 0)),
                   pl.BlockSpec((1, 1, S5_LANES), lambda i, j: (i, 0, 0))),
        scratch_shapes=[pltpu.VMEM((m, S5_LANES), f32), pltpu.VMEM((m, S5_LANES), f32),
                        pltpu.VMEM((chunk, S5_LANES), f32), pltpu.VMEM((chunk, S5_LANES), f32),
                        pltpu.VMEM((chunk, S5_INNER), f32), pltpu.VMEM((chunk, S5_INNER), f32),
                        pltpu.VMEM((8, S5_LANES), f32), pltpu.VMEM((8, S5_LANES), f32)],
        compiler_params=_cparams("arbitrary", "arbitrary"),
        name="s5_chain",
    )(proj, proj, proj, proj, x0r, x0i, *consts)


def _s5_batch(proj, ucol, x0r, x0i, sp, n_sub, m):
    t = n_sub * m
    consts = [sp[k] for k in ('lam_r', 'lam_i', 'bre', 'bim', 'cre', 'cim', 'dd', 'gw', 'gb')]
    return pl.pallas_call(
        functools.partial(_s5_batch_kernel, n_sub=n_sub, m=m),
        out_shape=(jax.ShapeDtypeStruct((S5_QUARTERS, t, S5_QC), f32),
                   jax.ShapeDtypeStruct((n_sub, S5_LANES), f32),
                   jax.ShapeDtypeStruct((n_sub, S5_LANES), f32)),
        grid=(1,),
        in_specs=[pl.BlockSpec((t, S5_QC), functools.partial(lambda i, c: (0, ucol + c), c=c))
                  for c in range(S5_QUARTERS)] + [
                  pl.BlockSpec((n_sub, S5_LANES), lambda i: (0, 0)),
                  pl.BlockSpec((n_sub, S5_LANES), lambda i: (0, 0))] + _s5_const_specs(1),
        out_specs=(pl.BlockSpec((S5_QUARTERS, t, S5_QC), lambda i: (0, 0, 0)),
                   pl.BlockSpec((n_sub, S5_LANES), lambda i: (0, 0)),
                   pl.BlockSpec((n_sub, S5_LANES), lambda i: (0, 0))),
        scratch_shapes=[pltpu.VMEM((t, S5_LANES), f32), pltpu.VMEM((t, S5_LANES), f32),
                        pltpu.VMEM((t, S5_INNER), f32), pltpu.VMEM((t, S5_INNER), f32)],
        compiler_params=_cparams("arbitrary"),
        name="s5_batch",
    )(proj, proj, proj, proj, x0r, x0i, *consts)


def _tri(n, strict=False, upper=False):
    r = lax.broadcasted_iota(jnp.int32, (n, n), 0)
    c = lax.broadcasted_iota(jnp.int32, (n, n), 1)
    if upper:
        r, c = c, r
    return jnp.where((c < r) if strict else (c <= r), 1.0, 0.0).astype(bf16)


def _dot_exact_lhs(a_exact_bf16, b):
    b0, b1, b2 = _split3(b)
    return _dot(a_exact_bf16, b0) + _dot(a_exact_bf16, b1) + _dot(a_exact_bf16, b2)


def _log_sigmoid(x):
    return jnp.minimum(x, 0.0) - jnp.log1p(jnp.exp(-jnp.abs(x)))


def _fcum_kernel(fr_ref, bias_ref, logf_ref, f_ref, ft_ref, carry):
    @pl.when(pl.program_id(1) == 0)
    def _():
        carry[...] = jnp.zeros_like(carry)

    n = fr_ref.shape[0]
    logf = _log_sigmoid(fr_ref[...] + bias_ref[...])
    logf_ref[...] = logf
    f = _dot_exact_lhs(_tri(n), logf) + carry[0:1, :]
    f_ref[...] = f
    ft_ref[0] = f.T[:8, :]
    carry[0:1, :] = f[n - 1:n, :]


def _fcum(proj, col, bias, b, l, chunk):
    nc = l // chunk
    return pl.pallas_call(
        _fcum_kernel,
        out_shape=(jax.ShapeDtypeStruct((b * l, LANES), f32),
                   jax.ShapeDtypeStruct((b * l, LANES), f32),
                   jax.ShapeDtypeStruct((b, 8, l), f32)),
        grid=(b, nc),
        in_specs=[pl.BlockSpec((chunk, LANES), lambda i, j: (i * nc + j, col)),
                  pl.BlockSpec((1, LANES), lambda i, j: (0, 0))],
        out_specs=(pl.BlockSpec((chunk, LANES), lambda i, j: (i * nc + j, 0)),
                   pl.BlockSpec((chunk, LANES), lambda i, j: (i * nc + j, 0)),
                   pl.BlockSpec((1, 8, chunk), lambda i, j: (i, 0, j))),
        scratch_shapes=[pltpu.VMEM((8, LANES), f32)],
        compiler_params=_cparams("arbitrary", "arbitrary"),
        name="fcum",
    )(proj, bias)


N_FPARTS = 3
FOX_STRIP = 256


def _fox_select_mats():
    rows = jnp.arange(N_FPARTS * LANES)[None, :, None]
    cols = jnp.arange(FOX_HEAD_DIM)[None, None, :]
    head = jnp.arange(FOX_HEADS)[:, None, None]
    hit = (rows % LANES == head)
    sq = jnp.where(hit & (cols == rows // LANES), 1.0, 0.0)
    sk = jnp.where(hit & (cols == N_FPARTS + rows // LANES), -1.0, 0.0)
    return sq.astype(bf16), sk.astype(bf16)


def _fox_prep_kernel(q_ref, k_ref, v_ref, f_ref, sq_ref, sk_ref, qa_ref, ka_ref, vt_ref, *, scale):
    tm = q_ref.shape[0]
    hd = FOX_HEAD_DIM
    fcat = jnp.concatenate(_split3(f_ref[...]), axis=1)
    lane = lax.broadcasted_iota(jnp.int32, (tm, hd), 1)
    ones_q = jnp.where((lane >= N_FPARTS) & (lane < 2 * N_FPARTS), 1.0, 0.0)
    ones_k = jnp.where(lane < N_FPARTS, 1.0, 0.0)
    vt = v_ref[...].T
    for h in range(FOX_HEADS):
        cols = slice(h * hd, (h + 1) * hd)
        eq = _dot(fcat, sq_ref[h]) + ones_q
        ek = _dot(fcat, sk_ref[h]) + ones_k
        qa_ref[h] = jnp.concatenate([q_ref[:, cols] * scale, eq], axis=1).astype(bf16)
        ka_ref[h] = jnp.concatenate([k_ref[:, cols], ek], axis=1).astype(bf16)
        vt_ref[h] = vt[h * hd:(h + 1) * hd, :].astype(bf16)


def _fox_prep(proj, f, t, tm):
    sq, sk = _fox_select_mats()
    sel_spec = pl.BlockSpec((FOX_HEADS, N_FPARTS * LANES, FOX_HEAD_DIM), lambda i: (0, 0, 0))
    return pl.pallas_call(
        functools.partial(_fox_prep_kernel, scale=FOX_HEAD_DIM ** -0.5),
        out_shape=(jax.ShapeDtypeStruct((FOX_HEADS, t, LANES), bf16),
                   jax.ShapeDtypeStruct((FOX_HEADS, t, LANES), bf16),
                   jax.ShapeDtypeStruct((FOX_HEADS, FOX_HEAD_DIM, t), bf16)),
        grid=(t // tm,),
        in_specs=[pl.BlockSpec((tm, FOX_INNER), lambda i: (i, 0)),
                  pl.BlockSpec((tm, FOX_INNER), lambda i: (i, 1)),
                  pl.BlockSpec((tm, FOX_INNER), lambda i: (i, 2)),
                  pl.BlockSpec((tm, LANES), lambda i: (i, 0)),
                  sel_spec, sel_spec],
        out_specs=(pl.BlockSpec((FOX_HEADS, tm, LANES), lambda i: (0, i, 0)),
                   pl.BlockSpec((FOX_HEADS, tm, LANES), lambda i: (0, i, 0)),
                   pl.BlockSpec((FOX_HEADS, FOX_HEAD_DIM, tm), lambda i: (0, 0, i))),
        compiler_params=_cparams("parallel"),
        name="fox_prep",
    )(proj, proj, proj, f, sq, sk)


def _fox_kernel(qt_ref, kt_ref, qa_ref, ka_ref, vt_ref, o_ref, m_scr, l_scr, acc_scr):
    step = pl.program_id(2)
    qi = qt_ref[step]
    ki = kt_ref[step]
    tq = qa_ref.shape[1]
    tk = ka_ref.shape[1]
    q_first = qi * tq
    k_first = ki * tk

    @pl.when(ki == 0)
    def _():
        m_scr[...] = jnp.full_like(m_scr, -jnp.inf)
        l_scr[...] = jnp.zeros_like(l_scr)
        acc_scr[...] = jnp.zeros_like(acc_scr)

    def update(masked):
        strip = min(FOX_STRIP, tq)
        chains = [(hh, q0) for hh in range(2) for q0 in range(0, tq, strip)]
        nks = [min(tk, q0 + strip) if (masked and tq == tk) else tk for _, q0 in chains]
        sts = [_dot_nt(ka_ref[hh, :nk, :], qa_ref[hh, q0:q0 + strip, :])
               for (hh, q0), nk in zip(chains, nks)]
        ps, alphas = [], []
        for (hh, q0), nk, st in zip(chains, nks, sts):
            qs = slice(q0, q0 + strip)
            if masked:
                kpos = lax.broadcasted_iota(jnp.int32, (nk, strip), 0) + k_first
                qpos = lax.broadcasted_iota(jnp.int32, (nk, strip), 1) + (q0 + q_first)
                st = jnp.where(kpos <= qpos, st, -jnp.inf)
            m_old = m_scr[hh, :, qs]
            m_new = jnp.maximum(m_old, jnp.max(st, axis=0, keepdims=True))
            alpha = jnp.exp(m_old - m_new)
            p = jnp.exp(st - m_new)
            l_scr[hh, :, qs] = alpha * l_scr[hh, :, qs] + jnp.sum(p, axis=0, keepdims=True)
            m_scr[hh, :, qs] = m_new
            ps.append(p.astype(bf16))
            alphas.append(alpha)
        pvs = [_dot(vt_ref[hh, :, :nk], p) for (hh, _), nk, p in zip(chains, nks, ps)]
        for (hh, q0), alpha, pv in zip(chains, alphas, pvs):
            qs = slice(q0, q0 + strip)
            acc_scr[hh, :, qs] = alpha * acc_scr[hh, :, qs] + pv

    full = k_first + (tk - 1) <= q_first

    @pl.when(full)
    def _():
        update(False)

    @pl.when(jnp.logical_not(full))
    def _():
        update(True)

    @pl.when(k_first + tk >= q_first + tq)
    def _():
        ot = jnp.concatenate([acc_scr[0] / l_scr[0], acc_scr[1] / l_scr[1]], axis=0)
        o_ref[...] = ot.T


def _fox_prompt(proj, f, b, l, tq, tk):
    nq = l // tq
    nk = l // tk
    n_hp = FOX_HEADS // 2
    qa, ka, vt = _fox_prep(proj, f, b * l, max(tq, tk))
    pairs = [(qi, ki) for qi in range(nq) for ki in range(((qi + 1) * tq - 1) // tk + 1)]
    qtab = jnp.array([pr[0] for pr in pairs], jnp.int32)
    ktab = jnp.array([pr[1] for pr in pairs], jnp.int32)
    grid_spec = pltpu.PrefetchScalarGridSpec(
        num_scalar_prefetch=2,
        grid=(b, n_hp, len(pairs)),
        in_specs=[
            pl.BlockSpec((2, tq, LANES), lambda i, h, s, qt, kt: (h, i * nq + qt[s], 0)),
            pl.BlockSpec((2, tk, LANES), lambda i, h, s, qt, kt: (h, i * nk + kt[s], 0)),
            pl.BlockSpec((2, FOX_HEAD_DIM, tk), lambda i, h, s, qt, kt: (h, 0, i * nk + kt[s])),
        ],
        out_specs=pl.BlockSpec((tq, LANES), lambda i, h, s, qt, kt: (i * nq + qt[s], h)),
        scratch_shapes=[pltpu.VMEM((2, 1, tq), f32), pltpu.VMEM((2, 1, tq), f32),
                        pltpu.VMEM((2, FOX_HEAD_DIM, tq), f32)],
    )
    return pl.pallas_call(
        _fox_kernel,
        out_shape=jax.ShapeDtypeStruct((b * l, FOX_INNER), f32),
        grid_spec=grid_spec,
        compiler_params=_cparams("parallel", "parallel", "arbitrary"),
        name="fox_prompt",
    )(qtab, ktab, qa, ka, vt)


PAGES_PER_STEP = 16


def _dot_exact_rhs(a, b_exact_bf16):
    a0, a1, a2 = _split3(a)
    return _dot(a0, b_exact_bf16) + _dot(a1, b_exact_bf16) + _dot(a2, b_exact_bf16)


def _dot_nt(a, b):
    return lax.dot_general(a, b, (((1,), (1,)), ((), ())), preferred_element_type=f32)


def _fox_paged_kernel(pt_ref, *refs, n_pages, pps, scale):
    k_refs, v_refs = refs[:pps], refs[pps:2 * pps]
    (lf_ref, q_ref, kn_ref, vn_ref, fr_ref, bias_ref, o_ref, logf_ref,
     s_scr, qbd_scr, acc_scr, psum_scr, m_scr, car_scr, fq_scr) = refs[2 * pps:]
    seq = pl.program_id(0)
    ph = pl.program_id(1)
    c = pl.program_id(2)
    last_c = pl.num_programs(2) - 1
    nq = q_ref.shape[0]
    hd = FOX_HEAD_DIM
    pg = LANES
    row = lax.broadcasted_iota(jnp.int32, (pg, pg), 0)
    lane = lax.broadcasted_iota(jnp.int32, (pg, pg), 1)
    upper = jnp.where(row <= lane, 1.0, 0.0).astype(bf16)

    def pad_rows(x, fill=0.0):
        return jnp.concatenate([x, jnp.full((pg - x.shape[0], x.shape[1]), fill, x.dtype)], axis=0)

    def per_head_rows(x):
        rep = jnp.broadcast_to(x[:, None, :], (FOX_HEADS, nq, x.shape[1])).reshape(FOX_HEADS * nq, x.shape[1])
        return pad_rows(rep)

    def key_sums(logf_rows):
        cum = _dot_exact_rhs(logf_rows, upper) + car_scr[...]
        car_scr[...] = jnp.broadcast_to(cum[:, pg - 1:pg], (pg, pg))
        return cum

    @pl.when((ph == 0) & (c == 0))
    def _():
        q_rep = jnp.broadcast_to((q_ref[...] * scale)[None], (FOX_HEADS, nq, FOX_INNER)).reshape(FOX_HEADS * nq, FOX_INNER)
        r5 = lax.broadcasted_iota(jnp.int32, (FOX_HEADS * nq, FOX_INNER), 0)
        l5 = lax.broadcasted_iota(jnp.int32, (FOX_HEADS * nq, FOX_INNER), 1)
        qbd_scr[...] = pad_rows(jnp.where(l5 // hd == r5 // nq, q_rep, 0.0)).astype(bf16)
        m_scr[...] = jnp.full_like(m_scr, -jnp.inf)
        car_scr[...] = jnp.zeros_like(car_scr)

    @pl.when(ph == 0)
    def _():
        qk = [_dot(qbd_scr[...], k_refs[i][0].reshape(FOX_INNER, pg).astype(bf16)) for i in range(pps)]
        pages = [pt_ref[seq * n_pages + c * pps + i] for i in range(pps)]
        local = [per_head_rows(_dot_exact_rhs(lf_ref[pages[i]], upper)) for i in range(pps)]
        car = car_scr[...]
        m = m_scr[...]
        for i in range(pps):
            cum = local[i] + car
            car = jnp.broadcast_to(cum[:, pg - 1:pg], (pg, pg))
            s = qk[i] - cum
            s_scr[c * pps + i] = s
            m = jnp.maximum(m, s)
        car_scr[...] = car
        m_scr[...] = m

    @pl.when((ph == 0) & (c == last_c))
    def _():
        logf_new = _log_sigmoid(fr_ref[...] + bias_ref[...])
        logf_ref[...] = logf_new
        cum = key_sums(per_head_rows(pad_rows(logf_new).T[:FOX_HEADS, :]))
        fq = jnp.sum(jnp.where(lane == row % nq, cum, 0.0), axis=1, keepdims=True)
        fq_scr[...] = jnp.broadcast_to(fq, (pg, pg))
        s = _dot(qbd_scr[...], pad_rows(kn_ref[...]).T.astype(bf16)) - cum
        s = jnp.where(lane <= row % nq, s, -jnp.inf)
        s_scr[n_pages] = s
        m = jnp.max(jnp.maximum(m_scr[...], s), axis=1, keepdims=True)
        m_scr[...] = jnp.broadcast_to(m, (pg, pg))

    @pl.when((ph == 1) & (c == 0))
    def _():
        psum_scr[...] = jnp.zeros_like(psum_scr)
        acc_scr[...] = jnp.zeros_like(acc_scr)

    def probs(s):
        fq = fq_scr[...]
        return jnp.exp((s + fq) - (m_scr[...] + fq))

    def accumulate(s, v_t_bf16):
        p = probs(s)
        psum_scr[...] += p
        acc_scr[...] += _dot_nt(v_t_bf16, p.astype(bf16))

    @pl.when(ph == 1)
    def _():
        ps = [probs(s_scr[c * pps + i]) for i in range(pps)]
        pvs = [_dot_nt(v_refs[i][0].reshape(FOX_INNER, pg).astype(bf16), ps[i].astype(bf16)) for i in range(pps)]
        psum = psum_scr[...]
        acc = acc_scr[...]
        for i in range(pps):
            psum = psum + ps[i]
            acc = acc + pvs[i]
        psum_scr[...] = psum
        acc_scr[...] = acc

    @pl.when((ph == 1) & (c == last_c))
    def _():
        accumulate(s_scr[n_pages], pad_rows(vn_ref[...]).T.astype(bf16))
        p0, p1, p2 = _split3(psum_scr[...])
        ones = jnp.ones((8, pg), bf16)
        l_row = (_dot_nt(ones, p0) + _dot_nt(ones, p1) + _dot_nt(ones, p2))[0:1, :]
        o_t = (acc_scr[...] / l_row).T
        o_ref[...] = jnp.concatenate([o_t[h * nq:(h + 1) * nq, h * hd:(h + 1) * hd] for h in range(FOX_HEADS)],
                                     axis=1)


def _fox_paged(proj, fcol, bias, cache_k, cache_v, cache_logf, page_table):
    n_seq, n_pages = page_table.shape
    nq = proj.shape[0] // n_seq
    pps = PAGES_PER_STEP
    n_chunks = n_pages // pps
    ck = jnp.transpose(cache_k, (0, 2, 3, 1))
    cv = jnp.transpose(cache_v, (0, 2, 3, 1))
    clf = jnp.transpose(cache_logf, (0, 2, 1))
    page = ck.shape[3]
    assert page == LANES and nq == 8

    def k_map(i):
        return lambda s, ph, c, pt: (pt[s * n_pages + jnp.where(ph == 0, c, n_chunks - 1) * pps + i], 0, 0, 0)

    def v_map(i):
        return lambda s, ph, c, pt: (pt[s * n_pages + jnp.where(ph == 0, 0, c) * pps + i], 0, 0, 0)

    blk = (1, FOX_HEADS, FOX_HEAD_DIM, page)
    in_specs = ([pl.BlockSpec(blk, k_map(i)) for i in range(pps)]
                + [pl.BlockSpec(blk, v_map(i)) for i in range(pps)]
                + [pl.BlockSpec(clf.shape, lambda s, ph, c, pt: (0, 0, 0), pipeline_mode=pl.Buffered(1)),
                   pl.BlockSpec((nq, FOX_INNER), lambda s, ph, c, pt: (s, 0)),
                   pl.BlockSpec((nq, FOX_INNER), lambda s, ph, c, pt: (s, 1)),
                   pl.BlockSpec((nq, FOX_INNER), lambda s, ph, c, pt: (s, 2)),
                   pl.BlockSpec((nq, LANES), lambda s, ph, c, pt: (s, fcol)),
                   pl.BlockSpec((1, LANES), lambda s, ph, c, pt: (0, 0))])
    grid_spec = pltpu.PrefetchScalarGridSpec(
        num_scalar_prefetch=1,
        grid=(n_seq, 2, n_chunks),
        in_specs=in_specs,
        out_specs=(pl.BlockSpec((nq, FOX_INNER), lambda s, ph, c, pt: (s, 0)),
                   pl.BlockSpec((nq, LANES), lambda s, ph, c, pt: (s, 0))),
        scratch_shapes=[pltpu.VMEM((n_pages + 1, page, page), f32),
                        pltpu.VMEM((page, FOX_INNER), bf16),
                        pltpu.VMEM((FOX_INNER, page), f32),
                        pltpu.VMEM((page, page), f32), pltpu.VMEM((page, page), f32),
                        pltpu.VMEM((page, page), f32), pltpu.VMEM((page, page), f32)],
    )
    return pl.pallas_call(
        functools.partial(_fox_paged_kernel, n_pages=n_pages, pps=pps, scale=FOX_HEAD_DIM ** -0.5),
        out_shape=(jax.ShapeDtypeStruct((n_seq * nq, FOX_INNER), f32),
                   jax.ShapeDtypeStruct((n_seq * nq, LANES), f32)),
        grid_spec=grid_spec,
        compiler_params=_cparams("arbitrary", "arbitrary", "arbitrary"),
        name="fox_paged",
    )(page_table.reshape(-1), *([ck] * pps), *([cv] * pps), clf, proj, proj, proj, proj, bias)


CONV_TAIL = 8
COL0_QKV = 0
COL0_XS = 3 * GDN_INNER
COL0_ZS = COL0_XS + SSD_INNER
COL0_ZG = COL0_ZS + SSD_INNER
COL0_BC = COL0_ZG + GDN_INNER
COL0_SM = COL0_BC + 2 * SSD_GROUPS * SSD_STATE
assert COL0_SM + LANES == IN0_PAD


def _softplus(x):
    return jnp.maximum(x, 0.0) + jnp.log1p(jnp.exp(-jnp.abs(x)))


def _silu(x):
    return x * jax.nn.sigmoid(x)


def _conv_silu(ubuf, u_refs, tail_ref, cw_ref, cb_ref, first, rows, lpad):
    @pl.when(first)
    def _():
        ubuf[0:CONV_TAIL, :] = tail_ref[0]

    @pl.when(jnp.logical_not(first))
    def _():
        ubuf[0:CONV_TAIL, :] = ubuf[rows:rows + CONV_TAIL, :]

    col = 0
    for u_ref in u_refs:
        ubuf[CONV_TAIL:CONV_TAIL + rows, col:col + u_ref.shape[1]] = u_ref[...]
        col += u_ref.shape[1]
    if lpad > rows:
        ubuf[CONV_TAIL + rows:, :] = jnp.zeros((lpad - rows, ubuf.shape[1]), f32)
    acc = cb_ref[...]
    for j in range(CONV_K):
        off = CONV_TAIL - (CONV_K - 1) + j
        acc = acc + cw_ref[j:j + 1, :] * ubuf[off:off + lpad, :]
    return _silu(acc)


def _pad_rows(x, lpad):
    rows = x.shape[0]
    if lpad == rows:
        return x
    return jnp.concatenate([x, jnp.zeros((lpad - rows, x.shape[1]), x.dtype)], axis=0)


def _head_scalars(sm_ref, bias_ref, coef_ref, rows, lpad):
    raw = _pad_rows(sm_ref[...], lpad)
    valid = lax.broadcasted_iota(jnp.int32, (lpad, LANES), 0) < rows
    sp = jnp.where(valid, _softplus(raw + bias_ref[...]), 0.0)
    a = sp * coef_ref[...]
    cum = _dot_exact_lhs(_tri(lpad), a)
    return raw, valid, sp, cum


def _ssd_kernel(ux_ref, ubc_ref, z_ref, sm_ref, tail_ref, h0_ref, cw_ref, cb_ref, bias_ref, coef_ref, dd_ref, nw_ref,
                y_ref, hout_ref, ubuf, *, rows, lpad):
    j = pl.program_id(1)
    n, p = SSD_STATE, SSD_HEAD_DIM
    xbc = _conv_silu(ubuf, (ux_ref, ubc_ref), tail_ref, cw_ref, cb_ref, j == 0, rows, lpad)

    @pl.when(j == 0)
    def _():
        hout_ref[...] = h0_ref[...]

    _, _, dt, acum = _head_scalars(sm_ref, bias_ref, coef_ref, rows, lpad)
    acum_t = acum.T
    dt_t = dt.T
    xs = xbc[:, :SSD_INNER]
    xs_t = xs.T
    r = lax.broadcasted_iota(jnp.int32, (lpad, lpad), 0)
    c = lax.broadcasted_iota(jnp.int32, (lpad, lpad), 1)
    causal = c <= r
    heads = range(SSD_HEADS)
    group_of = [h // (SSD_HEADS // SSD_GROUPS) for h in heads]
    bms = [xbc[:, SSD_INNER + g * n:SSD_INNER + (g + 1) * n] for g in range(SSD_GROUPS)]
    cms = [xbc[:, SSD_INNER + SSD_GROUPS * n + g * n:SSD_INNER + SSD_GROUPS * n + (g + 1) * n]
           for g in range(SSD_GROUPS)]
    cbs = [_dot_nt(cms[g].astype(bf16), bms[g].astype(bf16)) for g in range(SSD_GROUPS)]
    a_cols = [acum[:, h:h + 1] for h in heads]
    a_lasts = [acum[lpad - 1:lpad, h:h + 1] for h in heads]
    x_hs = [xs[:, h * p:(h + 1) * p] for h in heads]
    hsts = [hout_ref[0, h] for h in heads]
    scores = [(cbs[group_of[h]] * jnp.exp(jnp.where(causal, a_cols[h] - acum_t[h:h + 1, :], -jnp.inf))).astype(bf16)
              for h in heads]
    xdts = [(x_hs[h] * dt[:, h:h + 1]).astype(bf16) for h in heads]
    c_exps = [(cms[group_of[h]] * jnp.exp(a_cols[h])).astype(bf16) for h in heads]
    xdt_ts = [(xs_t[h * p:(h + 1) * p, :] * dt_t[h:h + 1, :]).astype(bf16) for h in heads]
    b_ends = [(bms[group_of[h]] * jnp.exp(a_lasts[h] - a_cols[h])).astype(bf16) for h in heads]
    y_diags = [_dot(scores[h], xdts[h]) for h in heads]
    y_offs = [_dot_nt(c_exps[h], hsts[h].astype(bf16)) for h in heads]
    upds = [_dot(xdt_ts[h], b_ends[h]) for h in heads]
    for h in heads:
        hout_ref[0, h] = hsts[h] * jnp.exp(a_lasts[h]) + upds[h]
    ys = [y_diags[h] + y_offs[h] + dd_ref[:, h * p:(h + 1) * p] * x_hs[h] for h in heads]
    y = jnp.concatenate(ys, axis=1)[:rows]
    gated = y * _silu(z_ref[...])
    ms = jnp.mean(gated * gated, axis=-1, keepdims=True)
    y_ref[...] = gated * lax.rsqrt(ms + EPS) * nw_ref[...]


def _ssd(proj, tail, h0, prm, b, l, rows, lpad):
    nc = l // rows
    bc_w = 2 * SSD_GROUPS * SSD_STATE
    cw, cb, bias, coef, dd, nw = prm
    c2 = lambda i, j: (0, 0)
    return pl.pallas_call(
        functools.partial(_ssd_kernel, rows=rows, lpad=lpad),
        out_shape=(jax.ShapeDtypeStruct((b * l, SSD_INNER), f32),
                   jax.ShapeDtypeStruct((b, SSD_HEADS, SSD_HEAD_DIM, SSD_STATE), f32)),
        grid=(b, nc),
        in_specs=[pl.BlockSpec((rows, SSD_INNER), lambda i, j: (i * nc + j, COL0_XS // SSD_INNER)),
                  pl.BlockSpec((rows, bc_w), lambda i, j: (i * nc + j, COL0_BC // bc_w)),
                  pl.BlockSpec((rows, SSD_INNER), lambda i, j: (i * nc + j, COL0_ZS // SSD_INNER)),
                  pl.BlockSpec((rows, LANES), lambda i, j: (i * nc + j, COL0_SM // LANES)),
                  pl.BlockSpec((1, CONV_TAIL, SSD_CONV_CH), lambda i, j: (i, 0, 0)),
                  pl.BlockSpec((1, SSD_HEADS, SSD_HEAD_DIM, SSD_STATE), lambda i, j: (i, 0, 0, 0)),
                  pl.BlockSpec((CONV_K, SSD_CONV_CH), c2), pl.BlockSpec((1, SSD_CONV_CH), c2),
                  pl.BlockSpec((1, LANES), c2), pl.BlockSpec((1, LANES), c2),
                  pl.BlockSpec((1, SSD_INNER), c2), pl.BlockSpec((1, SSD_INNER), c2)],
        out_specs=(pl.BlockSpec((rows, SSD_INNER), lambda i, j: (i * nc + j, 0)),
                   pl.BlockSpec((1, SSD_HEADS, SSD_HEAD_DIM, SSD_STATE), lambda i, j: (i, 0, 0, 0))),
        scratch_shapes=[pltpu.VMEM((CONV_TAIL + lpad, SSD_CONV_CH), f32)],
        compiler_params=_cparams("arbitrary", "arbitrary"),
        name="ssd",
    )(proj, proj, proj, proj, tail, h0, cw, cb, bias, coef, dd, nw)


def _split2(a):
    a_hi = a.astype(bf16)
    return a_hi, (a - a_hi.astype(f32)).astype(bf16)


def _dot3_split(a_split, b_split):
    a_hi, a_lo = a_split
    b_hi, b_lo = b_split
    return _dot(a_hi, b_hi) + _dot(a_lo, b_hi) + _dot(a_hi, b_lo)


INV_BASE = 16


def _l2n(x):
    return x * lax.rsqrt(jnp.sum(x * x, axis=-1, keepdims=True) + EPS)


GDN_A_LANE = SSD_HEADS
GDN_B_LANE = SSD_HEADS + GDN_HEADS


def _unit_lower_inverse_multi(ms, n):
    r = lax.broadcasted_iota(jnp.int32, (n, n), 0)
    c = lax.broadcasted_iota(jnp.int32, (n, n), 1)
    nb = min(INV_BASE, n)
    diag_blk = (r // nb) == (c // nb)
    eye = jnp.where(r == c, 1.0, 0.0)
    pws = [jnp.where(diag_blk, m, 0.0) for m in ms]
    invs = [eye - d for d in pws]
    pw_ss = [_split2(pw) for pw in pws]
    size = 2
    while size < nb:
        pws = [_dot3_split(pw_s, pw_s) for pw_s in pw_ss]
        pw_ss = [_split2(pw) for pw in pws]
        invs = [inv + _dot3_split(_split2(inv), pw_s) for inv, pw_s in zip(invs, pw_ss)]
        size *= 2
    s = nb
    while s < n:
        lower_left = ((r // (2 * s)) == (c // (2 * s))) & ((r // s) % 2 == 1) & ((c // s) % 2 == 0)
        inv_ss = [_split2(inv) for inv in invs]
        tmps = [_dot3_split(inv_s, _split2(jnp.where(lower_left, m, 0.0))) for inv_s, m in zip(inv_ss, ms)]
        invs = [inv - _dot3_split(_split2(tmp), inv_s) for inv, tmp, inv_s in zip(invs, tmps, inv_ss)]
        s *= 2
    return invs


def _gdn_staged_kernel(u_ref, z_ref, sm_ref, tail_ref, s0_ref, cw_ref, cb_ref, bias_ref, coef_ref, nw_ref,
                       o_ref, sout_ref, ubuf, *, rows, lpad):
    j = pl.program_id(1)
    dk = GDN_HEAD_DIM
    heads = range(GDN_HEADS)
    qkv = _conv_silu(ubuf, (u_ref,), tail_ref, cw_ref, cb_ref, j == 0, rows, lpad)

    @pl.when(j == 0)
    def _():
        sout_ref[...] = s0_ref[...]

    raw, valid, _, gcum = _head_scalars(sm_ref, bias_ref, coef_ref, rows, lpad)
    beta_all = jnp.where(valid, jax.nn.sigmoid(raw), 0.0)
    gcum_t = gcum.T
    r = lax.broadcasted_iota(jnp.int32, (lpad, lpad), 0)
    c = lax.broadcasted_iota(jnp.int32, (lpad, lpad), 1)
    qs = [_l2n(qkv[:, h * dk:(h + 1) * dk]) * dk ** -0.5 for h in heads]
    ks = [_l2n(qkv[:, GDN_INNER + h * dk:GDN_INNER + (h + 1) * dk]) for h in heads]
    vs = [qkv[:, 2 * GDN_INNER + h * dk:2 * GDN_INNER + (h + 1) * dk] for h in heads]
    g_cols = [gcum[:, GDN_A_LANE + h:GDN_A_LANE + h + 1] for h in heads]
    g_lasts = [gcum[lpad - 1:lpad, GDN_A_LANE + h:GDN_A_LANE + h + 1] for h in heads]
    betas = [beta_all[:, GDN_B_LANE + h:GDN_B_LANE + h + 1] for h in heads]
    decays = [jnp.exp(jnp.where(c <= r, g_cols[h] - gcum_t[GDN_A_LANE + h:GDN_A_LANE + h + 1, :], -jnp.inf))
              for h in heads]
    kbs = [ks[h] * betas[h] for h in heads]
    k_bs = [k.astype(bf16) for k in ks]
    kks = [_dot_nt(kbs[h].astype(bf16), k_bs[h]) for h in heads]
    attns = [_dot_nt(qs[h].astype(bf16), k_bs[h]) * decays[h] for h in heads]
    a_invs = _unit_lower_inverse_multi([jnp.where(c < r, kks[h] * decays[h], 0.0) for h in heads], lpad)
    a_his = [a.astype(bf16) for a in a_invs]
    a_los = [(a - ah.astype(f32)).astype(bf16) for a, ah in zip(a_invs, a_his)]
    vb_bs = [(vs[h] * betas[h]).astype(bf16) for h in heads]
    kbe_bs = [(kbs[h] * jnp.exp(g_cols[h])).astype(bf16) for h in heads]
    us = [_dot(a_his[h], vb_bs[h]) + _dot(a_los[h], vb_bs[h]) for h in heads]
    ws = [_dot(a_his[h], kbe_bs[h]) + _dot(a_los[h], kbe_bs[h]) for h in heads]
    sts = [sout_ref[0, h] for h in heads]
    st_bs = [st.astype(bf16) for st in sts]
    v_news = [(us[h] - _dot(ws[h].astype(bf16), st_bs[h])).astype(bf16) for h in heads]
    os_ = [_dot((qs[h] * jnp.exp(g_cols[h])).astype(bf16), st_bs[h]) + _dot(attns[h].astype(bf16), v_news[h])
           for h in heads]
    ke_ts = [(ks[h] * jnp.exp(g_lasts[h] - g_cols[h])).T.astype(bf16) for h in heads]
    for h in heads:
        sout_ref[0, h] = sts[h] * jnp.exp(g_lasts[h]) + _dot(ke_ts[h], v_news[h])
    outs = []
    for h in heads:
        o = os_[h][:rows]
        ms = jnp.mean(o * o, axis=-1, keepdims=True)
        outs.append(o * lax.rsqrt(ms + EPS) * nw_ref[...] * _silu(z_ref[:, h * dk:(h + 1) * dk]))
    o_ref[...] = jnp.concatenate(outs, axis=1)


def _gdn(proj, tail, s0, prm, b, l, rows, lpad):
    nc = l // rows
    cw, cb, bias, coef, nw = prm
    c2 = lambda i, j: (0, 0)
    width = 3 * GDN_INNER
    return pl.pallas_call(
        functools.partial(_gdn_staged_kernel, rows=rows, lpad=lpad),
        out_shape=(jax.ShapeDtypeStruct((b * l, GDN_INNER), f32),
                   jax.ShapeDtypeStruct((b, GDN_HEADS, GDN_HEAD_DIM, GDN_HEAD_DIM), f32)),
        grid=(b, nc),
        in_specs=[pl.BlockSpec((rows, width), lambda i, j: (i * nc + j, COL0_QKV // width)),
                  pl.BlockSpec((rows, GDN_INNER), lambda i, j: (i * nc + j, COL0_ZG // GDN_INNER)),
                  pl.BlockSpec((rows, LANES), lambda i, j: (i * nc + j, COL0_SM // LANES)),
                  pl.BlockSpec((1, CONV_TAIL, width), lambda i, j: (i, 0, 0)),
                  pl.BlockSpec((1, GDN_HEADS, GDN_HEAD_DIM, GDN_HEAD_DIM), lambda i, j: (i, 0, 0, 0)),
                  pl.BlockSpec((CONV_K, width), c2), pl.BlockSpec((1, width), c2),
                  pl.BlockSpec((1, LANES), c2), pl.BlockSpec((1, LANES), c2),
                  pl.BlockSpec((1, GDN_HEAD_DIM), c2)],
        out_specs=(pl.BlockSpec((rows, GDN_INNER), lambda i, j: (i * nc + j, 0)),
                   pl.BlockSpec((1, GDN_HEADS, GDN_HEAD_DIM, GDN_HEAD_DIM), lambda i, j: (i, 0, 0, 0))),
        scratch_shapes=[pltpu.VMEM((CONV_TAIL + lpad, width), f32)],
        compiler_params=_cparams("arbitrary", "arbitrary"),
        name="gdn",
    )(proj, proj, proj, tail, s0, cw, cb, bias, coef, nw)


def _layer0_params(p):
    def lanes(v, off):
        return jnp.zeros((1, LANES), f32).at[0, off:off + v.shape[0]].set(v.astype(f32))

    cw = p['conv_w'].astype(f32)
    cb = p['conv_b'].astype(f32).reshape(1, CONV_CH)
    bias = lanes(p['ssd_dt_bias'], 0) + lanes(p['gdn_dt_bias'], GDN_A_LANE)
    coef = lanes(-jnp.exp(p['ssd_A_log'].astype(f32)), 0) + lanes(-jnp.exp(p['gdn_A_log'].astype(f32)), GDN_A_LANE)
    dd = jnp.repeat(p['ssd_D'].astype(f32), SSD_HEAD_DIM).reshape(1, SSD_INNER)
    return {'ssd': (cw[:, :SSD_CONV_CH], cb[:, :SSD_CONV_CH], bias, coef, dd,
                    p['ssd_norm'].astype(f32).reshape(1, SSD_INNER)),
            'gdn': (cw[:, SSD_CONV_CH:], cb[:, SSD_CONV_CH:], bias, coef,
                    p['gdn_norm'].astype(f32).reshape(1, GDN_HEAD_DIM))}


def _ada_kernel(c_ref, w_ref, b_ref, o_ref):
    o_ref[...] = _dot(_silu(c_ref[...]).astype(bf16), w_ref[...].astype(bf16)) + b_ref[...]


def _ada(c, w_ada, b_ada, tn):
    rows, d = c.shape
    n = w_ada.shape[1]
    return pl.pallas_call(
        _ada_kernel,
        out_shape=jax.ShapeDtypeStruct((rows, n), f32),
        grid=(n // tn,),
        in_specs=[pl.BlockSpec((rows, d), lambda j: (0, 0)),
                  pl.BlockSpec((d, tn), lambda j: (0, j)),
                  pl.BlockSpec((1, tn), lambda j: (0, j))],
        out_specs=pl.BlockSpec((rows, tn), lambda j: (0, j)),
        compiler_params=_cparams("parallel"),
        name="ada",
    )(c, w_ada, b_ada.reshape(1, n))


def _mods(mod, l, tm):
    parts = jnp.split(mod, 6, axis=-1)
    if l % tm == 0:
        return [p[:, None, :] for p in parts]
    b = mod.shape[0]
    return [jnp.repeat(p, l, axis=0).reshape((b * l) // tm, tm, D_MODEL) for p in parts]


def _run_trunk(x, mod0, mod1, conv_buf, ssd_h0, gdn_s0, past, s5_re0, s5_im0, p, tm):
    b, l, d = x.shape
    t = b * l
    x2 = x.reshape(t, d)

    sh1, sc1, g1, sh2, sc2, g2 = _mods(mod0, l, tm)
    proj0 = _inproj(x2, p['norm_mix0'], sc1, sh1, p['w_in0'], tm, TN_INPROJ)
    tail = jnp.pad(conv_buf.astype(f32), ((0, 0), (CONV_TAIL - (CONV_K - 1), 0), (0, 0)))
    if l % SSD_ROWS == 0 and l % GDN_ROWS == 0:
        ssd_rows, ssd_lpad, gdn_rows, gdn_lpad = SSD_ROWS, SSD_ROWS, GDN_ROWS, GDN_ROWS
    else:
        ssd_rows, ssd_lpad, gdn_rows, gdn_lpad = l, PAD_ROWS, l, PAD_ROWS
    y_ssd, ssd_new = _ssd(proj0, tail[..., :SSD_CONV_CH], ssd_h0.astype(f32), p['l0']['ssd'], b, l, ssd_rows, ssd_lpad)
    o_gdn, gdn_new = _gdn(proj0, tail[..., SSD_CONV_CH:], gdn_s0.astype(f32), p['l0']['gdn'], b, l, gdn_rows, gdn_lpad)
    last = proj0.reshape(b, l, IN0_PAD)[:, l - (CONV_K - 1):]
    conv_new = jnp.concatenate([last[..., COL0_XS:COL0_XS + SSD_INNER], last[..., COL0_BC:COL0_SM],
                                last[..., COL0_QKV:COL0_QKV + 3 * GDN_INNER]], axis=-1)
    w_out0 = p['w_out0']
    x2 = _outproj([y_ssd, o_gdn], [w_out0[:SSD_INNER], w_out0[SSD_INNER:]], x2, g1, tm)
    x2 = _ffn(x2, p['norm_ffn0'], sc2, sh2, g2, p['ffn_w1'], p['ffn_w3'], p['ffn_w2'], min(tm, TM_FFN), TF_FFN)

    sh1, sc1, g1, sh2, sc2, g2 = _mods(mod1, l, tm)
    proj2 = _inproj(x2, p['norm_mix1'], sc1, sh1, p['w_in1'], tm, TN_INPROJ)
    proj = proj2.reshape(b, l, IN1_PAD)
    k_new = proj[..., FOX_INNER:2 * FOX_INNER].reshape(b, l, FOX_HEADS, FOX_HEAD_DIM)
    v_new = proj[..., 2 * FOX_INNER:3 * FOX_INNER].reshape(b, l, FOX_HEADS, FOX_HEAD_DIM)
    sp = p['s5']
    ucol = (3 * FOX_INNER) // S5_QC
    fcol = (3 * FOX_INNER + S5_INNER) // LANES
    if past is None:
        logf_pad, fcs, _ = _fcum(proj2, fcol, p['fox_f_bias'], b, l, FCUM_ROWS)
        logf = logf_pad[:, :FOX_HEADS].reshape(b, l, FOX_HEADS)
        o_fox = _fox_prompt(proj2, fcs, b, l, FOX_TQ, FOX_TK)
        o_s5, s5_re, s5_im = _s5_chain(proj2, ucol, s5_re0.reshape(b, 1, S5_LANES),
                                       s5_im0.reshape(b, 1, S5_LANES), sp, b, l, S5_STEPS)
    else:
        o_fox, logf_pad = _fox_paged(proj2, fcol, p['fox_f_bias'], *past)
        logf = logf_pad[:, :FOX_HEADS].reshape(b, l, FOX_HEADS)
        o_s5, s5_re, s5_im = _s5_batch(proj2, ucol, s5_re0.reshape(b, S5_LANES),
                                       s5_im0.reshape(b, S5_LANES), sp, b, l)
    s5_re = s5_re.reshape(b, S5_GROUPS, S5_STATE)
    s5_im = s5_im.reshape(b, S5_GROUPS, S5_STATE)
    w_out1 = p['w_out1']
    x2 = _outproj([o_fox, o_s5], [w_out1[:FOX_INNER], w_out1[FOX_INNER:].reshape(S5_QUARTERS, S5_QC, d)], x2, g1, tm)
    halves = 2 if t % (2 * tm) == 0 else 1
    y_out = _moe_grouped(x2, p['norm_ffn1'], sc2, sh2, g2, p['router_w'], p['moe_w1'], p['moe_w3'], p['moe_w2'],
                         p['norm_final'], tm, halves, TF_MOE, MOE_CAP).reshape(b, l, d)
    return y_out, conv_new, ssd_new, gdn_new, k_new, v_new, logf, s5_re, s5_im


def kernel(x_prompt, x_sample, state_conv0, state_ssd, state_gdn, cache_k, cache_v, cache_logf,
           state_s5_re, state_s5_im, page_table, c_prompt, c_sample,
           ada0_w, ada0_b, norm_mix0, w_in0, conv0_w, conv0_b, ssd_dt_bias, ssd_A_log, ssd_D,
           ssd_norm, gdn_dt_bias, gdn_A_log, gdn_norm, w_out0, norm_ffn0, ffn_w1, ffn_w3, ffn_w2,
           ada1_w, ada1_b, norm_mix1, w_in1, fox_f_bias, s5_A_re, s5_A_im, s5_log_step,
           s5_B_re, s5_B_im, s5_C_re, s5_C_im, s5_D, glu_w, glu_b, w_out1, norm_ffn1,
           router_w, moe_w1, moe_w3, moe_w2, norm_final):
    d = D_MODEL
    c0 = CONV_CH
    w_in0p = jnp.concatenate([
        w_in0[:, SSD_CONV_CH:c0],
        w_in0[:, :SSD_INNER],
        w_in0[:, c0:c0 + SSD_INNER],
        w_in0[:, c0 + SSD_INNER + SSD_HEADS:c0 + SSD_INNER + SSD_HEADS + GDN_INNER],
        w_in0[:, SSD_INNER:SSD_CONV_CH],
        w_in0[:, c0 + SSD_INNER:c0 + SSD_INNER + SSD_HEADS],
        w_in0[:, c0 + SSD_INNER + SSD_HEADS + GDN_INNER:],
        jnp.zeros((d, IN0_PAD - w_in0.shape[1]), f32)], axis=1).astype(bf16)
    f0 = 3 * FOX_INNER
    w_in1p = jnp.concatenate([
        w_in1[:, :f0],
        w_in1[:, f0 + FOX_HEADS:],
        w_in1[:, f0:f0 + FOX_HEADS],
        jnp.zeros((d, IN1_PAD - w_in1.shape[1]), f32)], axis=1).astype(bf16)
    router_wp = jnp.concatenate([router_w.astype(f32), jnp.zeros((d, LANES - N_EXPERTS), f32)], axis=1)
    p = {
        'ada0_w': ada0_w, 'ada0_b': ada0_b, 'norm_mix0': norm_mix0.reshape(1, d), 'w_in0': w_in0p,
        'conv0_w': conv0_w, 'conv0_b': conv0_b, 'ssd_dt_bias': ssd_dt_bias, 'ssd_A_log': ssd_A_log,
        'ssd_D': ssd_D, 'ssd_norm': ssd_norm, 'gdn_dt_bias': gdn_dt_bias, 'gdn_A_log': gdn_A_log,
        'gdn_norm': gdn_norm, 'w_out0': w_out0.astype(bf16), 'norm_ffn0': norm_ffn0.reshape(1, d),
        'ffn_w1': ffn_w1.astype(bf16), 'ffn_w3': ffn_w3.astype(bf16), 'ffn_w2': ffn_w2.astype(bf16),
        'ada1_w': ada1_w, 'ada1_b': ada1_b, 'norm_mix1': norm_mix1.reshape(1, d), 'w_in1': w_in1p,
        'fox_f_bias': jnp.concatenate([fox_f_bias.astype(f32), jnp.zeros((LANES - FOX_HEADS,), f32)]).reshape(1, LANES),
        's5_A_re': s5_A_re, 's5_A_im': s5_A_im, 's5_log_step': s5_log_step,
        's5_B_re': s5_B_re, 's5_B_im': s5_B_im, 's5_C_re': s5_C_re, 's5_C_im': s5_C_im, 's5_D': s5_D,
        'glu_w': glu_w, 'glu_b': glu_b, 'w_out1': w_out1.astype(bf16), 'norm_ffn1': norm_ffn1.reshape(1, d),
        'router_w': router_wp, 'moe_w1': moe_w1.astype(bf16), 'moe_w3': moe_w3.astype(bf16),
        'moe_w2': moe_w2.astype(bf16), 'norm_final': norm_final.astype(f32).reshape(1, d),
    }
    p['s5'] = _s5_params(p)
    p['l0'] = _layer0_params({'conv_w': conv0_w, 'conv_b': conv0_b, 'ssd_dt_bias': ssd_dt_bias, 'ssd_A_log': ssd_A_log,
                              'ssd_D': ssd_D, 'ssd_norm': ssd_norm, 'gdn_dt_bias': gdn_dt_bias,
                              'gdn_A_log': gdn_A_log, 'gdn_norm': gdn_norm})
    bp = x_prompt.shape[0]
    c_all = jnp.concatenate([c_prompt, c_sample], axis=0).astype(f32)
    mod0 = _ada(c_all, ada0_w, ada0_b, TN_ADA)
    mod1 = _ada(c_all, ada1_w, ada1_b, TN_ADA)
    outs_p = _run_trunk(
        x_prompt, mod0[:bp], mod1[:bp],
        jnp.zeros((bp, CONV_K - 1, CONV_CH), x_prompt.dtype),
        jnp.zeros((bp, SSD_HEADS, SSD_HEAD_DIM, SSD_STATE), f32),
        jnp.zeros((bp, GDN_HEADS, GDN_HEAD_DIM, GDN_HEAD_DIM), f32),
        None,
        jnp.zeros((bp, S5_GROUPS, S5_STATE), f32),
        jnp.zeros((bp, S5_GROUPS, S5_STATE), f32),
        p, TM_PROMPT)
    outs_s = _run_trunk(
        x_sample, mod0[bp:], mod1[bp:], state_conv0, state_ssd, state_gdn, (cache_k, cache_v, cache_logf, page_table),
        state_s5_re, state_s5_im, p, TM_SAMPLE)
    return (outs_p[0], outs_s[0]) + tuple(outs_p[1:]) + tuple(outs_s[1:])
```

```python
import functools
import math

import jax
import jax.numpy as jnp
from jax import lax
from jax.experimental import pallas as pl
from jax.experimental.pallas import tpu as pltpu

f32 = jnp.float32
bf16 = jnp.bfloat16

D_MODEL = 1024
CONV_K = 4
CHUNK = 64
SSD_HEADS = 8
SSD_HEAD_DIM = 64
SSD_INNER = 512
SSD_GROUPS = 2
SSD_STATE = 64
GDN_HEADS = 4
GDN_HEAD_DIM = 128
GDN_INNER = 512
SSD_CONV_CH = 768
CONV_CH = 2304
FOX_HEADS = 8
FOX_HEAD_DIM = 64
FOX_INNER = 512
Q_BLOCK = 128
S5_CH = 16
S5_GROUPS = 32
S5_INNER = 512
S5_STATE = 64
D_FF = 2816
N_EXPERTS = 8
D_FF_EXPERT = 3584
EPS = 1e-6

LANES = 128
IN0_PAD = 3456
IN1_PAD = 2304
VMEM_LIMIT = 56 * 1024 * 1024

TM_PROMPT = 1024
TM_SAMPLE = 256
TN_INPROJ = 1152
TM_FFN = 512
TF_FFN = 2816
TF_MOE = 512
MOE_CAP = 320
TN_ADA = 1536
SSD_ROWS = 256
GDN_ROWS = 128
PAD_ROWS = 128
FOX_TQ = 1024
FOX_TK = 1024
FCUM_ROWS = 512
S5_STEPS = 64


def _cparams(*sem):
    return pltpu.CompilerParams(dimension_semantics=sem, vmem_limit_bytes=VMEM_LIMIT)


def _modnorm(x, g, scale, shift):
    ms = jnp.mean(x * x, axis=-1, keepdims=True)
    y = x * lax.rsqrt(ms + EPS) * g
    return y * (1.0 + scale) + shift


def _split3(a):
    a0 = a.astype(bf16)
    r = a - a0.astype(f32)
    a1 = r.astype(bf16)
    a2 = (r - a1.astype(f32)).astype(bf16)
    return a0, a1, a2


def _dot(a, b):
    return jnp.dot(a, b, preferred_element_type=f32)


def _dot_f32(a, b):
    a0, a1, a2 = _split3(a)
    b0, b1, b2 = _split3(b)
    return (_dot(a0, b0) + _dot(a0, b1) + _dot(a1, b0)
            + _dot(a1, b1) + _dot(a0, b2) + _dot(a2, b0))


def _col_blocks(w, width):
    *lead, d, f = w.shape
    n = len(lead)
    return w.reshape(*lead, d, f // width, width).transpose(*range(n), n + 1, n, n + 2)


def _mod_spec(mod, n_tiles):
    n_mod, rows, d = mod.shape
    per = n_tiles // n_mod
    return pl.BlockSpec((1, rows, d), lambda i, *_: (i // per, 0, 0))


def _inproj_kernel(x_ref, g_ref, sc_ref, sh_ref, w_ref, o_ref, h_scr):
    @pl.when(pl.program_id(1) == 0)
    def _():
        h_scr[...] = _modnorm(x_ref[...], g_ref[...], sc_ref[0], sh_ref[0]).astype(bf16)

    o_ref[...] = _dot(h_scr[...], w_ref[...])


def _inproj(x, g, scale, shift, w, tm, tn):
    t, d = x.shape
    n = w.shape[1]
    n_tiles = t // tm
    return pl.pallas_call(
        _inproj_kernel,
        out_shape=jax.ShapeDtypeStruct((t, n), f32),
        grid=(n_tiles, n // tn),
        in_specs=[
            pl.BlockSpec((tm, d), lambda i, j: (i, 0)),
            pl.BlockSpec((1, d), lambda i, j: (0, 0)),
            _mod_spec(scale, n_tiles),
            _mod_spec(shift, n_tiles),
            pl.BlockSpec((d, tn), lambda i, j: (0, j)),
        ],
        out_specs=pl.BlockSpec((tm, tn), lambda i, j: (i, j)),
        scratch_shapes=[pltpu.VMEM((tm, d), bf16)],
        compiler_params=_cparams("parallel", "arbitrary"),
        name="inproj",
    )(x, g, scale, shift, w)


def _outproj_kernel(*refs, n_parts):
    mix_refs = refs[:n_parts]
    w_refs = refs[n_parts:2 * n_parts]
    x_ref, gate_ref, o_ref = refs[2 * n_parts:]
    acc = None
    for m_ref, w_ref in zip(mix_refs, w_refs):
        if len(m_ref.shape) == 3:
            terms = [_dot(m_ref[q].astype(bf16), w_ref[q]) for q in range(m_ref.shape[0])]
        else:
            terms = [_dot(m_ref[...].astype(bf16), w_ref[...])]
        for term in terms:
            acc = term if acc is None else acc + term
    o_ref[...] = x_ref[...] + gate_ref[0] * acc


def _outproj(parts, weights, x, gate, tm):
    t, d = x.shape
    n_tiles = t // tm
    in_specs = []
    for a in parts:
        if a.ndim == 3:
            in_specs.append(pl.BlockSpec((a.shape[0], tm, a.shape[2]), lambda i: (0, i, 0)))
        else:
            in_specs.append(pl.BlockSpec((tm, a.shape[1]), lambda i: (i, 0)))
    for w in weights:
        in_specs.append(pl.BlockSpec(w.shape, (lambda i: (0, 0, 0)) if w.ndim == 3 else (lambda i: (0, 0))))
    in_specs += [pl.BlockSpec((tm, d), lambda i: (i, 0)), _mod_spec(gate, n_tiles)]
    return pl.pallas_call(
        functools.partial(_outproj_kernel, n_parts=len(parts)),
        out_shape=jax.ShapeDtypeStruct((t, d), f32),
        grid=(n_tiles,),
        in_specs=in_specs,
        out_specs=pl.BlockSpec((tm, d), lambda i: (i, 0)),
        compiler_params=_cparams("parallel"),
        name="outproj",
    )(*parts, *weights, x, gate)


def _ffn_kernel(x_ref, g_ref, sc_ref, sh_ref, gate_ref, w1_ref, w3_ref, w2_ref, o_ref, h_scr):
    j = pl.program_id(1)

    @pl.when(j == 0)
    def _():
        h_scr[...] = _modnorm(x_ref[...], g_ref[...], sc_ref[0], sh_ref[0]).astype(bf16)
        o_ref[...] = jnp.zeros_like(o_ref)

    h = h_scr[...]
    a = _dot(h, w1_ref[0])
    b = _dot(h, w3_ref[0])
    act = (a * jax.nn.sigmoid(a)) * b
    o_ref[...] += _dot(act.astype(bf16), w2_ref[...])

    @pl.when(j == pl.num_programs(1) - 1)
    def _():
        o_ref[...] = x_ref[...] + gate_ref[0] * o_ref[...]


def _ffn(x, g, scale, shift, gate, w1, w3, w2, tm, tf):
    t, d = x.shape
    f = w1.shape[1]
    n_tiles = t // tm
    return pl.pallas_call(
        _ffn_kernel,
        out_shape=jax.ShapeDtypeStruct((t, d), f32),
        grid=(n_tiles, f // tf),
        in_specs=[
            pl.BlockSpec((tm, d), lambda i, j: (i, 0)),
            pl.BlockSpec((1, d), lambda i, j: (0, 0)),
            _mod_spec(scale, n_tiles),
            _mod_spec(shift, n_tiles),
            _mod_spec(gate, n_tiles),
            pl.BlockSpec((1, d, tf), lambda i, j: (j, 0, 0)),
            pl.BlockSpec((1, d, tf), lambda i, j: (j, 0, 0)),
            pl.BlockSpec((tf, d), lambda i, j: (j, 0)),
        ],
        out_specs=pl.BlockSpec((tm, d), lambda i, j: (i, 0)),
        scratch_shapes=[pltpu.VMEM((tm, d), bf16)],
        compiler_params=_cparams("parallel", "arbitrary"),
        name="ffn",
    )(x, g, scale, shift, gate, _col_blocks(w1, tf), _col_blocks(w3, tf), w2)


def _router_kernel(x_ref, g_ref, sc_ref, sh_ref, rw_ref, h_ref, comb_ref, rank_ref, rankt_ref, cnt_ref):
    tm = x_ref.shape[0]
    h = _modnorm(x_ref[...], g_ref[...], sc_ref[0], sh_ref[0])
    h_ref[...] = h.astype(bf16)
    logits = _dot_f32(h, rw_ref[...])
    lane = lax.broadcasted_iota(jnp.int32, (tm, LANES), 1)
    lg = jnp.where(lane < N_EXPERTS, logits, -jnp.inf)
    m1 = jnp.max(lg, axis=1, keepdims=True)
    i1 = jnp.min(jnp.where(lg == m1, lane, LANES), axis=1, keepdims=True)
    lg2 = jnp.where(lane == i1, -jnp.inf, lg)
    m2 = jnp.max(lg2, axis=1, keepdims=True)
    i2 = jnp.min(jnp.where(lg2 == m2, lane, LANES), axis=1, keepdims=True)
    e2 = jnp.exp(m2 - m1)
    den = 1.0 + e2
    comb_ref[...] = jnp.where(lane == i1, 1.0 / den, jnp.where(lane == i2, e2 / den, 0.0))
    sel = (lane == i1) | (lane == i2)
    ind = jnp.where(sel, 1.0, 0.0)
    row = lax.broadcasted_iota(jnp.int32, (tm, tm), 0)
    col = lax.broadcasted_iota(jnp.int32, (tm, tm), 1)
    below = jnp.where(col < row, 1.0, 0.0).astype(bf16)
    rank = jnp.where(sel, _dot(below, ind.astype(bf16)), -1.0)
    rank_ref[...] = rank
    rankt_ref[0] = rank.T[:N_EXPERTS, :]
    cnt_ref[0] = jnp.broadcast_to(jnp.sum(ind, axis=0, keepdims=True), (8, LANES))


def _router(x, g, scale, shift, rw, tm):
    t, d = x.shape
    n_tiles = t // tm
    return pl.pallas_call(
        _router_kernel,
        out_shape=(
            jax.ShapeDtypeStruct((t, d), bf16),
            jax.ShapeDtypeStruct((t, LANES), f32),
            jax.ShapeDtypeStruct((t, LANES), f32),
            jax.ShapeDtypeStruct((n_tiles, N_EXPERTS, tm), f32),
            jax.ShapeDtypeStruct((n_tiles, 8, LANES), f32),
        ),
        grid=(n_tiles,),
        in_specs=[
            pl.BlockSpec((tm, d), lambda i: (i, 0)),
            pl.BlockSpec((1, d), lambda i: (0, 0)),
            _mod_spec(scale, n_tiles),
            _mod_spec(shift, n_tiles),
            pl.BlockSpec((d, LANES), lambda i: (0, 0)),
        ],
        out_specs=(
            pl.BlockSpec((tm, d), lambda i: (i, 0)),
            pl.BlockSpec((tm, LANES), lambda i: (i, 0)),
            pl.BlockSpec((tm, LANES), lambda i: (i, 0)),
            pl.BlockSpec((1, N_EXPERTS, tm), lambda i: (i, 0, 0)),
            pl.BlockSpec((1, 8, LANES), lambda i: (i, 0, 0)),
        ),
        compiler_params=_cparams("parallel"),
        name="router",
    )(x, g, scale, shift, rw)


def _moe_tile_kernel(cnt_ref, *refs, cap, halves):
    h_ref, gate_ref, comb_ref, rank_ref = refs[:4]
    rankt_refs = refs[4:4 + halves]
    w1_ref, w3_ref, w2_ref, o_ref, hc_scr, y_scr = refs[4 + halves:]
    i = pl.program_id(0)
    e = pl.program_id(1)
    fc = pl.program_id(2)
    last_fc = pl.num_programs(2) - 1
    th = h_ref.shape[0] // halves
    group = halves * cap
    n_batches = (cnt_ref[(i * halves) * N_EXPERTS + e] + cap - 1) // cap
    for hf in range(1, halves):
        n_batches = jnp.maximum(n_batches, (cnt_ref[(i * halves + hf) * N_EXPERTS + e] + cap - 1) // cap)

    @pl.when((e == 0) & (fc == 0))
    def _():
        o_ref[...] = jnp.zeros_like(o_ref)

    @pl.when(fc == 0)
    def _():
        def compact(b, carry):
            rowid = (lax.broadcasted_iota(jnp.int32, (cap, th), 0) + b * cap).astype(f32)
            for hf in range(halves):
                r0 = pl.multiple_of(b * group + hf * cap, 8)
                onehot = jnp.where(rankt_refs[hf][0] == rowid, 1.0, 0.0).astype(bf16)
                hc_scr[pl.ds(r0, cap), :] = _dot(onehot, h_ref[hf * th:(hf + 1) * th, :]).astype(bf16)
            g0 = pl.multiple_of(b * group, 8)
            y_scr[pl.ds(g0, group), :] = jnp.zeros((group, y_scr.shape[1]), f32)
            return carry
        lax.fori_loop(0, n_batches, compact, 0)

    def expert(b, carry):
        g0 = pl.multiple_of(b * group, 8)
        hc = hc_scr[pl.ds(g0, group), :]
        a = _dot(hc, w1_ref[0])
        g = _dot(hc, w3_ref[0])
        act = (a * jax.nn.sigmoid(a)) * g
        y_scr[pl.ds(g0, group), :] += _dot(act.astype(bf16), w2_ref[0])
        return carry
    lax.fori_loop(0, n_batches, expert, 0)

    @pl.when(fc == last_fc)
    def _():
        lane = lax.broadcasted_iota(jnp.int32, (th, LANES), 1)
        for hf in range(halves):
            rows = slice(hf * th, (hf + 1) * th)
            rank_e = jnp.sum(jnp.where(lane == e, rank_ref[rows, :], 0.0), axis=1, keepdims=True)
            gate_e = jnp.sum(jnp.where(lane == e, comb_ref[rows, :], 0.0), axis=1, keepdims=True)
            gate_mod = gate_ref[0] if gate_ref.shape[1] == 1 else gate_ref[0, rows, :]

            def expand(b, carry):
                r0 = pl.multiple_of(b * group + hf * cap, 8)
                colid = (lax.broadcasted_iota(jnp.int32, (th, cap), 1) + b * cap).astype(f32)
                onehot = jnp.where(rank_e == colid, 1.0, 0.0).astype(bf16)
                o_ref[rows, :] += (gate_mod * gate_e) * _dot(onehot, y_scr[pl.ds(r0, cap), :].astype(bf16))
                return carry
            lax.fori_loop(0, n_batches, expand, 0)


def _resnorm_kernel(x_ref, dl_ref, nf_ref, o_ref):
    y = x_ref[...] + dl_ref[...]
    o_ref[...] = y * lax.rsqrt(jnp.mean(y * y, axis=-1, keepdims=True) + EPS) * nf_ref[...]


def _moe_grouped(x, g, scale, shift, gate, rw, w1, w3, w2, norm_final, th, halves, tf, cap):
    t, d = x.shape
    f = w1.shape[2]
    tm = th * halves
    n_tiles = t // tm
    cap = min(cap, th)
    n_cap = -(-th // cap)
    h, comb, rank, rankt, cnt = _router(x, g, scale, shift, rw, th)
    counts = cnt[:, 0, :N_EXPERTS].astype(jnp.int32).reshape(-1)
    rankt = rankt.reshape((t // th) * N_EXPERTS, 1, th)
    once = pl.Buffered(1)

    def rankt_map(hf):
        return lambda i, e, c, cnt: ((i * halves + hf) * N_EXPERTS + e, 0, 0)

    grid_spec = pltpu.PrefetchScalarGridSpec(
        num_scalar_prefetch=1,
        grid=(n_tiles, N_EXPERTS, f // tf),
        in_specs=[pl.BlockSpec((tm, d), lambda i, e, c, cnt: (i, 0), pipeline_mode=once),
                  _mod_spec(gate, n_tiles),
                  pl.BlockSpec((tm, LANES), lambda i, e, c, cnt: (i, 0), pipeline_mode=once),
                  pl.BlockSpec((tm, LANES), lambda i, e, c, cnt: (i, 0), pipeline_mode=once)]
                 + [pl.BlockSpec((1, 1, th), rankt_map(hf)) for hf in range(halves)]
                 + [pl.BlockSpec((1, d, tf), lambda i, e, c, cnt: (e, 0, c)),
                    pl.BlockSpec((1, d, tf), lambda i, e, c, cnt: (e, 0, c)),
                    pl.BlockSpec((1, tf, d), lambda i, e, c, cnt: (e, c, 0))],
        out_specs=pl.BlockSpec((tm, d), lambda i, e, c, cnt: (i, 0)),
        scratch_shapes=[pltpu.VMEM((n_cap * halves * cap, d), bf16),
                        pltpu.VMEM((n_cap * halves * cap, d), f32)],
    )
    delta = pl.pallas_call(
        functools.partial(_moe_tile_kernel, cap=cap, halves=halves),
        out_shape=jax.ShapeDtypeStruct((t, d), f32),
        grid_spec=grid_spec,
        compiler_params=_cparams("parallel", "arbitrary", "arbitrary"),
        name="moe_tile",
    )(counts, h, gate, comb, rank, *([rankt] * halves), w1, w3, w2)
    tr = min(t, 1024)
    return pl.pallas_call(
        _resnorm_kernel,
        out_shape=jax.ShapeDtypeStruct((t, d), f32),
        grid=(t // tr,),
        in_specs=[pl.BlockSpec((tr, d), lambda i: (i, 0)), pl.BlockSpec((tr, d), lambda i: (i, 0)),
                  pl.BlockSpec((1, d), lambda i: (0, 0))],
        out_specs=pl.BlockSpec((tr, d), lambda i: (i, 0)),
        compiler_params=_cparams("parallel"),
        name="resnorm",
    )(x, delta, norm_final)


S5_LANES = S5_GROUPS * S5_STATE
S5_QUARTERS = 4
S5_QS = S5_LANES // S5_QUARTERS
S5_QC = S5_INNER // S5_QUARTERS


def _s5_load_bu(u_refs, bre_ref, bim_ref, up_scr, bu_r, bu_i, n_sub, m):
    for c in range(S5_QUARTERS):
        for k in range(m):
            up_scr[k * n_sub:(k + 1) * n_sub, c * S5_QC:(c + 1) * S5_QC] = u_refs[c][pl.ds(k, n_sub, stride=m), :]
        uc = up_scr[:, c * S5_QC:(c + 1) * S5_QC].astype(bf16)
        bu_r[:, c * S5_QS:(c + 1) * S5_QS] = _dot(uc, bre_ref[c])
        bu_i[:, c * S5_QS:(c + 1) * S5_QS] = _dot(uc, bim_ref[c])


def _s5_local_scan(lam_r_ref, lam_i_ref, init_r_ref, init_i_ref, bu_r, bu_i, n_sub, m):
    width = 8192 // n_sub
    for c in range(S5_LANES // width):
        cols = slice(c * width, (c + 1) * width)
        lr = jnp.broadcast_to(lam_r_ref[:, cols], (n_sub, width))
        li = jnp.broadcast_to(lam_i_ref[:, cols], (n_sub, width))
        if init_r_ref is None:
            x0 = (jnp.zeros((n_sub, width), f32), jnp.zeros((n_sub, width), f32))
        else:
            x0 = (init_r_ref[:, cols], init_i_ref[:, cols])

        def step(k, carry):
            xr, xi = carry
            rows = pl.ds(pl.multiple_of(k * n_sub, 8), n_sub)
            nr = lr * xr - li * xi + bu_r[rows, cols]
            ni = lr * xi + li * xr + bu_i[rows, cols]
            bu_r[rows, cols] = nr
            bu_i[rows, cols] = ni
            return nr, ni
        lax.fori_loop(0, m, step, x0)


def _gelu_tanh(x):
    return 0.5 * x * (1.0 + jnp.tanh(math.sqrt(2.0 / math.pi) * (x + 0.044715 * (x * x * x))))


def _s5_output(xb_r, xb_i, up_scr, cre_ref, cim_ref, dd_ref, gw_ref, gb_ref, o_ref, op_scr, n_sub, m):
    ys = []
    for c in range(S5_QUARTERS):
        cols = slice(c * S5_QS, (c + 1) * S5_QS)
        ys.append(_dot(xb_r[:, cols].astype(bf16), cre_ref[c]) - _dot(xb_i[:, cols].astype(bf16), cim_ref[c]))
    y = jnp.concatenate(ys, axis=1) + dd_ref[...] * up_scr[...]
    hs = _gelu_tanh(y)
    op_scr[...] = hs * jax.nn.sigmoid(_dot(hs.astype(bf16), gw_ref[...]) + gb_ref[...])
    for c in range(S5_QUARTERS):
        for k in range(m):
            o_ref[c, pl.ds(k, n_sub, stride=m), :] = op_scr[k * n_sub:(k + 1) * n_sub, c * S5_QC:(c + 1) * S5_QC]


def _s5_chain_kernel(u0_ref, u1_ref, u2_ref, u3_ref, x0r_ref, x0i_ref, lam_r_ref, lam_i_ref, bre_ref, bim_ref, cre_ref, cim_ref,
                     dd_ref, gw_ref, gb_ref, o_ref, sr_ref, si_ref,
                     pow_r, pow_i, bu_r, bu_i, up_scr, op_scr, en_r, en_i, *, m):
    n_sub = 8
    j = pl.program_id(1)

    @pl.when((pl.program_id(0) == 0) & (j == 0))
    def _():
        def pstep(k, carry):
            pr, pi = carry
            pow_r[pl.ds(k, 1), :] = pr
            pow_i[pl.ds(k, 1), :] = pi
            lr = lam_r_ref[...]
            li = lam_i_ref[...]
            return lr * pr - li * pi, lr * pi + li * pr
        lax.fori_loop(0, m, pstep, (lam_r_ref[...], lam_i_ref[...]))

    @pl.when(j == 0)
    def _():
        sr_ref[0] = x0r_ref[0]
        si_ref[0] = x0i_ref[0]

    _s5_load_bu((u0_ref, u1_ref, u2_ref, u3_ref), bre_ref, bim_ref, up_scr, bu_r, bu_i, n_sub, m)
    _s5_local_scan(lam_r_ref, lam_i_ref, None, None, bu_r, bu_i, n_sub, m)

    pm_r = pow_r[m - 1:m, :]
    pm_i = pow_i[m - 1:m, :]
    e_r = sr_ref[0]
    e_i = si_ref[0]
    for s in range(n_sub):
        en_r[s:s + 1, :] = e_r
        en_i[s:s + 1, :] = e_i
        row = (m - 1) * n_sub + s
        e_r, e_i = (bu_r[row:row + 1, :] + pm_r * e_r - pm_i * e_i,
                    bu_i[row:row + 1, :] + pm_r * e_i + pm_i * e_r)
    sr_ref[0] = e_r
    si_ref[0] = e_i

    def fix(k, carry):
        rows = pl.ds(pl.multiple_of(k * n_sub, 8), n_sub)
        pr = pow_r[pl.ds(k, 1), :]
        pi = pow_i[pl.ds(k, 1), :]
        er = en_r[...]
        ei = en_i[...]
        bu_r[rows, :] = bu_r[rows, :] + pr * er - pi * ei
        bu_i[rows, :] = bu_i[rows, :] + pr * ei + pi * er
        return carry
    lax.fori_loop(0, m, fix, 0)

    _s5_output(bu_r, bu_i, up_scr, cre_ref, cim_ref, dd_ref, gw_ref, gb_ref, o_ref, op_scr, n_sub, m)


def _s5_batch_kernel(u0_ref, u1_ref, u2_ref, u3_ref, x0r_ref, x0i_ref, lam_r_ref, lam_i_ref, bre_ref, bim_ref, cre_ref, cim_ref,
                     dd_ref, gw_ref, gb_ref, o_ref, sr_ref, si_ref,
                     bu_r, bu_i, up_scr, op_scr, *, n_sub, m):
    _s5_load_bu((u0_ref, u1_ref, u2_ref, u3_ref), bre_ref, bim_ref, up_scr, bu_r, bu_i, n_sub, m)
    _s5_local_scan(lam_r_ref, lam_i_ref, x0r_ref, x0i_ref, bu_r, bu_i, n_sub, m)
    last = slice((m - 1) * n_sub, m * n_sub)
    sr_ref[...] = bu_r[last, :]
    si_ref[...] = bu_i[last, :]
    _s5_output(bu_r, bu_i, up_scr, cre_ref, cim_ref, dd_ref, gw_ref, gb_ref, o_ref, op_scr, n_sub, m)


def _s5_params(p):
    ar = p['s5_A_re'].astype(f32)
    ai = p['s5_A_im'].astype(f32)
    step = jnp.exp(p['s5_log_step'].astype(f32))[:, None]
    mag = jnp.exp(ar * step)
    lb_re = mag * jnp.cos(ai * step)
    lb_im = mag * jnp.sin(ai * step)
    den = ar * ar + ai * ai
    nr = lb_re - 1.0
    cr = (nr * ar + lb_im * ai) / den
    ci = (lb_im * ar - nr * ai) / den
    bb_re = cr[..., None] * p['s5_B_re'] - ci[..., None] * p['s5_B_im']
    bb_im = cr[..., None] * p['s5_B_im'] + ci[..., None] * p['s5_B_re']
    gq = S5_GROUPS // S5_QUARTERS
    eye = jnp.eye(gq, dtype=f32)

    def bq(bb):
        t = bb.reshape(S5_QUARTERS, gq, S5_STATE, S5_CH)
        return jnp.einsum('cgph,gk->cghkp', t, eye).reshape(S5_QUARTERS, S5_QC, S5_QS).astype(bf16)

    def cq(cc):
        t = cc.astype(f32).reshape(S5_QUARTERS, gq, S5_CH, S5_STATE)
        return jnp.einsum('cghp,gk->cgpkh', t, eye).reshape(S5_QUARTERS, S5_QS, S5_QC).astype(bf16)

    return dict(lam_r=lb_re.reshape(1, S5_LANES), lam_i=lb_im.reshape(1, S5_LANES),
                bre=bq(bb_re), bim=bq(bb_im), cre=cq(p['s5_C_re']), cim=cq(p['s5_C_im']),
                dd=p['s5_D'].astype(f32).reshape(1, S5_INNER),
                gw=p['glu_w'].astype(bf16), gb=p['glu_b'].astype(f32).reshape(1, S5_INNER))


def _s5_const_specs(nd):
    z2 = (lambda *_: (0, 0))
    z3 = (lambda *_: (0, 0, 0))
    return [
        pl.BlockSpec((1, S5_LANES), z2), pl.BlockSpec((1, S5_LANES), z2),
        pl.BlockSpec((S5_QUARTERS, S5_QC, S5_QS), z3), pl.BlockSpec((S5_QUARTERS, S5_QC, S5_QS), z3),
        pl.BlockSpec((S5_QUARTERS, S5_QS, S5_QC), z3), pl.BlockSpec((S5_QUARTERS, S5_QS, S5_QC), z3),
        pl.BlockSpec((1, S5_INNER), z2), pl.BlockSpec((S5_INNER, S5_INNER), z2), pl.BlockSpec((1, S5_INNER), z2),
    ]


def _s5_chain(proj, ucol, x0r, x0i, sp, b, l, m):
    chunk = 8 * m
    nc = l // chunk
    consts = [sp[k] for k in ('lam_r', 'lam_i', 'bre', 'bim', 'cre', 'cim', 'dd', 'gw', 'gb')]
    return pl.pallas_call(
        functools.partial(_s5_chain_kernel, m=m),
        out_shape=(jax.ShapeDtypeStruct((S5_QUARTERS, b * l, S5_QC), f32),
                   jax.ShapeDtypeStruct((b, 1, S5_LANES), f32),
                   jax.ShapeDtypeStruct((b, 1, S5_LANES), f32)),
        grid=(b, nc),
        in_specs=[pl.BlockSpec((chunk, S5_QC), functools.partial(lambda i, j, c: (i * nc + j, ucol + c), c=c))
                  for c in range(S5_QUARTERS)] + [
                  pl.BlockSpec((1, 1, S5_LANES), lambda i, j: (i, 0, 0)),
                  pl.BlockSpec((1, 1, S5_LANES), lambda i, j: (i, 0, 0))] + _s5_const_specs(2),
        out_specs=(pl.BlockSpec((S5_QUARTERS, chunk, S5_QC), lambda i, j: (0, i * nc + j, 0)),
                   pl.BlockSpec((1, 1, S5_LANES), lambda i, j: (i, 0, 0)),
                   pl.BlockSpec((1, 1, S5_LANES), lambda i, j: (i, 0, 0))),
        scratch_shapes=[pltpu.VMEM((m, S5_LANES), f32), pltpu.VMEM((m, S5_LANES), f32),
                        pltpu.VMEM((chunk, S5_LANES), f32), pltpu.VMEM((chunk, S5_LANES), f32),
                        pltpu.VMEM((chunk, S5_INNER), f32), pltpu.VMEM((chunk, S5_INNER), f32),
                        pltpu.VMEM((8, S5_LANES), f32), pltpu.VMEM((8, S5_LANES), f32)],
        compiler_params=_cparams("arbitrary", "arbitrary"),
        name="s5_chain",
    )(proj, proj, proj, proj, x0r, x0i, *consts)


def _s5_batch(proj, ucol, x0r, x0i, sp, n_sub, m):
    t = n_sub * m
    consts = [sp[k] for k in ('lam_r', 'lam_i', 'bre', 'bim', 'cre', 'cim', 'dd', 'gw', 'gb')]
    return pl.pallas_call(
        functools.partial(_s5_batch_kernel, n_sub=n_sub, m=m),
        out_shape=(jax.ShapeDtypeStruct((S5_QUARTERS, t, S5_QC), f32),
                   jax.ShapeDtypeStruct((n_sub, S5_LANES), f32),
                   jax.ShapeDtypeStruct((n_sub, S5_LANES), f32)),
        grid=(1,),
        in_specs=[pl.BlockSpec((t, S5_QC), functools.partial(lambda i, c: (0, ucol + c), c=c))
                  for c in range(S5_QUARTERS)] + [
                  pl.BlockSpec((n_sub, S5_LANES), lambda i: (0, 0)),
                  pl.BlockSpec((n_sub, S5_LANES), lambda i: (0, 0))] + _s5_const_specs(1),
        out_specs=(pl.BlockSpec((S5_QUARTERS, t, S5_QC), lambda i: (0, 0, 0)),
                   pl.BlockSpec((n_sub, S5_LANES), lambda i: (0, 0)),
                   pl.BlockSpec((n_sub, S5_LANES), lambda i: (0, 0))),
        scratch_shapes=[pltpu.VMEM((t, S5_LANES), f32), pltpu.VMEM((t, S5_LANES), f32),
                        pltpu.VMEM((t, S5_INNER), f32), pltpu.VMEM((t, S5_INNER), f32)],
        compiler_params=_cparams("arbitrary"),
        name="s5_batch",
    )(proj, proj, proj, proj, x0r, x0i, *consts)


def _tri(n, strict=False, upper=False):
    r = lax.broadcasted_iota(jnp.int32, (n, n), 0)
    c = lax.broadcasted_iota(jnp.int32, (n, n), 1)
    if upper:
        r, c = c, r
    return jnp.where((c < r) if strict else (c <= r), 1.0, 0.0).astype(bf16)


def _dot_exact_lhs(a_exact_bf16, b):
    b0, b1, b2 = _split3(b)
    return _dot(a_exact_bf16, b0) + _dot(a_exact_bf16, b1) + _dot(a_exact_bf16, b2)


def _log_sigmoid(x):
    return jnp.minimum(x, 0.0) - jnp.log1p(jnp.exp(-jnp.abs(x)))


def _fcum_kernel(fr_ref, bias_ref, logf_ref, f_ref, ft_ref, carry):
    @pl.when(pl.program_id(1) == 0)
    def _():
        carry[...] = jnp.zeros_like(carry)

    n = fr_ref.shape[0]
    logf = _log_sigmoid(fr_ref[...] + bias_ref[...])
    logf_ref[...] = logf
    f = _dot_exact_lhs(_tri(n), logf) + carry[0:1, :]
    f_ref[...] = f
    ft_ref[0] = f.T[:8, :]
    carry[0:1, :] = f[n - 1:n, :]


def _fcum(proj, col, bias, b, l, chunk):
    nc = l // chunk
    return pl.pallas_call(
        _fcum_kernel,
        out_shape=(jax.ShapeDtypeStruct((b * l, LANES), f32),
                   jax.ShapeDtypeStruct((b * l, LANES), f32),
                   jax.ShapeDtypeStruct((b, 8, l), f32)),
        grid=(b, nc),
        in_specs=[pl.BlockSpec((chunk, LANES), lambda i, j: (i * nc + j, col)),
                  pl.BlockSpec((1, LANES), lambda i, j: (0, 0))],
        out_specs=(pl.BlockSpec((chunk, LANES), lambda i, j: (i * nc + j, 0)),
                   pl.BlockSpec((chunk, LANES), lambda i, j: (i * nc + j, 0)),
                   pl.BlockSpec((1, 8, chunk), lambda i, j: (i, 0, j))),
        scratch_shapes=[pltpu.VMEM((8, LANES), f32)],
        compiler_params=_cparams("arbitrary", "arbitrary"),
        name="fcum",
    )(proj, bias)


N_FPARTS = 3
FOX_STRIP = 256


def _fox_select_mats():
    rows = jnp.arange(N_FPARTS * LANES)[None, :, None]
    cols = jnp.arange(FOX_HEAD_DIM)[None, None, :]
    head = jnp.arange(FOX_HEADS)[:, None, None]
    hit = (rows % LANES == head)
    sq = jnp.where(hit & (cols == rows // LANES), 1.0, 0.0)
    sk = jnp.where(hit & (cols == N_FPARTS + rows // LANES), -1.0, 0.0)
    return sq.astype(bf16), sk.astype(bf16)


def _fox_prep_kernel(q_ref, k_ref, v_ref, f_ref, sq_ref, sk_ref, qa_ref, ka_ref, vt_ref, *, scale):
    tm = q_ref.shape[0]
    hd = FOX_HEAD_DIM
    fcat = jnp.concatenate(_split3(f_ref[...]), axis=1)
    lane = lax.broadcasted_iota(jnp.int32, (tm, hd), 1)
    ones_q = jnp.where((lane >= N_FPARTS) & (lane < 2 * N_FPARTS), 1.0, 0.0)
    ones_k = jnp.where(lane < N_FPARTS, 1.0, 0.0)
    vt = v_ref[...].T
    for h in range(FOX_HEADS):
        cols = slice(h * hd, (h + 1) * hd)
        eq = _dot(fcat, sq_ref[h]) + ones_q
        ek = _dot(fcat, sk_ref[h]) + ones_k
        qa_ref[h] = jnp.concatenate([q_ref[:, cols] * scale, eq], axis=1).astype(bf16)
        ka_ref[h] = jnp.concatenate([k_ref[:, cols], ek], axis=1).astype(bf16)
        vt_ref[h] = vt[h * hd:(h + 1) * hd, :].astype(bf16)


def _fox_prep(proj, f, t, tm):
    sq, sk = _fox_select_mats()
    sel_spec = pl.BlockSpec((FOX_HEADS, N_FPARTS * LANES, FOX_HEAD_DIM), lambda i: (0, 0, 0))
    return pl.pallas_call(
        functools.partial(_fox_prep_kernel, scale=FOX_HEAD_DIM ** -0.5),
        out_shape=(jax.ShapeDtypeStruct((FOX_HEADS, t, LANES), bf16),
                   jax.ShapeDtypeStruct((FOX_HEADS, t, LANES), bf16),
                   jax.ShapeDtypeStruct((FOX_HEADS, FOX_HEAD_DIM, t), bf16)),
        grid=(t // tm,),
        in_specs=[pl.BlockSpec((tm, FOX_INNER), lambda i: (i, 0)),
                  pl.BlockSpec((tm, FOX_INNER), lambda i: (i, 1)),
                  pl.BlockSpec((tm, FOX_INNER), lambda i: (i, 2)),
                  pl.BlockSpec((tm, LANES), lambda i: (i, 0)),
                  sel_spec, sel_spec],
        out_specs=(pl.BlockSpec((FOX_HEADS, tm, LANES), lambda i: (0, i, 0)),
                   pl.BlockSpec((FOX_HEADS, tm, LANES), lambda i: (0, i, 0)),
                   pl.BlockSpec((FOX_HEADS, FOX_HEAD_DIM, tm), lambda i: (0, 0, i))),
        compiler_params=_cparams("parallel"),
        name="fox_prep",
    )(proj, proj, proj, f, sq, sk)


def _fox_kernel(qt_ref, kt_ref, qa_ref, ka_ref, vt_ref, o_ref, m_scr, l_scr, acc_scr):
    step = pl.program_id(2)
    qi = qt_ref[step]
    ki = kt_ref[step]
    tq = qa_ref.shape[1]
    tk = ka_ref.shape[1]
    q_first = qi * tq
    k_first = ki * tk

    @pl.when(ki == 0)
    def _():
        m_scr[...] = jnp.full_like(m_scr, -jnp.inf)
        l_scr[...] = jnp.zeros_like(l_scr)
        acc_scr[...] = jnp.zeros_like(acc_scr)

    def update(masked):
        strip = min(FOX_STRIP, tq)
        chains = [(hh, q0) for hh in range(2) for q0 in range(0, tq, strip)]
        nks = [min(tk, q0 + strip) if (masked and tq == tk) else tk for _, q0 in chains]
        sts = [_dot_nt(ka_ref[hh, :nk, :], qa_ref[hh, q0:q0 + strip, :])
               for (hh, q0), nk in zip(chains, nks)]
        ps, alphas = [], []
        for (hh, q0), nk, st in zip(chains, nks, sts):
            qs = slice(q0, q0 + strip)
            if masked:
                kpos = lax.broadcasted_iota(jnp.int32, (nk, strip), 0) + k_first
                qpos = lax.broadcasted_iota(jnp.int32, (nk, strip), 1) + (q0 + q_first)
                st = jnp.where(kpos <= qpos, st, -jnp.inf)
            m_old = m_scr[hh, :, qs]
            m_new = jnp.maximum(m_old, jnp.max(st, axis=0, keepdims=True))
            alpha = jnp.exp(m_old - m_new)
            p = jnp.exp(st - m_new)
            l_scr[hh, :, qs] = alpha * l_scr[hh, :, qs] + jnp.sum(p, axis=0, keepdims=True)
            m_scr[hh, :, qs] = m_new
            ps.append(p.astype(bf16))
            alphas.append(alpha)
        pvs = [_dot(vt_ref[hh, :, :nk], p) for (hh, _), nk, p in zip(chains, nks, ps)]
        for (hh, q0), alpha, pv in zip(chains, alphas, pvs):
            qs = slice(q0, q0 + strip)
            acc_scr[hh, :, qs] = alpha * acc_scr[hh, :, qs] + pv

    full = k_first + (tk - 1) <= q_first

    @pl.when(full)
    def _():
        update(False)

    @pl.when(jnp.logical_not(full))
    def _():
        update(True)

    @pl.when(k_first + tk >= q_first + tq)
    def _():
        ot = jnp.concatenate([acc_scr[0] / l_scr[0], acc_scr[1] / l_scr[1]], axis=0)
        o_ref[...] = ot.T


def _fox_prompt(proj, f, b, l, tq, tk):
    nq = l // tq
    nk = l // tk
    n_hp = FOX_HEADS // 2
    qa, ka, vt = _fox_prep(proj, f, b * l, max(tq, tk))
    pairs = [(qi, ki) for qi in range(nq) for ki in range(((qi + 1) * tq - 1) // tk + 1)]
    qtab = jnp.array([pr[0] for pr in pairs], jnp.int32)
    ktab = jnp.array([pr[1] for pr in pairs], jnp.int32)
    grid_spec = pltpu.PrefetchScalarGridSpec(
        num_scalar_prefetch=2,
        grid=(b, n_hp, len(pairs)),
        in_specs=[
            pl.BlockSpec((2, tq, LANES), lambda i, h, s, qt, kt: (h, i * nq + qt[s], 0)),
            pl.BlockSpec((2, tk, LANES), lambda i, h, s, qt, kt: (h, i * nk + kt[s], 0)),
            pl.BlockSpec((2, FOX_HEAD_DIM, tk), lambda i, h, s, qt, kt: (h, 0, i * nk + kt[s])),
        ],
        out_specs=pl.BlockSpec((tq, LANES), lambda i, h, s, qt, kt: (i * nq + qt[s], h)),
        scratch_shapes=[pltpu.VMEM((2, 1, tq), f32), pltpu.VMEM((2, 1, tq), f32),
                        pltpu.VMEM((2, FOX_HEAD_DIM, tq), f32)],
    )
    return pl.pallas_call(
        _fox_kernel,
        out_shape=jax.ShapeDtypeStruct((b * l, FOX_INNER), f32),
        grid_spec=grid_spec,
        compiler_params=_cparams("parallel", "parallel", "arbitrary"),
        name="fox_prompt",
    )(qtab, ktab, qa, ka, vt)


PAGES_PER_STEP = 32


def _dot_exact_rhs(a, b_exact_bf16):
    a0, a1, a2 = _split3(a)
    return _dot(a0, b_exact_bf16) + _dot(a1, b_exact_bf16) + _dot(a2, b_exact_bf16)


def _dot_nt(a, b):
    return lax.dot_general(a, b, (((1,), (1,)), ((), ())), preferred_element_type=f32)


def _fox_paged_kernel(pt_ref, *refs, n_pages, pps, scale):
    k_refs, v_refs = refs[:pps], refs[pps:2 * pps]
    (lf_ref, q_ref, kn_ref, vn_ref, fr_ref, bias_ref, o_ref, logf_ref,
     s_scr, qbd_scr, acc_scr, psum_scr, m_scr, car_scr, fq_scr) = refs[2 * pps:]
    seq = pl.program_id(0)
    ph = pl.program_id(1)
    c = pl.program_id(2)
    last_c = pl.num_programs(2) - 1
    nq = q_ref.shape[0]
    hd = FOX_HEAD_DIM
    pg = LANES
    row = lax.broadcasted_iota(jnp.int32, (pg, pg), 0)
    lane = lax.broadcasted_iota(jnp.int32, (pg, pg), 1)
    upper = jnp.where(row <= lane, 1.0, 0.0).astype(bf16)

    def pad_rows(x, fill=0.0):
        return jnp.concatenate([x, jnp.full((pg - x.shape[0], x.shape[1]), fill, x.dtype)], axis=0)

    def per_head_rows(x):
        rep = jnp.broadcast_to(x[:, None, :], (FOX_HEADS, nq, x.shape[1])).reshape(FOX_HEADS * nq, x.shape[1])
        return pad_rows(rep)

    def key_sums(logf_rows):
        cum = _dot_exact_rhs(logf_rows, upper) + car_scr[...]
        car_scr[...] = jnp.broadcast_to(cum[:, pg - 1:pg], (pg, pg))
        return cum

    @pl.when((ph == 0) & (c == 0))
    def _():
        q_rep = jnp.broadcast_to((q_ref[...] * scale)[None], (FOX_HEADS, nq, FOX_INNER)).reshape(FOX_HEADS * nq, FOX_INNER)
        r5 = lax.broadcasted_iota(jnp.int32, (FOX_HEADS * nq, FOX_INNER), 0)
        l5 = lax.broadcasted_iota(jnp.int32, (FOX_HEADS * nq, FOX_INNER), 1)
        qbd_scr[...] = pad_rows(jnp.where(l5 // hd == r5 // nq, q_rep, 0.0)).astype(bf16)
        m_scr[...] = jnp.full_like(m_scr, -jnp.inf)
        car_scr[...] = jnp.zeros_like(car_scr)

    @pl.when(ph == 0)
    def _():
        qk = [_dot(qbd_scr[...], k_refs[i][0].reshape(FOX_INNER, pg).astype(bf16)) for i in range(pps)]
        pages = [pt_ref[seq * n_pages + c * pps + i] for i in range(pps)]
        local = [per_head_rows(_dot_exact_rhs(lf_ref[pages[i]], upper)) for i in range(pps)]
        car = car_scr[...]
        m = m_scr[...]
        for i in range(pps):
            cum = local[i] + car
            car = jnp.broadcast_to(cum[:, pg - 1:pg], (pg, pg))
            s = qk[i] - cum
            s_scr[c * pps + i] = s
            m = jnp.maximum(m, s)
        car_scr[...] = car
        m_scr[...] = m

    @pl.when((ph == 0) & (c == last_c))
    def _():
        logf_new = _log_sigmoid(fr_ref[...] + bias_ref[...])
        logf_ref[...] = logf_new
        cum = key_sums(per_head_rows(pad_rows(logf_new).T[:FOX_HEADS, :]))
        fq = jnp.sum(jnp.where(lane == row % nq, cum, 0.0), axis=1, keepdims=True)
        fq_scr[...] = jnp.broadcast_to(fq, (pg, pg))
        s = _dot(qbd_scr[...], pad_rows(kn_ref[...]).T.astype(bf16)) - cum
        s = jnp.where(lane <= row % nq, s, -jnp.inf)
        s_scr[n_pages] = s
        m = jnp.max(jnp.maximum(m_scr[...], s), axis=1, keepdims=True)
        m_scr[...] = jnp.broadcast_to(m, (pg, pg))

    @pl.when((ph == 1) & (c == 0))
    def _():
        psum_scr[...] = jnp.zeros_like(psum_scr)
        acc_scr[...] = jnp.zeros_like(acc_scr)

    def probs(s):
        fq = fq_scr[...]
        return jnp.exp((s + fq) - (m_scr[...] + fq))

    def accumulate(s, v_t_bf16):
        p = probs(s)
        psum_scr[...] += p
        acc_scr[...] += _dot_nt(v_t_bf16, p.astype(bf16))

    @pl.when(ph == 1)
    def _():
        ps = [probs(s_scr[c * pps + i]) for i in range(pps)]
        pvs = [_dot_nt(v_refs[i][0].reshape(FOX_INNER, pg).astype(bf16), ps[i].astype(bf16)) for i in range(pps)]
        psum = psum_scr[...]
        acc = acc_scr[...]
        for i in range(pps):
            psum = psum + ps[i]
            acc = acc + pvs[i]
        psum_scr[...] = psum
        acc_scr[...] = acc

    @pl.when((ph == 1) & (c == last_c))
    def _():
        accumulate(s_scr[n_pages], pad_rows(vn_ref[...]).T.astype(bf16))
        p0, p1, p2 = _split3(psum_scr[...])
        ones = jnp.ones((8, pg), bf16)
        l_row = (_dot_nt(ones, p0) + _dot_nt(ones, p1) + _dot_nt(ones, p2))[0:1, :]
        o_t = (acc_scr[...] / l_row).T
        o_ref[...] = jnp.concatenate([o_t[h * nq:(h + 1) * nq, h * hd:(h + 1) * hd] for h in range(FOX_HEADS)],
                                     axis=1)


def _fox_paged(proj, fcol, bias, cache_k, cache_v, cache_logf, page_table):
    n_seq, n_pages = page_table.shape
    nq = proj.shape[0] // n_seq
    pps = PAGES_PER_STEP
    n_chunks = n_pages // pps
    ck = jnp.transpose(cache_k, (0, 2, 3, 1))
    cv = jnp.transpose(cache_v, (0, 2, 3, 1))
    clf = jnp.transpose(cache_logf, (0, 2, 1))
    page = ck.shape[3]
    assert page == LANES and nq == 8

    def k_map(i):
        return lambda s, ph, c, pt: (pt[s * n_pages + jnp.where(ph == 0, c, n_chunks - 1) * pps + i], 0, 0, 0)

    def v_map(i):
        return lambda s, ph, c, pt: (pt[s * n_pages + jnp.where(ph == 0, 0, c) * pps + i], 0, 0, 0)

    blk = (1, FOX_HEADS, FOX_HEAD_DIM, page)
    in_specs = ([pl.BlockSpec(blk, k_map(i)) for i in range(pps)]
                + [pl.BlockSpec(blk, v_map(i)) for i in range(pps)]
                + [pl.BlockSpec(clf.shape, lambda s, ph, c, pt: (0, 0, 0), pipeline_mode=pl.Buffered(1)),
                   pl.BlockSpec((nq, FOX_INNER), lambda s, ph, c, pt: (s, 0)),
                   pl.BlockSpec((nq, FOX_INNER), lambda s, ph, c, pt: (s, 1)),
                   pl.BlockSpec((nq, FOX_INNER), lambda s, ph, c, pt: (s, 2)),
                   pl.BlockSpec((nq, LANES), lambda s, ph, c, pt: (s, fcol)),
                   pl.BlockSpec((1, LANES), lambda s, ph, c, pt: (0, 0))])
    grid_spec = pltpu.PrefetchScalarGridSpec(
        num_scalar_prefetch=1,
        grid=(n_seq, 2, n_chunks),
        in_specs=in_specs,
        out_specs=(pl.BlockSpec((nq, FOX_INNER), lambda s, ph, c, pt: (s, 0)),
                   pl.BlockSpec((nq, LANES), lambda s, ph, c, pt: (s, 0))),
        scratch_shapes=[pltpu.VMEM((n_pages + 1, page, page), f32),
                        pltpu.VMEM((page, FOX_INNER), bf16),
                        pltpu.VMEM((FOX_INNER, page), f32),
                        pltpu.VMEM((page, page), f32), pltpu.VMEM((page, page), f32),
                        pltpu.VMEM((page, page), f32), pltpu.VMEM((page, page), f32)],
    )
    return pl.pallas_call(
        functools.partial(_fox_paged_kernel, n_pages=n_pages, pps=pps, scale=FOX_HEAD_DIM ** -0.5),
        out_shape=(jax.ShapeDtypeStruct((n_seq * nq, FOX_INNER), f32),
                   jax.ShapeDtypeStruct((n_seq * nq, LANES), f32)),
        grid_spec=grid_spec,
        compiler_params=_cparams("arbitrary", "arbitrary", "arbitrary"),
        name="fox_paged",
    )(page_table.reshape(-1), *([ck] * pps), *([cv] * pps), clf, proj, proj, proj, proj, bias)


CONV_TAIL = 8
COL0_QKV = 0
COL0_XS = 3 * GDN_INNER
COL0_ZS = COL0_XS + SSD_INNER
COL0_ZG = COL0_ZS + SSD_INNER
COL0_BC = COL0_ZG + GDN_INNER
COL0_SM = COL0_BC + 2 * SSD_GROUPS * SSD_STATE
assert COL0_SM + LANES == IN0_PAD


def _softplus(x):
    return jnp.maximum(x, 0.0) + jnp.log1p(jnp.exp(-jnp.abs(x)))


def _silu(x):
    return x * jax.nn.sigmoid(x)


def _conv_silu(ubuf, u_refs, tail_ref, cw_ref, cb_ref, first, rows, lpad):
    @pl.when(first)
    def _():
        ubuf[0:CONV_TAIL, :] = tail_ref[0]

    @pl.when(jnp.logical_not(first))
    def _():
        ubuf[0:CONV_TAIL, :] = ubuf[rows:rows + CONV_TAIL, :]

    col = 0
    for u_ref in u_refs:
        ubuf[CONV_TAIL:CONV_TAIL + rows, col:col + u_ref.shape[1]] = u_ref[...]
        col += u_ref.shape[1]
    if lpad > rows:
        ubuf[CONV_TAIL + rows:, :] = jnp.zeros((lpad - rows, ubuf.shape[1]), f32)
    acc = cb_ref[...]
    for j in range(CONV_K):
        off = CONV_TAIL - (CONV_K - 1) + j
        acc = acc + cw_ref[j:j + 1, :] * ubuf[off:off + lpad, :]
    return _silu(acc)


def _pad_rows(x, lpad):
    rows = x.shape[0]
    if lpad == rows:
        return x
    return jnp.concatenate([x, jnp.zeros((lpad - rows, x.shape[1]), x.dtype)], axis=0)


def _head_scalars(sm_ref, bias_ref, coef_ref, rows, lpad):
    raw = _pad_rows(sm_ref[...], lpad)
    valid = lax.broadcasted_iota(jnp.int32, (lpad, LANES), 0) < rows
    sp = jnp.where(valid, _softplus(raw + bias_ref[...]), 0.0)
    a = sp * coef_ref[...]
    cum = _dot_exact_lhs(_tri(lpad), a)
    return raw, valid, sp, cum


def _ssd_kernel(ux_ref, ubc_ref, z_ref, sm_ref, tail_ref, h0_ref, cw_ref, cb_ref, bias_ref, coef_ref, dd_ref, nw_ref,
                y_ref, hout_ref, ubuf, *, rows, lpad):
    j = pl.program_id(1)
    n, p = SSD_STATE, SSD_HEAD_DIM
    xbc = _conv_silu(ubuf, (ux_ref, ubc_ref), tail_ref, cw_ref, cb_ref, j == 0, rows, lpad)

    @pl.when(j == 0)
    def _():
        hout_ref[...] = h0_ref[...]

    _, _, dt, acum = _head_scalars(sm_ref, bias_ref, coef_ref, rows, lpad)
    acum_t = acum.T
    dt_t = dt.T
    xs = xbc[:, :SSD_INNER]
    xs_t = xs.T
    r = lax.broadcasted_iota(jnp.int32, (lpad, lpad), 0)
    c = lax.broadcasted_iota(jnp.int32, (lpad, lpad), 1)
    causal = c <= r
    heads = range(SSD_HEADS)
    group_of = [h // (SSD_HEADS // SSD_GROUPS) for h in heads]
    bms = [xbc[:, SSD_INNER + g * n:SSD_INNER + (g + 1) * n] for g in range(SSD_GROUPS)]
    cms = [xbc[:, SSD_INNER + SSD_GROUPS * n + g * n:SSD_INNER + SSD_GROUPS * n + (g + 1) * n]
           for g in range(SSD_GROUPS)]
    cbs = [_dot_nt(cms[g].astype(bf16), bms[g].astype(bf16)) for g in range(SSD_GROUPS)]
    a_cols = [acum[:, h:h + 1] for h in heads]
    a_lasts = [acum[lpad - 1:lpad, h:h + 1] for h in heads]
    x_hs = [xs[:, h * p:(h + 1) * p] for h in heads]
    hsts = [hout_ref[0, h] for h in heads]
    scores = [(cbs[group_of[h]] * jnp.exp(jnp.where(causal, a_cols[h] - acum_t[h:h + 1, :], -jnp.inf))).astype(bf16)
              for h in heads]
    xdts = [(x_hs[h] * dt[:, h:h + 1]).astype(bf16) for h in heads]
    c_exps = [(cms[group_of[h]] * jnp.exp(a_cols[h])).astype(bf16) for h in heads]
    xdt_ts = [(xs_t[h * p:(h + 1) * p, :] * dt_t[h:h + 1, :]).astype(bf16) for h in heads]
    b_ends = [(bms[group_of[h]] * jnp.exp(a_lasts[h] - a_cols[h])).astype(bf16) for h in heads]
    y_diags = [_dot(scores[h], xdts[h]) for h in heads]
    y_offs = [_dot_nt(c_exps[h], hsts[h].astype(bf16)) for h in heads]
    upds = [_dot(xdt_ts[h], b_ends[h]) for h in heads]
    for h in heads:
        hout_ref[0, h] = hsts[h] * jnp.exp(a_lasts[h]) + upds[h]
    ys = [y_diags[h] + y_offs[h] + dd_ref[:, h * p:(h + 1) * p] * x_hs[h] for h in heads]
    y = jnp.concatenate(ys, axis=1)[:rows]
    gated = y * _silu(z_ref[...])
    ms = jnp.mean(gated * gated, axis=-1, keepdims=True)
    y_ref[...] = gated * lax.rsqrt(ms + EPS) * nw_ref[...]


def _ssd(proj, tail, h0, prm, b, l, rows, lpad):
    nc = l // rows
    bc_w = 2 * SSD_GROUPS * SSD_STATE
    cw, cb, bias, coef, dd, nw = prm
    c2 = lambda i, j: (0, 0)
    return pl.pallas_call(
        functools.partial(_ssd_kernel, rows=rows, lpad=lpad),
        out_shape=(jax.ShapeDtypeStruct((b * l, SSD_INNER), f32),
                   jax.ShapeDtypeStruct((b, SSD_HEADS, SSD_HEAD_DIM, SSD_STATE), f32)),
        grid=(b, nc),
        in_specs=[pl.BlockSpec((rows, SSD_INNER), lambda i, j: (i * nc + j, COL0_XS // SSD_INNER)),
                  pl.BlockSpec((rows, bc_w), lambda i, j: (i * nc + j, COL0_BC // bc_w)),
                  pl.BlockSpec((rows, SSD_INNER), lambda i, j: (i * nc + j, COL0_ZS // SSD_INNER)),
                  pl.BlockSpec((rows, LANES), lambda i, j: (i * nc + j, COL0_SM // LANES)),
                  pl.BlockSpec((1, CONV_TAIL, SSD_CONV_CH), lambda i, j: (i, 0, 0)),
                  pl.BlockSpec((1, SSD_HEADS, SSD_HEAD_DIM, SSD_STATE), lambda i, j: (i, 0, 0, 0)),
                  pl.BlockSpec((CONV_K, SSD_CONV_CH), c2), pl.BlockSpec((1, SSD_CONV_CH), c2),
                  pl.BlockSpec((1, LANES), c2), pl.BlockSpec((1, LANES), c2),
                  pl.BlockSpec((1, SSD_INNER), c2), pl.BlockSpec((1, SSD_INNER), c2)],
        out_specs=(pl.BlockSpec((rows, SSD_INNER), lambda i, j: (i * nc + j, 0)),
                   pl.BlockSpec((1, SSD_HEADS, SSD_HEAD_DIM, SSD_STATE), lambda i, j: (i, 0, 0, 0))),
        scratch_shapes=[pltpu.VMEM((CONV_TAIL + lpad, SSD_CONV_CH), f32)],
        compiler_params=_cparams("arbitrary", "arbitrary"),
        name="ssd",
    )(proj, proj, proj, proj, tail, h0, cw, cb, bias, coef, dd, nw)


def _split2(a):
    a_hi = a.astype(bf16)
    return a_hi, (a - a_hi.astype(f32)).astype(bf16)


def _dot3_split(a_split, b_split):
    a_hi, a_lo = a_split
    b_hi, b_lo = b_split
    return _dot(a_hi, b_hi) + _dot(a_lo, b_hi) + _dot(a_hi, b_lo)


INV_BASE = 16


def _l2n(x):
    return x * lax.rsqrt(jnp.sum(x * x, axis=-1, keepdims=True) + EPS)


GDN_A_LANE = SSD_HEADS
GDN_B_LANE = SSD_HEADS + GDN_HEADS


def _unit_lower_inverse_multi(ms, n):
    r = lax.broadcasted_iota(jnp.int32, (n, n), 0)
    c = lax.broadcasted_iota(jnp.int32, (n, n), 1)
    nb = min(INV_BASE, n)
    diag_blk = (r // nb) == (c // nb)
    eye = jnp.where(r == c, 1.0, 0.0)
    pws = [jnp.where(diag_blk, m, 0.0) for m in ms]
    invs = [eye - d for d in pws]
    pw_ss = [_split2(pw) for pw in pws]
    size = 2
    while size < nb:
        pws = [_dot3_split(pw_s, pw_s) for pw_s in pw_ss]
        pw_ss = [_split2(pw) for pw in pws]
        invs = [inv + _dot3_split(_split2(inv), pw_s) for inv, pw_s in zip(invs, pw_ss)]
        size *= 2
    s = nb
    while s < n:
        lower_left = ((r // (2 * s)) == (c // (2 * s))) & ((r // s) % 2 == 1) & ((c // s) % 2 == 0)
        inv_ss = [_split2(inv) for inv in invs]
        tmps = [_dot3_split(inv_s, _split2(jnp.where(lower_left, m, 0.0))) for inv_s, m in zip(inv_ss, ms)]
        invs = [inv - _dot3_split(_split2(tmp), inv_s) for inv, tmp, inv_s in zip(invs, tmps, inv_ss)]
        s *= 2
    return invs


def _gdn_staged_kernel(u_ref, z_ref, sm_ref, tail_ref, s0_ref, cw_ref, cb_ref, bias_ref, coef_ref, nw_ref,
                       o_ref, sout_ref, ubuf, *, rows, lpad):
    j = pl.program_id(1)
    dk = GDN_HEAD_DIM
    heads = range(GDN_HEADS)
    qkv = _conv_silu(ubuf, (u_ref,), tail_ref, cw_ref, cb_ref, j == 0, rows, lpad)

    @pl.when(j == 0)
    def _():
        sout_ref[...] = s0_ref[...]

    raw, valid, _, gcum = _head_scalars(sm_ref, bias_ref, coef_ref, rows, lpad)
    beta_all = jnp.where(valid, jax.nn.sigmoid(raw), 0.0)
    gcum_t = gcum.T
    r = lax.broadcasted_iota(jnp.int32, (lpad, lpad), 0)
    c = lax.broadcasted_iota(jnp.int32, (lpad, lpad), 1)
    qs = [_l2n(qkv[:, h * dk:(h + 1) * dk]) * dk ** -0.5 for h in heads]
    ks = [_l2n(qkv[:, GDN_INNER + h * dk:GDN_INNER + (h + 1) * dk]) for h in heads]
    vs = [qkv[:, 2 * GDN_INNER + h * dk:2 * GDN_INNER + (h + 1) * dk] for h in heads]
    g_cols = [gcum[:, GDN_A_LANE + h:GDN_A_LANE + h + 1] for h in heads]
    g_lasts = [gcum[lpad - 1:lpad, GDN_A_LANE + h:GDN_A_LANE + h + 1] for h in heads]
    betas = [beta_all[:, GDN_B_LANE + h:GDN_B_LANE + h + 1] for h in heads]
    decays = [jnp.exp(jnp.where(c <= r, g_cols[h] - gcum_t[GDN_A_LANE + h:GDN_A_LANE + h + 1, :], -jnp.inf))
              for h in heads]
    kbs = [ks[h] * betas[h] for h in heads]
    k_bs = [k.astype(bf16) for k in ks]
    kks = [_dot_nt(kbs[h].astype(bf16), k_bs[h]) for h in heads]
    attns = [_dot_nt(qs[h].astype(bf16), k_bs[h]) * decays[h] for h in heads]
    a_invs = _unit_lower_inverse_multi([jnp.where(c < r, kks[h] * decays[h], 0.0) for h in heads], lpad)
    a_his = [a.astype(bf16) for a in a_invs]
    a_los = [(a - ah.astype(f32)).astype(bf16) for a, ah in zip(a_invs, a_his)]
    vb_bs = [(vs[h] * betas[h]).astype(bf16) for h in heads]
    kbe_bs = [(kbs[h] * jnp.exp(g_cols[h])).astype(bf16) for h in heads]
    us = [_dot(a_his[h], vb_bs[h]) + _dot(a_los[h], vb_bs[h]) for h in heads]
    ws = [_dot(a_his[h], kbe_bs[h]) + _dot(a_los[h], kbe_bs[h]) for h in heads]
    sts = [sout_ref[0, h] for h in heads]
    st_bs = [st.astype(bf16) for st in sts]
    v_news = [(us[h] - _dot(ws[h].astype(bf16), st_bs[h])).astype(bf16) for h in heads]
    os_ = [_dot((qs[h] * jnp.exp(g_cols[h])).astype(bf16), st_bs[h]) + _dot(attns[h].astype(bf16), v_news[h])
           for h in heads]
    ke_ts = [(ks[h] * jnp.exp(g_lasts[h] - g_cols[h])).T.astype(bf16) for h in heads]
    for h in heads:
        sout_ref[0, h] = sts[h] * jnp.exp(g_lasts[h]) + _dot(ke_ts[h], v_news[h])
    outs = []
    for h in heads:
        o = os_[h][:rows]
        ms = jnp.mean(o * o, axis=-1, keepdims=True)
        outs.append(o * lax.rsqrt(ms + EPS) * nw_ref[...] * _silu(z_ref[:, h * dk:(h + 1) * dk]))
    o_ref[...] = jnp.concatenate(outs, axis=1)


def _gdn(proj, tail, s0, prm, b, l, rows, lpad):
    nc = l // rows
    cw, cb, bias, coef, nw = prm
    c2 = lambda i, j: (0, 0)
    width = 3 * GDN_INNER
    return pl.pallas_call(
        functools.partial(_gdn_staged_kernel, rows=rows, lpad=lpad),
        out_shape=(jax.ShapeDtypeStruct((b * l, GDN_INNER), f32),
                   jax.ShapeDtypeStruct((b, GDN_HEADS, GDN_HEAD_DIM, GDN_HEAD_DIM), f32)),
        grid=(b, nc),
        in_specs=[pl.BlockSpec((rows, width), lambda i, j: (i * nc + j, COL0_QKV // width)),
                  pl.BlockSpec((rows, GDN_INNER), lambda i, j: (i * nc + j, COL0_ZG // GDN_INNER)),
                  pl.BlockSpec((rows, LANES), lambda i, j: (i * nc + j, COL0_SM // LANES)),
                  pl.BlockSpec((1, CONV_TAIL, width), lambda i, j: (i, 0, 0)),
                  pl.BlockSpec((1, GDN_HEADS, GDN_HEAD_DIM, GDN_HEAD_DIM), lambda i, j: (i, 0, 0, 0)),
                  pl.BlockSpec((CONV_K, width), c2), pl.BlockSpec((1, width), c2),
                  pl.BlockSpec((1, LANES), c2), pl.BlockSpec((1, LANES), c2),
                  pl.BlockSpec((1, GDN_HEAD_DIM), c2)],
        out_specs=(pl.BlockSpec((rows, GDN_INNER), lambda i, j: (i * nc + j, 0)),
                   pl.BlockSpec((1, GDN_HEADS, GDN_HEAD_DIM, GDN_HEAD_DIM), lambda i, j: (i, 0, 0, 0))),
        scratch_shapes=[pltpu.VMEM((CONV_TAIL + lpad, width), f32)],
        compiler_params=_cparams("arbitrary", "arbitrary"),
        name="gdn",
    )(proj, proj, proj, tail, s0, cw, cb, bias, coef, nw)


def _layer0_params(p):
    def lanes(v, off):
        return jnp.zeros((1, LANES), f32).at[0, off:off + v.shape[0]].set(v.astype(f32))

    cw = p['conv_w'].astype(f32)
    cb = p['conv_b'].astype(f32).reshape(1, CONV_CH)
    bias = lanes(p['ssd_dt_bias'], 0) + lanes(p['gdn_dt_bias'], GDN_A_LANE)
    coef = lanes(-jnp.exp(p['ssd_A_log'].astype(f32)), 0) + lanes(-jnp.exp(p['gdn_A_log'].astype(f32)), GDN_A_LANE)
    dd = jnp.repeat(p['ssd_D'].astype(f32), SSD_HEAD_DIM).reshape(1, SSD_INNER)
    return {'ssd': (cw[:, :SSD_CONV_CH], cb[:, :SSD_CONV_CH], bias, coef, dd,
                    p['ssd_norm'].astype(f32).reshape(1, SSD_INNER)),
            'gdn': (cw[:, SSD_CONV_CH:], cb[:, SSD_CONV_CH:], bias, coef,
                    p['gdn_norm'].astype(f32).reshape(1, GDN_HEAD_DIM))}


def _ada_kernel(c_ref, w_ref, b_ref, o_ref):
    o_ref[...] = _dot(_silu(c_ref[...]).astype(bf16), w_ref[...].astype(bf16)) + b_ref[...]


def _ada(c, w_ada, b_ada, tn):
    rows, d = c.shape
    n = w_ada.shape[1]
    return pl.pallas_call(
        _ada_kernel,
        out_shape=jax.ShapeDtypeStruct((rows, n), f32),
        grid=(n // tn,),
        in_specs=[pl.BlockSpec((rows, d), lambda j: (0, 0)),
                  pl.BlockSpec((d, tn), lambda j: (0, j)),
                  pl.BlockSpec((1, tn), lambda j: (0, j))],
        out_specs=pl.BlockSpec((rows, tn), lambda j: (0, j)),
        compiler_params=_cparams("parallel"),
        name="ada",
    )(c, w_ada, b_ada.reshape(1, n))


def _mods(mod, l, tm):
    parts = jnp.split(mod, 6, axis=-1)
    if l % tm == 0:
        return [p[:, None, :] for p in parts]
    b = mod.shape[0]
    return [jnp.repeat(p, l, axis=0).reshape((b * l) // tm, tm, D_MODEL) for p in parts]


def _run_trunk(x, mod0, mod1, conv_buf, ssd_h0, gdn_s0, past, s5_re0, s5_im0, p, tm):
    b, l, d = x.shape
    t = b * l
    x2 = x.reshape(t, d)

    sh1, sc1, g1, sh2, sc2, g2 = _mods(mod0, l, tm)
    proj0 = _inproj(x2, p['norm_mix0'], sc1, sh1, p['w_in0'], tm, TN_INPROJ)
    tail = jnp.pad(conv_buf.astype(f32), ((0, 0), (CONV_TAIL - (CONV_K - 1), 0), (0, 0)))
    if l % SSD_ROWS == 0 and l % GDN_ROWS == 0:
        ssd_rows, ssd_lpad, gdn_rows, gdn_lpad = SSD_ROWS, SSD_ROWS, GDN_ROWS, GDN_ROWS
    else:
        ssd_rows, ssd_lpad, gdn_rows, gdn_lpad = l, PAD_ROWS, l, PAD_ROWS
    y_ssd, ssd_new = _ssd(proj0, tail[..., :SSD_CONV_CH], ssd_h0.astype(f32), p['l0']['ssd'], b, l, ssd_rows, ssd_lpad)
    o_gdn, gdn_new = _gdn(proj0, tail[..., SSD_CONV_CH:], gdn_s0.astype(f32), p['l0']['gdn'], b, l, gdn_rows, gdn_lpad)
    last = proj0.reshape(b, l, IN0_PAD)[:, l - (CONV_K - 1):]
    conv_new = jnp.concatenate([last[..., COL0_XS:COL0_XS + SSD_INNER], last[..., COL0_BC:COL0_SM],
                                last[..., COL0_QKV:COL0_QKV + 3 * GDN_INNER]], axis=-1)
    w_out0 = p['w_out0']
    x2 = _outproj([y_ssd, o_gdn], [w_out0[:SSD_INNER], w_out0[SSD_INNER:]], x2, g1, tm)
    x2 = _ffn(x2, p['norm_ffn0'], sc2, sh2, g2, p['ffn_w1'], p['ffn_w3'], p['ffn_w2'], min(tm, TM_FFN), TF_FFN)

    sh1, sc1, g1, sh2, sc2, g2 = _mods(mod1, l, tm)
    proj2 = _inproj(x2, p['norm_mix1'], sc1, sh1, p['w_in1'], tm, TN_INPROJ)
    proj = proj2.reshape(b, l, IN1_PAD)
    k_new = proj[..., FOX_INNER:2 * FOX_INNER].reshape(b, l, FOX_HEADS, FOX_HEAD_DIM)
    v_new = proj[..., 2 * FOX_INNER:3 * FOX_INNER].reshape(b, l, FOX_HEADS, FOX_HEAD_DIM)
    sp = p['s5']
    ucol = (3 * FOX_INNER) // S5_QC
    fcol = (3 * FOX_INNER + S5_INNER) // LANES
    if past is None:
        logf_pad, fcs, _ = _fcum(proj2, fcol, p['fox_f_bias'], b, l, FCUM_ROWS)
        logf = logf_pad[:, :FOX_HEADS].reshape(b, l, FOX_HEADS)
        o_fox = _fox_prompt(proj2, fcs, b, l, FOX_TQ, FOX_TK)
        o_s5, s5_re, s5_im = _s5_chain(proj2, ucol, s5_re0.reshape(b, 1, S5_LANES),
                                       s5_im0.reshape(b, 1, S5_LANES), sp, b, l, S5_STEPS)
    else:
        o_fox, logf_pad = _fox_paged(proj2, fcol, p['fox_f_bias'], *past)
        logf = logf_pad[:, :FOX_HEADS].reshape(b, l, FOX_HEADS)
        o_s5, s5_re, s5_im = _s5_batch(proj2, ucol, s5_re0.reshape(b, S5_LANES),
                                       s5_im0.reshape(b, S5_LANES), sp, b, l)
    s5_re = s5_re.reshape(b, S5_GROUPS, S5_STATE)
    s5_im = s5_im.reshape(b, S5_GROUPS, S5_STATE)
    w_out1 = p['w_out1']
    x2 = _outproj([o_fox, o_s5], [w_out1[:FOX_INNER], w_out1[FOX_INNER:].reshape(S5_QUARTERS, S5_QC, d)], x2, g1, tm)
    halves = 2 if t % (2 * tm) == 0 else 1
    y_out = _moe_grouped(x2, p['norm_ffn1'], sc2, sh2, g2, p['router_w'], p['moe_w1'], p['moe_w3'], p['moe_w2'],
                         p['norm_final'], tm, halves, TF_MOE, MOE_CAP).reshape(b, l, d)
    return y_out, conv_new, ssd_new, gdn_new, k_new, v_new, logf, s5_re, s5_im


def kernel(x_prompt, x_sample, state_conv0, state_ssd, state_gdn, cache_k, cache_v, cache_logf,
           state_s5_re, state_s5_im, page_table, c_prompt, c_sample,
           ada0_w, ada0_b, norm_mix0, w_in0, conv0_w, conv0_b, ssd_dt_bias, ssd_A_log, ssd_D,
           ssd_norm, gdn_dt_bias, gdn_A_log, gdn_norm, w_out0, norm_ffn0, ffn_w1, ffn_w3, ffn_w2,
           ada1_w, ada1_b, norm_mix1, w_in1, fox_f_bias, s5_A_re, s5_A_im, s5_log_step,
           s5_B_re, s5_B_im, s5_C_re, s5_C_im, s5_D, glu_w, glu_b, w_out1, norm_ffn1,
           router_w, moe_w1, moe_w3, moe_w2, norm_final):
    d = D_MODEL
    c0 = CONV_CH
    w_in0p = jnp.concatenate([
        w_in0[:, SSD_CONV_CH:c0],
        w_in0[:, :SSD_INNER],
        w_in0[:, c0:c0 + SSD_INNER],
        w_in0[:, c0 + SSD_INNER + SSD_HEADS:c0 + SSD_INNER + SSD_HEADS + GDN_INNER],
        w_in0[:, SSD_INNER:SSD_CONV_CH],
        w_in0[:, c0 + SSD_INNER:c0 + SSD_INNER + SSD_HEADS],
        w_in0[:, c0 + SSD_INNER + SSD_HEADS + GDN_INNER:],
        jnp.zeros((d, IN0_PAD - w_in0.shape[1]), f32)], axis=1).astype(bf16)
    f0 = 3 * FOX_INNER
    w_in1p = jnp.concatenate([
        w_in1[:, :f0],
        w_in1[:, f0 + FOX_HEADS:],
        w_in1[:, f0:f0 + FOX_HEADS],
        jnp.zeros((d, IN1_PAD - w_in1.shape[1]), f32)], axis=1).astype(bf16)
    router_wp = jnp.concatenate([router_w.astype(f32), jnp.zeros((d, LANES - N_EXPERTS), f32)], axis=1)
    p = {
        'ada0_w': ada0_w, 'ada0_b': ada0_b, 'norm_mix0': norm_mix0.reshape(1, d), 'w_in0': w_in0p,
        'conv0_w': conv0_w, 'conv0_b': conv0_b, 'ssd_dt_bias': ssd_dt_bias, 'ssd_A_log': ssd_A_log,
        'ssd_D': ssd_D, 'ssd_norm': ssd_norm, 'gdn_dt_bias': gdn_dt_bias, 'gdn_A_log': gdn_A_log,
        'gdn_norm': gdn_norm, 'w_out0': w_out0.astype(bf16), 'norm_ffn0': norm_ffn0.reshape(1, d),
        'ffn_w1': ffn_w1.astype(bf16), 'ffn_w3': ffn_w3.astype(bf16), 'ffn_w2': ffn_w2.astype(bf16),
        'ada1_w': ada1_w, 'ada1_b': ada1_b, 'norm_mix1': norm_mix1.reshape(1, d), 'w_in1': w_in1p,
        'fox_f_bias': jnp.concatenate([fox_f_bias.astype(f32), jnp.zeros((LANES - FOX_HEADS,), f32)]).reshape(1, LANES),
        's5_A_re': s5_A_re, 's5_A_im': s5_A_im, 's5_log_step': s5_log_step,
        's5_B_re': s5_B_re, 's5_B_im': s5_B_im, 's5_C_re': s5_C_re, 's5_C_im': s5_C_im, 's5_D': s5_D,
        'glu_w': glu_w, 'glu_b': glu_b, 'w_out1': w_out1.astype(bf16), 'norm_ffn1': norm_ffn1.reshape(1, d),
        'router_w': router_wp, 'moe_w1': moe_w1.astype(bf16), 'moe_w3': moe_w3.astype(bf16),
        'moe_w2': moe_w2.astype(bf16), 'norm_final': norm_final.astype(f32).reshape(1, d),
    }
    p['s5'] = _s5_params(p)
    p['l0'] = _layer0_params({'conv_w': conv0_w, 'conv_b': conv0_b, 'ssd_dt_bias': ssd_dt_bias, 'ssd_A_log': ssd_A_log,
                              'ssd_D': ssd_D, 'ssd_norm': ssd_norm, 'gdn_dt_bias': gdn_dt_bias,
                              'gdn_A_log': gdn_A_log, 'gdn_norm': gdn_norm})
    bp = x_prompt.shape[0]
    c_all = jnp.concatenate([c_prompt, c_sample], axis=0).astype(f32)
    mod0 = _ada(c_all, ada0_w, ada0_b, TN_ADA)
    mod1 = _ada(c_all, ada1_w, ada1_b, TN_ADA)
    outs_p = _run_trunk(
        x_prompt, mod0[:bp], mod1[:bp],
        jnp.zeros((bp, CONV_K - 1, CONV_CH), x_prompt.dtype),
        jnp.zeros((bp, SSD_HEADS, SSD_HEAD_DIM, SSD_STATE), f32),
        jnp.zeros((bp, GDN_HEADS, GDN_HEAD_DIM, GDN_HEAD_DIM), f32),
        None,
        jnp.zeros((bp, S5_GROUPS, S5_STATE), f32),
        jnp.zeros((bp, S5_GROUPS, S5_STATE), f32),
        p, TM_PROMPT)
    outs_s = _run_trunk(
        x_sample, mod0[bp:], mod1[bp:], state_conv0, state_ssd, state_gdn, (cache_k, cache_v, cache_logf, page_table),
        state_s5_re, state_s5_im, p, TM_SAMPLE)
    return (outs_p[0], outs_s[0]) + tuple(outs_p[1:]) + tuple(outs_s[1:])
```
